```python
import math
import numpy as np
import jax
import jax.numpy as jnp
from jax import lax

D_MODEL = 1024
BATCH = 16
SEQ = 256
DEPTH = 4
DEC_BATCH = 4
DEC_SEQ = 4096
PAST_LEN = 256

GRID_W = 64
RET_HEADS = 4
RET_DK = 64
RET_DV = 64
DA_HEADS = 4
DA_DQK = 64
DA_DV = 2 * DA_DQK
HG_HEADS = 4
HG_DK = 64
HG_DV = 64
RET_W = RET_HEADS * RET_DV
DA_W = DA_HEADS * DA_DV
HG_W = HG_HEADS * HG_DV
MIX_W = RET_W + DA_W + HG_W
COL_SIZES = (RET_HEADS * RET_DK, RET_HEADS * RET_DK, RET_W, RET_W,
             DA_HEADS * 2 * DA_DQK, DA_HEADS * 2 * DA_DQK, DA_W,
             HG_HEADS * HG_DK, HG_HEADS * HG_DK, HG_HEADS * HG_DK, HG_W, HG_W)
IN_COLS = sum(COL_SIZES)
CHUNK = 64
Q_BLOCK = 128
ROPE_BASE = 10000.0
ROPE_PAIRS = DA_DQK // 4
RMS_EPS = 1e-6
N_GROUPS = 4
EXPERTS_PER_GROUP = 4
N_EXPERTS = N_GROUPS * EXPERTS_PER_GROUP
TOP_K_INNER = 2
EXPERT_FF = 512

kernel_name = 'hybrid_retention_diffattn_hgrn2_hmoe_dit_step'

F32 = jnp.float32


def rms_norm(x, gain=None):
    xf = x.astype(F32)
    y = xf * lax.rsqrt(jnp.mean(xf * xf, axis=-1, keepdims=True) + RMS_EPS)
    if gain is not None:
        y = y * gain.astype(F32)
    return y.astype(x.dtype)


def modulation(cond, w_mod, b_mod):
    m = jax.nn.silu(cond) @ w_mod + b_mod
    return jnp.split(m[..., None, :], 6, axis=-1)


def to_chunks(x):
    B, T, H, d = x.shape
    return x.reshape(B, T // CHUNK, CHUNK, H, d).transpose(1, 0, 3, 2, 4)


def from_chunks(x):
    N, B, H, C, d = x.shape
    return x.transpose(1, 0, 3, 2, 4).reshape(B, N * C, H, d)


def flip_t(x):
    return jnp.flip(x, axis=1)


def retention_log_decay(reverse):
    lg = jnp.log(1.0 - 2.0 ** (-5.0 - jnp.arange(RET_HEADS, dtype=F32)))
    return lg[::-1] if reverse else lg


def retention_scan(q, k, v, log_gamma, s0):
    pos = jnp.arange(CHUNK, dtype=F32)
    diff = pos[:, None] - pos[None, :]
    lg = log_gamma[:, None, None]
    decay = jnp.exp(jnp.where(diff >= 0, diff * lg, -jnp.inf))
    q_dec = jnp.exp((pos + 1.0)[None, :] * log_gamma[:, None])[..., None]
    k_dec = jnp.exp((CHUNK - 1.0 - pos)[None, :] * log_gamma[:, None])[..., None]
    c_dec = jnp.exp(CHUNK * log_gamma)[:, None, None]

    def step(s, blk):
        qb, kb, vb = blk
        scores = jnp.einsum('bhid,bhjd->bhij', qb, kb) * decay
        o = (jnp.einsum('bhij,bhjv->bhiv', scores, vb)
             + jnp.einsum('bhid,bhdv->bhiv', qb * q_dec, s))
        s = s * c_dec + jnp.einsum('bhjd,bhjv->bhdv', kb * k_dec, vb)
        return s, o

    s_final, o = lax.scan(step, s0, (to_chunks(q), to_chunks(k), to_chunks(v)))
    return from_chunks(o), s_final


def gated_state_scan(q, k, v, log_f, s0):
    mask = (jnp.arange(CHUNK)[:, None] >= jnp.arange(CHUNK)[None, :])[:, :, None]

    def step(s, blk):
        qb, kb, vb, gb = blk
        b = jnp.cumsum(gb, axis=2)
        rel = jnp.exp(jnp.where(mask, b[:, :, :, None, :] - b[:, :, None, :, :], -jnp.inf))
        scores = jnp.einsum('bhijd,bhjd->bhij', rel * qb[:, :, :, None, :], kb)
        o = (jnp.einsum('bhij,bhjv->bhiv', scores, vb)
             + jnp.einsum('bhid,bhdv->bhiv', qb * jnp.exp(b), s))
        b_last = b[:, :, -1:, :]
        s = (s * jnp.exp(b_last[:, :, 0, :, None])
             + jnp.einsum('bhjd,bhjv->bhdv', kb * jnp.exp(b_last - b), vb))
        return s, o

    s_final, o = lax.scan(step, s0, (to_chunks(q), to_chunks(k), to_chunks(v), to_chunks(log_f)))
    return from_chunks(o), s_final


def retention_mixer(q, k, v, g, s0):
    B, T, H, _ = q.shape
    qf = q.astype(F32)
    kf = k.astype(F32) * (RET_DK ** -0.5)
    vf = v.astype(F32)
    s0 = s0.astype(F32)
    o_f, s_f = retention_scan(qf, kf, vf, retention_log_decay(False), s0[:, 0])
    o_b, s_b = retention_scan(flip_t(qf), flip_t(kf), flip_t(vf), retention_log_decay(True), s0[:, 1])
    o = rms_norm(o_f + flip_t(o_b)) * jax.nn.silu(g.astype(F32))
    return o.reshape(B, T, RET_W).astype(g.dtype), jnp.stack([s_f, s_b], axis=1)


def hgrn2_mixer(q, zf, zb, i, g, lb, norm_g, s0):
    B, T, H, _ = q.shape
    lb = lb.reshape(HG_HEADS, HG_DK)
    log_f_f = jnp.log(lb + (1.0 - lb) * jax.nn.sigmoid(zf.astype(F32)))
    log_f_b = jnp.log(lb + (1.0 - lb) * jax.nn.sigmoid(zb.astype(F32)))
    qf = q.astype(F32)
    vf = i.astype(F32)
    s0 = s0.astype(F32)
    o_f, s_f = gated_state_scan(qf, 1.0 - jnp.exp(log_f_f), vf, log_f_f, s0[:, 0])
    o_b, s_b = gated_state_scan(flip_t(qf), flip_t(1.0 - jnp.exp(log_f_b)), flip_t(vf),
                                flip_t(log_f_b), s0[:, 1])
    o = rms_norm(o_f + flip_t(o_b), norm_g) * jax.nn.silu(g.astype(F32))
    return o.reshape(B, T, HG_W).astype(g.dtype), jnp.stack([s_f, s_b], axis=1)


def axial_rope_tables(n_tokens):
    n_rows = n_tokens // GRID_W
    rows = jnp.repeat(jnp.arange(n_rows), GRID_W).astype(F32)
    cols = jnp.tile(jnp.arange(GRID_W), n_rows).astype(F32)
    inv_freq = ROPE_BASE ** (-(jnp.arange(ROPE_PAIRS, dtype=F32) / ROPE_PAIRS))
    ang_r = rows[:, None] * inv_freq[None, :]
    ang_c = cols[:, None] * inv_freq[None, :]
    return (jnp.cos(ang_r), jnp.sin(ang_r), jnp.cos(ang_c), jnp.sin(ang_c))


def apply_axial_rope(x, tables):
    cos_r, sin_r, cos_c, sin_c = tables

    def rot(xa, cos, sin):
        cos = cos[None, :, None, None, :].astype(xa.dtype)
        sin = sin[None, :, None, None, :].astype(xa.dtype)
        x1, x2 = jnp.split(xa, 2, axis=-1)
        return jnp.concatenate([x1 * cos - x2 * sin, x1 * sin + x2 * cos], axis=-1)

    xr, xc = jnp.split(x, 2, axis=-1)
    return jnp.concatenate([rot(xr, cos_r, sin_r), rot(xc, cos_c, sin_c)], axis=-1)


def diff_softmax_attention(q, k, v, lam):
    B, T, H, _, d = q.shape
    nb = T // Q_BLOCK
    qb = q.reshape(B, nb, Q_BLOCK, H, 2, d).transpose(1, 0, 2, 3, 4, 5)
    scale = d ** -0.5

    def block(qi):
        s = jnp.einsum('bqhcd,bkhcd->bhcqk', qi, k, preferred_element_type=F32) * scale
        p = jax.nn.softmax(s, axis=-1)
        a = p[:, :, 0] - lam * p[:, :, 1]
        return jnp.einsum('bhqk,bkhv->bqhv', a.astype(v.dtype), v)

    o = lax.map(block, qb)
    return o.transpose(1, 0, 2, 3, 4).reshape(B, T, H, v.shape[-1])


def diff_attention(q, k, v, qn_g, kn_g, lam_p, subln_g, layer_idx, ctx_k, ctx_v, rope):
    B, T, H, _ = v.shape
    q = rms_norm(q, qn_g)
    k = rms_norm(k, kn_g)
    lam_init = 0.8 - 0.6 * math.exp(-0.3 * layer_idx)
    lp = lam_p.astype(F32)
    lam = jnp.exp(jnp.sum(lp[0] * lp[1])) - jnp.exp(jnp.sum(lp[2] * lp[3])) + lam_init
    if ctx_k is None:
        qr, keys, vals = q, k, v
    else:
        qr = apply_axial_rope(q, rope)
        keys = jnp.concatenate([apply_axial_rope(k, rope), ctx_k.astype(k.dtype)], axis=1)
        vals = jnp.concatenate([v, ctx_v.astype(v.dtype)], axis=1)
    o = diff_softmax_attention(qr, keys, vals, lam)
    o = rms_norm(o, subln_g) * (1.0 - lam_init)
    return o.reshape(B, T, H * DA_DV), k, v


def hier_moe(h, w_group, b_group, w_router, b_router, w_gate, w_up, w_down):
    hf = h.astype(F32)
    g_logits = hf @ w_group.astype(F32) + b_group.astype(F32)
    g_prob = jax.nn.softmax(g_logits, axis=-1)
    g_sel = jnp.argmax(g_logits, axis=-1)
    g_w = jnp.take_along_axis(g_prob, g_sel[:, None], axis=-1)
    e_logits = (hf @ w_router.astype(F32) + b_router.astype(F32)).reshape(-1, N_GROUPS, EXPERTS_PER_GROUP)
    e_in = jnp.take_along_axis(e_logits, g_sel[:, None, None], axis=1)[:, 0]
    top_v, top_i = lax.top_k(e_in, TOP_K_INNER)
    weights = g_w * jax.nn.softmax(top_v, axis=-1)
    expert_id = g_sel[:, None] * EXPERTS_PER_GROUP + top_i
    combine = jnp.sum(jax.nn.one_hot(expert_id, N_EXPERTS, dtype=F32) * weights[..., None], axis=1)
    out = jnp.zeros(hf.shape, F32)
    for e in range(N_EXPERTS):
        a = jax.nn.silu(h @ w_gate[e]) * (h @ w_up[e])
        out = out + combine[:, e:e + 1] * (a @ w_down[e]).astype(F32)
    return out.astype(h.dtype)


def trunk_layer(x, cond, p, layer_idx, ret_s0, hg_s0, ctx_k, ctx_v, rope):
    B, T, D = x.shape
    shift1, scale1, gate1, shift2, scale2, gate2 = modulation(cond, p['w_mod'], p['b_mod'])
    h = rms_norm(x, p['norm1']) * (1.0 + scale1) + shift1
    split_at = np.cumsum(COL_SIZES)[:-1].tolist()
    rq, rk, rv, rg, dq, dk, dv, hq, hzf, hzb, hi, hgate = jnp.split(h @ p['w_in'], split_at, axis=-1)
    ret_out, ret_s = retention_mixer(
        rq.reshape(B, T, RET_HEADS, RET_DK), rk.reshape(B, T, RET_HEADS, RET_DK),
        rv.reshape(B, T, RET_HEADS, RET_DV), rg.reshape(B, T, RET_HEADS, RET_DV), ret_s0)
    da_out, k_new, v_new = diff_attention(
        dq.reshape(B, T, DA_HEADS, 2, DA_DQK), dk.reshape(B, T, DA_HEADS, 2, DA_DQK),
        dv.reshape(B, T, DA_HEADS, DA_DV), p['da_qn'], p['da_kn'], p['da_lambda'],
        p['da_subln'], layer_idx, ctx_k, ctx_v, rope)
    hg_out, hg_s = hgrn2_mixer(
        hq.reshape(B, T, HG_HEADS, HG_DK), hzf.reshape(B, T, HG_HEADS, HG_DK),
        hzb.reshape(B, T, HG_HEADS, HG_DK), hi.reshape(B, T, HG_HEADS, HG_DV),
        hgate.reshape(B, T, HG_HEADS, HG_DV), p['hg_lb'], p['hg_norm'], hg_s0)
    mixed = jnp.concatenate([ret_out, da_out, hg_out], axis=-1) @ p['w_out']
    x = x + gate1 * mixed
    h = rms_norm(x, p['norm2']) * (1.0 + scale2) + shift2
    ffn = hier_moe(h.reshape(B * T, D), p['w_group'], p['b_group'], p['w_router'], p['b_router'],
                   p['w_gate'], p['w_up'], p['w_down']).reshape(B, T, D)
    x = x + gate2 * ffn
    return x, ret_s, hg_s, k_new, v_new


def setup_inputs(seed: int = 0) -> dict:
    key = jax.random.key(seed)
    ks = jax.random.split(key, 32)
    nrm = jax.random.normal
    d = D_MODEL
    return {
        'x_prompt': nrm(ks[0], (BATCH, SEQ, d), F32),
        'x_sample': nrm(ks[1], (DEC_BATCH, DEC_SEQ, d), F32),
        'cache_k': nrm(ks[2], (DEC_BATCH, DEPTH, PAST_LEN, DA_HEADS, 2, DA_DQK), F32),
        'cache_v': nrm(ks[3], (DEC_BATCH, DEPTH, PAST_LEN, DA_HEADS, DA_DV), F32),
        'state_ret': nrm(ks[4], (DEC_BATCH, DEPTH, 2, RET_HEADS, RET_DK, RET_DV), F32),
        'state_hgrn': 0.5 * nrm(ks[5], (DEC_BATCH, DEPTH, 2, HG_HEADS, HG_DK, HG_DV), F32),
        'c': nrm(ks[6], (DEC_BATCH, d), F32),
        'c_ctx': nrm(ks[7], (d,), F32),
        'norm1': 1.0 + 0.02 * nrm(ks[8], (DEPTH, d), F32),
        'norm2': 1.0 + 0.02 * nrm(ks[9], (DEPTH, d), F32),
        'w_mod': 0.5 * d ** -0.5 * nrm(ks[10], (DEPTH, d, 6 * d), F32),
        'b_mod': 0.01 * nrm(ks[11], (DEPTH, 6 * d), F32),
        'w_in': d ** -0.5 * nrm(ks[12], (DEPTH, d, IN_COLS), F32),
        'w_out': MIX_W ** -0.5 * nrm(ks[13], (DEPTH, MIX_W, d), F32),
        'da_qn': 1.0 + 0.02 * nrm(ks[14], (DEPTH, DA_DQK), F32),
        'da_kn': 1.0 + 0.02 * nrm(ks[15], (DEPTH, DA_DQK), F32),
        'da_lambda': 0.1 * nrm(ks[16], (DEPTH, 4, DA_DQK), F32),
        'da_subln': 1.0 + 0.02 * nrm(ks[17], (DEPTH, DA_DV), F32),
        'hg_lb': nrm(ks[18], (DEPTH, HG_HEADS * HG_DK), F32),
        'hg_norm': 1.0 + 0.02 * nrm(ks[19], (DEPTH, HG_DV), F32),
        'w_group': d ** -0.5 * nrm(ks[20], (DEPTH, d, N_GROUPS), F32),
        'b_group': 0.01 * nrm(ks[21], (DEPTH, N_GROUPS), F32),
        'w_router': d ** -0.5 * nrm(ks[22], (DEPTH, d, N_EXPERTS), F32),
        'b_router': 0.01 * nrm(ks[23], (DEPTH, N_EXPERTS), F32),
        'w_gate': d ** -0.5 * nrm(ks[24], (DEPTH, N_EXPERTS, d, EXPERT_FF), F32),
        'w_up': d ** -0.5 * nrm(ks[25], (DEPTH, N_EXPERTS, d, EXPERT_FF), F32),
        'w_down': EXPERT_FF ** -0.5 * nrm(ks[26], (DEPTH, N_EXPERTS, EXPERT_FF, d), F32),
    }


def reference(x_prompt, x_sample, cache_k, cache_v, state_ret, state_hgrn, c, c_ctx,
              norm1, norm2, w_mod, b_mod, w_in, w_out, da_qn, da_kn, da_lambda, da_subln,
              hg_lb, hg_norm, w_group, b_group, w_router, b_router, w_gate, w_up, w_down):
    rope = axial_rope_tables(x_sample.shape[1])
    lb_soft = jax.nn.softmax(hg_lb.astype(F32), axis=0)
    lb_all = jnp.cumsum(lb_soft, axis=0) - lb_soft[0]
    n_ctx = x_prompt.shape[0]
    zero_ret = jnp.zeros((n_ctx, 2, RET_HEADS, RET_DK, RET_DV), F32)
    zero_hg = jnp.zeros((n_ctx, 2, HG_HEADS, HG_DK, HG_DV), F32)
    yp = x_prompt
    ys = x_sample
    ks_out, vs_out, rets_out, hgs_out = [], [], [], []
    for l in range(DEPTH):
        p = {'norm1': norm1[l], 'norm2': norm2[l], 'w_mod': w_mod[l], 'b_mod': b_mod[l],
             'w_in': w_in[l], 'w_out': w_out[l], 'da_qn': da_qn[l], 'da_kn': da_kn[l],
             'da_lambda': da_lambda[l], 'da_subln': da_subln[l], 'hg_lb': lb_all[l],
             'hg_norm': hg_norm[l], 'w_group': w_group[l], 'b_group': b_group[l],
             'w_router': w_router[l], 'b_router': b_router[l], 'w_gate': w_gate[l],
             'w_up': w_up[l], 'w_down': w_down[l]}
        yp, ret_s, hg_s, k_new, v_new = trunk_layer(yp, c_ctx, p, l, zero_ret, zero_hg, None, None, None)
        ks_out.append(k_new)
        vs_out.append(v_new)
        rets_out.append(ret_s)
        hgs_out.append(hg_s)
        ys, _, _, _, _ = trunk_layer(ys, c, p, l, state_ret[:, l], state_hgrn[:, l],
                                     cache_k[:, l], cache_v[:, l], rope)
    new_cache_k = jnp.stack(ks_out, axis=1)
    new_cache_v = jnp.stack(vs_out, axis=1)
    new_state_ret = jnp.stack(rets_out, axis=1).astype(x_prompt.dtype)
    new_state_hgrn = jnp.stack(hgs_out, axis=1).astype(x_prompt.dtype)
    return (yp, ys, new_cache_k, new_cache_v, new_state_ret, new_state_hgrn)
```

```python
import functools
import math

import numpy as np
import jax
import jax.numpy as jnp
from jax import lax
from jax.experimental import pallas as pl
from jax.experimental.pallas import tpu as pltpu

F32 = jnp.float32
BF16 = jnp.bfloat16

D_MODEL = 1024
BATCH = 16
SEQ = 256
DEPTH = 4
DEC_BATCH = 4
DEC_SEQ = 4096
PAST_LEN = 256
GRID_W = 64
HEADS = 4
HEAD_D = 64
MIX_G = HEADS * HEAD_D
DA_W = 512
IN_COLS = 3840
ROPE_BASE = 10000.0
ROPE_PAIRS = 16
RMS_EPS = 1e-6
N_GROUPS = 4
EPG = 4
N_EXPERTS = 16
EXPERT_FF = 512

TILE = 256
N_CTX_ROWS = BATCH * SEQ
N_LAT_ROWS = DEC_BATCH * DEC_SEQ
N_ROWS = N_CTX_ROWS + N_LAT_ROWS
CTX_TILES = N_CTX_ROWS // TILE
LAT_TILES_PER_SEQ = DEC_SEQ // TILE
N_TILES = N_ROWS // TILE
N_COND = 8

CHUNK = 64
SUB = 16
NSUB = CHUNK // SUB
NCHUNK = TILE // CHUNK

VMEM_LIMIT = 48 * 1024 * 1024
NEG_BIG = -1e30

HI = lax.Precision.HIGHEST


def _cparams(n_axes):
    return pltpu.CompilerParams(dimension_semantics=("arbitrary",) * n_axes,
                                vmem_limit_bytes=VMEM_LIMIT)


def _dot(a, b):
    return jnp.dot(a, b, preferred_element_type=F32)


def _dot_nt(a, b):
    return lax.dot_general(a, b, (((1,), (1,)), ((), ())), preferred_element_type=F32)


def _dot_tn(a, b):
    return lax.dot_general(a, b, (((0,), (0,)), ((), ())), preferred_element_type=F32)


def _split_dot(x, m_bf16):
    hi = x.astype(BF16)
    lo = (x - hi.astype(F32)).astype(BF16)
    return _dot(hi, m_bf16) + _dot(lo, m_bf16)


def _silu(x):
    return x * (1.0 / (1.0 + jnp.exp(-x)))


def _sigmoid(x):
    return 1.0 / (1.0 + jnp.exp(-x))


def _mod_kernel(cond_ref, w_ref, b_ref, lb_ref, mod_ref, lbo_ref):
    c = cond_ref[...]
    m = jnp.dot(_silu(c), w_ref[...], preferred_element_type=F32, precision=HI)
    mod_ref[...] = m + b_ref[...]
    z = lb_ref[...]
    zmax = jnp.max(z, axis=0, keepdims=True)
    e = jnp.exp(z - zmax)
    p = e / jnp.sum(e, axis=0, keepdims=True)
    rows = [jnp.zeros_like(p[0:1])]
    for l in range(1, DEPTH):
        rows.append(rows[-1] + p[l:l + 1])
    lbo_ref[...] = jnp.concatenate(rows, axis=0)


def _modulation(cond, w_mod, b_mod, hg_lb):
    nblk = 6
    return pl.pallas_call(
        _mod_kernel,
        grid=(DEPTH, nblk),
        in_specs=[
            pl.BlockSpec((N_COND, D_MODEL), lambda l, j: (0, 0)),
            pl.BlockSpec((None, D_MODEL, D_MODEL), lambda l, j: (l, 0, j)),
            pl.BlockSpec((None, 1, D_MODEL), lambda l, j: (l, 0, j)),
            pl.BlockSpec((DEPTH, MIX_G), lambda l, j: (0, 0)),
        ],
        out_specs=[
            pl.BlockSpec((None, N_COND, D_MODEL), lambda l, j: (l, 0, j)),
            pl.BlockSpec((DEPTH, MIX_G), lambda l, j: (0, 0)),
        ],
        out_shape=[jax.ShapeDtypeStruct((DEPTH, N_COND, 6 * D_MODEL), F32),
                   jax.ShapeDtypeStruct((DEPTH, MIX_G), F32)],
        compiler_params=_cparams(2),
        name="modulation",
    )(cond, w_mod, b_mod.reshape(DEPTH, 1, 6 * D_MODEL), hg_lb)


def _group_mean_sq(x, bd_ref, group):
    return _split_dot(x * x, bd_ref[...]) * (1.0 / group)


def _swap16(x):
    w = x.shape[-1]
    lane = lax.broadcasted_iota(jnp.int32, x.shape, x.ndim - 1)
    up = pltpu.roll(x, w - 16, x.ndim - 1)
    dn = pltpu.roll(x, 16, x.ndim - 1)
    return jnp.where((lane % 32) < 16, up, dn)


def _inproj_kernel(latent, x_ref, mod_ref, n1_ref, w_ref, qn_ref, kn_ref, lb_ref, bd_ref, *rest):
    if latent:
        cos_ref, sin_ref = rest[0], rest[1]
        outs = rest[2:]
    else:
        outs = rest
    (rq_ref, rk_ref, rv_ref, rsg_ref, dq_ref, dk_ref, dv_ref,
     hq_ref, hkf_ref, hkb_ref, hv_ref, hgf_ref, hgb_ref, hsg_ref) = outs[:14]

    x = x_ref[...]
    ms = jnp.mean(x * x, axis=-1, keepdims=True)
    shift1 = mod_ref[0:1, :]
    scale1 = mod_ref[1:2, :]
    h = x * lax.rsqrt(ms + RMS_EPS) * n1_ref[...] * (1.0 + scale1) + shift1
    y = _dot(h.astype(BF16), w_ref[...])

    rq_ref[...] = y[:, 0:256].astype(BF16)
    rk_ref[...] = (y[:, 256:512] * (HEAD_D ** -0.5)).astype(BF16)
    rv_ref[...] = y[:, 512:768].astype(BF16)
    rsg_ref[...] = _silu(y[:, 768:1024]).astype(BF16)

    dq = y[:, 1024:1536]
    dk = y[:, 1536:2048]
    qn = dq * lax.rsqrt(_group_mean_sq(dq, bd_ref, HEAD_D) + RMS_EPS) * qn_ref[...]
    kn = dk * lax.rsqrt(_group_mean_sq(dk, bd_ref, HEAD_D) + RMS_EPS) * kn_ref[...]
    if latent:
        cos = jnp.concatenate([cos_ref[...]] * 4, axis=1)
        sin = jnp.concatenate([sin_ref[...]] * 4, axis=1)
        qr = qn * cos + _swap16(qn) * sin
        kr = kn * cos + _swap16(kn) * sin
    else:
        qr, kr = qn, kn
        ck_ref, cv_ref = outs[14], outs[15]
        ck_ref[...] = kn
        cv_ref[...] = y[:, 2048:2560]
    dq_ref[...] = (qr * (HEAD_D ** -0.5)).astype(BF16)
    dk_ref[...] = kr.astype(BF16)
    dv_ref[...] = y[:, 2048:2560].astype(BF16)

    lb = lb_ref[...]
    f_f = lb + (1.0 - lb) * _sigmoid(y[:, 2816:3072])
    f_b = lb + (1.0 - lb) * _sigmoid(y[:, 3072:3328])
    hq_ref[...] = y[:, 2560:2816].astype(BF16)
    hkf_ref[...] = (1.0 - f_f).astype(BF16)
    hkb_ref[...] = (1.0 - f_b).astype(BF16)
    hgf_ref[...] = jnp.log(f_f)
    hgb_ref[...] = jnp.log(f_b)
    hv_ref[...] = y[:, 3328:3584].astype(BF16)
    hsg_ref[...] = _silu(y[:, 3584:3840]).astype(BF16)


def _in_projection(latent, layer, x, mod, norm1, w_in_bf, qn_t, kn_t, lb_all, bd512, rope_cos, rope_sin):
    nrows = x.shape[0]
    ntiles = nrows // TILE
    row = lambda t: (t, 0)
    cond = (lambda t: 1 + t // LAT_TILES_PER_SEQ) if latent else (lambda t: 0)

    in_specs = [
        pl.BlockSpec((TILE, D_MODEL), row),
        pl.BlockSpec((None, None, 6, D_MODEL), lambda t: (layer, cond(t), 0, 0)),
        pl.BlockSpec((None, 1, D_MODEL), lambda t: (layer, 0, 0)),
        pl.BlockSpec((None, D_MODEL, IN_COLS), lambda t: (layer, 0, 0)),
        pl.BlockSpec((None, 1, DA_W), lambda t: (layer, 0, 0)),
        pl.BlockSpec((None, 1, DA_W), lambda t: (layer, 0, 0)),
        pl.BlockSpec((None, 1, MIX_G), lambda t: (layer, 0, 0)),
        pl.BlockSpec((DA_W, DA_W), lambda t: (0, 0)),
    ]
    args = [x, mod, norm1, w_in_bf, qn_t, kn_t, lb_all, bd512]
    if latent:
        in_specs += [pl.BlockSpec((TILE, 128), lambda t: (t % LAT_TILES_PER_SEQ, 0))] * 2
        args += [rope_cos, rope_sin]

    def o(width, dtype):
        return jax.ShapeDtypeStruct((nrows, width), dtype), pl.BlockSpec((TILE, width), lambda t: (t, 0))

    outs = [o(256, BF16), o(256, BF16), o(256, BF16), o(256, BF16),
            o(512, BF16), o(512, BF16), o(512, BF16),
            o(256, BF16), o(256, BF16), o(256, BF16), o(256, BF16), o(256, F32), o(256, F32), o(256, BF16)]
    if not latent:
        outs += [o(512, F32), o(512, F32)]
    return pl.pallas_call(
        functools.partial(_inproj_kernel, latent),
        grid=(ntiles,),
        in_specs=in_specs,
        out_specs=[s for _, s in outs],
        out_shape=[s for s, _ in outs],
        compiler_params=_cparams(1),
        name="in_projection_lat" if latent else "in_projection_ctx",
    )(*args)


def _head_masks():
    lane = lax.broadcasted_iota(jnp.int32, (1, MIX_G), 1)
    return [(lane // HEAD_D == h).astype(F32) for h in range(HEADS)]


def _chunk_step(q, k, v, g, st, rev, consts):
    tri, bd_mask, ones_bd, gsel, hmask = consts
    b = jnp.dot(tri, g, preferred_element_type=F32, precision=HI)

    zero_row = jnp.zeros((1, MIX_G), F32)
    r_rows, e_rows = [], []
    for s in range(NSUB):
        lo, hi = s * SUB, s * SUB + SUB - 1
        if not rev:
            r_rows.append(b[lo - 1:lo, :] if s > 0 else zero_row)
            e_rows.append(b[hi:hi + 1, :])
        else:
            r_rows.append(b[hi + 1:hi + 2, :] if s < NSUB - 1 else zero_row)
            e_rows.append(b[lo:lo + 1, :])
    btot = e_rows[NSUB - 1] if not rev else e_rows[0]
    r_full = jnp.concatenate([jnp.broadcast_to(r, (SUB, MIX_G)) for r in r_rows], axis=0)
    e_full = jnp.concatenate([jnp.broadcast_to(e, (SUB, MIX_G)) for e in e_rows], axis=0)

    bl = b - r_full
    qp = q * jnp.exp(bl)
    kend = k * jnp.exp(e_full - b)
    q_inter = qp * jnp.exp(r_full)
    k_state = kend * jnp.exp(btot - e_full)
    v16 = v

    o = _dot_nt(q_inter.astype(BF16), st.astype(BF16))
    ut = _dot_tn(v16, k_state.astype(BF16))
    st_new = st * jnp.exp(btot) + ut * bd_mask

    pairs = [(i, j) for i in range(NSUB) for j in range(NSUB) if (j < i if not rev else j > i)]
    lhs = [qp[i * SUB:(i + 1) * SUB, :] * jnp.exp(r_rows[i] - e_rows[j]) for (i, j) in pairs]
    lh = jnp.concatenate(lhs, axis=0)
    lhs4 = jnp.concatenate([lh * hmask[h] for h in range(HEADS)], axis=0).astype(BF16)
    sc = _dot_nt(lhs4, kend.astype(BF16))
    col = lax.broadcasted_iota(jnp.int32, (SUB, CHUNK), 1) // SUB
    npairs = len(pairs)
    p_rows = []
    for h in range(HEADS):
        for i in range(NSUB):
            acc = None
            for p, (pi, pj) in enumerate(pairs):
                if pi != i:
                    continue
                base = (h * npairs + p) * SUB
                blk = jnp.where(col == pj, sc[base:base + SUB, :], 0.0)
                acc = blk if acc is None else acc + blk
            if acc is None:
                acc = jnp.zeros((SUB, CHUNK), F32)
            p_rows.append(acc)
    ps = jnp.concatenate(p_rows, axis=0).astype(BF16)
    full = _dot(ps, v16)
    for h in range(HEADS):
        o = o + full[h * CHUNK:(h + 1) * CHUNK, :] * hmask[h]

    jrow = lax.broadcasted_iota(jnp.int32, (SUB, MIX_G), 0)
    pp, vt = [], []
    for s in range(NSUB):
        sl = slice(s * SUB, (s + 1) * SUB)
        bl_s, q_s, k_s = bl[sl, :], q[sl, :], k[sl, :]
        for i in range(SUB):
            d = bl_s[i:i + 1, :] - bl_s
            keep = (jrow <= i) if not rev else (jrow >= i)
            e = jnp.exp(jnp.where(keep, d, NEG_BIG))
            pp.append((e * q_s[i:i + 1, :] * k_s).astype(BF16))
        vt.extend([v16[sl, :]] * SUB)
    pp = jnp.concatenate(pp, axis=0)
    sb = _dot(pp, ones_bd)
    w = (sb * jnp.concatenate(vt, axis=0).astype(F32)).astype(BF16)
    o = o + _dot(gsel, w)
    return o, st_new


def _scan_consts(rev):
    r = lax.broadcasted_iota(jnp.int32, (CHUNK, CHUNK), 0)
    c = lax.broadcasted_iota(jnp.int32, (CHUNK, CHUNK), 1)
    tri = ((c <= r) if not rev else (c >= r)).astype(F32)
    rr = lax.broadcasted_iota(jnp.int32, (MIX_G, MIX_G), 0) // HEAD_D
    cc = lax.broadcasted_iota(jnp.int32, (MIX_G, MIX_G), 1) // HEAD_D
    bd_mask = (rr == cc).astype(F32)
    ones_bd = bd_mask.astype(BF16)
    gi = lax.broadcasted_iota(jnp.int32, (CHUNK, CHUNK * SUB), 0)
    gj = lax.broadcasted_iota(jnp.int32, (CHUNK, CHUNK * SUB), 1) // SUB
    gsel = (gi == gj).astype(BF16)
    return tri, bd_mask, ones_bd, gsel, _head_masks()


def _scan_kernel(nt, qf_ref, kf_ref, vf_ref, gf_ref, qb_ref, kb_ref, vb_ref, gb_ref, s0_ref,
                 of_ref, ob_ref, sfin_ref, st_scr):
    t = pl.program_id(1)

    @pl.when(t == 0)
    def _():
        st_scr[...] = s0_ref[...]

    cf = _scan_consts(False)
    cb = _scan_consts(True)

    def body(c, carry):
        rf = pl.multiple_of(c * CHUNK, CHUNK)
        rb = pl.multiple_of((NCHUNK - 1 - c) * CHUNK, CHUNK)
        o_f, st_f = _chunk_step(qf_ref[pl.ds(rf, CHUNK), :].astype(F32),
                                kf_ref[pl.ds(rf, CHUNK), :].astype(F32),
                                vf_ref[pl.ds(rf, CHUNK), :],
                                gf_ref[pl.ds(rf, CHUNK), :], st_scr[0], False, cf)
        of_ref[pl.ds(rf, CHUNK), :] = o_f
        st_scr[0] = st_f
        o_b, st_b = _chunk_step(qb_ref[pl.ds(rb, CHUNK), :].astype(F32),
                                kb_ref[pl.ds(rb, CHUNK), :].astype(F32),
                                vb_ref[pl.ds(rb, CHUNK), :],
                                gb_ref[pl.ds(rb, CHUNK), :], st_scr[1], True, cb)
        ob_ref[pl.ds(rb, CHUNK), :] = o_b
        st_scr[1] = st_b
        return carry

    lax.fori_loop(0, NCHUNK, body, 0)

    @pl.when(t == nt - 1)
    def _():
        sfin_ref[...] = st_scr[...]


def _gated_scan(n_seq, nt, tile0, q, k_f, k_b, v, g_f, g_b, g_const, s0, name):
    def fwd(s, t):
        return (tile0 + s * nt + t, 0)

    def bwd(s, t):
        return (tile0 + s * nt + (nt - 1 - t), 0)

    const = lambda s, t: (0, 0)
    blk = lambda im: pl.BlockSpec((TILE, MIX_G), im)
    in_specs = [blk(fwd), blk(fwd), blk(fwd), blk(const if g_const else fwd),
                blk(bwd), blk(bwd), blk(bwd), blk(const if g_const else bwd),
                pl.BlockSpec((None, 2, MIX_G, MIX_G), lambda s, t: (s, 0, 0, 0))]
    nrows = n_seq * nt * TILE
    return pl.pallas_call(
        functools.partial(_scan_kernel, nt),
        grid=(n_seq, nt),
        in_specs=in_specs,
        out_specs=[pl.BlockSpec((TILE, MIX_G), lambda s, t: (s * nt + t, 0)),
                   pl.BlockSpec((TILE, MIX_G), lambda s, t: (s * nt + (nt - 1 - t), 0)),
                   pl.BlockSpec((None, 2, MIX_G, MIX_G), lambda s, t: (s, 0, 0, 0))],
        out_shape=[jax.ShapeDtypeStruct((nrows, MIX_G), F32),
                   jax.ShapeDtypeStruct((nrows, MIX_G), F32),
                   jax.ShapeDtypeStruct((n_seq, 2, MIX_G, MIX_G), F32)],
        scratch_shapes=[pltpu.VMEM((2, MIX_G, MIX_G), F32)],
        compiler_params=_cparams(2),
        name=name,
    )(q, k_f, v, g_f, q, k_b, v, g_b, s0)


def _attn_kernel(lam_init, has_cache, q_ref, k_ref, v_ref, *rest):
    if has_cache:
        ck_ref, cv_ref, lam_ref, sub_ref, o_ref = rest
    else:
        lam_ref, sub_ref, o_ref = rest
    q = q_ref[...]
    k = k_ref[...]
    lp = lam_ref[...]
    lam = (jnp.exp(jnp.sum(lp[0:1] * lp[1:2], axis=-1, keepdims=True))
           - jnp.exp(jnp.sum(lp[2:3] * lp[3:4], axis=-1, keepdims=True)) + lam_init)
    a_lat, a_ctx = None, None
    for c in range(2):
        qc = q[:, c * HEAD_D:(c + 1) * HEAD_D]
        s = _dot_nt(qc, k[:, c * HEAD_D:(c + 1) * HEAD_D])
        m = jnp.max(s, axis=-1, keepdims=True)
        if has_cache:
            s2 = _dot_nt(qc, ck_ref[:, c * HEAD_D:(c + 1) * HEAD_D])
            m = jnp.maximum(m, jnp.max(s2, axis=-1, keepdims=True))
        p = jnp.exp(s - m)
        l = jnp.sum(p, axis=-1, keepdims=True)
        if has_cache:
            p2 = jnp.exp(s2 - m)
            l = l + jnp.sum(p2, axis=-1, keepdims=True)
        coef = (1.0 / l) if c == 0 else (-lam / l)
        a_lat = p * coef if a_lat is None else a_lat + p * coef
        if has_cache:
            a_ctx = p2 * coef if a_ctx is None else a_ctx + p2 * coef
    o = _dot(a_lat.astype(BF16), v_ref[...])
    if has_cache:
        o = o + _dot(a_ctx.astype(BF16), cv_ref[...])
    ms = jnp.mean(o * o, axis=-1, keepdims=True)
    o_ref[...] = (o * lax.rsqrt(ms + RMS_EPS) * sub_ref[...] * (1.0 - lam_init)).astype(BF16)


def _diff_attention(layer, n_seq, seq_len, q, k, v, cache_k, cache_v, da_lambda, subln):
    tq = TILE
    nq = seq_len // tq
    has_cache = cache_k is not None
    lam_init = 0.8 - 0.6 * math.exp(-0.3 * layer)
    in_specs = [pl.BlockSpec((tq, 128), lambda b, h, i: (b * nq + i, h)),
                pl.BlockSpec((seq_len, 128), lambda b, h, i: (b, h)),
                pl.BlockSpec((seq_len, 128), lambda b, h, i: (b, h))]
    args = [q, k, v]
    if has_cache:
        in_specs += [pl.BlockSpec((None, None, PAST_LEN, 128), lambda b, h, i: (b, layer, 0, h))] * 2
        args += [cache_k, cache_v]
    in_specs += [pl.BlockSpec((None, 4, HEAD_D), lambda b, h, i: (layer, 0, 0)),
                 pl.BlockSpec((None, 1, 128), lambda b, h, i: (layer, 0, 0))]
    args += [da_lambda, subln]
    return pl.pallas_call(
        functools.partial(_attn_kernel, lam_init, has_cache),
        grid=(n_seq, HEADS, nq),
        in_specs=in_specs,
        out_specs=pl.BlockSpec((tq, 128), lambda b, h, i: (b * nq + i, h)),
        out_shape=jax.ShapeDtypeStruct((n_seq * seq_len, DA_W), BF16),
        compiler_params=_cparams(3),
        name="diff_attention_lat" if has_cache else "diff_attention_ctx",
    )(*args)


def _route(logits):
    lane = lax.broadcasted_iota(jnp.int32, logits.shape, 1)
    lane_f = lane.astype(F32)
    big = 1e9
    is_g = jnp.logical_and(lane >= N_EXPERTS, lane < N_EXPERTS + N_GROUPS)
    gl = jnp.where(is_g, logits, NEG_BIG)
    gmax = jnp.max(gl, axis=-1, keepdims=True)
    gsel = jnp.min(jnp.where(gl == gmax, lane_f, big), axis=-1, keepdims=True) - N_EXPERTS
    g_w = 1.0 / jnp.sum(jnp.exp(gl - gmax), axis=-1, keepdims=True)
    in_grp = jnp.logical_and(lane < N_EXPERTS, (lane // EPG).astype(F32) == gsel)
    el = jnp.where(in_grp, logits, NEG_BIG)
    v1 = jnp.max(el, axis=-1, keepdims=True)
    i1 = jnp.min(jnp.where(el == v1, lane_f, big), axis=-1, keepdims=True)
    el2 = jnp.where(lane_f == i1, NEG_BIG, el)
    v2 = jnp.max(el2, axis=-1, keepdims=True)
    i2 = jnp.min(jnp.where(el2 == v2, lane_f, big), axis=-1, keepdims=True)
    t = jnp.exp(v2 - v1)
    w1 = g_w / (1.0 + t)
    w2 = w1 * t
    return jnp.where(lane_f == i1, w1, 0.0) + jnp.where(lane_f == i2, w2, 0.0)


def _outproj_kernel(x_ref, mod_ref, n2_ref, rof_ref, rob_ref, rsg_ref, da_ref, hof_ref, hob_ref, hsg_ref,
                    hgn_ref, bd_ref, wout_ref, wr_ref, br_ref, x1_ref, h2_ref, comb_ref):
    ro = rof_ref[...] + rob_ref[...]
    ro = ro * lax.rsqrt(_group_mean_sq(ro, bd_ref, HEAD_D) + RMS_EPS) * rsg_ref[...].astype(F32)
    ho = hof_ref[...] + hob_ref[...]
    ho = ho * lax.rsqrt(_group_mean_sq(ho, bd_ref, HEAD_D) + RMS_EPS) * hgn_ref[...] * hsg_ref[...].astype(F32)
    mix = jnp.concatenate([ro.astype(BF16), da_ref[...], ho.astype(BF16)], axis=1)
    mixed = _dot(mix, wout_ref[...])
    gate1 = mod_ref[2:3, :]
    shift2 = mod_ref[3:4, :]
    scale2 = mod_ref[4:5, :]
    x1 = x_ref[...] + gate1 * mixed
    ms = jnp.mean(x1 * x1, axis=-1, keepdims=True)
    h2 = x1 * lax.rsqrt(ms + RMS_EPS) * n2_ref[...] * (1.0 + scale2) + shift2
    x1_ref[...] = x1
    h2_ref[...] = h2.astype(BF16)
    logits = jnp.dot(h2, wr_ref[...], preferred_element_type=F32, precision=HI) + br_ref[...]
    comb_ref[...] = _route(logits)


def _out_projection(latent, layer, x, mod, norm2, ret_of, ret_ob, ret_sg, da_o, hg_of, hg_ob, hg_sg,
                    hgn_t, bd256, w_out_bf, w_route, b_route):
    ntiles = x.shape[0] // TILE
    row = lambda t: (t, 0)
    cond = (lambda t: 1 + t // LAT_TILES_PER_SEQ) if latent else (lambda t: 0)
    lay = lambda t: (layer, 0, 0)
    in_specs = [
        pl.BlockSpec((TILE, D_MODEL), row),
        pl.BlockSpec((None, None, 6, D_MODEL), lambda t: (layer, cond(t), 0, 0)),
        pl.BlockSpec((None, 1, D_MODEL), lay),
        pl.BlockSpec((TILE, MIX_G), row), pl.BlockSpec((TILE, MIX_G), row), pl.BlockSpec((TILE, MIX_G), row),
        pl.BlockSpec((TILE, DA_W), row),
        pl.BlockSpec((TILE, MIX_G), row), pl.BlockSpec((TILE, MIX_G), row), pl.BlockSpec((TILE, MIX_G), row),
        pl.BlockSpec((None, 1, MIX_G), lay),
        pl.BlockSpec((MIX_G, MIX_G), lambda t: (0, 0)),
        pl.BlockSpec((None, D_MODEL, D_MODEL), lay),
        pl.BlockSpec((None, D_MODEL, 128), lay),
        pl.BlockSpec((None, 1, 128), lay),
    ]
    return pl.pallas_call(
        _outproj_kernel,
        grid=(ntiles,),
        in_specs=in_specs,
        out_specs=[pl.BlockSpec((TILE, D_MODEL), row), pl.BlockSpec((TILE, D_MODEL), row),
                   pl.BlockSpec((TILE, 128), row)],
        out_shape=[jax.ShapeDtypeStruct(x.shape, F32), jax.ShapeDtypeStruct(x.shape, BF16),
                   jax.ShapeDtypeStruct((x.shape[0], 128), F32)],
        compiler_params=_cparams(1),
        name="out_projection_lat" if latent else "out_projection_ctx",
    )(x, mod, norm2, ret_of, ret_ob, ret_sg, da_o, hg_of, hg_ob, hg_sg, hgn_t, bd256, w_out_bf, w_route, b_route)


MOE_TILE = 1024


def _moe_kernel(h_ref, comb_ref, wg_ref, wu_ref, wd_ref, x1_ref, mod_ref, out_ref, acc_ref):
    e = pl.program_id(1)

    @pl.when(e == 0)
    def _():
        acc_ref[...] = jnp.zeros_like(acc_ref)

    h = h_ref[...]
    a = _silu(_dot(h, wg_ref[...])) * _dot(h, wu_ref[...])
    y = _dot(a.astype(BF16), wd_ref[...])
    comb = comb_ref[...]
    lane = lax.broadcasted_iota(jnp.int32, comb.shape, 1)
    w_e = jnp.sum(jnp.where(lane == e, comb, 0.0), axis=-1, keepdims=True)
    acc_ref[...] += w_e * y

    @pl.when(e == N_EXPERTS - 1)
    def _():
        out_ref[...] = x1_ref[...] + mod_ref[5:6, :] * acc_ref[...]


def _moe(latent, layer, h2, comb, x1, mod, w_gate_bf, w_up_bf, w_down_bf):
    ntiles = h2.shape[0] // MOE_TILE
    per_seq = DEC_SEQ // MOE_TILE
    cond = (lambda i: 1 + i // per_seq) if latent else (lambda i: 0)
    row = lambda i, e: (i, 0)
    return pl.pallas_call(
        _moe_kernel,
        grid=(ntiles, N_EXPERTS),
        in_specs=[
            pl.BlockSpec((MOE_TILE, D_MODEL), row),
            pl.BlockSpec((MOE_TILE, 128), row),
            pl.BlockSpec((None, None, D_MODEL, EXPERT_FF), lambda i, e: (layer, e, 0, 0)),
            pl.BlockSpec((None, None, D_MODEL, EXPERT_FF), lambda i, e: (layer, e, 0, 0)),
            pl.BlockSpec((None, None, EXPERT_FF, D_MODEL), lambda i, e: (layer, e, 0, 0)),
            pl.BlockSpec((MOE_TILE, D_MODEL), row),
            pl.BlockSpec((None, None, 6, D_MODEL), lambda i, e: (layer, cond(i), 0, 0)),
        ],
        out_specs=pl.BlockSpec((MOE_TILE, D_MODEL), row),
        out_shape=jax.ShapeDtypeStruct(x1.shape, F32),
        scratch_shapes=[pltpu.VMEM((MOE_TILE, D_MODEL), F32)],
        compiler_params=_cparams(2),
        name="moe_lat" if latent else "moe_ctx",
    )(h2, comb, w_gate_bf, w_up_bf, w_down_bf, x1, mod)


def _block_diag_ones(n, group):
    i = np.arange(n) // group
    return jnp.asarray((i[:, None] == i[None, :]).astype(np.float32), dtype=BF16)


def _rope_tables():
    pos = np.arange(DEC_SEQ)
    rows = (pos // GRID_W).astype(np.float32)
    cols = (pos % GRID_W).astype(np.float32)
    inv_freq = (ROPE_BASE ** (-(np.arange(ROPE_PAIRS, dtype=np.float32) / ROPE_PAIRS))).astype(np.float32)
    ang_r = rows[:, None] * inv_freq[None, :]
    ang_c = cols[:, None] * inv_freq[None, :]
    cos64 = np.concatenate([np.cos(ang_r), np.cos(ang_r), np.cos(ang_c), np.cos(ang_c)], axis=1)
    sin64 = np.concatenate([-np.sin(ang_r), np.sin(ang_r), -np.sin(ang_c), np.sin(ang_c)], axis=1)
    return (jnp.asarray(np.tile(cos64, (1, 2)), F32), jnp.asarray(np.tile(sin64, (1, 2)), F32))


def _state_to_blockdiag_t(s):
    eye = jnp.eye(HEADS, dtype=F32)
    return jnp.einsum('bzhdv,hg->bzhvgd', s.astype(F32), eye).reshape(s.shape[0], 2, MIX_G, MIX_G)


def _blockdiag_t_to_state(st):
    s6 = st.reshape(st.shape[0], 2, HEADS, HEAD_D, HEADS, HEAD_D)
    diag = jnp.stack([s6[:, :, h, :, h, :] for h in range(HEADS)], axis=2)
    return jnp.swapaxes(diag, -1, -2)


def _retention_log_decay_tiles():
    lg = np.log(1.0 - 2.0 ** (-5.0 - np.arange(HEADS, dtype=np.float32))).astype(np.float32)
    fwd = np.broadcast_to(np.repeat(lg, HEAD_D)[None, :], (TILE, MIX_G))
    bwd = np.broadcast_to(np.repeat(lg[::-1], HEAD_D)[None, :], (TILE, MIX_G))
    return jnp.asarray(fwd, F32), jnp.asarray(bwd, F32)


def _trunk_layer(latent, layer, x, consts, params, ret_s0, hg_s0, cache_k, cache_v):
    (mod, lb_all, bd512, bd256, rope_cos, rope_sin, ret_gf, ret_gb) = consts
    n_seq = DEC_BATCH if latent else BATCH
    seq_len = DEC_SEQ if latent else SEQ
    nt = seq_len // TILE
    proj = _in_projection(latent, layer, x, mod, params['norm1'], params['w_in'], params['da_qn'],
                          params['da_kn'], lb_all, bd512, rope_cos, rope_sin)
    (rq, rk, rv, rsg, dq, dk, dv, hq, hkf, hkb, hv, hgf, hgb, hsg) = proj[:14]
    ret_of, ret_ob, ret_fin = _gated_scan(n_seq, nt, 0, rq, rk, rk, rv, ret_gf, ret_gb, True, ret_s0,
                                          "retention_scan_lat" if latent else "retention_scan_ctx")
    hg_of, hg_ob, hg_fin = _gated_scan(n_seq, nt, 0, hq, hkf, hkb, hv, hgf, hgb, False, hg_s0,
                                       "hgrn_scan_lat" if latent else "hgrn_scan_ctx")
    da_o = _diff_attention(layer, n_seq, seq_len, dq, dk, dv, cache_k, cache_v,
                           params['da_lambda'], params['da_subln'])
    x1, h2, comb = _out_projection(latent, layer, x, mod, params['norm2'], ret_of, ret_ob, rsg, da_o,
                                   hg_of, hg_ob, hsg, params['hg_norm'], bd256, params['w_out'],
                                   params['w_route'], params['b_route'])
    x2 = _moe(latent, layer, h2, comb, x1, mod, params['w_gate'], params['w_up'], params['w_down'])
    extras = None if latent else (proj[14], proj[15], ret_fin, hg_fin)
    return x2, extras


def kernel(x_prompt, x_sample, cache_k, cache_v, state_ret, state_hgrn, c, c_ctx, norm1, norm2, w_mod, b_mod,
           w_in, w_out, da_qn, da_kn, da_lambda, da_subln, hg_lb, hg_norm, w_group, b_group, w_router,
           b_router, w_gate, w_up, w_down):
    cond = jnp.zeros((N_COND, D_MODEL), F32).at[0].set(c_ctx).at[1:1 + DEC_BATCH].set(c)
    mod, lb_all = _modulation(cond, w_mod, b_mod, hg_lb)
    mod = mod.reshape(DEPTH, N_COND, 6, D_MODEL)
    rope_cos, rope_sin = _rope_tables()
    ret_gf, ret_gb = _retention_log_decay_tiles()
    consts = (mod, lb_all.reshape(DEPTH, 1, MIX_G), _block_diag_ones(DA_W, HEAD_D),
              _block_diag_ones(MIX_G, HEAD_D), rope_cos, rope_sin, ret_gf, ret_gb)
    pad = jnp.zeros((DEPTH, D_MODEL, 128 - N_EXPERTS - N_GROUPS), F32)
    params = {
        'norm1': norm1.reshape(DEPTH, 1, D_MODEL), 'norm2': norm2.reshape(DEPTH, 1, D_MODEL),
        'w_in': w_in.astype(BF16), 'w_out': w_out.astype(BF16),
        'da_qn': jnp.tile(da_qn, (1, DA_W // HEAD_D)).reshape(DEPTH, 1, DA_W),
        'da_kn': jnp.tile(da_kn, (1, DA_W // HEAD_D)).reshape(DEPTH, 1, DA_W),
        'da_lambda': da_lambda, 'da_subln': da_subln.reshape(DEPTH, 1, 128),
        'hg_norm': jnp.tile(hg_norm, (1, HEADS)).reshape(DEPTH, 1, MIX_G),
        'w_route': jnp.concatenate([w_router, w_group, pad], axis=-1),
        'b_route': jnp.concatenate([b_router, b_group, pad[:, 0, :]], axis=-1).reshape(DEPTH, 1, 128),
        'w_gate': w_gate.astype(BF16), 'w_up': w_up.astype(BF16), 'w_down': w_down.astype(BF16),
    }
    ck_bf = cache_k.astype(BF16).reshape(DEC_BATCH, DEPTH, PAST_LEN, DA_W)
    cv_bf = cache_v.astype(BF16).reshape(DEC_BATCH, DEPTH, PAST_LEN, DA_W)
    zero_state = jnp.zeros((BATCH, 2, MIX_G, MIX_G), F32)

    yp = x_prompt.reshape(N_CTX_ROWS, D_MODEL)
    ys = x_sample.reshape(N_LAT_ROWS, D_MODEL)
    ks_out, vs_out, rets_out, hgs_out = [], [], [], []
    for l in range(DEPTH):
        yp, (k_new, v_new, ret_fin, hg_fin) = _trunk_layer(False, l, yp, consts, params, zero_state, zero_state,
                                                           None, None)
        ks_out.append(k_new.reshape(BATCH, SEQ, HEADS, 2, HEAD_D))
        vs_out.append(v_new.reshape(BATCH, SEQ, HEADS, 2 * HEAD_D))
        rets_out.append(_blockdiag_t_to_state(ret_fin))
        hgs_out.append(_blockdiag_t_to_state(hg_fin))
        ys, _ = _trunk_layer(True, l, ys, consts, params, _state_to_blockdiag_t(state_ret[:, l]),
                             _state_to_blockdiag_t(state_hgrn[:, l]), ck_bf, cv_bf)
    return (yp.reshape(BATCH, SEQ, D_MODEL), ys.reshape(DEC_BATCH, DEC_SEQ, D_MODEL),
            jnp.stack(ks_out, axis=1), jnp.stack(vs_out, axis=1),
            jnp.stack(rets_out, axis=1), jnp.stack(hgs_out, axis=1))
```

```python
import functools
import math

import numpy as np
import jax
import jax.numpy as jnp
from jax import lax
from jax.experimental import pallas as pl
from jax.experimental.pallas import tpu as pltpu

F32 = jnp.float32
BF16 = jnp.bfloat16

D_MODEL = 1024
BATCH = 16
SEQ = 256
DEPTH = 4
DEC_BATCH = 4
DEC_SEQ = 4096
PAST_LEN = 256
GRID_W = 64
HEADS = 4
HEAD_D = 64
MIX_G = HEADS * HEAD_D
DA_W = 512
IN_COLS = 3840
ROPE_BASE = 10000.0
ROPE_PAIRS = 16
RMS_EPS = 1e-6
N_GROUPS = 4
EPG = 4
N_EXPERTS = 16
EXPERT_FF = 512

TILE = 256
N_CTX_ROWS = BATCH * SEQ
N_LAT_ROWS = DEC_BATCH * DEC_SEQ
N_ROWS = N_CTX_ROWS + N_LAT_ROWS
CTX_TILES = N_CTX_ROWS // TILE
LAT_TILES_PER_SEQ = DEC_SEQ // TILE
N_TILES = N_ROWS // TILE
N_COND = 8

CHUNK = 64
SUB = 16
NSUB = CHUNK // SUB
NCHUNK = TILE // CHUNK

VMEM_LIMIT = 48 * 1024 * 1024
NEG_BIG = -1e30
LOG2_E = 1.4426950408889634

HI = lax.Precision.HIGHEST


def _cparams(n_axes):
    return pltpu.CompilerParams(dimension_semantics=("arbitrary",) * n_axes,
                                vmem_limit_bytes=VMEM_LIMIT)


def _dot(a, b):
    return jnp.dot(a, b, preferred_element_type=F32)


def _dot_nt(a, b):
    return lax.dot_general(a, b, (((1,), (1,)), ((), ())), preferred_element_type=F32)


def _dot_tn(a, b):
    return lax.dot_general(a, b, (((0,), (0,)), ((), ())), preferred_element_type=F32)


def _split_dot(x, m_bf16):
    hi = x.astype(BF16)
    lo = (x - hi.astype(F32)).astype(BF16)
    return _dot(hi, m_bf16) + _dot(lo, m_bf16)


def _silu(x):
    return x * (1.0 / (1.0 + jnp.exp(-x)))


def _sigmoid(x):
    return 1.0 / (1.0 + jnp.exp(-x))


def _mod_kernel(cond_ref, w_ref, b_ref, lb_ref, mod_ref, lbo_ref):
    c = cond_ref[...]
    m = jnp.dot(_silu(c), w_ref[...], preferred_element_type=F32, precision=HI)
    mod_ref[...] = m + b_ref[...]
    z = lb_ref[...]
    zmax = jnp.max(z, axis=0, keepdims=True)
    e = jnp.exp(z - zmax)
    p = e / jnp.sum(e, axis=0, keepdims=True)
    rows = [jnp.zeros_like(p[0:1])]
    for l in range(1, DEPTH):
        rows.append(rows[-1] + p[l:l + 1])
    lbo_ref[...] = jnp.concatenate(rows, axis=0)


def _modulation(cond, w_mod, b_mod, hg_lb):
    nblk = 6
    return pl.pallas_call(
        _mod_kernel,
        grid=(DEPTH, nblk),
        in_specs=[
            pl.BlockSpec((N_COND, D_MODEL), lambda l, j: (0, 0)),
            pl.BlockSpec((None, D_MODEL, D_MODEL), lambda l, j: (l, 0, j)),
            pl.BlockSpec((None, 1, D_MODEL), lambda l, j: (l, 0, j)),
            pl.BlockSpec((DEPTH, MIX_G), lambda l, j: (0, 0)),
        ],
        out_specs=[
            pl.BlockSpec((None, N_COND, D_MODEL), lambda l, j: (l, 0, j)),
            pl.BlockSpec((DEPTH, MIX_G), lambda l, j: (0, 0)),
        ],
        out_shape=[jax.ShapeDtypeStruct((DEPTH, N_COND, 6 * D_MODEL), F32),
                   jax.ShapeDtypeStruct((DEPTH, MIX_G), F32)],
        compiler_params=_cparams(2),
        name="modulation",
    )(cond, w_mod, b_mod.reshape(DEPTH, 1, 6 * D_MODEL), hg_lb)


def _group_mean_sq(x, bd_ref, group):
    return _split_dot(x * x, bd_ref[...]) * (1.0 / group)


def _swap16(x):
    w = x.shape[-1]
    lane = lax.broadcasted_iota(jnp.int32, x.shape, x.ndim - 1)
    up = pltpu.roll(x, w - 16, x.ndim - 1)
    dn = pltpu.roll(x, 16, x.ndim - 1)
    return jnp.where((lane % 32) < 16, up, dn)


def _inproj_kernel(latent, x_ref, mod_ref, n1_ref, w_ref, qn_ref, kn_ref, lb_ref, bd_ref, *rest):
    if latent:
        cos_ref, sin_ref = rest[0], rest[1]
        outs = rest[2:]
    else:
        outs = rest
    (rq_ref, rk_ref, rv_ref, rsg_ref, dq_ref, dk_ref, dv_ref,
     hq_ref, hkf_ref, hkb_ref, hv_ref, hgf_ref, hgb_ref, hsg_ref) = outs[:14]

    x = x_ref[...]
    ms = jnp.mean(x * x, axis=-1, keepdims=True)
    shift1 = mod_ref[0:1, :]
    scale1 = mod_ref[1:2, :]
    h = x * lax.rsqrt(ms + RMS_EPS) * n1_ref[...] * (1.0 + scale1) + shift1
    y = _dot(h.astype(BF16), w_ref[...])

    rq_ref[...] = y[:, 0:256].astype(BF16)
    rk_ref[...] = (y[:, 256:512] * (HEAD_D ** -0.5)).astype(BF16)
    rv_ref[...] = y[:, 512:768].astype(BF16)
    rsg_ref[...] = _silu(y[:, 768:1024]).astype(BF16)

    dq = y[:, 1024:1536]
    dk = y[:, 1536:2048]
    qn = dq * lax.rsqrt(_group_mean_sq(dq, bd_ref, HEAD_D) + RMS_EPS) * qn_ref[...]
    kn = dk * lax.rsqrt(_group_mean_sq(dk, bd_ref, HEAD_D) + RMS_EPS) * kn_ref[...]
    if latent:
        cos = jnp.concatenate([cos_ref[...]] * 4, axis=1)
        sin = jnp.concatenate([sin_ref[...]] * 4, axis=1)
        qr = qn * cos + _swap16(qn) * sin
        kr = kn * cos + _swap16(kn) * sin
    else:
        qr, kr = qn, kn
        ck_ref, cv_ref = outs[14], outs[15]
        ck_ref[...] = kn
        cv_ref[...] = y[:, 2048:2560]
    dq_ref[...] = (qr * (HEAD_D ** -0.5 * LOG2_E)).astype(BF16)
    dk_ref[...] = kr.astype(BF16)
    dv_ref[...] = y[:, 2048:2560].astype(BF16)

    lb = lb_ref[...]
    f_f = lb + (1.0 - lb) * _sigmoid(y[:, 2816:3072])
    f_b = lb + (1.0 - lb) * _sigmoid(y[:, 3072:3328])
    hq_ref[...] = y[:, 2560:2816].astype(BF16)
    hkf_ref[...] = (1.0 - f_f).astype(BF16)
    hkb_ref[...] = (1.0 - f_b).astype(BF16)
    hgf_ref[...] = jnp.log(f_f)
    hgb_ref[...] = jnp.log(f_b)
    hv_ref[...] = y[:, 3328:3584].astype(BF16)
    hsg_ref[...] = _silu(y[:, 3584:3840]).astype(BF16)


def _in_projection(latent, layer, x, mod, norm1, w_in_bf, qn_t, kn_t, lb_all, bd512, rope_cos, rope_sin):
    nrows = x.shape[0]
    ntiles = nrows // TILE
    row = lambda t: (t, 0)
    cond = (lambda t: 1 + t // LAT_TILES_PER_SEQ) if latent else (lambda t: 0)

    in_specs = [
        pl.BlockSpec((TILE, D_MODEL), row),
        pl.BlockSpec((None, None, 6, D_MODEL), lambda t: (layer, cond(t), 0, 0)),
        pl.BlockSpec((None, 1, D_MODEL), lambda t: (layer, 0, 0)),
        pl.BlockSpec((None, D_MODEL, IN_COLS), lambda t: (layer, 0, 0)),
        pl.BlockSpec((None, 1, DA_W), lambda t: (layer, 0, 0)),
        pl.BlockSpec((None, 1, DA_W), lambda t: (layer, 0, 0)),
        pl.BlockSpec((None, 1, MIX_G), lambda t: (layer, 0, 0)),
        pl.BlockSpec((DA_W, DA_W), lambda t: (0, 0)),
    ]
    args = [x, mod, norm1, w_in_bf, qn_t, kn_t, lb_all, bd512]
    if latent:
        in_specs += [pl.BlockSpec((TILE, 128), lambda t: (t % LAT_TILES_PER_SEQ, 0))] * 2
        args += [rope_cos, rope_sin]

    def o(width, dtype):
        return jax.ShapeDtypeStruct((nrows, width), dtype), pl.BlockSpec((TILE, width), lambda t: (t, 0))

    outs = [o(256, BF16), o(256, BF16), o(256, BF16), o(256, BF16),
            o(512, BF16), o(512, BF16), o(512, BF16),
            o(256, BF16), o(256, BF16), o(256, BF16), o(256, BF16), o(256, F32), o(256, F32), o(256, BF16)]
    if not latent:
        outs += [o(512, F32), o(512, F32)]
    return pl.pallas_call(
        functools.partial(_inproj_kernel, latent),
        grid=(ntiles,),
        in_specs=in_specs,
        out_specs=[s for _, s in outs],
        out_shape=[s for s, _ in outs],
        compiler_params=_cparams(1),
        name="in_projection_lat" if latent else "in_projection_ctx",
    )(*args)


def _head_masks():
    lane = lax.broadcasted_iota(jnp.int32, (1, MIX_G), 1)
    return [(lane // HEAD_D == h).astype(F32) for h in range(HEADS)]


def _chunk_step(q, k, v, g, st, rev, consts):
    tri, bd_mask, ones_bd, gsel, hmask = consts
    b = jnp.dot(tri, g, preferred_element_type=F32, precision=HI)

    zero_row = jnp.zeros((1, MIX_G), F32)
    r_rows, e_rows = [], []
    for s in range(NSUB):
        lo, hi = s * SUB, s * SUB + SUB - 1
        if not rev:
            r_rows.append(b[lo - 1:lo, :] if s > 0 else zero_row)
            e_rows.append(b[hi:hi + 1, :])
        else:
            r_rows.append(b[hi + 1:hi + 2, :] if s < NSUB - 1 else zero_row)
            e_rows.append(b[lo:lo + 1, :])
    btot = e_rows[NSUB - 1] if not rev else e_rows[0]
    r_full = jnp.concatenate([jnp.broadcast_to(r, (SUB, MIX_G)) for r in r_rows], axis=0)
    e_full = jnp.concatenate([jnp.broadcast_to(e, (SUB, MIX_G)) for e in e_rows], axis=0)

    bl = b - r_full
    qp = q * jnp.exp(bl)
    kend = k * jnp.exp(e_full - b)
    q_inter = qp * jnp.exp(r_full)
    k_state = kend * jnp.exp(btot - e_full)
    v16 = v

    o = _dot_nt(q_inter.astype(BF16), st.astype(BF16))
    ut = _dot_tn(v16, k_state.astype(BF16))
    st_new = st * jnp.exp(btot) + ut * bd_mask

    pairs = [(i, j) for i in range(NSUB) for j in range(NSUB) if (j < i if not rev else j > i)]
    lhs = [qp[i * SUB:(i + 1) * SUB, :] * jnp.exp(r_rows[i] - e_rows[j]) for (i, j) in pairs]
    lh = jnp.concatenate(lhs, axis=0)
    lhs4 = jnp.concatenate([lh * hmask[h] for h in range(HEADS)], axis=0).astype(BF16)
    sc = _dot_nt(lhs4, kend.astype(BF16))
    col = lax.broadcasted_iota(jnp.int32, (SUB, CHUNK), 1) // SUB
    npairs = len(pairs)
    p_rows = []
    for h in range(HEADS):
        for i in range(NSUB):
            acc = None
            for p, (pi, pj) in enumerate(pairs):
                if pi != i:
                    continue
                base = (h * npairs + p) * SUB
                blk = jnp.where(col == pj, sc[base:base + SUB, :], 0.0)
                acc = blk if acc is None else acc + blk
            if acc is None:
                acc = jnp.zeros((SUB, CHUNK), F32)
            p_rows.append(acc)
    ps = jnp.concatenate(p_rows, axis=0).astype(BF16)
    full = _dot(ps, v16)
    for h in range(HEADS):
        o = o + full[h * CHUNK:(h + 1) * CHUNK, :] * hmask[h]

    jrow = lax.broadcasted_iota(jnp.int32, (SUB, MIX_G), 0)
    pp, vt = [], []
    for s in range(NSUB):
        sl = slice(s * SUB, (s + 1) * SUB)
        bl_s, q_s, k_s = bl[sl, :], q[sl, :], k[sl, :]
        for i in range(SUB):
            d = bl_s[i:i + 1, :] - bl_s
            keep = (jrow <= i) if not rev else (jrow >= i)
            e = jnp.exp(jnp.where(keep, d, NEG_BIG))
            pp.append((e * q_s[i:i + 1, :] * k_s).astype(BF16))
        vt.extend([v16[sl, :]] * SUB)
    pp = jnp.concatenate(pp, axis=0)
    sb = _dot(pp, ones_bd)
    w = (sb * jnp.concatenate(vt, axis=0).astype(F32)).astype(BF16)
    o = o + _dot(gsel, w)
    return o, st_new


def _scan_consts(rev):
    r = lax.broadcasted_iota(jnp.int32, (CHUNK, CHUNK), 0)
    c = lax.broadcasted_iota(jnp.int32, (CHUNK, CHUNK), 1)
    tri = ((c <= r) if not rev else (c >= r)).astype(F32)
    rr = lax.broadcasted_iota(jnp.int32, (MIX_G, MIX_G), 0) // HEAD_D
    cc = lax.broadcasted_iota(jnp.int32, (MIX_G, MIX_G), 1) // HEAD_D
    bd_mask = (rr == cc).astype(F32)
    ones_bd = bd_mask.astype(BF16)
    gi = lax.broadcasted_iota(jnp.int32, (CHUNK, CHUNK * SUB), 0)
    gj = lax.broadcasted_iota(jnp.int32, (CHUNK, CHUNK * SUB), 1) // SUB
    gsel = (gi == gj).astype(BF16)
    return tri, bd_mask, ones_bd, gsel, _head_masks()


def _scan_kernel(nt, qf_ref, kf_ref, vf_ref, gf_ref, qb_ref, kb_ref, vb_ref, gb_ref, s0_ref,
                 of_ref, ob_ref, sfin_ref, st_scr):
    t = pl.program_id(1)

    @pl.when(t == 0)
    def _():
        st_scr[...] = s0_ref[...]

    cf = _scan_consts(False)
    cb = _scan_consts(True)

    def body(c, carry):
        rf = pl.multiple_of(c * CHUNK, CHUNK)
        rb = pl.multiple_of((NCHUNK - 1 - c) * CHUNK, CHUNK)
        o_f, st_f = _chunk_step(qf_ref[pl.ds(rf, CHUNK), :].astype(F32),
                                kf_ref[pl.ds(rf, CHUNK), :].astype(F32),
                                vf_ref[pl.ds(rf, CHUNK), :],
                                gf_ref[pl.ds(rf, CHUNK), :], st_scr[0], False, cf)
        of_ref[pl.ds(rf, CHUNK), :] = o_f
        st_scr[0] = st_f
        o_b, st_b = _chunk_step(qb_ref[pl.ds(rb, CHUNK), :].astype(F32),
                                kb_ref[pl.ds(rb, CHUNK), :].astype(F32),
                                vb_ref[pl.ds(rb, CHUNK), :],
                                gb_ref[pl.ds(rb, CHUNK), :], st_scr[1], True, cb)
        ob_ref[pl.ds(rb, CHUNK), :] = o_b
        st_scr[1] = st_b
        return carry

    lax.fori_loop(0, NCHUNK, body, 0)

    @pl.when(t == nt - 1)
    def _():
        sfin_ref[...] = st_scr[...]


def _gated_scan(n_seq, nt, q, k_f, k_b, v, g_f, g_b, s0, name):
    fwd = lambda s, t: (s * nt + t, 0)
    bwd = lambda s, t: (s * nt + (nt - 1 - t), 0)
    blk = lambda im: pl.BlockSpec((TILE, MIX_G), im)
    in_specs = [blk(fwd), blk(fwd), blk(fwd), blk(fwd), blk(bwd), blk(bwd), blk(bwd), blk(bwd),
                pl.BlockSpec((None, 2, MIX_G, MIX_G), lambda s, t: (s, 0, 0, 0))]
    nrows = n_seq * nt * TILE
    return pl.pallas_call(
        functools.partial(_scan_kernel, nt),
        grid=(n_seq, nt),
        in_specs=in_specs,
        out_specs=[pl.BlockSpec((TILE, MIX_G), lambda s, t: (s * nt + t, 0)),
                   pl.BlockSpec((TILE, MIX_G), lambda s, t: (s * nt + (nt - 1 - t), 0)),
                   pl.BlockSpec((None, 2, MIX_G, MIX_G), lambda s, t: (s, 0, 0, 0))],
        out_shape=[jax.ShapeDtypeStruct((nrows, MIX_G), F32),
                   jax.ShapeDtypeStruct((nrows, MIX_G), F32),
                   jax.ShapeDtypeStruct((n_seq, 2, MIX_G, MIX_G), F32)],
        scratch_shapes=[pltpu.VMEM((2, MIX_G, MIX_G), F32)],
        compiler_params=_cparams(2),
        name=name,
    )(q, k_f, v, g_f, q, k_b, v, g_b, s0)


def _retention_tables():
    gam = 1.0 - 2.0 ** (-5.0 - np.arange(HEADS, dtype=np.float64))
    gam_r = gam[::-1]
    i = np.arange(TILE, dtype=np.float64)
    diff = i[:, None] - i[None, :]
    dcomb = np.zeros((HEADS, TILE, TILE))
    for h in range(HEADS):
        lower = np.where(diff > 0, gam[h] ** np.maximum(diff, 0), 0.0)
        upper = np.where(diff < 0, gam_r[h] ** np.maximum(-diff, 0), 0.0)
        dcomb[h] = lower + upper + 2.0 * (diff == 0)
    lanes = lambda per_head: np.repeat(per_head, HEAD_D, axis=-1)
    qd_f = lanes(gam[None, :] ** (i[:, None] + 1.0))
    kd_f = lanes(gam[None, :] ** (TILE - 1.0 - i[:, None]))
    qd_b = lanes(gam_r[None, :] ** (TILE - i[:, None]))
    kd_b = lanes(gam_r[None, :] ** i[:, None])
    blk = (np.arange(MIX_G)[:, None] // HEAD_D) == (np.arange(MIX_G)[None, :] // HEAD_D)
    c_f = np.where(blk, lanes(gam ** TILE)[None, :], 0.0) * np.ones((MIX_G, 1))
    c_b = np.where(blk, lanes(gam_r ** TILE)[None, :], 0.0) * np.ones((MIX_G, 1))
    f = lambda a: jnp.asarray(a, F32)
    return (f(dcomb.reshape(HEADS * TILE, TILE)), f(qd_f), f(kd_f), f(qd_b), f(kd_b), f(c_f), f(c_b))


def _ret_kernel(nt, qf_ref, kf_ref, vf_ref, qb_ref, kb_ref, vb_ref, s0_ref, dcomb_ref, qdf_ref, kdf_ref,
                qdb_ref, kdb_ref, cf_ref, cb_ref, of_ref, ob_ref, sfin_ref, st_scr):
    t = pl.program_id(1)

    @pl.when(t == 0)
    def _():
        st_scr[...] = s0_ref[...]

    lane_head = lax.broadcasted_iota(jnp.int32, (TILE, MIX_G), 1) // HEAD_D
    rr = lax.broadcasted_iota(jnp.int32, (MIX_G, MIX_G), 0) // HEAD_D
    cc = lax.broadcasted_iota(jnp.int32, (MIX_G, MIX_G), 1) // HEAD_D
    same_head = rr == cc

    q = qf_ref[...]
    k = kf_ref[...]
    v = vf_ref[...]
    qs = jnp.concatenate([jnp.where(lane_head == h, q, jnp.zeros_like(q)) for h in range(HEADS)], axis=0)
    p = (_dot_nt(qs, k) * dcomb_ref[...]).astype(BF16)
    full = _dot(p, v)
    o = _dot((q.astype(F32) * qdf_ref[...]).astype(BF16), st_scr[0].astype(BF16))
    for h in range(HEADS):
        o = o + jnp.where(lane_head == h, full[h * TILE:(h + 1) * TILE, :], 0.0)
    of_ref[...] = o
    u = _dot_tn((k.astype(F32) * kdf_ref[...]).astype(BF16), v)
    st_scr[0] = st_scr[0] * cf_ref[...] + jnp.where(same_head, u, 0.0)

    qb = qb_ref[...]
    kb = kb_ref[...]
    vb = vb_ref[...]
    ob_ref[...] = _dot((qb.astype(F32) * qdb_ref[...]).astype(BF16), st_scr[1].astype(BF16))
    ub = _dot_tn((kb.astype(F32) * kdb_ref[...]).astype(BF16), vb)
    st_scr[1] = st_scr[1] * cb_ref[...] + jnp.where(same_head, ub, 0.0)

    @pl.when(t == nt - 1)
    def _():
        sfin_ref[...] = st_scr[...]


def _retention(n_seq, nt, q, k, v, s0, tables, name):
    fwd = lambda s, t: (s * nt + t, 0)
    bwd = lambda s, t: (s * nt + (nt - 1 - t), 0)
    const = lambda s, t: (0, 0)
    blk = lambda im: pl.BlockSpec((TILE, MIX_G), im)
    sq = lambda: pl.BlockSpec((MIX_G, MIX_G), const)
    in_specs = [blk(fwd), blk(fwd), blk(fwd), blk(bwd), blk(bwd), blk(bwd),
                pl.BlockSpec((None, 2, MIX_G, MIX_G), lambda s, t: (s, 0, 0, 0)),
                pl.BlockSpec((HEADS * TILE, TILE), const), sq(), sq(), sq(), sq(), sq(), sq()]
    nrows = n_seq * nt * TILE
    return pl.pallas_call(
        functools.partial(_ret_kernel, nt),
        grid=(n_seq, nt),
        in_specs=in_specs,
        out_specs=[blk(fwd), blk(bwd), pl.BlockSpec((None, 2, MIX_G, MIX_G), lambda s, t: (s, 0, 0, 0))],
        out_shape=[jax.ShapeDtypeStruct((nrows, MIX_G), F32),
                   jax.ShapeDtypeStruct((nrows, MIX_G), F32),
                   jax.ShapeDtypeStruct((n_seq, 2, MIX_G, MIX_G), F32)],
        scratch_shapes=[pltpu.VMEM((2, MIX_G, MIX_G), F32)],
        compiler_params=_cparams(2),
        name=name,
    )(q, k, v, q, k, v, s0, *tables)


def _attn_step(lam_init, has_cache, refs, s_write, s_read):
    if has_cache:
        q_ref, k_ref, ck_ref, v_ref, cv_ref, lam_ref, sub_ref, o_ref = refs
        sw, sw2, mw = s_write
        sr, sr2, mr = s_read
    else:
        q_ref, k_ref, v_ref, lam_ref, sub_ref, o_ref = refs
        sw, mw = s_write
        sr, mr = s_read

    lp = lam_ref[...]
    lam = (jnp.exp(jnp.sum(lp[0:1] * lp[1:2], axis=-1, keepdims=True))
           - jnp.exp(jnp.sum(lp[2:3] * lp[3:4], axis=-1, keepdims=True)) + lam_init)
    v_ext = jnp.concatenate([v_ref[...], jnp.ones(v_ref.shape, BF16)], axis=1)
    if has_cache:
        cv_ext = jnp.concatenate([cv_ref[...], jnp.ones(cv_ref.shape, BF16)], axis=1)
    parts = []
    for c in range(2):
        m = mr[c]
        oe = _dot(jnp.exp2((sr[c] - m).astype(BF16)), v_ext)
        if has_cache:
            oe = oe + _dot(jnp.exp2((sr2[c] - m).astype(BF16)), cv_ext)
        parts.append(oe[:, :128] / oe[:, 128:129])
    o = parts[0] - lam * parts[1]
    ms = jnp.mean(o * o, axis=-1, keepdims=True)
    o_ref[...] = (o * lax.rsqrt(ms + RMS_EPS) * sub_ref[...] * (1.0 - lam_init)).astype(BF16)

    q = q_ref[...]
    k = k_ref[...]
    for c in range(2):
        qc = q[:, c * HEAD_D:(c + 1) * HEAD_D]
        s = _dot_nt(qc, k[:, c * HEAD_D:(c + 1) * HEAD_D])
        m = jnp.max(s, axis=-1, keepdims=True)
        sw[c] = s
        if has_cache:
            s2 = _dot_nt(qc, ck_ref[:, c * HEAD_D:(c + 1) * HEAD_D])
            m = jnp.maximum(m, jnp.max(s2, axis=-1, keepdims=True))
            sw2[c] = s2
        mw[c] = m


def _attn_kernel(lam_init, has_cache, *refs):
    n_in = 8 if has_cache else 6
    io, scr = refs[:n_in], refs[n_in:]
    half = len(scr) // 2
    buf_a, buf_b = scr[:half], scr[half:]
    u = pl.program_id(0)

    @pl.when(u == 0)
    def _():
        for r in buf_b:
            r[...] = jnp.zeros(r.shape, r.dtype)

    @pl.when(u % 2 == 0)
    def _():
        _attn_step(lam_init, has_cache, io, buf_a, buf_b)

    @pl.when(u % 2 == 1)
    def _():
        _attn_step(lam_init, has_cache, io, buf_b, buf_a)


def _diff_attention(layer, n_seq, seq_len, q, k, v, cache_k, cache_v, da_lambda, subln):
    tq = TILE
    nq = seq_len // tq
    n_units = n_seq * HEADS * nq
    has_cache = cache_k is not None
    lam_init = 0.8 - 0.6 * math.exp(-0.3 * layer)

    cur = lambda u: jnp.minimum(u, n_units - 1)
    prev = lambda u: jnp.maximum(u - 1, 0)
    seq_of = lambda w: w // (HEADS * nq)
    head_of = lambda w: (w // nq) % HEADS
    rows_of = lambda w: seq_of(w) * nq + w % nq

    q_spec = pl.BlockSpec((tq, 128), lambda u: (rows_of(cur(u)), head_of(cur(u))))
    k_spec = pl.BlockSpec((seq_len, 128), lambda u: (seq_of(cur(u)), head_of(cur(u))))
    v_spec = pl.BlockSpec((seq_len, 128), lambda u: (seq_of(prev(u)), head_of(prev(u))))
    ck_spec = pl.BlockSpec((None, None, PAST_LEN, 128), lambda u: (seq_of(cur(u)), layer, 0, head_of(cur(u))))
    cv_spec = pl.BlockSpec((None, None, PAST_LEN, 128), lambda u: (seq_of(prev(u)), layer, 0, head_of(prev(u))))
    tail = [pl.BlockSpec((None, 4, HEAD_D), lambda u: (layer, 0, 0)),
            pl.BlockSpec((None, 1, 128), lambda u: (layer, 0, 0))]
    if has_cache:
        in_specs = [q_spec, k_spec, ck_spec, v_spec, cv_spec] + tail
        args = [q, k, cache_k, v, cache_v, da_lambda, subln]
        scratch = [pltpu.VMEM((2, tq, seq_len), F32), pltpu.VMEM((2, tq, PAST_LEN), F32),
                   pltpu.VMEM((2, tq, 1), F32)] * 2
    else:
        in_specs = [q_spec, k_spec, v_spec] + tail
        args = [q, k, v, da_lambda, subln]
        scratch = [pltpu.VMEM((2, tq, seq_len), F32), pltpu.VMEM((2, tq, 1), F32)] * 2
    return pl.pallas_call(
        functools.partial(_attn_kernel, lam_init, has_cache),
        grid=(n_units + 1,),
        in_specs=in_specs,
        out_specs=pl.BlockSpec((tq, 128), lambda u: (rows_of(prev(u)), head_of(prev(u)))),
        out_shape=jax.ShapeDtypeStruct((n_seq * seq_len, DA_W), BF16),
        scratch_shapes=scratch,
        compiler_params=_cparams(1),
        name="diff_attention_lat" if has_cache else "diff_attention_ctx",
    )(*args)


def _route(logits):
    lane = lax.broadcasted_iota(jnp.int32, logits.shape, 1)
    lane_f = lane.astype(F32)
    big = 1e9
    is_g = jnp.logical_and(lane >= N_EXPERTS, lane < N_EXPERTS + N_GROUPS)
    gl = jnp.where(is_g, logits, NEG_BIG)
    gmax = jnp.max(gl, axis=-1, keepdims=True)
    gsel = jnp.min(jnp.where(gl == gmax, lane_f, big), axis=-1, keepdims=True) - N_EXPERTS
    g_w = 1.0 / jnp.sum(jnp.exp(gl - gmax), axis=-1, keepdims=True)
    in_grp = jnp.logical_and(lane < N_EXPERTS, (lane // EPG).astype(F32) == gsel)
    el = jnp.where(in_grp, logits, NEG_BIG)
    v1 = jnp.max(el, axis=-1, keepdims=True)
    i1 = jnp.min(jnp.where(el == v1, lane_f, big), axis=-1, keepdims=True)
    el2 = jnp.where(lane_f == i1, NEG_BIG, el)
    v2 = jnp.max(el2, axis=-1, keepdims=True)
    i2 = jnp.min(jnp.where(el2 == v2, lane_f, big), axis=-1, keepdims=True)
    t = jnp.exp(v2 - v1)
    w1 = g_w / (1.0 + t)
    w2 = w1 * t
    return jnp.where(lane_f == i1, w1, 0.0) + jnp.where(lane_f == i2, w2, 0.0)


def _outproj_kernel(x_ref, mod_ref, n2_ref, rof_ref, rob_ref, rsg_ref, da_ref, hof_ref, hob_ref, hsg_ref,
                    hgn_ref, bd_ref, wout_ref, wr_ref, br_ref, x1_ref, h2_ref, comb_ref):
    ro = rof_ref[...] + rob_ref[...]
    ro = ro * lax.rsqrt(_group_mean_sq(ro, bd_ref, HEAD_D) + RMS_EPS) * rsg_ref[...].astype(F32)
    ho = hof_ref[...] + hob_ref[...]
    ho = ho * lax.rsqrt(_group_mean_sq(ho, bd_ref, HEAD_D) + RMS_EPS) * hgn_ref[...] * hsg_ref[...].astype(F32)
    mix = jnp.concatenate([ro.astype(BF16), da_ref[...], ho.astype(BF16)], axis=1)
    mixed = _dot(mix, wout_ref[...])
    gate1 = mod_ref[2:3, :]
    shift2 = mod_ref[3:4, :]
    scale2 = mod_ref[4:5, :]
    x1 = x_ref[...] + gate1 * mixed
    ms = jnp.mean(x1 * x1, axis=-1, keepdims=True)
    h2 = x1 * lax.rsqrt(ms + RMS_EPS) * n2_ref[...] * (1.0 + scale2) + shift2
    x1_ref[...] = x1
    h2_ref[...] = h2.astype(BF16)
    logits = jnp.dot(h2, wr_ref[...], preferred_element_type=F32, precision=HI) + br_ref[...]
    comb_ref[...] = _route(logits)


def _out_projection(latent, layer, x, mod, norm2, ret_of, ret_ob, ret_sg, da_o, hg_of, hg_ob, hg_sg,
                    hgn_t, bd256, w_out_bf, w_route, b_route):
    ntiles = x.shape[0] // TILE
    row = lambda t: (t, 0)
    cond = (lambda t: 1 + t // LAT_TILES_PER_SEQ) if latent else (lambda t: 0)
    lay = lambda t: (layer, 0, 0)
    in_specs = [
        pl.BlockSpec((TILE, D_MODEL), row),
        pl.BlockSpec((None, None, 6, D_MODEL), lambda t: (layer, cond(t), 0, 0)),
        pl.BlockSpec((None, 1, D_MODEL), lay),
        pl.BlockSpec((TILE, MIX_G), row), pl.BlockSpec((TILE, MIX_G), row), pl.BlockSpec((TILE, MIX_G), row),
        pl.BlockSpec((TILE, DA_W), row),
        pl.BlockSpec((TILE, MIX_G), row), pl.BlockSpec((TILE, MIX_G), row), pl.BlockSpec((TILE, MIX_G), row),
        pl.BlockSpec((None, 1, MIX_G), lay),
        pl.BlockSpec((MIX_G, MIX_G), lambda t: (0, 0)),
        pl.BlockSpec((None, D_MODEL, D_MODEL), lay),
        pl.BlockSpec((None, D_MODEL, 128), lay),
        pl.BlockSpec((None, 1, 128), lay),
    ]
    return pl.pallas_call(
        _outproj_kernel,
        grid=(ntiles,),
        in_specs=in_specs,
        out_specs=[pl.BlockSpec((TILE, D_MODEL), row), pl.BlockSpec((TILE, D_MODEL), row),
                   pl.BlockSpec((TILE, 128), row)],
        out_shape=[jax.ShapeDtypeStruct(x.shape, F32), jax.ShapeDtypeStruct(x.shape, BF16),
                   jax.ShapeDtypeStruct((x.shape[0], 128), F32)],
        compiler_params=_cparams(1),
        name="out_projection_lat" if latent else "out_projection_ctx",
    )(x, mod, norm2, ret_of, ret_ob, ret_sg, da_o, hg_of, hg_ob, hg_sg, hgn_t, bd256, w_out_bf, w_route, b_route)


MOE_TILE = 1024


def _moe_kernel(h_ref, comb_ref, wg_ref, wu_ref, wd_ref, x1_ref, mod_ref, out_ref, acc_ref):
    e = pl.program_id(1)

    @pl.when(e == 0)
    def _():
        acc_ref[...] = jnp.zeros_like(acc_ref)

    h = h_ref[...]
    a = _silu(_dot(h, wg_ref[...])) * _dot(h, wu_ref[...])
    y = _dot(a.astype(BF16), wd_ref[...])
    comb = comb_ref[...]
    lane = lax.broadcasted_iota(jnp.int32, comb.shape, 1)
    w_e = jnp.sum(jnp.where(lane == e, comb, 0.0), axis=-1, keepdims=True)
    acc_ref[...] += w_e * y

    @pl.when(e == N_EXPERTS - 1)
    def _():
        out_ref[...] = x1_ref[...] + mod_ref[5:6, :] * acc_ref[...]


def _moe(latent, layer, h2, comb, x1, mod, w_gate_bf, w_up_bf, w_down_bf):
    ntiles = h2.shape[0] // MOE_TILE
    per_seq = DEC_SEQ // MOE_TILE
    cond = (lambda i: 1 + i // per_seq) if latent else (lambda i: 0)
    row = lambda i, e: (i, 0)
    return pl.pallas_call(
        _moe_kernel,
        grid=(ntiles, N_EXPERTS),
        in_specs=[
            pl.BlockSpec((MOE_TILE, D_MODEL), row),
            pl.BlockSpec((MOE_TILE, 128), row),
            pl.BlockSpec((None, None, D_MODEL, EXPERT_FF), lambda i, e: (layer, e, 0, 0)),
            pl.BlockSpec((None, None, D_MODEL, EXPERT_FF), lambda i, e: (layer, e, 0, 0)),
            pl.BlockSpec((None, None, EXPERT_FF, D_MODEL), lambda i, e: (layer, e, 0, 0)),
            pl.BlockSpec((MOE_TILE, D_MODEL), row),
            pl.BlockSpec((None, None, 6, D_MODEL), lambda i, e: (layer, cond(i), 0, 0)),
        ],
        out_specs=pl.BlockSpec((MOE_TILE, D_MODEL), row),
        out_shape=jax.ShapeDtypeStruct(x1.shape, F32),
        scratch_shapes=[pltpu.VMEM((MOE_TILE, D_MODEL), F32)],
        compiler_params=_cparams(2),
        name="moe_lat" if latent else "moe_ctx",
    )(h2, comb, w_gate_bf, w_up_bf, w_down_bf, x1, mod)


def _block_diag_ones(n, group):
    i = np.arange(n) // group
    return jnp.asarray((i[:, None] == i[None, :]).astype(np.float32), dtype=BF16)


def _rope_tables():
    pos = np.arange(DEC_SEQ)
    rows = (pos // GRID_W).astype(np.float32)
    cols = (pos % GRID_W).astype(np.float32)
    inv_freq = (ROPE_BASE ** (-(np.arange(ROPE_PAIRS, dtype=np.float32) / ROPE_PAIRS))).astype(np.float32)
    ang_r = rows[:, None] * inv_freq[None, :]
    ang_c = cols[:, None] * inv_freq[None, :]
    cos64 = np.concatenate([np.cos(ang_r), np.cos(ang_r), np.cos(ang_c), np.cos(ang_c)], axis=1)
    sin64 = np.concatenate([-np.sin(ang_r), np.sin(ang_r), -np.sin(ang_c), np.sin(ang_c)], axis=1)
    return (jnp.asarray(np.tile(cos64, (1, 2)), F32), jnp.asarray(np.tile(sin64, (1, 2)), F32))


def _state_to_blockdiag_t(s):
    eye = jnp.eye(HEADS, dtype=F32)
    return jnp.einsum('bzhdv,hg->bzhvgd', s.astype(F32), eye).reshape(s.shape[0], 2, MIX_G, MIX_G)


def _blockdiag_t_to_state(st):
    s6 = st.reshape(st.shape[0], 2, HEADS, HEAD_D, HEADS, HEAD_D)
    diag = jnp.stack([s6[:, :, h, :, h, :] for h in range(HEADS)], axis=2)
    return jnp.swapaxes(diag, -1, -2)


def _state_to_blockdiag(s):
    eye = jnp.eye(HEADS, dtype=F32)
    return jnp.einsum('bzhdv,hg->bzhdgv', s.astype(F32), eye).reshape(s.shape[0], 2, MIX_G, MIX_G)


def _blockdiag_to_state(st):
    s6 = st.reshape(st.shape[0], 2, HEADS, HEAD_D, HEADS, HEAD_D)
    return jnp.stack([s6[:, :, h, :, h, :] for h in range(HEADS)], axis=2)


def _trunk_layer(latent, layer, x, consts, params, ret_s0, hg_s0, cache_k, cache_v):
    (mod, lb_all, bd512, bd256, rope_cos, rope_sin, ret_tables) = consts
    n_seq = DEC_BATCH if latent else BATCH
    seq_len = DEC_SEQ if latent else SEQ
    nt = seq_len // TILE
    proj = _in_projection(latent, layer, x, mod, params['norm1'], params['w_in'], params['da_qn'],
                          params['da_kn'], lb_all, bd512, rope_cos, rope_sin)
    (rq, rk, rv, rsg, dq, dk, dv, hq, hkf, hkb, hv, hgf, hgb, hsg) = proj[:14]
    ret_of, ret_ob, ret_fin = _retention(n_seq, nt, rq, rk, rv, ret_s0, ret_tables,
                                         "retention_lat" if latent else "retention_ctx")
    hg_of, hg_ob, hg_fin = _gated_scan(n_seq, nt, hq, hkf, hkb, hv, hgf, hgb, hg_s0,
                                       "hgrn_scan_lat" if latent else "hgrn_scan_ctx")
    da_o = _diff_attention(layer, n_seq, seq_len, dq, dk, dv, cache_k, cache_v,
                           params['da_lambda'], params['da_subln'])
    x1, h2, comb = _out_projection(latent, layer, x, mod, params['norm2'], ret_of, ret_ob, rsg, da_o,
                                   hg_of, hg_ob, hsg, params['hg_norm'], bd256, params['w_out'],
                                   params['w_route'], params['b_route'])
    x2 = _moe(latent, layer, h2, comb, x1, mod, params['w_gate'], params['w_up'], params['w_down'])
    extras = None if latent else (proj[14], proj[15], ret_fin, hg_fin)
    return x2, extras


def kernel(x_prompt, x_sample, cache_k, cache_v, state_ret, state_hgrn, c, c_ctx, norm1, norm2, w_mod, b_mod,
           w_in, w_out, da_qn, da_kn, da_lambda, da_subln, hg_lb, hg_norm, w_group, b_group, w_router,
           b_router, w_gate, w_up, w_down):
    cond = jnp.zeros((N_COND, D_MODEL), F32).at[0].set(c_ctx).at[1:1 + DEC_BATCH].set(c)
    mod, lb_all = _modulation(cond, w_mod, b_mod, hg_lb)
    mod = mod.reshape(DEPTH, N_COND, 6, D_MODEL)
    rope_cos, rope_sin = _rope_tables()
    consts = (mod, lb_all.reshape(DEPTH, 1, MIX_G), _block_diag_ones(DA_W, HEAD_D),
              _block_diag_ones(MIX_G, HEAD_D), rope_cos, rope_sin, _retention_tables())
    pad = jnp.zeros((DEPTH, D_MODEL, 128 - N_EXPERTS - N_GROUPS), F32)
    params = {
        'norm1': norm1.reshape(DEPTH, 1, D_MODEL), 'norm2': norm2.reshape(DEPTH, 1, D_MODEL),
        'w_in': w_in.astype(BF16), 'w_out': w_out.astype(BF16),
        'da_qn': jnp.tile(da_qn, (1, DA_W // HEAD_D)).reshape(DEPTH, 1, DA_W),
        'da_kn': jnp.tile(da_kn, (1, DA_W // HEAD_D)).reshape(DEPTH, 1, DA_W),
        'da_lambda': da_lambda, 'da_subln': da_subln.reshape(DEPTH, 1, 128),
        'hg_norm': jnp.tile(hg_norm, (1, HEADS)).reshape(DEPTH, 1, MIX_G),
        'w_route': jnp.concatenate([w_router, w_group, pad], axis=-1),
        'b_route': jnp.concatenate([b_router, b_group, pad[:, 0, :]], axis=-1).reshape(DEPTH, 1, 128),
        'w_gate': w_gate.astype(BF16), 'w_up': w_up.astype(BF16), 'w_down': w_down.astype(BF16),
    }
    ck_bf = cache_k.astype(BF16).reshape(DEC_BATCH, DEPTH, PAST_LEN, DA_W)
    cv_bf = cache_v.astype(BF16).reshape(DEC_BATCH, DEPTH, PAST_LEN, DA_W)
    zero_state = jnp.zeros((BATCH, 2, MIX_G, MIX_G), F32)

    yp = x_prompt.reshape(N_CTX_ROWS, D_MODEL)
    ys = x_sample.reshape(N_LAT_ROWS, D_MODEL)
    ks_out, vs_out, rets_out, hgs_out = [], [], [], []
    for l in range(DEPTH):
        yp, (k_new, v_new, ret_fin, hg_fin) = _trunk_layer(False, l, yp, consts, params, zero_state, zero_state,
                                                           None, None)
        ks_out.append(k_new.reshape(BATCH, SEQ, HEADS, 2, HEAD_D))
        vs_out.append(v_new.reshape(BATCH, SEQ, HEADS, 2 * HEAD_D))
        rets_out.append(_blockdiag_to_state(ret_fin))
        hgs_out.append(_blockdiag_t_to_state(hg_fin))
        ys, _ = _trunk_layer(True, l, ys, consts, params, _state_to_blockdiag(state_ret[:, l]),
                             _state_to_blockdiag_t(state_hgrn[:, l]), ck_bf, cv_bf)
    return (yp.reshape(BATCH, SEQ, D_MODEL), ys.reshape(DEC_BATCH, DEC_SEQ, D_MODEL),
            jnp.stack(ks_out, axis=1), jnp.stack(vs_out, axis=1),
            jnp.stack(rets_out, axis=1), jnp.stack(hgs_out, axis=1))
```

```python
import functools
import math

import numpy as np
import jax
import jax.numpy as jnp
from jax import lax
from jax.experimental import pallas as pl
from jax.experimental.pallas import tpu as pltpu

F32 = jnp.float32
BF16 = jnp.bfloat16

D_MODEL = 1024
BATCH = 16
SEQ = 256
DEPTH = 4
DEC_BATCH = 4
DEC_SEQ = 4096
PAST_LEN = 256
GRID_W = 64
HEADS = 4
HEAD_D = 64
MIX_G = HEADS * HEAD_D
DA_W = 512
IN_COLS = 3840
ROPE_BASE = 10000.0
ROPE_PAIRS = 16
RMS_EPS = 1e-6
N_GROUPS = 4
EPG = 4
N_EXPERTS = 16
EXPERT_FF = 512

TILE = 256
N_CTX_ROWS = BATCH * SEQ
N_LAT_ROWS = DEC_BATCH * DEC_SEQ
N_ROWS = N_CTX_ROWS + N_LAT_ROWS
CTX_TILES = N_CTX_ROWS // TILE
LAT_TILES_PER_SEQ = DEC_SEQ // TILE
N_TILES = N_ROWS // TILE
N_COND = 8

N_PAIRS = EPG * (EPG - 1) // 2
N_BUCKETS = N_GROUPS * N_PAIRS
MOE_TILE = 256
MOE_MAX_TILES = N_TILES + N_BUCKETS
N_SLOTS = MOE_MAX_TILES * MOE_TILE
HALF_D = D_MODEL // 2
ROW_W = HALF_D + 128

CHUNK = 64
SUB = 16
NSUB = CHUNK // SUB
NCHUNK = TILE // CHUNK

VMEM_LIMIT = 48 * 1024 * 1024
NEG_BIG = -1e30
LOG2_E = 1.4426950408889634

HI = lax.Precision.HIGHEST


def _cparams(n_axes):
    return pltpu.CompilerParams(dimension_semantics=("arbitrary",) * n_axes,
                                vmem_limit_bytes=VMEM_LIMIT)


def _dot(a, b):
    return jnp.dot(a, b, preferred_element_type=F32)


def _dot_nt(a, b):
    return lax.dot_general(a, b, (((1,), (1,)), ((), ())), preferred_element_type=F32)


def _dot_tn(a, b):
    return lax.dot_general(a, b, (((0,), (0,)), ((), ())), preferred_element_type=F32)


def _split_dot(x, m_bf16):
    hi = x.astype(BF16)
    lo = (x - hi.astype(F32)).astype(BF16)
    return _dot(hi, m_bf16) + _dot(lo, m_bf16)


def _silu(x):
    return x * (1.0 / (1.0 + jnp.exp(-x)))


def _sigmoid(x):
    return 1.0 / (1.0 + jnp.exp(-x))


def _mod_kernel(cond_ref, w_ref, b_ref, lb_ref, mod_ref, lbo_ref):
    c = cond_ref[...]
    m = jnp.dot(_silu(c), w_ref[...], preferred_element_type=F32, precision=HI)
    mod_ref[...] = m + b_ref[...]
    z = lb_ref[...]
    zmax = jnp.max(z, axis=0, keepdims=True)
    e = jnp.exp(z - zmax)
    p = e / jnp.sum(e, axis=0, keepdims=True)
    rows = [jnp.zeros_like(p[0:1])]
    for l in range(1, DEPTH):
        rows.append(rows[-1] + p[l:l + 1])
    lbo_ref[...] = jnp.concatenate(rows, axis=0)


def _modulation(cond, w_mod, b_mod, hg_lb):
    nblk = 6
    return pl.pallas_call(
        _mod_kernel,
        grid=(DEPTH, nblk),
        in_specs=[
            pl.BlockSpec((N_COND, D_MODEL), lambda l, j: (0, 0)),
            pl.BlockSpec((None, D_MODEL, D_MODEL), lambda l, j: (l, 0, j)),
            pl.BlockSpec((None, 1, D_MODEL), lambda l, j: (l, 0, j)),
            pl.BlockSpec((DEPTH, MIX_G), lambda l, j: (0, 0)),
        ],
        out_specs=[
            pl.BlockSpec((None, N_COND, D_MODEL), lambda l, j: (l, 0, j)),
            pl.BlockSpec((DEPTH, MIX_G), lambda l, j: (0, 0)),
        ],
        out_shape=[jax.ShapeDtypeStruct((DEPTH, N_COND, 6 * D_MODEL), F32),
                   jax.ShapeDtypeStruct((DEPTH, MIX_G), F32)],
        compiler_params=_cparams(2),
        name="modulation",
    )(cond, w_mod, b_mod.reshape(DEPTH, 1, 6 * D_MODEL), hg_lb)


def _group_mean_sq(x, bd_ref, group):
    return _split_dot(x * x, bd_ref[...]) * (1.0 / group)


def _swap16(x):
    w = x.shape[-1]
    lane = lax.broadcasted_iota(jnp.int32, x.shape, x.ndim - 1)
    up = pltpu.roll(x, w - 16, x.ndim - 1)
    dn = pltpu.roll(x, 16, x.ndim - 1)
    return jnp.where((lane % 32) < 16, up, dn)


def _inproj_kernel(latent, x_ref, mod_ref, n1_ref, w_ref, qn_ref, kn_ref, lb_ref, bd_ref, *rest):
    if latent:
        cos_ref, sin_ref = rest[0], rest[1]
        outs = rest[2:]
    else:
        outs = rest
    (rq_ref, rk_ref, rv_ref, rsg_ref, dq_ref, dk_ref, dv_ref,
     hq_ref, hkf_ref, hkb_ref, hv_ref, hgf_ref, hgb_ref, hsg_ref) = outs[:14]

    x = x_ref[...]
    ms = jnp.mean(x * x, axis=-1, keepdims=True)
    shift1 = mod_ref[0:1, :]
    scale1 = mod_ref[1:2, :]
    h = x * lax.rsqrt(ms + RMS_EPS) * n1_ref[...] * (1.0 + scale1) + shift1
    y = _dot(h.astype(BF16), w_ref[...])

    rq_ref[...] = y[:, 0:256].astype(BF16)
    rk_ref[...] = (y[:, 256:512] * (HEAD_D ** -0.5)).astype(BF16)
    rv_ref[...] = y[:, 512:768].astype(BF16)
    rsg_ref[...] = _silu(y[:, 768:1024]).astype(BF16)

    dq = y[:, 1024:1536]
    dk = y[:, 1536:2048]
    qn = dq * lax.rsqrt(_group_mean_sq(dq, bd_ref, HEAD_D) + RMS_EPS) * qn_ref[...]
    kn = dk * lax.rsqrt(_group_mean_sq(dk, bd_ref, HEAD_D) + RMS_EPS) * kn_ref[...]
    if latent:
        cos = jnp.concatenate([cos_ref[...]] * 4, axis=1)
        sin = jnp.concatenate([sin_ref[...]] * 4, axis=1)
        qr = qn * cos + _swap16(qn) * sin
        kr = kn * cos + _swap16(kn) * sin
    else:
        qr, kr = qn, kn
        ck_ref, cv_ref = outs[14], outs[15]
        ck_ref[...] = kn
        cv_ref[...] = y[:, 2048:2560]
    dq_ref[...] = (qr * (HEAD_D ** -0.5 * LOG2_E)).astype(BF16)
    dk_ref[...] = kr.astype(BF16)
    dv_ref[...] = y[:, 2048:2560].astype(BF16)

    lb = lb_ref[...]
    f_f = lb + (1.0 - lb) * _sigmoid(y[:, 2816:3072])
    f_b = lb + (1.0 - lb) * _sigmoid(y[:, 3072:3328])
    hq_ref[...] = y[:, 2560:2816].astype(BF16)
    hkf_ref[...] = (1.0 - f_f).astype(BF16)
    hkb_ref[...] = (1.0 - f_b).astype(BF16)
    hgf_ref[...] = jnp.log(f_f)
    hgb_ref[...] = jnp.log(f_b)
    hv_ref[...] = y[:, 3328:3584].astype(BF16)
    hsg_ref[...] = _silu(y[:, 3584:3840]).astype(BF16)


def _in_projection(latent, layer, x, mod, norm1, w_in_bf, qn_t, kn_t, lb_all, bd512, rope_cos, rope_sin):
    tile0 = CTX_TILES if latent else 0
    ntiles = (N_TILES - CTX_TILES) if latent else CTX_TILES
    nrows = ntiles * TILE

    in_specs = [
        pl.BlockSpec((TILE, D_MODEL), lambda t: (tile0 + t, 0)),
        pl.BlockSpec((None, None, 6, D_MODEL), lambda t: (layer, _cond_of_tile(tile0 + t), 0, 0)),
        pl.BlockSpec((None, 1, D_MODEL), lambda t: (layer, 0, 0)),
        pl.BlockSpec((None, D_MODEL, IN_COLS), lambda t: (layer, 0, 0)),
        pl.BlockSpec((None, 1, DA_W), lambda t: (layer, 0, 0)),
        pl.BlockSpec((None, 1, DA_W), lambda t: (layer, 0, 0)),
        pl.BlockSpec((None, 1, MIX_G), lambda t: (layer, 0, 0)),
        pl.BlockSpec((DA_W, DA_W), lambda t: (0, 0)),
    ]
    args = [x, mod, norm1, w_in_bf, qn_t, kn_t, lb_all, bd512]
    if latent:
        in_specs += [pl.BlockSpec((TILE, 128), lambda t: (t % LAT_TILES_PER_SEQ, 0))] * 2
        args += [rope_cos, rope_sin]

    def o(width, dtype):
        return jax.ShapeDtypeStruct((nrows, width), dtype), pl.BlockSpec((TILE, width), lambda t: (t, 0))

    outs = [o(256, BF16), o(256, BF16), o(256, BF16), o(256, BF16),
            o(512, BF16), o(512, BF16), o(512, BF16),
            o(256, BF16), o(256, BF16), o(256, BF16), o(256, BF16), o(256, F32), o(256, F32), o(256, BF16)]
    if not latent:
        outs += [o(512, F32), o(512, F32)]
    return pl.pallas_call(
        functools.partial(_inproj_kernel, latent),
        grid=(ntiles,),
        in_specs=in_specs,
        out_specs=[s for _, s in outs],
        out_shape=[s for s, _ in outs],
        compiler_params=_cparams(1),
        name="in_projection_lat" if latent else "in_projection_ctx",
    )(*args)


def _head_masks():
    lane = lax.broadcasted_iota(jnp.int32, (1, MIX_G), 1)
    return [(lane // HEAD_D == h).astype(F32) for h in range(HEADS)]


def _chunk_step(q, k, v, g, st, rev, consts):
    tri, bd_mask, ones_bd, gsel, hmask = consts
    b = jnp.dot(tri, g, preferred_element_type=F32, precision=HI)

    zero_row = jnp.zeros((1, MIX_G), F32)
    r_rows, e_rows = [], []
    for s in range(NSUB):
        lo, hi = s * SUB, s * SUB + SUB - 1
        if not rev:
            r_rows.append(b[lo - 1:lo, :] if s > 0 else zero_row)
            e_rows.append(b[hi:hi + 1, :])
        else:
            r_rows.append(b[hi + 1:hi + 2, :] if s < NSUB - 1 else zero_row)
            e_rows.append(b[lo:lo + 1, :])
    btot = e_rows[NSUB - 1] if not rev else e_rows[0]
    r_full = jnp.concatenate([jnp.broadcast_to(r, (SUB, MIX_G)) for r in r_rows], axis=0)
    e_full = jnp.concatenate([jnp.broadcast_to(e, (SUB, MIX_G)) for e in e_rows], axis=0)

    bl = b - r_full
    qp = q * jnp.exp(bl)
    kend = k * jnp.exp(e_full - b)
    q_inter = qp * jnp.exp(r_full)
    k_state = kend * jnp.exp(btot - e_full)
    v16 = v

    o = _dot_nt(q_inter.astype(BF16), st.astype(BF16))
    ut = _dot_tn(v16, k_state.astype(BF16))
    st_new = st * jnp.exp(btot) + ut * bd_mask

    pairs = [(i, j) for i in range(NSUB) for j in range(NSUB) if (j < i if not rev else j > i)]
    lhs = [qp[i * SUB:(i + 1) * SUB, :] * jnp.exp(r_rows[i] - e_rows[j]) for (i, j) in pairs]
    lh = jnp.concatenate(lhs, axis=0)
    lhs4 = jnp.concatenate([lh * hmask[h] for h in range(HEADS)], axis=0).astype(BF16)
    sc = _dot_nt(lhs4, kend.astype(BF16))
    col = lax.broadcasted_iota(jnp.int32, (SUB, CHUNK), 1) // SUB
    npairs = len(pairs)
    p_rows = []
    for h in range(HEADS):
        for i in range(NSUB):
            acc = None
            for p, (pi, pj) in enumerate(pairs):
                if pi != i:
                    continue
                base = (h * npairs + p) * SUB
                blk = jnp.where(col == pj, sc[base:base + SUB, :], 0.0)
                acc = blk if acc is None else acc + blk
            if acc is None:
                acc = jnp.zeros((SUB, CHUNK), F32)
            p_rows.append(acc)
    ps = jnp.concatenate(p_rows, axis=0).astype(BF16)
    full = _dot(ps, v16)
    for h in range(HEADS):
        o = o + full[h * CHUNK:(h + 1) * CHUNK, :] * hmask[h]

    jrow = lax.broadcasted_iota(jnp.int32, (SUB, MIX_G), 0)
    pp, vt = [], []
    for s in range(NSUB):
        sl = slice(s * SUB, (s + 1) * SUB)
        bl_s, q_s, k_s = bl[sl, :], q[sl, :], k[sl, :]
        for i in range(SUB):
            d = bl_s[i:i + 1, :] - bl_s
            keep = (jrow <= i) if not rev else (jrow >= i)
            e = jnp.exp(jnp.where(keep, d, NEG_BIG))
            pp.append((e * q_s[i:i + 1, :] * k_s).astype(BF16))
        vt.extend([v16[sl, :]] * SUB)
    pp = jnp.concatenate(pp, axis=0)
    sb = _dot(pp, ones_bd)
    w = (sb * jnp.concatenate(vt, axis=0).astype(F32)).astype(BF16)
    o = o + _dot(gsel, w)
    return o, st_new


def _scan_consts(rev):
    r = lax.broadcasted_iota(jnp.int32, (CHUNK, CHUNK), 0)
    c = lax.broadcasted_iota(jnp.int32, (CHUNK, CHUNK), 1)
    tri = ((c <= r) if not rev else (c >= r)).astype(F32)
    rr = lax.broadcasted_iota(jnp.int32, (MIX_G, MIX_G), 0) // HEAD_D
    cc = lax.broadcasted_iota(jnp.int32, (MIX_G, MIX_G), 1) // HEAD_D
    bd_mask = (rr == cc).astype(F32)
    ones_bd = bd_mask.astype(BF16)
    gi = lax.broadcasted_iota(jnp.int32, (CHUNK, CHUNK * SUB), 0)
    gj = lax.broadcasted_iota(jnp.int32, (CHUNK, CHUNK * SUB), 1) // SUB
    gsel = (gi == gj).astype(BF16)
    return tri, bd_mask, ones_bd, gsel, _head_masks()


def _scan_kernel(nt, qf_ref, kf_ref, vf_ref, gf_ref, qb_ref, kb_ref, vb_ref, gb_ref, s0_ref,
                 of_ref, ob_ref, sfin_ref, st_scr):
    t = pl.program_id(1)

    @pl.when(t == 0)
    def _():
        st_scr[...] = s0_ref[...]

    cf = _scan_consts(False)
    cb = _scan_consts(True)

    def body(c, carry):
        rf = pl.multiple_of(c * CHUNK, CHUNK)
        rb = pl.multiple_of((NCHUNK - 1 - c) * CHUNK, CHUNK)
        o_f, st_f = _chunk_step(qf_ref[pl.ds(rf, CHUNK), :].astype(F32),
                                kf_ref[pl.ds(rf, CHUNK), :].astype(F32),
                                vf_ref[pl.ds(rf, CHUNK), :],
                                gf_ref[pl.ds(rf, CHUNK), :], st_scr[0], False, cf)
        of_ref[pl.ds(rf, CHUNK), :] = o_f
        st_scr[0] = st_f
        o_b, st_b = _chunk_step(qb_ref[pl.ds(rb, CHUNK), :].astype(F32),
                                kb_ref[pl.ds(rb, CHUNK), :].astype(F32),
                                vb_ref[pl.ds(rb, CHUNK), :],
                                gb_ref[pl.ds(rb, CHUNK), :], st_scr[1], True, cb)
        ob_ref[pl.ds(rb, CHUNK), :] = o_b
        st_scr[1] = st_b
        return carry

    lax.fori_loop(0, NCHUNK, body, 0)

    @pl.when(t == nt - 1)
    def _():
        sfin_ref[...] = st_scr[...]


def _gated_scan(n_seq, nt, q, k_f, k_b, v, g_f, g_b, s0, name):
    fwd = lambda s, t: (s * nt + t, 0)
    bwd = lambda s, t: (s * nt + (nt - 1 - t), 0)
    blk = lambda im: pl.BlockSpec((TILE, MIX_G), im)
    in_specs = [blk(fwd), blk(fwd), blk(fwd), blk(fwd), blk(bwd), blk(bwd), blk(bwd), blk(bwd),
                pl.BlockSpec((None, 2, MIX_G, MIX_G), lambda s, t: (s, 0, 0, 0))]
    nrows = n_seq * nt * TILE
    return pl.pallas_call(
        functools.partial(_scan_kernel, nt),
        grid=(n_seq, nt),
        in_specs=in_specs,
        out_specs=[pl.BlockSpec((TILE, MIX_G), lambda s, t: (s * nt + t, 0)),
                   pl.BlockSpec((TILE, MIX_G), lambda s, t: (s * nt + (nt - 1 - t), 0)),
                   pl.BlockSpec((None, 2, MIX_G, MIX_G), lambda s, t: (s, 0, 0, 0))],
        out_shape=[jax.ShapeDtypeStruct((nrows, MIX_G), F32),
                   jax.ShapeDtypeStruct((nrows, MIX_G), F32),
                   jax.ShapeDtypeStruct((n_seq, 2, MIX_G, MIX_G), F32)],
        scratch_shapes=[pltpu.VMEM((2, MIX_G, MIX_G), F32)],
        compiler_params=_cparams(2),
        name=name,
    )(q, k_f, v, g_f, q, k_b, v, g_b, s0)


def _retention_tables():
    gam = 1.0 - 2.0 ** (-5.0 - np.arange(HEADS, dtype=np.float64))
    gam_r = gam[::-1]
    i = np.arange(TILE, dtype=np.float64)
    diff = i[:, None] - i[None, :]
    dcomb = np.zeros((HEADS, TILE, TILE))
    for h in range(HEADS):
        lower = np.where(diff > 0, gam[h] ** np.maximum(diff, 0), 0.0)
        upper = np.where(diff < 0, gam_r[h] ** np.maximum(-diff, 0), 0.0)
        dcomb[h] = lower + upper + 2.0 * (diff == 0)
    lanes = lambda per_head: np.repeat(per_head, HEAD_D, axis=-1)
    qd_f = lanes(gam[None, :] ** (i[:, None] + 1.0))
    kd_f = lanes(gam[None, :] ** (TILE - 1.0 - i[:, None]))
    qd_b = lanes(gam_r[None, :] ** (TILE - i[:, None]))
    kd_b = lanes(gam_r[None, :] ** i[:, None])
    blk = (np.arange(MIX_G)[:, None] // HEAD_D) == (np.arange(MIX_G)[None, :] // HEAD_D)
    c_f = np.where(blk, lanes(gam ** TILE)[None, :], 0.0) * np.ones((MIX_G, 1))
    c_b = np.where(blk, lanes(gam_r ** TILE)[None, :], 0.0) * np.ones((MIX_G, 1))
    f = lambda a: jnp.asarray(a, F32)
    return (f(dcomb.reshape(HEADS * TILE, TILE)), f(qd_f), f(kd_f), f(qd_b), f(kd_b), f(c_f), f(c_b))


def _ret_kernel(nt, qf_ref, kf_ref, vf_ref, qb_ref, kb_ref, vb_ref, s0_ref, dcomb_ref, qdf_ref, kdf_ref,
                qdb_ref, kdb_ref, cf_ref, cb_ref, of_ref, ob_ref, sfin_ref, st_scr):
    t = pl.program_id(1)

    @pl.when(t == 0)
    def _():
        st_scr[...] = s0_ref[...]

    lane_head = lax.broadcasted_iota(jnp.int32, (TILE, MIX_G), 1) // HEAD_D
    rr = lax.broadcasted_iota(jnp.int32, (MIX_G, MIX_G), 0) // HEAD_D
    cc = lax.broadcasted_iota(jnp.int32, (MIX_G, MIX_G), 1) // HEAD_D
    same_head = rr == cc

    q = qf_ref[...]
    k = kf_ref[...]
    v = vf_ref[...]
    qs = jnp.concatenate([jnp.where(lane_head == h, q, jnp.zeros_like(q)) for h in range(HEADS)], axis=0)
    p = (_dot_nt(qs, k) * dcomb_ref[...]).astype(BF16)
    full = _dot(p, v)
    o = _dot((q.astype(F32) * qdf_ref[...]).astype(BF16), st_scr[0].astype(BF16))
    for h in range(HEADS):
        o = o + jnp.where(lane_head == h, full[h * TILE:(h + 1) * TILE, :], 0.0)
    of_ref[...] = o
    u = _dot_tn((k.astype(F32) * kdf_ref[...]).astype(BF16), v)
    st_scr[0] = st_scr[0] * cf_ref[...] + jnp.where(same_head, u, 0.0)

    qb = qb_ref[...]
    kb = kb_ref[...]
    vb = vb_ref[...]
    ob_ref[...] = _dot((qb.astype(F32) * qdb_ref[...]).astype(BF16), st_scr[1].astype(BF16))
    ub = _dot_tn((kb.astype(F32) * kdb_ref[...]).astype(BF16), vb)
    st_scr[1] = st_scr[1] * cb_ref[...] + jnp.where(same_head, ub, 0.0)

    @pl.when(t == nt - 1)
    def _():
        sfin_ref[...] = st_scr[...]


def _retention(n_seq, nt, q, k, v, s0, tables, name):
    fwd = lambda s, t: (s * nt + t, 0)
    bwd = lambda s, t: (s * nt + (nt - 1 - t), 0)
    const = lambda s, t: (0, 0)
    blk = lambda im: pl.BlockSpec((TILE, MIX_G), im)
    sq = lambda: pl.BlockSpec((MIX_G, MIX_G), const)
    in_specs = [blk(fwd), blk(fwd), blk(fwd), blk(bwd), blk(bwd), blk(bwd),
                pl.BlockSpec((None, 2, MIX_G, MIX_G), lambda s, t: (s, 0, 0, 0)),
                pl.BlockSpec((HEADS * TILE, TILE), const), sq(), sq(), sq(), sq(), sq(), sq()]
    nrows = n_seq * nt * TILE
    return pl.pallas_call(
        functools.partial(_ret_kernel, nt),
        grid=(n_seq, nt),
        in_specs=in_specs,
        out_specs=[blk(fwd), blk(bwd), pl.BlockSpec((None, 2, MIX_G, MIX_G), lambda s, t: (s, 0, 0, 0))],
        out_shape=[jax.ShapeDtypeStruct((nrows, MIX_G), F32),
                   jax.ShapeDtypeStruct((nrows, MIX_G), F32),
                   jax.ShapeDtypeStruct((n_seq, 2, MIX_G, MIX_G), F32)],
        scratch_shapes=[pltpu.VMEM((2, MIX_G, MIX_G), F32)],
        compiler_params=_cparams(2),
        name=name,
    )(q, k, v, q, k, v, s0, *tables)


def _attn_step(lam_init, has_cache, refs, s_write, s_read):
    if has_cache:
        q_ref, k_ref, ck_ref, v_ref, cv_ref, lam_ref, sub_ref, o_ref = refs
        sw, sw2, mw = s_write
        sr, sr2, mr = s_read
    else:
        q_ref, k_ref, v_ref, lam_ref, sub_ref, o_ref = refs
        sw, mw = s_write
        sr, mr = s_read

    lp = lam_ref[...]
    lam = (jnp.exp(jnp.sum(lp[0:1] * lp[1:2], axis=-1, keepdims=True))
           - jnp.exp(jnp.sum(lp[2:3] * lp[3:4], axis=-1, keepdims=True)) + lam_init)
    v_ext = jnp.concatenate([v_ref[...], jnp.ones(v_ref.shape, BF16)], axis=1)
    if has_cache:
        cv_ext = jnp.concatenate([cv_ref[...], jnp.ones(cv_ref.shape, BF16)], axis=1)
    parts = []
    for c in range(2):
        m = mr[c]
        oe = _dot(jnp.exp2((sr[c] - m).astype(BF16)), v_ext)
        if has_cache:
            oe = oe + _dot(jnp.exp2((sr2[c] - m).astype(BF16)), cv_ext)
        parts.append(oe[:, :128] / oe[:, 128:129])
    o = parts[0] - lam * parts[1]
    ms = jnp.mean(o * o, axis=-1, keepdims=True)
    o_ref[...] = (o * lax.rsqrt(ms + RMS_EPS) * sub_ref[...] * (1.0 - lam_init)).astype(BF16)

    q = q_ref[...]
    k = k_ref[...]
    for c in range(2):
        qc = q[:, c * HEAD_D:(c + 1) * HEAD_D]
        s = _dot_nt(qc, k[:, c * HEAD_D:(c + 1) * HEAD_D])
        m = jnp.max(s, axis=-1, keepdims=True)
        sw[c] = s
        if has_cache:
            s2 = _dot_nt(qc, ck_ref[:, c * HEAD_D:(c + 1) * HEAD_D])
            m = jnp.maximum(m, jnp.max(s2, axis=-1, keepdims=True))
            sw2[c] = s2
        mw[c] = m


def _attn_kernel(lam_init, has_cache, *refs):
    n_in = 8 if has_cache else 6
    io, scr = refs[:n_in], refs[n_in:]
    half = len(scr) // 2
    buf_a, buf_b = scr[:half], scr[half:]
    u = pl.program_id(0)

    @pl.when(u == 0)
    def _():
        for r in buf_b:
            r[...] = jnp.zeros(r.shape, r.dtype)

    @pl.when(u % 2 == 0)
    def _():
        _attn_step(lam_init, has_cache, io, buf_a, buf_b)

    @pl.when(u % 2 == 1)
    def _():
        _attn_step(lam_init, has_cache, io, buf_b, buf_a)


def _diff_attention(layer, n_seq, seq_len, q, k, v, cache_k, cache_v, da_lambda, subln):
    tq = TILE
    nq = seq_len // tq
    n_units = n_seq * HEADS * nq
    has_cache = cache_k is not None
    lam_init = 0.8 - 0.6 * math.exp(-0.3 * layer)

    cur = lambda u: jnp.minimum(u, n_units - 1)
    prev = lambda u: jnp.maximum(u - 1, 0)
    seq_of = lambda w: w // (HEADS * nq)
    head_of = lambda w: (w // nq) % HEADS
    rows_of = lambda w: seq_of(w) * nq + w % nq

    q_spec = pl.BlockSpec((tq, 128), lambda u: (rows_of(cur(u)), head_of(cur(u))))
    k_spec = pl.BlockSpec((seq_len, 128), lambda u: (seq_of(cur(u)), head_of(cur(u))))
    v_spec = pl.BlockSpec((seq_len, 128), lambda u: (seq_of(prev(u)), head_of(prev(u))))
    ck_spec = pl.BlockSpec((None, None, PAST_LEN, 128), lambda u: (seq_of(cur(u)), layer, 0, head_of(cur(u))))
    cv_spec = pl.BlockSpec((None, None, PAST_LEN, 128), lambda u: (seq_of(prev(u)), layer, 0, head_of(prev(u))))
    tail = [pl.BlockSpec((None, 4, HEAD_D), lambda u: (layer, 0, 0)),
            pl.BlockSpec((None, 1, 128), lambda u: (layer, 0, 0))]
    if has_cache:
        in_specs = [q_spec, k_spec, ck_spec, v_spec, cv_spec] + tail
        args = [q, k, cache_k, v, cache_v, da_lambda, subln]
        scratch = [pltpu.VMEM((2, tq, seq_len), F32), pltpu.VMEM((2, tq, PAST_LEN), F32),
                   pltpu.VMEM((2, tq, 1), F32)] * 2
    else:
        in_specs = [q_spec, k_spec, v_spec] + tail
        args = [q, k, v, da_lambda, subln]
        scratch = [pltpu.VMEM((2, tq, seq_len), F32), pltpu.VMEM((2, tq, 1), F32)] * 2
    return pl.pallas_call(
        functools.partial(_attn_kernel, lam_init, has_cache),
        grid=(n_units + 1,),
        in_specs=in_specs,
        out_specs=pl.BlockSpec((tq, 128), lambda u: (rows_of(prev(u)), head_of(prev(u)))),
        out_shape=jax.ShapeDtypeStruct((n_seq * seq_len, DA_W), BF16),
        scratch_shapes=scratch,
        compiler_params=_cparams(1),
        name="diff_attention_lat" if has_cache else "diff_attention_ctx",
    )(*args)


def _route(logits):
    lane = lax.broadcasted_iota(jnp.int32, logits.shape, 1)
    lane_f = lane.astype(F32)
    big = 1e9
    is_g = jnp.logical_and(lane >= N_EXPERTS, lane < N_EXPERTS + N_GROUPS)
    gl = jnp.where(is_g, logits, NEG_BIG)
    gmax = jnp.max(gl, axis=-1, keepdims=True)
    gsel = jnp.min(jnp.where(gl == gmax, lane_f, big), axis=-1, keepdims=True) - N_EXPERTS
    g_w = 1.0 / jnp.sum(jnp.exp(gl - gmax), axis=-1, keepdims=True)
    in_grp = jnp.logical_and(lane < N_EXPERTS, (lane // EPG).astype(F32) == gsel)
    el = jnp.where(in_grp, logits, NEG_BIG)
    v1 = jnp.max(el, axis=-1, keepdims=True)
    i1 = jnp.min(jnp.where(el == v1, lane_f, big), axis=-1, keepdims=True)
    el2 = jnp.where(lane_f == i1, NEG_BIG, el)
    v2 = jnp.max(el2, axis=-1, keepdims=True)
    i2 = jnp.min(jnp.where(el2 == v2, lane_f, big), axis=-1, keepdims=True)
    t = jnp.exp(v2 - v1)
    w1 = g_w / (1.0 + t)
    w2 = w1 * t
    first_lo = i1 < i2
    a = jnp.minimum(i1, i2) - EPG * gsel
    b = jnp.maximum(i1, i2) - EPG * gsel
    bucket = gsel * N_PAIRS + a * (7.0 - a) * 0.5 + (b - a - 1.0)
    w_lo = jnp.where(first_lo, w1, w2)
    w_hi = jnp.where(first_lo, w2, w1)
    return jnp.where(lane == 0, bucket, jnp.where(lane == 1, w_lo, jnp.where(lane == 2, w_hi, 0.0)))


def _pack_bf16_pairs(x):
    n = x.shape[1] // 2
    bits = pltpu.bitcast(x.astype(BF16).astype(F32), jnp.uint32)
    return bits[:, :n] | lax.shift_right_logical(bits[:, n:], jnp.uint32(16))


def _unpack_bf16_pairs(w):
    hi = pltpu.bitcast(w & jnp.uint32(0xFFFF0000), F32)
    lo = pltpu.bitcast(lax.shift_left(w, jnp.uint32(16)), F32)
    return jnp.concatenate([hi, lo], axis=1)


def _outproj_body(x_ref, mod_ref, n2_ref, streams, hgn_ref, bd_ref, wout_ref, wr_ref, br_ref,
                  x1_ref, hp_ref, route_ref):
    rof_ref, rob_ref, rsg_ref, da_ref, hof_ref, hob_ref, hsg_ref = streams
    ro = rof_ref[...] + rob_ref[...]
    ro = ro * lax.rsqrt(_group_mean_sq(ro, bd_ref, HEAD_D) + RMS_EPS) * rsg_ref[...].astype(F32)
    ho = hof_ref[...] + hob_ref[...]
    ho = ho * lax.rsqrt(_group_mean_sq(ho, bd_ref, HEAD_D) + RMS_EPS) * hgn_ref[...] * hsg_ref[...].astype(F32)
    mix = jnp.concatenate([ro.astype(BF16), da_ref[...], ho.astype(BF16)], axis=1)
    mixed = _dot(mix, wout_ref[...])
    gate1 = mod_ref[2:3, :]
    shift2 = mod_ref[3:4, :]
    scale2 = mod_ref[4:5, :]
    x1 = x_ref[...] + gate1 * mixed
    ms = jnp.mean(x1 * x1, axis=-1, keepdims=True)
    h2 = x1 * lax.rsqrt(ms + RMS_EPS) * n2_ref[...] * (1.0 + scale2) + shift2
    logits = jnp.dot(h2, wr_ref[...], preferred_element_type=F32, precision=HI) + br_ref[...]
    route = _route(logits)
    x1_ref[...] = x1
    route_ref[...] = route
    hp_ref[...] = jnp.concatenate([_pack_bf16_pairs(h2), pltpu.bitcast(route, jnp.uint32)], axis=1)


def _outproj_kernel(x_ref, mod_ref, n2_ref, *rest):
    ctx_streams, lat_streams, tail = rest[0:7], rest[7:14], rest[14:]
    t = pl.program_id(0)

    @pl.when(t < CTX_TILES)
    def _():
        _outproj_body(x_ref, mod_ref, n2_ref, ctx_streams, *tail)

    @pl.when(t >= CTX_TILES)
    def _():
        _outproj_body(x_ref, mod_ref, n2_ref, lat_streams, *tail)


def _cond_of_tile(t):
    return jnp.where(t < CTX_TILES, 0, 1 + (t - CTX_TILES) // LAT_TILES_PER_SEQ)


def _out_projection(layer, x, mod, norm2, ctx_streams, lat_streams, hgn_t, bd256, w_out_bf, w_route, b_route):
    row = lambda t: (t, 0)
    ctx_row = lambda t: (jnp.minimum(t, CTX_TILES - 1), 0)
    lat_row = lambda t: (jnp.maximum(t - CTX_TILES, 0), 0)
    lay = lambda t: (layer, 0, 0)
    widths = (MIX_G, MIX_G, MIX_G, DA_W, MIX_G, MIX_G, MIX_G)
    in_specs = [
        pl.BlockSpec((TILE, D_MODEL), row),
        pl.BlockSpec((None, None, 6, D_MODEL), lambda t: (layer, _cond_of_tile(t), 0, 0)),
        pl.BlockSpec((None, 1, D_MODEL), lay),
    ]
    in_specs += [pl.BlockSpec((TILE, w), ctx_row) for w in widths]
    in_specs += [pl.BlockSpec((TILE, w), lat_row) for w in widths]
    in_specs += [
        pl.BlockSpec((None, 1, MIX_G), lay),
        pl.BlockSpec((MIX_G, MIX_G), lambda t: (0, 0)),
        pl.BlockSpec((None, D_MODEL, D_MODEL), lay),
        pl.BlockSpec((None, D_MODEL, 128), lay),
        pl.BlockSpec((None, 1, 128), lay),
    ]
    return pl.pallas_call(
        _outproj_kernel,
        grid=(N_TILES,),
        in_specs=in_specs,
        out_specs=[pl.BlockSpec((TILE, D_MODEL), row), pl.BlockSpec((TILE, ROW_W), row),
                   pl.BlockSpec((TILE, 128), row)],
        out_shape=[jax.ShapeDtypeStruct((N_ROWS, D_MODEL), F32), jax.ShapeDtypeStruct((N_ROWS, ROW_W), jnp.uint32),
                   jax.ShapeDtypeStruct((N_ROWS, 128), F32)],
        compiler_params=_cparams(1),
        name="out_projection",
    )(x, mod, norm2, *ctx_streams, *lat_streams, hgn_t, bd256, w_out_bf, w_route, b_route)


def _dispatch_kernel(route_ref, dest_ref, meta_ref, cnt_ref):
    ph = pl.program_id(0)
    t = pl.program_id(1)
    lane = lax.broadcasted_iota(jnp.int32, (TILE, 128), 1).astype(F32)
    onehot = (lane == route_ref[:, 0:1]).astype(F32)
    tile_cnt = jnp.sum(onehot, axis=0, keepdims=True)

    @pl.when(jnp.logical_and(ph == 0, t == 0))
    def _():
        cnt_ref[...] = jnp.zeros_like(cnt_ref)

    @pl.when(ph == 0)
    def _():
        cnt_ref[0:1, :] += tile_cnt

    @pl.when(jnp.logical_and(ph == 1, t == 0))
    def _():
        cnt = cnt_ref[0:1, :]
        padded = jnp.floor((cnt + (MOE_TILE - 1.0)) * (1.0 / MOE_TILE)) * MOE_TILE
        r = lax.broadcasted_iota(jnp.int32, (128, 128), 0)
        c = lax.broadcasted_iota(jnp.int32, (128, 128), 1)
        off = jnp.dot(jnp.broadcast_to(padded, (8, 128)), (r < c).astype(F32),
                      preferred_element_type=F32, precision=HI)[0:1, :]
        cnt_ref[1:2, :] = off
        end = off + padded
        end_col = jnp.sum(jnp.where(r == c, jnp.broadcast_to(end, (128, 128)), 0.0), axis=1, keepdims=True)
        ended = jnp.logical_and(end_col <= c.astype(F32) * MOE_TILE, r < N_BUCKETS)
        tile_bucket = jnp.sum(ended.astype(F32), axis=0, keepdims=True)
        lane1 = lax.broadcasted_iota(jnp.int32, (1, 128), 1)
        tile_idx = lane1.astype(F32)
        n_valid = jnp.max(jnp.where(lane1 < N_BUCKETS, end, 0.0), axis=1, keepdims=True) * (1.0 / MOE_TILE)
        last = jnp.max(jnp.where(tile_idx < n_valid, tile_bucket, 0.0), axis=1, keepdims=True)
        tb = jnp.where(tile_idx < n_valid, tile_bucket, last)
        g = jnp.floor(tb * (1.0 / N_PAIRS))
        pid = tb - g * N_PAIRS
        a = (pid >= 3.0).astype(F32) + (pid >= 5.0).astype(F32)
        b = pid + 1.0 - 2.0 * (a >= 1.0).astype(F32) - (a >= 2.0).astype(F32)
        rows = [EPG * g + a, EPG * g + b, jnp.broadcast_to(n_valid, (1, 128)), jnp.zeros((5, 128), F32)]
        meta_ref[...] = jnp.concatenate(rows, axis=0).astype(jnp.int32)

    @pl.when(ph == 1)
    def _():
        base = cnt_ref[1:2, :]
        oh16 = onehot.astype(BF16)
        same = _dot_nt(oh16, oh16)
        rr = lax.broadcasted_iota(jnp.int32, (TILE, TILE), 0)
        cc = lax.broadcasted_iota(jnp.int32, (TILE, TILE), 1)
        rank = jnp.sum(jnp.where(rr < cc, same, 0.0), axis=0, keepdims=True)
        base_row = lax.dot_general(jnp.broadcast_to(base, (8, 128)), onehot, (((1,), (1,)), ((), ())),
                                   preferred_element_type=F32, precision=HI)[0:1, :]
        dest_ref[...] = (base_row + rank).astype(jnp.int32)
        cnt_ref[1:2, :] = base + tile_cnt


def _moe_dispatch(route):
    dest, meta = pl.pallas_call(
        _dispatch_kernel,
        grid=(2, N_TILES),
        in_specs=[pl.BlockSpec((TILE, 128), lambda ph, t: (t, 0))],
        out_specs=[pl.BlockSpec((None, 1, TILE), lambda ph, t: (t * ph, 0, 0)),
                   pl.BlockSpec((8, 128), lambda ph, t: (0, 0))],
        out_shape=[jax.ShapeDtypeStruct((N_TILES, 1, TILE), jnp.int32),
                   jax.ShapeDtypeStruct((8, 128), jnp.int32)],
        scratch_shapes=[pltpu.VMEM((8, 128), F32)],
        compiler_params=_cparams(2),
        name="moe_dispatch",
    )(route)
    return dest.reshape(N_ROWS), meta[0:3].reshape(3 * 128)


def _dma_params():
    return pltpu.CompilerParams(dimension_semantics=("arbitrary",), vmem_limit_bytes=VMEM_LIMIT,
                                disable_bounds_checks=True)


def _moe_scatter_kernel(dest_ref, h_ref, hs_init_ref, hs_ref, sem):
    del hs_init_ref
    base = pl.program_id(0) * TILE

    def issue(i, carry):
        d = dest_ref[base + i]
        pltpu.make_async_copy(h_ref.at[pl.ds(i, 1)], hs_ref.at[pl.ds(d, 1)], sem).start()
        return carry

    lax.fori_loop(0, TILE, issue, 0)
    pltpu.make_async_copy(h_ref, hs_ref.at[pl.ds(0, TILE)], sem).wait()


def _moe_scatter(dest, hp):
    hs_init = jnp.zeros((N_SLOTS, ROW_W), jnp.uint32)
    return pl.pallas_call(
        _moe_scatter_kernel,
        grid_spec=pltpu.PrefetchScalarGridSpec(
            num_scalar_prefetch=1,
            grid=(N_TILES,),
            in_specs=[pl.BlockSpec((TILE, ROW_W), lambda t, dest: (t, 0)),
                      pl.BlockSpec(memory_space=pl.ANY)],
            out_specs=pl.BlockSpec(memory_space=pl.ANY),
            scratch_shapes=[pltpu.SemaphoreType.DMA(())],
        ),
        out_shape=jax.ShapeDtypeStruct((N_SLOTS, ROW_W), jnp.uint32),
        input_output_aliases={2: 0},
        compiler_params=_dma_params(),
        name="moe_scatter",
    )(dest, hp, hs_init)


def _moe_ffn_kernel(meta_ref, hs_ref, wgl_ref, wul_ref, wdl_ref, wgh_ref, wuh_ref, wdh_ref, ys_ref):
    t = pl.program_id(0)
    n_valid = meta_ref[2 * 128]

    @pl.when(t < n_valid)
    def _():
        w = hs_ref[...]
        x = _unpack_bf16_pairs(w[:, :HALF_D]).astype(BF16)
        rw = pltpu.bitcast(w[:, HALF_D:], F32)

        def expert(wg_ref, wu_ref, wd_ref, weight):
            a = _silu(_dot(x, wg_ref[...])) * _dot(x, wu_ref[...])
            return _dot((a * weight).astype(BF16), wd_ref[...])

        y = expert(wgl_ref, wul_ref, wdl_ref, rw[:, 1:2]) + expert(wgh_ref, wuh_ref, wdh_ref, rw[:, 2:3])
        ys_ref[...] = _pack_bf16_pairs(y)

    @pl.when(t >= n_valid)
    def _():
        ys_ref[...] = jnp.zeros_like(ys_ref)


def _moe_ffn(layer, meta, hs, w_gate_bf, w_up_bf, w_down_bf):
    lo = lambda t, m: (layer, m[t], 0, 0)
    hi = lambda t, m: (layer, m[128 + t], 0, 0)
    up = lambda im: pl.BlockSpec((None, None, D_MODEL, EXPERT_FF), im)
    down = lambda im: pl.BlockSpec((None, None, EXPERT_FF, D_MODEL), im)
    return pl.pallas_call(
        _moe_ffn_kernel,
        grid_spec=pltpu.PrefetchScalarGridSpec(
            num_scalar_prefetch=1,
            grid=(MOE_MAX_TILES,),
            in_specs=[pl.BlockSpec((MOE_TILE, ROW_W), lambda t, m: (jnp.minimum(t, m[2 * 128] - 1), 0)),
                      up(lo), up(lo), down(lo), up(hi), up(hi), down(hi)],
            out_specs=pl.BlockSpec((MOE_TILE, HALF_D), lambda t, m: (t, 0)),
        ),
        out_shape=jax.ShapeDtypeStruct((N_SLOTS, HALF_D), jnp.uint32),
        compiler_params=_cparams(1),
        name="moe_ffn",
    )(meta, hs, w_gate_bf, w_up_bf, w_down_bf, w_gate_bf, w_up_bf, w_down_bf)


def _moe_combine_kernel(dest_ref, x1_ref, mod_ref, ys_ref, out_ref, buf, sems):
    t = pl.program_id(0)
    nt = pl.num_programs(0)

    def issue(tile, slot):
        def body(i, carry):
            d = dest_ref[tile * TILE + i]
            pltpu.make_async_copy(ys_ref.at[pl.ds(d, 1)], buf.at[slot, pl.ds(i, 1)], sems.at[slot]).start()
            return carry
        lax.fori_loop(0, TILE, body, 0)

    @pl.when(t == 0)
    def _():
        issue(0, 0)

    @pl.when(t + 1 < nt)
    def _():
        issue(t + 1, (t + 1) % 2)

    slot = t % 2
    pltpu.make_async_copy(ys_ref.at[pl.ds(0, TILE)], buf.at[slot], sems.at[slot]).wait()
    out_ref[...] = x1_ref[...] + mod_ref[5:6, :] * _unpack_bf16_pairs(buf[slot])


def _moe_combine(layer, dest, x1, mod, ys):
    return pl.pallas_call(
        _moe_combine_kernel,
        grid_spec=pltpu.PrefetchScalarGridSpec(
            num_scalar_prefetch=1,
            grid=(N_TILES,),
            in_specs=[pl.BlockSpec((TILE, D_MODEL), lambda t, dest: (t, 0)),
                      pl.BlockSpec((None, None, 6, D_MODEL), lambda t, dest: (layer, _cond_of_tile(t), 0, 0)),
                      pl.BlockSpec(memory_space=pl.ANY)],
            out_specs=pl.BlockSpec((TILE, D_MODEL), lambda t, dest: (t, 0)),
            scratch_shapes=[pltpu.VMEM((2, TILE, HALF_D), jnp.uint32), pltpu.SemaphoreType.DMA((2,))],
        ),
        out_shape=jax.ShapeDtypeStruct((N_ROWS, D_MODEL), F32),
        compiler_params=_dma_params(),
        name="moe_combine",
    )(dest, x1, mod, ys)


def _block_diag_ones(n, group):
    i = np.arange(n) // group
    return jnp.asarray((i[:, None] == i[None, :]).astype(np.float32), dtype=BF16)


def _rope_tables():
    pos = np.arange(DEC_SEQ)
    rows = (pos // GRID_W).astype(np.float32)
    cols = (pos % GRID_W).astype(np.float32)
    inv_freq = (ROPE_BASE ** (-(np.arange(ROPE_PAIRS, dtype=np.float32) / ROPE_PAIRS))).astype(np.float32)
    ang_r = rows[:, None] * inv_freq[None, :]
    ang_c = cols[:, None] * inv_freq[None, :]
    cos64 = np.concatenate([np.cos(ang_r), np.cos(ang_r), np.cos(ang_c), np.cos(ang_c)], axis=1)
    sin64 = np.concatenate([-np.sin(ang_r), np.sin(ang_r), -np.sin(ang_c), np.sin(ang_c)], axis=1)
    return (jnp.asarray(np.tile(cos64, (1, 2)), F32), jnp.asarray(np.tile(sin64, (1, 2)), F32))


def _state_to_blockdiag_t(s):
    eye = jnp.eye(HEADS, dtype=F32)
    return jnp.einsum('bzhdv,hg->bzhvgd', s.astype(F32), eye).reshape(s.shape[0], 2, MIX_G, MIX_G)


def _blockdiag_t_to_state(st):
    s6 = st.reshape(st.shape[0], 2, HEADS, HEAD_D, HEADS, HEAD_D)
    diag = jnp.stack([s6[:, :, h, :, h, :] for h in range(HEADS)], axis=2)
    return jnp.swapaxes(diag, -1, -2)


def _state_to_blockdiag(s):
    eye = jnp.eye(HEADS, dtype=F32)
    return jnp.einsum('bzhdv,hg->bzhdgv', s.astype(F32), eye).reshape(s.shape[0], 2, MIX_G, MIX_G)


def _blockdiag_to_state(st):
    s6 = st.reshape(st.shape[0], 2, HEADS, HEAD_D, HEADS, HEAD_D)
    return jnp.stack([s6[:, :, h, :, h, :] for h in range(HEADS)], axis=2)


def _mixers(latent, layer, x, consts, params, ret_s0, hg_s0, cache_k, cache_v):
    (mod, lb_all, bd512, bd256, rope_cos, rope_sin, ret_tables) = consts
    n_seq = DEC_BATCH if latent else BATCH
    seq_len = DEC_SEQ if latent else SEQ
    nt = seq_len // TILE
    proj = _in_projection(latent, layer, x, mod, params['norm1'], params['w_in'], params['da_qn'],
                          params['da_kn'], lb_all, bd512, rope_cos, rope_sin)
    (rq, rk, rv, rsg, dq, dk, dv, hq, hkf, hkb, hv, hgf, hgb, hsg) = proj[:14]
    ret_of, ret_ob, ret_fin = _retention(n_seq, nt, rq, rk, rv, ret_s0, ret_tables,
                                         "retention_lat" if latent else "retention_ctx")
    hg_of, hg_ob, hg_fin = _gated_scan(n_seq, nt, hq, hkf, hkb, hv, hgf, hgb, hg_s0,
                                       "hgrn_scan_lat" if latent else "hgrn_scan_ctx")
    da_o = _diff_attention(layer, n_seq, seq_len, dq, dk, dv, cache_k, cache_v,
                           params['da_lambda'], params['da_subln'])
    streams = (ret_of, ret_ob, rsg, da_o, hg_of, hg_ob, hsg)
    extras = None if latent else (proj[14], proj[15], ret_fin, hg_fin)
    return streams, extras


def _trunk_layer(layer, x, consts, params, lat_ret_s0, lat_hg_s0, zero_state, cache_k, cache_v):
    mod, bd256 = consts[0], consts[3]
    ctx_streams, extras = _mixers(False, layer, x, consts, params, zero_state, zero_state, None, None)
    lat_streams, _ = _mixers(True, layer, x, consts, params, lat_ret_s0, lat_hg_s0, cache_k, cache_v)
    x1, hp, route = _out_projection(layer, x, mod, params['norm2'], ctx_streams, lat_streams, params['hg_norm'],
                                    bd256, params['w_out'], params['w_route'], params['b_route'])
    dest, meta = _moe_dispatch(route)
    hs = _moe_scatter(dest, hp)
    ys = _moe_ffn(layer, meta, hs, params['w_gate'], params['w_up'], params['w_down'])
    return _moe_combine(layer, dest, x1, mod, ys), extras


def kernel(x_prompt, x_sample, cache_k, cache_v, state_ret, state_hgrn, c, c_ctx, norm1, norm2, w_mod, b_mod,
           w_in, w_out, da_qn, da_kn, da_lambda, da_subln, hg_lb, hg_norm, w_group, b_group, w_router,
           b_router, w_gate, w_up, w_down):
    cond = jnp.zeros((N_COND, D_MODEL), F32).at[0].set(c_ctx).at[1:1 + DEC_BATCH].set(c)
    mod, lb_all = _modulation(cond, w_mod, b_mod, hg_lb)
    mod = mod.reshape(DEPTH, N_COND, 6, D_MODEL)
    rope_cos, rope_sin = _rope_tables()
    consts = (mod, lb_all.reshape(DEPTH, 1, MIX_G), _block_diag_ones(DA_W, HEAD_D),
              _block_diag_ones(MIX_G, HEAD_D), rope_cos, rope_sin, _retention_tables())
    pad = jnp.zeros((DEPTH, D_MODEL, 128 - N_EXPERTS - N_GROUPS), F32)
    params = {
        'norm1': norm1.reshape(DEPTH, 1, D_MODEL), 'norm2': norm2.reshape(DEPTH, 1, D_MODEL),
        'w_in': w_in.astype(BF16), 'w_out': w_out.astype(BF16),
        'da_qn': jnp.tile(da_qn, (1, DA_W // HEAD_D)).reshape(DEPTH, 1, DA_W),
        'da_kn': jnp.tile(da_kn, (1, DA_W // HEAD_D)).reshape(DEPTH, 1, DA_W),
        'da_lambda': da_lambda, 'da_subln': da_subln.reshape(DEPTH, 1, 128),
        'hg_norm': jnp.tile(hg_norm, (1, HEADS)).reshape(DEPTH, 1, MIX_G),
        'w_route': jnp.concatenate([w_router, w_group, pad], axis=-1),
        'b_route': jnp.concatenate([b_router, b_group, pad[:, 0, :]], axis=-1).reshape(DEPTH, 1, 128),
        'w_gate': w_gate.astype(BF16), 'w_up': w_up.astype(BF16), 'w_down': w_down.astype(BF16),
    }
    ck_bf = cache_k.astype(BF16).reshape(DEC_BATCH, DEPTH, PAST_LEN, DA_W)
    cv_bf = cache_v.astype(BF16).reshape(DEC_BATCH, DEPTH, PAST_LEN, DA_W)
    zero_state = jnp.zeros((BATCH, 2, MIX_G, MIX_G), F32)

    x = jnp.concatenate([x_prompt.reshape(N_CTX_ROWS, D_MODEL), x_sample.reshape(N_LAT_ROWS, D_MODEL)], axis=0)
    ks_out, vs_out, rets_out, hgs_out = [], [], [], []
    for l in range(DEPTH):
        x, (k_new, v_new, ret_fin, hg_fin) = _trunk_layer(
            l, x, consts, params, _state_to_blockdiag(state_ret[:, l]), _state_to_blockdiag_t(state_hgrn[:, l]),
            zero_state, ck_bf, cv_bf)
        ks_out.append(k_new.reshape(BATCH, SEQ, HEADS, 2, HEAD_D))
        vs_out.append(v_new.reshape(BATCH, SEQ, HEADS, 2 * HEAD_D))
        rets_out.append(_blockdiag_to_state(ret_fin))
        hgs_out.append(_blockdiag_t_to_state(hg_fin))
    yp, ys = x[:N_CTX_ROWS], x[N_CTX_ROWS:]
    return (yp.reshape(BATCH, SEQ, D_MODEL), ys.reshape(DEC_BATCH, DEC_SEQ, D_MODEL),
            jnp.stack(ks_out, axis=1), jnp.stack(vs_out, axis=1),
            jnp.stack(rets_out, axis=1), jnp.stack(hgs_out, axis=1))
```

```python
import functools
import math

import numpy as np
import jax
import jax.numpy as jnp
from jax import lax
from jax.experimental import pallas as pl
from jax.experimental.pallas import tpu as pltpu

F32 = jnp.float32
BF16 = jnp.bfloat16

D_MODEL = 1024
BATCH = 16
SEQ = 256
DEPTH = 4
DEC_BATCH = 4
DEC_SEQ = 4096
PAST_LEN = 256
GRID_W = 64
HEADS = 4
HEAD_D = 64
MIX_G = HEADS * HEAD_D
DA_W = 512
IN_COLS = 3840
ROPE_BASE = 10000.0
ROPE_PAIRS = 16
RMS_EPS = 1e-6
N_GROUPS = 4
EPG = 4
N_EXPERTS = 16
EXPERT_FF = 512

TILE = 256
N_CTX_ROWS = BATCH * SEQ
N_LAT_ROWS = DEC_BATCH * DEC_SEQ
N_ROWS = N_CTX_ROWS + N_LAT_ROWS
CTX_TILES = N_CTX_ROWS // TILE
LAT_TILES_PER_SEQ = DEC_SEQ // TILE
N_TILES = N_ROWS // TILE
N_COND = 8

N_PAIRS = EPG * (EPG - 1) // 2
N_BUCKETS = N_GROUPS * N_PAIRS
MOE_TILE = 256
MOE_MAX_TILES = N_TILES + N_BUCKETS
DISPATCH_TILES = 4
N_SLOTS = MOE_MAX_TILES * MOE_TILE
ROW_W = D_MODEL + 128

CHUNK = 64
SUB = 16
NSUB = CHUNK // SUB
NCHUNK = TILE // CHUNK

VMEM_LIMIT = 48 * 1024 * 1024
NEG_BIG = -1e30
LOG2_E = 1.4426950408889634

HI = lax.Precision.HIGHEST


def _cparams(n_axes):
    return pltpu.CompilerParams(dimension_semantics=("arbitrary",) * n_axes,
                                vmem_limit_bytes=VMEM_LIMIT)


def _dot(a, b):
    return jnp.dot(a, b, preferred_element_type=F32)


def _dot_nt(a, b):
    return lax.dot_general(a, b, (((1,), (1,)), ((), ())), preferred_element_type=F32)


def _dot_tn(a, b):
    return lax.dot_general(a, b, (((0,), (0,)), ((), ())), preferred_element_type=F32)


def _split_dot(x, m_bf16):
    hi = x.astype(BF16)
    lo = (x - hi.astype(F32)).astype(BF16)
    return _dot(hi, m_bf16) + _dot(lo, m_bf16)


def _silu(x):
    return x * (1.0 / (1.0 + jnp.exp(-x)))


def _sigmoid(x):
    return 1.0 / (1.0 + jnp.exp(-x))


def _mod_kernel(cond_ref, w_ref, b_ref, lb_ref, mod_ref, lbo_ref):
    c = cond_ref[...]
    m = jnp.dot(_silu(c), w_ref[...], preferred_element_type=F32, precision=HI)
    mod_ref[...] = m + b_ref[...]
    z = lb_ref[...]
    zmax = jnp.max(z, axis=0, keepdims=True)
    e = jnp.exp(z - zmax)
    p = e / jnp.sum(e, axis=0, keepdims=True)
    rows = [jnp.zeros_like(p[0:1])]
    for l in range(1, DEPTH):
        rows.append(rows[-1] + p[l:l + 1])
    lbo_ref[...] = jnp.concatenate(rows, axis=0)


def _modulation(cond, w_mod, b_mod, hg_lb):
    nblk = 6
    return pl.pallas_call(
        _mod_kernel,
        grid=(DEPTH, nblk),
        in_specs=[
            pl.BlockSpec((N_COND, D_MODEL), lambda l, j: (0, 0)),
            pl.BlockSpec((None, D_MODEL, D_MODEL), lambda l, j: (l, 0, j)),
            pl.BlockSpec((None, 1, D_MODEL), lambda l, j: (l, 0, j)),
            pl.BlockSpec((DEPTH, MIX_G), lambda l, j: (0, 0)),
        ],
        out_specs=[
            pl.BlockSpec((None, N_COND, D_MODEL), lambda l, j: (l, 0, j)),
            pl.BlockSpec((DEPTH, MIX_G), lambda l, j: (0, 0)),
        ],
        out_shape=[jax.ShapeDtypeStruct((DEPTH, N_COND, 6 * D_MODEL), F32),
                   jax.ShapeDtypeStruct((DEPTH, MIX_G), F32)],
        compiler_params=_cparams(2),
        name="modulation",
    )(cond, w_mod, b_mod.reshape(DEPTH, 1, 6 * D_MODEL), hg_lb)


def _group_mean_sq(x, bd_ref, group):
    return _split_dot(x * x, bd_ref[...]) * (1.0 / group)


def _swap16(x):
    w = x.shape[-1]
    lane = lax.broadcasted_iota(jnp.int32, x.shape, x.ndim - 1)
    up = pltpu.roll(x, w - 16, x.ndim - 1)
    dn = pltpu.roll(x, 16, x.ndim - 1)
    return jnp.where((lane % 32) < 16, up, dn)


def _inproj_kernel(latent, x_ref, mod_ref, n1_ref, w_ref, qn_ref, kn_ref, lb_ref, bd_ref, *rest):
    if latent:
        cos_ref, sin_ref = rest[0], rest[1]
        outs = rest[2:]
    else:
        outs = rest
    (rq_ref, rk_ref, rv_ref, rsg_ref, dq_ref, dk_ref, dv_ref,
     hq_ref, hkf_ref, hkb_ref, hv_ref, hgf_ref, hgb_ref, hsg_ref) = outs[:14]

    x = x_ref[...]
    ms = jnp.mean(x * x, axis=-1, keepdims=True)
    shift1 = mod_ref[0:1, :]
    scale1 = mod_ref[1:2, :]
    h = x * lax.rsqrt(ms + RMS_EPS) * n1_ref[...] * (1.0 + scale1) + shift1
    y = _dot(h.astype(BF16), w_ref[...])

    rq_ref[...] = y[:, 0:256].astype(BF16)
    rk_ref[...] = (y[:, 256:512] * (HEAD_D ** -0.5)).astype(BF16)
    rv_ref[...] = y[:, 512:768].astype(BF16)
    rsg_ref[...] = _silu(y[:, 768:1024]).astype(BF16)

    dq = y[:, 1024:1536]
    dk = y[:, 1536:2048]
    qn = dq * lax.rsqrt(_group_mean_sq(dq, bd_ref, HEAD_D) + RMS_EPS) * qn_ref[...]
    kn = dk * lax.rsqrt(_group_mean_sq(dk, bd_ref, HEAD_D) + RMS_EPS) * kn_ref[...]
    if latent:
        cos = jnp.concatenate([cos_ref[...]] * 4, axis=1)
        sin = jnp.concatenate([sin_ref[...]] * 4, axis=1)
        qr = qn * cos + _swap16(qn) * sin
        kr = kn * cos + _swap16(kn) * sin
    else:
        qr, kr = qn, kn
        ck_ref, cv_ref = outs[14], outs[15]
        ck_ref[...] = kn
        cv_ref[...] = y[:, 2048:2560]
    dq_ref[...] = (qr * (HEAD_D ** -0.5 * LOG2_E)).astype(BF16)
    dk_ref[...] = kr.astype(BF16)
    dv_ref[...] = y[:, 2048:2560].astype(BF16)

    lb = lb_ref[...]
    f_f = lb + (1.0 - lb) * _sigmoid(y[:, 2816:3072])
    f_b = lb + (1.0 - lb) * _sigmoid(y[:, 3072:3328])
    hq_ref[...] = y[:, 2560:2816].astype(BF16)
    hkf_ref[...] = (1.0 - f_f).astype(BF16)
    hkb_ref[...] = (1.0 - f_b).astype(BF16)
    hgf_ref[...] = jnp.log2(f_f)
    hgb_ref[...] = jnp.log2(f_b)
    hv_ref[...] = y[:, 3328:3584].astype(BF16)
    hsg_ref[...] = _silu(y[:, 3584:3840]).astype(BF16)


def _in_projection(latent, layer, x, mod, norm1, w_in_bf, qn_t, kn_t, lb_all, bd512, rope_cos, rope_sin):
    tile0 = CTX_TILES if latent else 0
    ntiles = (N_TILES - CTX_TILES) if latent else CTX_TILES
    nrows = ntiles * TILE

    in_specs = [
        pl.BlockSpec((TILE, D_MODEL), lambda t: (tile0 + t, 0)),
        pl.BlockSpec((None, None, 6, D_MODEL), lambda t: (layer, _cond_of_tile(tile0 + t), 0, 0)),
        pl.BlockSpec((None, 1, D_MODEL), lambda t: (layer, 0, 0)),
        pl.BlockSpec((None, D_MODEL, IN_COLS), lambda t: (layer, 0, 0)),
        pl.BlockSpec((None, 1, DA_W), lambda t: (layer, 0, 0)),
        pl.BlockSpec((None, 1, DA_W), lambda t: (layer, 0, 0)),
        pl.BlockSpec((None, 1, MIX_G), lambda t: (layer, 0, 0)),
        pl.BlockSpec((DA_W, DA_W), lambda t: (0, 0)),
    ]
    args = [x, mod, norm1, w_in_bf, qn_t, kn_t, lb_all, bd512]
    if latent:
        in_specs += [pl.BlockSpec((TILE, 128), lambda t: (t % LAT_TILES_PER_SEQ, 0))] * 2
        args += [rope_cos, rope_sin]

    def o(width, dtype):
        return jax.ShapeDtypeStruct((nrows, width), dtype), pl.BlockSpec((TILE, width), lambda t: (t, 0))

    outs = [o(256, BF16), o(256, BF16), o(256, BF16), o(256, BF16),
            o(512, BF16), o(512, BF16), o(512, BF16),
            o(256, BF16), o(256, BF16), o(256, BF16), o(256, BF16), o(256, F32), o(256, F32), o(256, BF16)]
    if not latent:
        outs += [o(512, F32), o(512, F32)]
    return pl.pallas_call(
        functools.partial(_inproj_kernel, latent),
        grid=(ntiles,),
        in_specs=in_specs,
        out_specs=[s for _, s in outs],
        out_shape=[s for s, _ in outs],
        compiler_params=_cparams(1),
        name="in_projection_lat" if latent else "in_projection_ctx",
    )(*args)


def _head_masks():
    lane = lax.broadcasted_iota(jnp.int32, (1, MIX_G), 1)
    return [(lane // HEAD_D == h).astype(F32) for h in range(HEADS)]


def _scan_chunks(items, st_f, st_b, consts_f, consts_b):
    n = len(items)
    cs = [consts_b if it[4] else consts_f for it in items]
    zero_row = jnp.zeros((1, MIX_G), F32)

    bs = [jnp.dot(cs[x][0], items[x][3], preferred_element_type=F32, precision=HI) for x in range(n)]

    prep = []
    for x in range(n):
        q, k, v, _, rev = items[x]
        hmask = cs[x][4]
        b = bs[x]
        r_rows, e_rows = [], []
        for s in range(NSUB):
            lo, hi = s * SUB, s * SUB + SUB - 1
            if not rev:
                r_rows.append(b[lo - 1:lo, :] if s > 0 else zero_row)
                e_rows.append(b[hi:hi + 1, :])
            else:
                r_rows.append(b[hi + 1:hi + 2, :] if s < NSUB - 1 else zero_row)
                e_rows.append(b[lo:lo + 1, :])
        btot = e_rows[NSUB - 1] if not rev else e_rows[0]
        r_full = jnp.concatenate([jnp.broadcast_to(r, (SUB, MIX_G)) for r in r_rows], axis=0)
        e_full = jnp.concatenate([jnp.broadcast_to(e, (SUB, MIX_G)) for e in e_rows], axis=0)
        bl = b - r_full
        qp = q * jnp.exp2(bl)
        kend = k * jnp.exp2(e_full - b)
        pairs = [(i, j) for i in range(NSUB) for j in range(NSUB) if (j < i if not rev else j > i)]
        lh = jnp.concatenate([qp[i * SUB:(i + 1) * SUB, :] * jnp.exp2(r_rows[i] - e_rows[j])
                              for (i, j) in pairs], axis=0)
        prep.append(dict(
            bl=bl, btot=btot, pairs=pairs,
            q_inter=(qp * jnp.exp2(r_full)).astype(BF16),
            k_state=(kend * jnp.exp2(btot - e_full)).astype(BF16),
            kend=kend.astype(BF16),
            lhs4=jnp.concatenate([lh * hmask[h] for h in range(HEADS)], axis=0).astype(BF16)))

    uts = [_dot_tn(items[x][2], prep[x]['k_state']) for x in range(n)]
    scs = [_dot_nt(prep[x]['lhs4'], prep[x]['kend']) for x in range(n)]

    st_in = []
    for x in range(n):
        rev = items[x][4]
        st = st_b if rev else st_f
        st_in.append(st.astype(BF16))
        st = st * jnp.exp2(prep[x]['btot']) + uts[x] * cs[x][1]
        if rev:
            st_b = st
        else:
            st_f = st
    col = lax.broadcasted_iota(jnp.int32, (SUB, CHUNK), 1) // SUB
    jrow = lax.broadcasted_iota(jnp.int32, (SUB, MIX_G), 0)
    pss, pps, vts = [], [], []
    for x in range(n):
        q, k, v, _, rev = items[x]
        pairs, sc, bl = prep[x]['pairs'], scs[x], prep[x]['bl']
        p_rows = []
        for h in range(HEADS):
            for i in range(NSUB):
                acc = None
                for p, (pi, pj) in enumerate(pairs):
                    if pi != i:
                        continue
                    base = (h * len(pairs) + p) * SUB
                    blk = jnp.where(col == pj, sc[base:base + SUB, :], 0.0)
                    acc = blk if acc is None else acc + blk
                p_rows.append(jnp.zeros((SUB, CHUNK), F32) if acc is None else acc)
        pss.append(jnp.concatenate(p_rows, axis=0).astype(BF16))
        pp, vt = [], []
        for s in range(NSUB):
            sl = slice(s * SUB, (s + 1) * SUB)
            bl_s, q_s, k_s = bl[sl, :], q[sl, :], k[sl, :]
            for i in range(SUB):
                d = bl_s[i:i + 1, :] - bl_s
                keep = (jrow <= i) if not rev else (jrow >= i)
                e = jnp.exp2(jnp.where(keep, d, NEG_BIG))
                pp.append((e * q_s[i:i + 1, :] * k_s).astype(BF16))
            vt.extend([v[sl, :]] * SUB)
        pps.append(jnp.concatenate(pp, axis=0))
        vts.append(jnp.concatenate(vt, axis=0))

    o_int = [_dot_nt(prep[x]['q_inter'], st_in[x]) for x in range(n)]
    fulls = [_dot(pss[x], items[x][2]) for x in range(n)]
    sbs = [_dot(pps[x], cs[x][2]) for x in range(n)]

    ws = [(sbs[x] * vts[x].astype(F32)).astype(BF16) for x in range(n)]
    o_diag = [_dot(cs[x][3], ws[x]) for x in range(n)]

    outs = []
    for x in range(n):
        hmask = cs[x][4]
        o = o_int[x] + o_diag[x]
        for h in range(HEADS):
            o = o + fulls[x][h * CHUNK:(h + 1) * CHUNK, :] * hmask[h]
        outs.append(o)
    return outs, st_f, st_b


def _scan_consts(rev):
    r = lax.broadcasted_iota(jnp.int32, (CHUNK, CHUNK), 0)
    c = lax.broadcasted_iota(jnp.int32, (CHUNK, CHUNK), 1)
    tri = ((c <= r) if not rev else (c >= r)).astype(F32)
    rr = lax.broadcasted_iota(jnp.int32, (MIX_G, MIX_G), 0) // HEAD_D
    cc = lax.broadcasted_iota(jnp.int32, (MIX_G, MIX_G), 1) // HEAD_D
    bd_mask = (rr == cc).astype(F32)
    ones_bd = bd_mask.astype(BF16)
    gi = lax.broadcasted_iota(jnp.int32, (CHUNK, CHUNK * SUB), 0)
    gj = lax.broadcasted_iota(jnp.int32, (CHUNK, CHUNK * SUB), 1) // SUB
    gsel = (gi == gj).astype(BF16)
    return tri, bd_mask, ones_bd, gsel, _head_masks()


def _scan_kernel(nt, qf_ref, kf_ref, vf_ref, gf_ref, qb_ref, kb_ref, vb_ref, gb_ref, s0_ref,
                 of_ref, ob_ref, sfin_ref, st_scr):
    t = pl.program_id(1)

    @pl.when(t == 0)
    def _():
        st_scr[...] = s0_ref[...]

    cf = _scan_consts(False)
    cb = _scan_consts(True)

    items, rows = [], []
    for c in range(NCHUNK):
        rf = slice(c * CHUNK, (c + 1) * CHUNK)
        rb = slice((NCHUNK - 1 - c) * CHUNK, (NCHUNK - c) * CHUNK)
        items.append((qf_ref[rf, :].astype(F32), kf_ref[rf, :].astype(F32), vf_ref[rf, :], gf_ref[rf, :], False))
        items.append((qb_ref[rb, :].astype(F32), kb_ref[rb, :].astype(F32), vb_ref[rb, :], gb_ref[rb, :], True))
        rows += [(of_ref, rf), (ob_ref, rb)]
    outs, st_f, st_b = _scan_chunks(items, st_scr[0], st_scr[1], cf, cb)
    for (ref, sl), o in zip(rows, outs):
        ref[sl, :] = o
    st_scr[0] = st_f
    st_scr[1] = st_b

    @pl.when(t == nt - 1)
    def _():
        sfin_ref[...] = st_scr[...]


def _gated_scan(n_seq, nt, q, k_f, k_b, v, g_f, g_b, s0, name):
    fwd = lambda s, t: (s * nt + t, 0)
    bwd = lambda s, t: (s * nt + (nt - 1 - t), 0)
    blk = lambda im: pl.BlockSpec((TILE, MIX_G), im)
    in_specs = [blk(fwd), blk(fwd), blk(fwd), blk(fwd), blk(bwd), blk(bwd), blk(bwd), blk(bwd),
                pl.BlockSpec((None, 2, MIX_G, MIX_G), lambda s, t: (s, 0, 0, 0))]
    nrows = n_seq * nt * TILE
    return pl.pallas_call(
        functools.partial(_scan_kernel, nt),
        grid=(n_seq, nt),
        in_specs=in_specs,
        out_specs=[pl.BlockSpec((TILE, MIX_G), lambda s, t: (s * nt + t, 0)),
                   pl.BlockSpec((TILE, MIX_G), lambda s, t: (s * nt + (nt - 1 - t), 0)),
                   pl.BlockSpec((None, 2, MIX_G, MIX_G), lambda s, t: (s, 0, 0, 0))],
        out_shape=[jax.ShapeDtypeStruct((nrows, MIX_G), F32),
                   jax.ShapeDtypeStruct((nrows, MIX_G), F32),
                   jax.ShapeDtypeStruct((n_seq, 2, MIX_G, MIX_G), F32)],
        scratch_shapes=[pltpu.VMEM((2, MIX_G, MIX_G), F32)],
        compiler_params=_cparams(2),
        name=name,
    )(q, k_f, v, g_f, q, k_b, v, g_b, s0)


def _retention_tables():
    gam = 1.0 - 2.0 ** (-5.0 - np.arange(HEADS, dtype=np.float64))
    gam_r = gam[::-1]
    i = np.arange(TILE, dtype=np.float64)
    diff = i[:, None] - i[None, :]
    dcomb = np.zeros((HEADS, TILE, TILE))
    for h in range(HEADS):
        lower = np.where(diff > 0, gam[h] ** np.maximum(diff, 0), 0.0)
        upper = np.where(diff < 0, gam_r[h] ** np.maximum(-diff, 0), 0.0)
        dcomb[h] = lower + upper + 2.0 * (diff == 0)
    lanes = lambda per_head: np.repeat(per_head, HEAD_D, axis=-1)
    qd_f = lanes(gam[None, :] ** (i[:, None] + 1.0))
    kd_f = lanes(gam[None, :] ** (TILE - 1.0 - i[:, None]))
    qd_b = lanes(gam_r[None, :] ** (TILE - i[:, None]))
    kd_b = lanes(gam_r[None, :] ** i[:, None])
    blk = (np.arange(MIX_G)[:, None] // HEAD_D) == (np.arange(MIX_G)[None, :] // HEAD_D)
    c_f = np.where(blk, lanes(gam ** TILE)[None, :], 0.0) * np.ones((MIX_G, 1))
    c_b = np.where(blk, lanes(gam_r ** TILE)[None, :], 0.0) * np.ones((MIX_G, 1))
    f = lambda a: jnp.asarray(a, F32)
    return (f(dcomb.reshape(HEADS * TILE, TILE)), f(qd_f), f(kd_f), f(qd_b), f(kd_b), f(c_f), f(c_b))


def _ret_kernel(nt, qf_ref, kf_ref, vf_ref, qb_ref, kb_ref, vb_ref, s0_ref, dcomb_ref, qdf_ref, kdf_ref,
                qdb_ref, kdb_ref, cf_ref, cb_ref, of_ref, ob_ref, sfin_ref, st_scr):
    t = pl.program_id(1)

    @pl.when(t == 0)
    def _():
        st_scr[...] = s0_ref[...]

    lane_head = lax.broadcasted_iota(jnp.int32, (TILE, MIX_G), 1) // HEAD_D
    rr = lax.broadcasted_iota(jnp.int32, (MIX_G, MIX_G), 0) // HEAD_D
    cc = lax.broadcasted_iota(jnp.int32, (MIX_G, MIX_G), 1) // HEAD_D
    same_head = rr == cc

    q = qf_ref[...]
    k = kf_ref[...]
    v = vf_ref[...]
    qs = jnp.concatenate([jnp.where(lane_head == h, q, jnp.zeros_like(q)) for h in range(HEADS)], axis=0)
    p = (_dot_nt(qs, k) * dcomb_ref[...]).astype(BF16)
    full = _dot(p, v)
    o = _dot((q.astype(F32) * qdf_ref[...]).astype(BF16), st_scr[0].astype(BF16))
    for h in range(HEADS):
        o = o + jnp.where(lane_head == h, full[h * TILE:(h + 1) * TILE, :], 0.0)
    of_ref[...] = o
    u = _dot_tn((k.astype(F32) * kdf_ref[...]).astype(BF16), v)
    st_scr[0] = st_scr[0] * cf_ref[...] + jnp.where(same_head, u, 0.0)

    qb = qb_ref[...]
    kb = kb_ref[...]
    vb = vb_ref[...]
    ob_ref[...] = _dot((qb.astype(F32) * qdb_ref[...]).astype(BF16), st_scr[1].astype(BF16))
    ub = _dot_tn((kb.astype(F32) * kdb_ref[...]).astype(BF16), vb)
    st_scr[1] = st_scr[1] * cb_ref[...] + jnp.where(same_head, ub, 0.0)

    @pl.when(t == nt - 1)
    def _():
        sfin_ref[...] = st_scr[...]


def _retention(n_seq, nt, q, k, v, s0, tables, name):
    fwd = lambda s, t: (s * nt + t, 0)
    bwd = lambda s, t: (s * nt + (nt - 1 - t), 0)
    const = lambda s, t: (0, 0)
    blk = lambda im: pl.BlockSpec((TILE, MIX_G), im)
    sq = lambda: pl.BlockSpec((MIX_G, MIX_G), const)
    in_specs = [blk(fwd), blk(fwd), blk(fwd), blk(bwd), blk(bwd), blk(bwd),
                pl.BlockSpec((None, 2, MIX_G, MIX_G), lambda s, t: (s, 0, 0, 0)),
                pl.BlockSpec((HEADS * TILE, TILE), const), sq(), sq(), sq(), sq(), sq(), sq()]
    nrows = n_seq * nt * TILE
    return pl.pallas_call(
        functools.partial(_ret_kernel, nt),
        grid=(n_seq, nt),
        in_specs=in_specs,
        out_specs=[blk(fwd), blk(bwd), pl.BlockSpec((None, 2, MIX_G, MIX_G), lambda s, t: (s, 0, 0, 0))],
        out_shape=[jax.ShapeDtypeStruct((nrows, MIX_G), F32),
                   jax.ShapeDtypeStruct((nrows, MIX_G), F32),
                   jax.ShapeDtypeStruct((n_seq, 2, MIX_G, MIX_G), F32)],
        scratch_shapes=[pltpu.VMEM((2, MIX_G, MIX_G), F32)],
        compiler_params=_cparams(2),
        name=name,
    )(q, k, v, q, k, v, s0, *tables)


def _attn_step(lam_init, has_cache, refs, s_write, s_read):
    if has_cache:
        q_ref, k_ref, ck_ref, v_ref, cv_ref, lam_ref, sub_ref, o_ref = refs
        sw, sw2, mw = s_write
        sr, sr2, mr = s_read
    else:
        q_ref, k_ref, v_ref, lam_ref, sub_ref, o_ref = refs
        sw, mw = s_write
        sr, mr = s_read

    lp = lam_ref[...]
    lam = (jnp.exp(jnp.sum(lp[0:1] * lp[1:2], axis=-1, keepdims=True))
           - jnp.exp(jnp.sum(lp[2:3] * lp[3:4], axis=-1, keepdims=True)) + lam_init)
    v_ext = jnp.concatenate([v_ref[...], jnp.ones(v_ref.shape, BF16)], axis=1)
    if has_cache:
        cv_ext = jnp.concatenate([cv_ref[...], jnp.ones(cv_ref.shape, BF16)], axis=1)
    parts = []
    for c in range(2):
        m = mr[c]
        oe = _dot(jnp.exp2((sr[c] - m).astype(BF16)), v_ext)
        if has_cache:
            oe = oe + _dot(jnp.exp2((sr2[c] - m).astype(BF16)), cv_ext)
        parts.append(oe[:, :128] / oe[:, 128:129])
    o = parts[0] - lam * parts[1]
    ms = jnp.mean(o * o, axis=-1, keepdims=True)
    o_ref[...] = (o * lax.rsqrt(ms + RMS_EPS) * sub_ref[...] * (1.0 - lam_init)).astype(BF16)

    q = q_ref[...]
    k = k_ref[...]
    for c in range(2):
        qc = q[:, c * HEAD_D:(c + 1) * HEAD_D]
        s = _dot_nt(qc, k[:, c * HEAD_D:(c + 1) * HEAD_D])
        m = jnp.max(s, axis=-1, keepdims=True)
        sw[c] = s
        if has_cache:
            s2 = _dot_nt(qc, ck_ref[:, c * HEAD_D:(c + 1) * HEAD_D])
            m = jnp.maximum(m, jnp.max(s2, axis=-1, keepdims=True))
            sw2[c] = s2
        mw[c] = m


def _attn_kernel(lam_init, has_cache, *refs):
    n_in = 8 if has_cache else 6
    io, scr = refs[:n_in], refs[n_in:]
    half = len(scr) // 2
    buf_a, buf_b = scr[:half], scr[half:]
    u = pl.program_id(0)

    @pl.when(u == 0)
    def _():
        for r in buf_b:
            r[...] = jnp.zeros(r.shape, r.dtype)

    @pl.when(u % 2 == 0)
    def _():
        _attn_step(lam_init, has_cache, io, buf_a, buf_b)

    @pl.when(u % 2 == 1)
    def _():
        _attn_step(lam_init, has_cache, io, buf_b, buf_a)


def _diff_attention(layer, n_seq, seq_len, q, k, v, cache_k, cache_v, da_lambda, subln):
    tq = TILE
    nq = seq_len // tq
    n_units = n_seq * HEADS * nq
    has_cache = cache_k is not None
    lam_init = 0.8 - 0.6 * math.exp(-0.3 * layer)

    cur = lambda u: jnp.minimum(u, n_units - 1)
    prev = lambda u: jnp.maximum(u - 1, 0)
    seq_of = lambda w: w // (HEADS * nq)
    head_of = lambda w: (w // nq) % HEADS
    rows_of = lambda w: seq_of(w) * nq + w % nq

    q_spec = pl.BlockSpec((tq, 128), lambda u: (rows_of(cur(u)), head_of(cur(u))))
    k_spec = pl.BlockSpec((seq_len, 128), lambda u: (seq_of(cur(u)), head_of(cur(u))))
    v_spec = pl.BlockSpec((seq_len, 128), lambda u: (seq_of(prev(u)), head_of(prev(u))))
    ck_spec = pl.BlockSpec((None, None, PAST_LEN, 128), lambda u: (seq_of(cur(u)), layer, 0, head_of(cur(u))))
    cv_spec = pl.BlockSpec((None, None, PAST_LEN, 128), lambda u: (seq_of(prev(u)), layer, 0, head_of(prev(u))))
    tail = [pl.BlockSpec((None, 4, HEAD_D), lambda u: (layer, 0, 0)),
            pl.BlockSpec((None, 1, 128), lambda u: (layer, 0, 0))]
    if has_cache:
        in_specs = [q_spec, k_spec, ck_spec, v_spec, cv_spec] + tail
        args = [q, k, cache_k, v, cache_v, da_lambda, subln]
        scratch = [pltpu.VMEM((2, tq, seq_len), F32), pltpu.VMEM((2, tq, PAST_LEN), F32),
                   pltpu.VMEM((2, tq, 1), F32)] * 2
    else:
        in_specs = [q_spec, k_spec, v_spec] + tail
        args = [q, k, v, da_lambda, subln]
        scratch = [pltpu.VMEM((2, tq, seq_len), F32), pltpu.VMEM((2, tq, 1), F32)] * 2
    return pl.pallas_call(
        functools.partial(_attn_kernel, lam_init, has_cache),
        grid=(n_units + 1,),
        in_specs=in_specs,
        out_specs=pl.BlockSpec((tq, 128), lambda u: (rows_of(prev(u)), head_of(prev(u)))),
        out_shape=jax.ShapeDtypeStruct((n_seq * seq_len, DA_W), BF16),
        scratch_shapes=scratch,
        compiler_params=_cparams(1),
        name="diff_attention_lat" if has_cache else "diff_attention_ctx",
    )(*args)


def _route(logits):
    lane = lax.broadcasted_iota(jnp.int32, logits.shape, 1)
    lane_f = lane.astype(F32)
    big = 1e9
    is_g = jnp.logical_and(lane >= N_EXPERTS, lane < N_EXPERTS + N_GROUPS)
    gl = jnp.where(is_g, logits, NEG_BIG)
    gmax = jnp.max(gl, axis=-1, keepdims=True)
    gsel = jnp.min(jnp.where(gl == gmax, lane_f, big), axis=-1, keepdims=True) - N_EXPERTS
    g_w = 1.0 / jnp.sum(jnp.exp(gl - gmax), axis=-1, keepdims=True)
    in_grp = jnp.logical_and(lane < N_EXPERTS, (lane // EPG).astype(F32) == gsel)
    el = jnp.where(in_grp, logits, NEG_BIG)
    v1 = jnp.max(el, axis=-1, keepdims=True)
    i1 = jnp.min(jnp.where(el == v1, lane_f, big), axis=-1, keepdims=True)
    el2 = jnp.where(lane_f == i1, NEG_BIG, el)
    v2 = jnp.max(el2, axis=-1, keepdims=True)
    i2 = jnp.min(jnp.where(el2 == v2, lane_f, big), axis=-1, keepdims=True)
    t = jnp.exp(v2 - v1)
    w1 = g_w / (1.0 + t)
    w2 = w1 * t
    first_lo = i1 < i2
    a = jnp.minimum(i1, i2) - EPG * gsel
    b = jnp.maximum(i1, i2) - EPG * gsel
    bucket = gsel * N_PAIRS + a * (7.0 - a) * 0.5 + (b - a - 1.0)
    w_lo = jnp.where(first_lo, w1, w2)
    w_hi = jnp.where(first_lo, w2, w1)
    return jnp.where(lane == 0, bucket, jnp.where(lane == 1, w_lo, jnp.where(lane == 2, w_hi, 0.0)))


def _outproj_body(x_ref, mod_ref, n2_ref, streams, hgn_ref, bd_ref, wout_ref, wr_ref, br_ref,
                  x1_ref, hp_ref, route_ref):
    rof_ref, rob_ref, rsg_ref, da_ref, hof_ref, hob_ref, hsg_ref = streams
    ro = rof_ref[...] + rob_ref[...]
    ro = ro * lax.rsqrt(_group_mean_sq(ro, bd_ref, HEAD_D) + RMS_EPS) * rsg_ref[...].astype(F32)
    ho = hof_ref[...] + hob_ref[...]
    ho = ho * lax.rsqrt(_group_mean_sq(ho, bd_ref, HEAD_D) + RMS_EPS) * hgn_ref[...] * hsg_ref[...].astype(F32)
    mix = jnp.concatenate([ro.astype(BF16), da_ref[...], ho.astype(BF16)], axis=1)
    mixed = _dot(mix, wout_ref[...])
    gate1 = mod_ref[2:3, :]
    shift2 = mod_ref[3:4, :]
    scale2 = mod_ref[4:5, :]
    x1 = x_ref[...] + gate1 * mixed
    ms = jnp.mean(x1 * x1, axis=-1, keepdims=True)
    h2 = x1 * lax.rsqrt(ms + RMS_EPS) * n2_ref[...] * (1.0 + scale2) + shift2
    logits = jnp.dot(h2, wr_ref[...], preferred_element_type=F32, precision=HI) + br_ref[...]
    route = _route(logits)
    x1_ref[...] = x1
    route_ref[...] = route
    hp_ref[...] = jnp.concatenate([h2, route], axis=1)


def _outproj_kernel(x_ref, mod_ref, n2_ref, *rest):
    ctx_streams, lat_streams, tail = rest[0:7], rest[7:14], rest[14:]
    t = pl.program_id(0)

    @pl.when(t < CTX_TILES)
    def _():
        _outproj_body(x_ref, mod_ref, n2_ref, ctx_streams, *tail)

    @pl.when(t >= CTX_TILES)
    def _():
        _outproj_body(x_ref, mod_ref, n2_ref, lat_streams, *tail)


def _cond_of_tile(t):
    return jnp.where(t < CTX_TILES, 0, 1 + (t - CTX_TILES) // LAT_TILES_PER_SEQ)


def _out_projection(layer, x, mod, norm2, ctx_streams, lat_streams, hgn_t, bd256, w_out_bf, w_route, b_route):
    row = lambda t: (t, 0)
    ctx_row = lambda t: (jnp.minimum(t, CTX_TILES - 1), 0)
    lat_row = lambda t: (jnp.maximum(t - CTX_TILES, 0), 0)
    lay = lambda t: (layer, 0, 0)
    widths = (MIX_G, MIX_G, MIX_G, DA_W, MIX_G, MIX_G, MIX_G)
    in_specs = [
        pl.BlockSpec((TILE, D_MODEL), row),
        pl.BlockSpec((None, None, 6, D_MODEL), lambda t: (layer, _cond_of_tile(t), 0, 0)),
        pl.BlockSpec((None, 1, D_MODEL), lay),
    ]
    in_specs += [pl.BlockSpec((TILE, w), ctx_row) for w in widths]
    in_specs += [pl.BlockSpec((TILE, w), lat_row) for w in widths]
    in_specs += [
        pl.BlockSpec((None, 1, MIX_G), lay),
        pl.BlockSpec((MIX_G, MIX_G), lambda t: (0, 0)),
        pl.BlockSpec((None, D_MODEL, D_MODEL), lay),
        pl.BlockSpec((None, D_MODEL, 128), lay),
        pl.BlockSpec((None, 1, 128), lay),
    ]
    return pl.pallas_call(
        _outproj_kernel,
        grid=(N_TILES,),
        in_specs=in_specs,
        out_specs=[pl.BlockSpec((TILE, D_MODEL), row), pl.BlockSpec((TILE, ROW_W), row),
                   pl.BlockSpec((TILE, 128), row)],
        out_shape=[jax.ShapeDtypeStruct((N_ROWS, D_MODEL), F32), jax.ShapeDtypeStruct((N_ROWS, ROW_W), F32),
                   jax.ShapeDtypeStruct((N_ROWS, 128), F32)],
        compiler_params=_cparams(1),
        name="out_projection",
    )(x, mod, norm2, *ctx_streams, *lat_streams, hgn_t, bd256, w_out_bf, w_route, b_route)


def _dispatch_kernel(route_ref, dest_ref, meta_ref, cnt_ref):
    ph = pl.program_id(0)
    t = pl.program_id(1)
    lane = lax.broadcasted_iota(jnp.int32, (TILE, 128), 1).astype(F32)
    onehots = [(lane == route_ref[j * TILE:(j + 1) * TILE, 0:1]).astype(F32) for j in range(DISPATCH_TILES)]
    tile_cnts = [jnp.sum(oh, axis=0, keepdims=True) for oh in onehots]

    @pl.when(jnp.logical_and(ph == 0, t == 0))
    def _():
        cnt_ref[...] = jnp.zeros_like(cnt_ref)

    @pl.when(ph == 0)
    def _():
        cnt_ref[0:1, :] += sum(tile_cnts[1:], tile_cnts[0])

    @pl.when(jnp.logical_and(ph == 1, t == 0))
    def _():
        cnt = cnt_ref[0:1, :]
        padded = jnp.floor((cnt + (MOE_TILE - 1.0)) * (1.0 / MOE_TILE)) * MOE_TILE
        r = lax.broadcasted_iota(jnp.int32, (128, 128), 0)
        c = lax.broadcasted_iota(jnp.int32, (128, 128), 1)
        off = jnp.dot(jnp.broadcast_to(padded, (8, 128)), (r < c).astype(F32),
                      preferred_element_type=F32, precision=HI)[0:1, :]
        cnt_ref[1:2, :] = off
        end = off + padded
        end_col = jnp.sum(jnp.where(r == c, jnp.broadcast_to(end, (128, 128)), 0.0), axis=1, keepdims=True)
        ended = jnp.logical_and(end_col <= c.astype(F32) * MOE_TILE, r < N_BUCKETS)
        tile_bucket = jnp.sum(ended.astype(F32), axis=0, keepdims=True)
        lane1 = lax.broadcasted_iota(jnp.int32, (1, 128), 1)
        tile_idx = lane1.astype(F32)
        n_valid = jnp.max(jnp.where(lane1 < N_BUCKETS, end, 0.0), axis=1, keepdims=True) * (1.0 / MOE_TILE)
        last = jnp.max(jnp.where(tile_idx < n_valid, tile_bucket, 0.0), axis=1, keepdims=True)
        tb = jnp.where(tile_idx < n_valid, tile_bucket, last)
        g = jnp.floor(tb * (1.0 / N_PAIRS))
        pid = tb - g * N_PAIRS
        a = (pid >= 3.0).astype(F32) + (pid >= 5.0).astype(F32)
        b = pid + 1.0 - 2.0 * (a >= 1.0).astype(F32) - (a >= 2.0).astype(F32)
        rows = [EPG * g + a, EPG * g + b, jnp.broadcast_to(n_valid, (1, 128)), jnp.zeros((5, 128), F32)]
        meta_ref[...] = jnp.concatenate(rows, axis=0).astype(jnp.int32)

    @pl.when(ph == 1)
    def _():
        base = cnt_ref[1:2, :]
        rr = lax.broadcasted_iota(jnp.int32, (TILE, TILE), 0)
        cc = lax.broadcasted_iota(jnp.int32, (TILE, TILE), 1)
        for j, onehot in enumerate(onehots):
            oh16 = onehot.astype(BF16)
            same = _dot_nt(oh16, oh16)
            rank = jnp.sum(jnp.where(rr < cc, same, 0.0), axis=0, keepdims=True)
            base_row = lax.dot_general(jnp.broadcast_to(base, (8, 128)), onehot, (((1,), (1,)), ((), ())),
                                       preferred_element_type=F32, precision=HI)[0:1, :]
            dest_ref[j] = (base_row + rank).astype(jnp.int32)
            base = base + tile_cnts[j]
        cnt_ref[1:2, :] = base


def _moe_dispatch(route):
    dest, meta = pl.pallas_call(
        _dispatch_kernel,
        grid=(2, N_TILES // DISPATCH_TILES),
        in_specs=[pl.BlockSpec((DISPATCH_TILES * TILE, 128), lambda ph, t: (t, 0))],
        out_specs=[pl.BlockSpec((DISPATCH_TILES, 1, TILE), lambda ph, t: (t * ph, 0, 0)),
                   pl.BlockSpec((8, 128), lambda ph, t: (0, 0))],
        out_shape=[jax.ShapeDtypeStruct((N_TILES, 1, TILE), jnp.int32),
                   jax.ShapeDtypeStruct((8, 128), jnp.int32)],
        scratch_shapes=[pltpu.VMEM((8, 128), F32)],
        compiler_params=_cparams(2),
        name="moe_dispatch",
    )(route)
    return dest.reshape(N_ROWS), meta[0:3].reshape(3 * 128)


def _dma_params():
    return pltpu.CompilerParams(dimension_semantics=("arbitrary",), vmem_limit_bytes=VMEM_LIMIT,
                                disable_bounds_checks=True)


def _moe_scatter_kernel(dest_ref, h_ref, hs_init_ref, hs_ref, sem):
    del hs_init_ref
    base = pl.program_id(0) * TILE

    def issue(i, carry):
        d = dest_ref[base + i]
        pltpu.make_async_copy(h_ref.at[pl.ds(i, 1)], hs_ref.at[pl.ds(d, 1)], sem).start()
        return carry

    lax.fori_loop(0, TILE, issue, 0, unroll=8)
    pltpu.make_async_copy(h_ref, hs_ref.at[pl.ds(0, TILE)], sem).wait()


def _moe_scatter(dest, hp):
    hs_init = jnp.zeros((N_SLOTS, ROW_W), F32)
    return pl.pallas_call(
        _moe_scatter_kernel,
        grid_spec=pltpu.PrefetchScalarGridSpec(
            num_scalar_prefetch=1,
            grid=(N_TILES,),
            in_specs=[pl.BlockSpec((TILE, ROW_W), lambda t, dest: (t, 0)),
                      pl.BlockSpec(memory_space=pl.ANY)],
            out_specs=pl.BlockSpec(memory_space=pl.ANY),
            scratch_shapes=[pltpu.SemaphoreType.DMA(())],
        ),
        out_shape=jax.ShapeDtypeStruct((N_SLOTS, ROW_W), F32),
        input_output_aliases={2: 0},
        compiler_params=_dma_params(),
        name="moe_scatter",
    )(dest, hp, hs_init)


def _moe_ffn_kernel(meta_ref, hs_ref, wgl_ref, wul_ref, wdl_ref, wgh_ref, wuh_ref, wdh_ref, ys_ref):
    t = pl.program_id(0)
    n_valid = meta_ref[2 * 128]

    @pl.when(t < n_valid)
    def _():
        w = hs_ref[...]
        x = w[:, :D_MODEL].astype(BF16)
        rw = w[:, D_MODEL:]

        def expert(wg_ref, wu_ref, wd_ref, weight):
            a = _silu(_dot(x, wg_ref[...])) * _dot(x, wu_ref[...])
            return _dot((a * weight).astype(BF16), wd_ref[...])

        y = expert(wgl_ref, wul_ref, wdl_ref, rw[:, 1:2]) + expert(wgh_ref, wuh_ref, wdh_ref, rw[:, 2:3])
        ys_ref[...] = y

    @pl.when(t >= n_valid)
    def _():
        ys_ref[...] = jnp.zeros_like(ys_ref)


def _moe_ffn(layer, meta, hs, w_gate_bf, w_up_bf, w_down_bf):
    lo = lambda t, m: (layer, m[t], 0, 0)
    hi = lambda t, m: (layer, m[128 + t], 0, 0)
    up = lambda im: pl.BlockSpec((None, None, D_MODEL, EXPERT_FF), im)
    down = lambda im: pl.BlockSpec((None, None, EXPERT_FF, D_MODEL), im)
    return pl.pallas_call(
        _moe_ffn_kernel,
        grid_spec=pltpu.PrefetchScalarGridSpec(
            num_scalar_prefetch=1,
            grid=(MOE_MAX_TILES,),
            in_specs=[pl.BlockSpec((MOE_TILE, ROW_W), lambda t, m: (jnp.minimum(t, m[2 * 128] - 1), 0)),
                      up(lo), up(lo), down(lo), up(hi), up(hi), down(hi)],
            out_specs=pl.BlockSpec((MOE_TILE, D_MODEL), lambda t, m: (t, 0)),
        ),
        out_shape=jax.ShapeDtypeStruct((N_SLOTS, D_MODEL), F32),
        compiler_params=_cparams(1),
        name="moe_ffn",
    )(meta, hs, w_gate_bf, w_up_bf, w_down_bf, w_gate_bf, w_up_bf, w_down_bf)


def _moe_combine_kernel(dest_ref, x1_ref, mod_ref, ys_ref, out_ref, buf, sems):
    t = pl.program_id(0)
    nt = pl.num_programs(0)

    def issue(tile, slot):
        def body(i, carry):
            d = dest_ref[tile * TILE + i]
            pltpu.make_async_copy(ys_ref.at[pl.ds(d, 1)], buf.at[slot, pl.ds(i, 1)], sems.at[slot]).start()
            return carry
        lax.fori_loop(0, TILE, body, 0, unroll=8)

    @pl.when(t == 0)
    def _():
        issue(0, 0)

    @pl.when(t + 1 < nt)
    def _():
        issue(t + 1, (t + 1) % 2)

    slot = t % 2
    pltpu.make_async_copy(ys_ref.at[pl.ds(0, TILE)], buf.at[slot], sems.at[slot]).wait()
    out_ref[...] = x1_ref[...] + mod_ref[5:6, :] * buf[slot]


def _moe_combine(layer, dest, x1, mod, ys):
    return pl.pallas_call(
        _moe_combine_kernel,
        grid_spec=pltpu.PrefetchScalarGridSpec(
            num_scalar_prefetch=1,
            grid=(N_TILES,),
            in_specs=[pl.BlockSpec((TILE, D_MODEL), lambda t, dest: (t, 0)),
                      pl.BlockSpec((None, None, 6, D_MODEL), lambda t, dest: (layer, _cond_of_tile(t), 0, 0)),
                      pl.BlockSpec(memory_space=pl.ANY)],
            out_specs=pl.BlockSpec((TILE, D_MODEL), lambda t, dest: (t, 0)),
            scratch_shapes=[pltpu.VMEM((2, TILE, D_MODEL), F32), pltpu.SemaphoreType.DMA((2,))],
        ),
        out_shape=jax.ShapeDtypeStruct((N_ROWS, D_MODEL), F32),
        compiler_params=_dma_params(),
        name="moe_combine",
    )(dest, x1, mod, ys)


def _block_diag_ones(n, group):
    i = np.arange(n) // group
    return jnp.asarray((i[:, None] == i[None, :]).astype(np.float32), dtype=BF16)


def _rope_tables():
    pos = np.arange(DEC_SEQ)
    rows = (pos // GRID_W).astype(np.float32)
    cols = (pos % GRID_W).astype(np.float32)
    inv_freq = (ROPE_BASE ** (-(np.arange(ROPE_PAIRS, dtype=np.float32) / ROPE_PAIRS))).astype(np.float32)
    ang_r = rows[:, None] * inv_freq[None, :]
    ang_c = cols[:, None] * inv_freq[None, :]
    cos64 = np.concatenate([np.cos(ang_r), np.cos(ang_r), np.cos(ang_c), np.cos(ang_c)], axis=1)
    sin64 = np.concatenate([-np.sin(ang_r), np.sin(ang_r), -np.sin(ang_c), np.sin(ang_c)], axis=1)
    return (jnp.asarray(np.tile(cos64, (1, 2)), F32), jnp.asarray(np.tile(sin64, (1, 2)), F32))


def _state_to_blockdiag_t(s):
    eye = jnp.eye(HEADS, dtype=F32)
    return jnp.einsum('bzhdv,hg->bzhvgd', s.astype(F32), eye).reshape(s.shape[0], 2, MIX_G, MIX_G)


def _blockdiag_t_to_state(st):
    s6 = st.reshape(st.shape[0], 2, HEADS, HEAD_D, HEADS, HEAD_D)
    diag = jnp.stack([s6[:, :, h, :, h, :] for h in range(HEADS)], axis=2)
    return jnp.swapaxes(diag, -1, -2)


def _state_to_blockdiag(s):
    eye = jnp.eye(HEADS, dtype=F32)
    return jnp.einsum('bzhdv,hg->bzhdgv', s.astype(F32), eye).reshape(s.shape[0], 2, MIX_G, MIX_G)


def _blockdiag_to_state(st):
    s6 = st.reshape(st.shape[0], 2, HEADS, HEAD_D, HEADS, HEAD_D)
    return jnp.stack([s6[:, :, h, :, h, :] for h in range(HEADS)], axis=2)


def _mixers(latent, layer, x, consts, params, ret_s0, hg_s0, cache_k, cache_v):
    (mod, lb_all, bd512, bd256, rope_cos, rope_sin, ret_tables) = consts
    n_seq = DEC_BATCH if latent else BATCH
    seq_len = DEC_SEQ if latent else SEQ
    nt = seq_len // TILE
    proj = _in_projection(latent, layer, x, mod, params['norm1'], params['w_in'], params['da_qn'],
                          params['da_kn'], lb_all, bd512, rope_cos, rope_sin)
    (rq, rk, rv, rsg, dq, dk, dv, hq, hkf, hkb, hv, hgf, hgb, hsg) = proj[:14]
    ret_of, ret_ob, ret_fin = _retention(n_seq, nt, rq, rk, rv, ret_s0, ret_tables,
                                         "retention_lat" if latent else "retention_ctx")
    hg_of, hg_ob, hg_fin = _gated_scan(n_seq, nt, hq, hkf, hkb, hv, hgf, hgb, hg_s0,
                                       "hgrn_scan_lat" if latent else "hgrn_scan_ctx")
    da_o = _diff_attention(layer, n_seq, seq_len, dq, dk, dv, cache_k, cache_v,
                           params['da_lambda'], params['da_subln'])
    streams = (ret_of, ret_ob, rsg, da_o, hg_of, hg_ob, hsg)
    extras = None if latent else (proj[14], proj[15], ret_fin, hg_fin)
    return streams, extras


def _trunk_layer(layer, x, consts, params, lat_ret_s0, lat_hg_s0, zero_state, cache_k, cache_v):
    mod, bd256 = consts[0], consts[3]
    ctx_streams, extras = _mixers(False, layer, x, consts, params, zero_state, zero_state, None, None)
    lat_streams, _ = _mixers(True, layer, x, consts, params, lat_ret_s0, lat_hg_s0, cache_k, cache_v)
    x1, hp, route = _out_projection(layer, x, mod, params['norm2'], ctx_streams, lat_streams, params['hg_norm'],
                                    bd256, params['w_out'], params['w_route'], params['b_route'])
    dest, meta = _moe_dispatch(route)
    hs = _moe_scatter(dest, hp)
    ys = _moe_ffn(layer, meta, hs, params['w_gate'], params['w_up'], params['w_down'])
    return _moe_combine(layer, dest, x1, mod, ys), extras


def kernel(x_prompt, x_sample, cache_k, cache_v, state_ret, state_hgrn, c, c_ctx, norm1, norm2, w_mod, b_mod,
           w_in, w_out, da_qn, da_kn, da_lambda, da_subln, hg_lb, hg_norm, w_group, b_group, w_router,
           b_router, w_gate, w_up, w_down):
    cond = jnp.zeros((N_COND, D_MODEL), F32).at[0].set(c_ctx).at[1:1 + DEC_BATCH].set(c)
    mod, lb_all = _modulation(cond, w_mod, b_mod, hg_lb)
    mod = mod.reshape(DEPTH, N_COND, 6, D_MODEL)
    rope_cos, rope_sin = _rope_tables()
    consts = (mod, lb_all.reshape(DEPTH, 1, MIX_G), _block_diag_ones(DA_W, HEAD_D),
              _block_diag_ones(MIX_G, HEAD_D), rope_cos, rope_sin, _retention_tables())
    pad = jnp.zeros((DEPTH, D_MODEL, 128 - N_EXPERTS - N_GROUPS), F32)
    params = {
        'norm1': norm1.reshape(DEPTH, 1, D_MODEL), 'norm2': norm2.reshape(DEPTH, 1, D_MODEL),
        'w_in': w_in.astype(BF16), 'w_out': w_out.astype(BF16),
        'da_qn': jnp.tile(da_qn, (1, DA_W // HEAD_D)).reshape(DEPTH, 1, DA_W),
        'da_kn': jnp.tile(da_kn, (1, DA_W // HEAD_D)).reshape(DEPTH, 1, DA_W),
        'da_lambda': da_lambda, 'da_subln': da_subln.reshape(DEPTH, 1, 128),
        'hg_norm': jnp.tile(hg_norm, (1, HEADS)).reshape(DEPTH, 1, MIX_G),
        'w_route': jnp.concatenate([w_router, w_group, pad], axis=-1),
        'b_route': jnp.concatenate([b_router, b_group, pad[:, 0, :]], axis=-1).reshape(DEPTH, 1, 128),
        'w_gate': w_gate.astype(BF16), 'w_up': w_up.astype(BF16), 'w_down': w_down.astype(BF16),
    }
    ck_bf = cache_k.astype(BF16).reshape(DEC_BATCH, DEPTH, PAST_LEN, DA_W)
    cv_bf = cache_v.astype(BF16).reshape(DEC_BATCH, DEPTH, PAST_LEN, DA_W)
    zero_state = jnp.zeros((BATCH, 2, MIX_G, MIX_G), F32)

    x = jnp.concatenate([x_prompt.reshape(N_CTX_ROWS, D_MODEL), x_sample.reshape(N_LAT_ROWS, D_MODEL)], axis=0)
    ks_out, vs_out, rets_out, hgs_out = [], [], [], []
    for l in range(DEPTH):
        x, (k_new, v_new, ret_fin, hg_fin) = _trunk_layer(
            l, x, consts, params, _state_to_blockdiag(state_ret[:, l]), _state_to_blockdiag_t(state_hgrn[:, l]),
            zero_state, ck_bf, cv_bf)
        ks_out.append(k_new.reshape(BATCH, SEQ, HEADS, 2, HEAD_D))
        vs_out.append(v_new.reshape(BATCH, SEQ, HEADS, 2 * HEAD_D))
        rets_out.append(_blockdiag_to_state(ret_fin))
        hgs_out.append(_blockdiag_t_to_state(hg_fin))
    yp, ys = x[:N_CTX_ROWS], x[N_CTX_ROWS:]
    return (yp.reshape(BATCH, SEQ, D_MODEL), ys.reshape(DEC_BATCH, DEC_SEQ, D_MODEL),
            jnp.stack(ks_out, axis=1), jnp.stack(vs_out, axis=1),
            jnp.stack(rets_out, axis=1), jnp.stack(hgs_out, axis=1))
```

```python
import functools
import math

import numpy as np
import jax
import jax.numpy as jnp
from jax import lax
from jax.experimental import pallas as pl
from jax.experimental.pallas import tpu as pltpu

F32 = jnp.float32
BF16 = jnp.bfloat16

D_MODEL = 1024
BATCH = 16
SEQ = 256
DEPTH = 4
DEC_BATCH = 4
DEC_SEQ = 4096
PAST_LEN = 256
GRID_W = 64
HEADS = 4
HEAD_D = 64
MIX_G = HEADS * HEAD_D
DA_W = 512
IN_COLS = 3840
ROPE_BASE = 10000.0
ROPE_PAIRS = 16
RMS_EPS = 1e-6
N_GROUPS = 4
EPG = 4
N_EXPERTS = 16
EXPERT_FF = 512

TILE = 256
N_CTX_ROWS = BATCH * SEQ
N_LAT_ROWS = DEC_BATCH * DEC_SEQ
N_ROWS = N_CTX_ROWS + N_LAT_ROWS
CTX_TILES = N_CTX_ROWS // TILE
LAT_TILES_PER_SEQ = DEC_SEQ // TILE
N_TILES = N_ROWS // TILE
PTILE = 512
CTX_PTILES = N_CTX_ROWS // PTILE
LAT_PTILES_PER_SEQ = DEC_SEQ // PTILE
N_COND = 8

N_PAIRS = EPG * (EPG - 1) // 2
N_BUCKETS = N_GROUPS * N_PAIRS
MOE_TILE = 256
MOE_MAX_TILES = N_TILES + N_BUCKETS
DISPATCH_TILES = 4
N_SLOTS = MOE_MAX_TILES * MOE_TILE
ROW_W = D_MODEL + 128

CHUNK = 64
SUB = 16
NSUB = CHUNK // SUB
NCHUNK = TILE // CHUNK

VMEM_LIMIT = 48 * 1024 * 1024
NEG_BIG = -1e30
LOG2_E = 1.4426950408889634

HI = lax.Precision.HIGHEST


def _cparams(n_axes):
    return pltpu.CompilerParams(dimension_semantics=("arbitrary",) * n_axes,
                                vmem_limit_bytes=VMEM_LIMIT)


def _dot(a, b):
    return jnp.dot(a, b, preferred_element_type=F32)


def _dot_nt(a, b):
    return lax.dot_general(a, b, (((1,), (1,)), ((), ())), preferred_element_type=F32)


def _dot_tn(a, b):
    return lax.dot_general(a, b, (((0,), (0,)), ((), ())), preferred_element_type=F32)


def _silu(x):
    return x * (1.0 / (1.0 + jnp.exp(-x)))


def _sigmoid(x):
    return 1.0 / (1.0 + jnp.exp(-x))


def _mod_kernel(cond_ref, w_ref, b_ref, lb_ref, mod_ref, lbo_ref):
    c = cond_ref[...]
    m = jnp.dot(_silu(c), w_ref[...], preferred_element_type=F32, precision=HI)
    mod_ref[...] = m + b_ref[...]
    z = lb_ref[...]
    zmax = jnp.max(z, axis=0, keepdims=True)
    e = jnp.exp(z - zmax)
    p = e / jnp.sum(e, axis=0, keepdims=True)
    rows = [jnp.zeros_like(p[0:1])]
    for l in range(1, DEPTH):
        rows.append(rows[-1] + p[l:l + 1])
    lbo_ref[...] = jnp.concatenate(rows, axis=0)


def _modulation(cond, w_mod, b_mod, hg_lb):
    nblk = 6
    return pl.pallas_call(
        _mod_kernel,
        grid=(DEPTH, nblk),
        in_specs=[
            pl.BlockSpec((N_COND, D_MODEL), lambda l, j: (0, 0)),
            pl.BlockSpec((None, D_MODEL, D_MODEL), lambda l, j: (l, 0, j)),
            pl.BlockSpec((None, 1, D_MODEL), lambda l, j: (l, 0, j)),
            pl.BlockSpec((DEPTH, MIX_G), lambda l, j: (0, 0)),
        ],
        out_specs=[
            pl.BlockSpec((None, N_COND, D_MODEL), lambda l, j: (l, 0, j)),
            pl.BlockSpec((DEPTH, MIX_G), lambda l, j: (0, 0)),
        ],
        out_shape=[jax.ShapeDtypeStruct((DEPTH, N_COND, 6 * D_MODEL), F32),
                   jax.ShapeDtypeStruct((DEPTH, MIX_G), F32)],
        compiler_params=_cparams(2),
        name="modulation",
    )(cond, w_mod, b_mod.reshape(DEPTH, 1, 6 * D_MODEL), hg_lb)


def _group_mean_sq(x, bd_ref, group):
    return _dot((x * x).astype(BF16), bd_ref[...]) * (1.0 / group)


def _swap16(x):
    w = x.shape[-1]
    lane = lax.broadcasted_iota(jnp.int32, x.shape, x.ndim - 1)
    up = pltpu.roll(x, w - 16, x.ndim - 1)
    dn = pltpu.roll(x, 16, x.ndim - 1)
    return jnp.where((lane % 32) < 16, up, dn)


def _inproj_kernel(latent, x_ref, mod_ref, n1_ref, w_ref, qn_ref, kn_ref, lb_ref, bd_ref, *rest):
    if latent:
        cos_ref, sin_ref = rest[0], rest[1]
        outs = rest[2:]
    else:
        outs = rest
    (rq_ref, rk_ref, rv_ref, rsg_ref, dq_ref, dk_ref, dv_ref,
     hq_ref, hkf_ref, hkb_ref, hv_ref, hgf_ref, hgb_ref, hsg_ref) = outs[:14]

    x = x_ref[...]
    ms = jnp.mean(x * x, axis=-1, keepdims=True)
    shift1 = mod_ref[0:1, :]
    scale1 = mod_ref[1:2, :]
    h = x * lax.rsqrt(ms + RMS_EPS) * n1_ref[...] * (1.0 + scale1) + shift1
    y = _dot(h.astype(BF16), w_ref[...])

    rq_ref[...] = y[:, 0:256].astype(BF16)
    rk_ref[...] = (y[:, 256:512] * (HEAD_D ** -0.5)).astype(BF16)
    rv_ref[...] = y[:, 512:768].astype(BF16)
    rsg_ref[...] = _silu(y[:, 768:1024]).astype(BF16)

    dq = y[:, 1024:1536]
    dk = y[:, 1536:2048]
    qn = dq * lax.rsqrt(_group_mean_sq(dq, bd_ref, HEAD_D) + RMS_EPS) * qn_ref[...]
    kn = dk * lax.rsqrt(_group_mean_sq(dk, bd_ref, HEAD_D) + RMS_EPS) * kn_ref[...]
    if latent:
        cos = jnp.concatenate([cos_ref[...]] * 4, axis=1)
        sin = jnp.concatenate([sin_ref[...]] * 4, axis=1)
        qr = qn * cos + _swap16(qn) * sin
        kr = kn * cos + _swap16(kn) * sin
    else:
        qr, kr = qn, kn
        ck_ref, cv_ref = outs[14], outs[15]
        ck_ref[...] = kn
        cv_ref[...] = y[:, 2048:2560]
    dq_ref[...] = (qr * (HEAD_D ** -0.5 * LOG2_E)).astype(BF16)
    dk_ref[...] = kr.astype(BF16)
    dv_ref[...] = y[:, 2048:2560].astype(BF16)

    lb = lb_ref[...]
    f_f = lb + (1.0 - lb) * _sigmoid(y[:, 2816:3072])
    f_b = lb + (1.0 - lb) * _sigmoid(y[:, 3072:3328])
    hq_ref[...] = y[:, 2560:2816].astype(BF16)
    hkf_ref[...] = (1.0 - f_f).astype(BF16)
    hkb_ref[...] = (1.0 - f_b).astype(BF16)
    hgf_ref[...] = jnp.log2(f_f)
    hgb_ref[...] = jnp.log2(f_b)
    hv_ref[...] = y[:, 3328:3584].astype(BF16)
    hsg_ref[...] = _silu(y[:, 3584:3840]).astype(BF16)


def _in_projection(latent, layer, x, mod, norm1, w_in_bf, qn_t, kn_t, lb_all, bd512, rope_cos, rope_sin):
    nrows = x.shape[0]
    ntiles = nrows // PTILE
    cond = (lambda t: 1 + t // LAT_PTILES_PER_SEQ) if latent else (lambda t: 0)

    in_specs = [
        pl.BlockSpec((PTILE, D_MODEL), lambda t: (t, 0)),
        pl.BlockSpec((None, None, 6, D_MODEL), lambda t: (layer, cond(t), 0, 0)),
        pl.BlockSpec((None, 1, D_MODEL), lambda t: (layer, 0, 0)),
        pl.BlockSpec((None, D_MODEL, IN_COLS), lambda t: (layer, 0, 0)),
        pl.BlockSpec((None, 1, DA_W), lambda t: (layer, 0, 0)),
        pl.BlockSpec((None, 1, DA_W), lambda t: (layer, 0, 0)),
        pl.BlockSpec((None, 1, MIX_G), lambda t: (layer, 0, 0)),
        pl.BlockSpec((DA_W, DA_W), lambda t: (0, 0)),
    ]
    args = [x, mod, norm1, w_in_bf, qn_t, kn_t, lb_all, bd512]
    if latent:
        in_specs += [pl.BlockSpec((PTILE, 128), lambda t: (t % LAT_PTILES_PER_SEQ, 0))] * 2
        args += [rope_cos, rope_sin]

    def o(width, dtype):
        return jax.ShapeDtypeStruct((nrows, width), dtype), pl.BlockSpec((PTILE, width), lambda t: (t, 0))

    outs = [o(256, BF16), o(256, BF16), o(256, BF16), o(256, BF16),
            o(512, BF16), o(512, BF16), o(512, BF16),
            o(256, BF16), o(256, BF16), o(256, BF16), o(256, BF16), o(256, F32), o(256, F32), o(256, BF16)]
    if not latent:
        outs += [o(512, F32), o(512, F32)]
    return pl.pallas_call(
        functools.partial(_inproj_kernel, latent),
        grid=(ntiles,),
        in_specs=in_specs,
        out_specs=[s for _, s in outs],
        out_shape=[s for s, _ in outs],
        compiler_params=_cparams(1),
        name="in_projection_lat" if latent else "in_projection_ctx",
    )(*args)


def _head_masks():
    lane = lax.broadcasted_iota(jnp.int32, (1, MIX_G), 1)
    return [(lane // HEAD_D == h).astype(F32) for h in range(HEADS)]


def _scan_chunks(items, st_f, st_b, consts_f, consts_b):
    n = len(items)
    cs = [consts_b if it[4] else consts_f for it in items]
    zero_row = jnp.zeros((1, MIX_G), F32)

    bs = [jnp.dot(cs[x][0], items[x][3], preferred_element_type=F32, precision=HI) for x in range(n)]

    prep = []
    for x in range(n):
        q, k, v, _, rev = items[x]
        hmask = cs[x][4]
        b = bs[x]
        r_rows, e_rows = [], []
        for s in range(NSUB):
            lo, hi = s * SUB, s * SUB + SUB - 1
            if not rev:
                r_rows.append(b[lo - 1:lo, :] if s > 0 else zero_row)
                e_rows.append(b[hi:hi + 1, :])
            else:
                r_rows.append(b[hi + 1:hi + 2, :] if s < NSUB - 1 else zero_row)
                e_rows.append(b[lo:lo + 1, :])
        btot = e_rows[NSUB - 1] if not rev else e_rows[0]
        r_full = jnp.concatenate([jnp.broadcast_to(r, (SUB, MIX_G)) for r in r_rows], axis=0)
        e_full = jnp.concatenate([jnp.broadcast_to(e, (SUB, MIX_G)) for e in e_rows], axis=0)
        bl = b - r_full
        qp = q * jnp.exp2(bl)
        kend = k * jnp.exp2(e_full - b)
        pairs = [(i, j) for i in range(NSUB) for j in range(NSUB) if (j < i if not rev else j > i)]
        lh = jnp.concatenate([qp[i * SUB:(i + 1) * SUB, :] * jnp.exp2(r_rows[i] - e_rows[j])
                              for (i, j) in pairs], axis=0)
        prep.append(dict(
            bl=bl, btot=btot, pairs=pairs,
            q_inter=(qp * jnp.exp2(r_full)).astype(BF16),
            k_state=(kend * jnp.exp2(btot - e_full)).astype(BF16),
            kend=kend.astype(BF16),
            lhs4=jnp.concatenate([lh * hmask[h] for h in range(HEADS)], axis=0).astype(BF16)))

    uts = [_dot_tn(items[x][2], prep[x]['k_state']) for x in range(n)]
    scs = [_dot_nt(prep[x]['lhs4'], prep[x]['kend']) for x in range(n)]

    st_in = []
    for x in range(n):
        rev = items[x][4]
        st = st_b if rev else st_f
        st_in.append(st.astype(BF16))
        st = st * jnp.exp2(prep[x]['btot']) + uts[x] * cs[x][1]
        if rev:
            st_b = st
        else:
            st_f = st
    col = lax.broadcasted_iota(jnp.int32, (SUB, CHUNK), 1) // SUB
    jrow = lax.broadcasted_iota(jnp.int32, (SUB, MIX_G), 0)
    pss, pps, vts = [], [], []
    for x in range(n):
        q, k, v, _, rev = items[x]
        pairs, sc, bl = prep[x]['pairs'], scs[x], prep[x]['bl']
        p_rows = []
        for h in range(HEADS):
            for i in range(NSUB):
                acc = None
                for p, (pi, pj) in enumerate(pairs):
                    if pi != i:
                        continue
                    base = (h * len(pairs) + p) * SUB
                    blk = jnp.where(col == pj, sc[base:base + SUB, :], 0.0)
                    acc = blk if acc is None else acc + blk
                p_rows.append(jnp.zeros((SUB, CHUNK), F32) if acc is None else acc)
        pss.append(jnp.concatenate(p_rows, axis=0).astype(BF16))
        pp, vt = [], []
        for s in range(NSUB):
            sl = slice(s * SUB, (s + 1) * SUB)
            bl_s, q_s, k_s = bl[sl, :], q[sl, :], k[sl, :]
            for i in range(SUB):
                d = bl_s[i:i + 1, :] - bl_s
                keep = (jrow <= i) if not rev else (jrow >= i)
                e = jnp.exp2(jnp.where(keep, d, NEG_BIG))
                pp.append((e * q_s[i:i + 1, :] * k_s).astype(BF16))
            vt.extend([v[sl, :]] * SUB)
        pps.append(jnp.concatenate(pp, axis=0))
        vts.append(jnp.concatenate(vt, axis=0))

    o_int = [_dot_nt(prep[x]['q_inter'], st_in[x]) for x in range(n)]
    fulls = [_dot(pss[x], items[x][2]) for x in range(n)]
    sbs = [_dot(pps[x], cs[x][2]) for x in range(n)]

    ws = [(sbs[x] * vts[x].astype(F32)).astype(BF16) for x in range(n)]
    o_diag = [_dot(cs[x][3], ws[x]) for x in range(n)]

    outs = []
    for x in range(n):
        hmask = cs[x][4]
        o = o_int[x] + o_diag[x]
        for h in range(HEADS):
            o = o + fulls[x][h * CHUNK:(h + 1) * CHUNK, :] * hmask[h]
        outs.append(o)
    return outs, st_f, st_b


def _scan_consts(rev):
    r = lax.broadcasted_iota(jnp.int32, (CHUNK, CHUNK), 0)
    c = lax.broadcasted_iota(jnp.int32, (CHUNK, CHUNK), 1)
    tri = ((c <= r) if not rev else (c >= r)).astype(F32)
    rr = lax.broadcasted_iota(jnp.int32, (MIX_G, MIX_G), 0) // HEAD_D
    cc = lax.broadcasted_iota(jnp.int32, (MIX_G, MIX_G), 1) // HEAD_D
    bd_mask = (rr == cc).astype(F32)
    ones_bd = bd_mask.astype(BF16)
    gi = lax.broadcasted_iota(jnp.int32, (CHUNK, CHUNK * SUB), 0)
    gj = lax.broadcasted_iota(jnp.int32, (CHUNK, CHUNK * SUB), 1) // SUB
    gsel = (gi == gj).astype(BF16)
    return tri, bd_mask, ones_bd, gsel, _head_masks()


def _scan_kernel(nt, qf_ref, kf_ref, vf_ref, gf_ref, qb_ref, kb_ref, vb_ref, gb_ref, s0_ref,
                 of_ref, ob_ref, sfin_ref, st_scr):
    t = pl.program_id(1)

    @pl.when(t == 0)
    def _():
        st_scr[...] = s0_ref[...]

    cf = _scan_consts(False)
    cb = _scan_consts(True)

    items, rows = [], []
    for c in range(NCHUNK):
        rf = slice(c * CHUNK, (c + 1) * CHUNK)
        rb = slice((NCHUNK - 1 - c) * CHUNK, (NCHUNK - c) * CHUNK)
        items.append((qf_ref[rf, :].astype(F32), kf_ref[rf, :].astype(F32), vf_ref[rf, :], gf_ref[rf, :], False))
        items.append((qb_ref[rb, :].astype(F32), kb_ref[rb, :].astype(F32), vb_ref[rb, :], gb_ref[rb, :], True))
        rows += [(of_ref, rf), (ob_ref, rb)]
    outs, st_f, st_b = _scan_chunks(items, st_scr[0], st_scr[1], cf, cb)
    for (ref, sl), o in zip(rows, outs):
        ref[sl, :] = o
    st_scr[0] = st_f
    st_scr[1] = st_b

    @pl.when(t == nt - 1)
    def _():
        sfin_ref[...] = st_scr[...]


def _gated_scan(n_seq, nt, q, k_f, k_b, v, g_f, g_b, s0, name):
    fwd = lambda s, t: (s * nt + t, 0)
    bwd = lambda s, t: (s * nt + (nt - 1 - t), 0)
    blk = lambda im: pl.BlockSpec((TILE, MIX_G), im)
    in_specs = [blk(fwd), blk(fwd), blk(fwd), blk(fwd), blk(bwd), blk(bwd), blk(bwd), blk(bwd),
                pl.BlockSpec((None, 2, MIX_G, MIX_G), lambda s, t: (s, 0, 0, 0))]
    nrows = n_seq * nt * TILE
    return pl.pallas_call(
        functools.partial(_scan_kernel, nt),
        grid=(n_seq, nt),
        in_specs=in_specs,
        out_specs=[pl.BlockSpec((TILE, MIX_G), lambda s, t: (s * nt + t, 0)),
                   pl.BlockSpec((TILE, MIX_G), lambda s, t: (s * nt + (nt - 1 - t), 0)),
                   pl.BlockSpec((None, 2, MIX_G, MIX_G), lambda s, t: (s, 0, 0, 0))],
        out_shape=[jax.ShapeDtypeStruct((nrows, MIX_G), F32),
                   jax.ShapeDtypeStruct((nrows, MIX_G), F32),
                   jax.ShapeDtypeStruct((n_seq, 2, MIX_G, MIX_G), F32)],
        scratch_shapes=[pltpu.VMEM((2, MIX_G, MIX_G), F32)],
        compiler_params=_cparams(2),
        name=name,
    )(q, k_f, v, g_f, q, k_b, v, g_b, s0)


def _retention_tables():
    gam = 1.0 - 2.0 ** (-5.0 - np.arange(HEADS, dtype=np.float64))
    gam_r = gam[::-1]
    i = np.arange(TILE, dtype=np.float64)
    diff = i[:, None] - i[None, :]
    dcomb = np.zeros((HEADS, TILE, TILE))
    for h in range(HEADS):
        lower = np.where(diff > 0, gam[h] ** np.maximum(diff, 0), 0.0)
        upper = np.where(diff < 0, gam_r[h] ** np.maximum(-diff, 0), 0.0)
        dcomb[h] = lower + upper + 2.0 * (diff == 0)
    lanes = lambda per_head: np.repeat(per_head, HEAD_D, axis=-1)
    qd_f = lanes(gam[None, :] ** (i[:, None] + 1.0))
    kd_f = lanes(gam[None, :] ** (TILE - 1.0 - i[:, None]))
    qd_b = lanes(gam_r[None, :] ** (TILE - i[:, None]))
    kd_b = lanes(gam_r[None, :] ** i[:, None])
    blk = (np.arange(MIX_G)[:, None] // HEAD_D) == (np.arange(MIX_G)[None, :] // HEAD_D)
    c_f = np.where(blk, lanes(gam ** TILE)[None, :], 0.0) * np.ones((MIX_G, 1))
    c_b = np.where(blk, lanes(gam_r ** TILE)[None, :], 0.0) * np.ones((MIX_G, 1))
    f = lambda a: jnp.asarray(a, F32)
    return (f(dcomb.reshape(HEADS * TILE, TILE)), f(qd_f), f(kd_f), f(qd_b), f(kd_b), f(c_f), f(c_b))


def _ret_kernel(nt, qf_ref, kf_ref, vf_ref, qb_ref, kb_ref, vb_ref, s0_ref, dcomb_ref, qdf_ref, kdf_ref,
                qdb_ref, kdb_ref, cf_ref, cb_ref, of_ref, ob_ref, sfin_ref, st_scr):
    t = pl.program_id(1)

    @pl.when(t == 0)
    def _():
        st_scr[...] = s0_ref[...]

    lane_head = lax.broadcasted_iota(jnp.int32, (TILE, MIX_G), 1) // HEAD_D
    rr = lax.broadcasted_iota(jnp.int32, (MIX_G, MIX_G), 0) // HEAD_D
    cc = lax.broadcasted_iota(jnp.int32, (MIX_G, MIX_G), 1) // HEAD_D
    same_head = rr == cc

    q = qf_ref[...]
    k = kf_ref[...]
    v = vf_ref[...]
    qs = jnp.concatenate([jnp.where(lane_head == h, q, jnp.zeros_like(q)) for h in range(HEADS)], axis=0)
    p = (_dot_nt(qs, k) * dcomb_ref[...]).astype(BF16)
    full = _dot(p, v)
    o = _dot((q.astype(F32) * qdf_ref[...]).astype(BF16), st_scr[0].astype(BF16))
    for h in range(HEADS):
        o = o + jnp.where(lane_head == h, full[h * TILE:(h + 1) * TILE, :], 0.0)
    of_ref[...] = o
    u = _dot_tn((k.astype(F32) * kdf_ref[...]).astype(BF16), v)
    st_scr[0] = st_scr[0] * cf_ref[...] + jnp.where(same_head, u, 0.0)

    qb = qb_ref[...]
    kb = kb_ref[...]
    vb = vb_ref[...]
    ob_ref[...] = _dot((qb.astype(F32) * qdb_ref[...]).astype(BF16), st_scr[1].astype(BF16))
    ub = _dot_tn((kb.astype(F32) * kdb_ref[...]).astype(BF16), vb)
    st_scr[1] = st_scr[1] * cb_ref[...] + jnp.where(same_head, ub, 0.0)

    @pl.when(t == nt - 1)
    def _():
        sfin_ref[...] = st_scr[...]


def _retention(n_seq, nt, q, k, v, s0, tables, name):
    fwd = lambda s, t: (s * nt + t, 0)
    bwd = lambda s, t: (s * nt + (nt - 1 - t), 0)
    const = lambda s, t: (0, 0)
    blk = lambda im: pl.BlockSpec((TILE, MIX_G), im)
    sq = lambda: pl.BlockSpec((MIX_G, MIX_G), const)
    in_specs = [blk(fwd), blk(fwd), blk(fwd), blk(bwd), blk(bwd), blk(bwd),
                pl.BlockSpec((None, 2, MIX_G, MIX_G), lambda s, t: (s, 0, 0, 0)),
                pl.BlockSpec((HEADS * TILE, TILE), const), sq(), sq(), sq(), sq(), sq(), sq()]
    nrows = n_seq * nt * TILE
    return pl.pallas_call(
        functools.partial(_ret_kernel, nt),
        grid=(n_seq, nt),
        in_specs=in_specs,
        out_specs=[blk(fwd), blk(bwd), pl.BlockSpec((None, 2, MIX_G, MIX_G), lambda s, t: (s, 0, 0, 0))],
        out_shape=[jax.ShapeDtypeStruct((nrows, MIX_G), F32),
                   jax.ShapeDtypeStruct((nrows, MIX_G), F32),
                   jax.ShapeDtypeStruct((n_seq, 2, MIX_G, MIX_G), F32)],
        scratch_shapes=[pltpu.VMEM((2, MIX_G, MIX_G), F32)],
        compiler_params=_cparams(2),
        name=name,
    )(q, k, v, q, k, v, s0, *tables)


def _attn_step(lam_init, has_cache, refs, s_write, s_read):
    if has_cache:
        q_ref, k_ref, ck_ref, v_ref, cv_ref, lam_ref, sub_ref, o_ref = refs
        sw, sw2, mw = s_write
        sr, sr2, mr = s_read
    else:
        q_ref, k_ref, v_ref, lam_ref, sub_ref, o_ref = refs
        sw, mw = s_write
        sr, mr = s_read

    lp = lam_ref[...]
    lam = (jnp.exp(jnp.sum(lp[0:1] * lp[1:2], axis=-1, keepdims=True))
           - jnp.exp(jnp.sum(lp[2:3] * lp[3:4], axis=-1, keepdims=True)) + lam_init)
    v_ext = jnp.concatenate([v_ref[...], jnp.ones(v_ref.shape, BF16)], axis=1)
    if has_cache:
        cv_ext = jnp.concatenate([cv_ref[...], jnp.ones(cv_ref.shape, BF16)], axis=1)
    parts = []
    for c in range(2):
        m = mr[c]
        oe = _dot(jnp.exp2((sr[c] - m).astype(BF16)), v_ext)
        if has_cache:
            oe = oe + _dot(jnp.exp2((sr2[c] - m).astype(BF16)), cv_ext)
        parts.append(oe[:, :128] / oe[:, 128:129])
    o = parts[0] - lam * parts[1]
    ms = jnp.mean(o * o, axis=-1, keepdims=True)
    o_ref[...] = (o * lax.rsqrt(ms + RMS_EPS) * sub_ref[...] * (1.0 - lam_init)).astype(BF16)

    q = q_ref[...]
    k = k_ref[...]
    for c in range(2):
        qc = q[:, c * HEAD_D:(c + 1) * HEAD_D]
        s = _dot_nt(qc, k[:, c * HEAD_D:(c + 1) * HEAD_D])
        m = jnp.max(s, axis=-1, keepdims=True)
        sw[c] = s
        if has_cache:
            s2 = _dot_nt(qc, ck_ref[:, c * HEAD_D:(c + 1) * HEAD_D])
            m = jnp.maximum(m, jnp.max(s2, axis=-1, keepdims=True))
            sw2[c] = s2
        mw[c] = m


def _attn_kernel(lam_init, has_cache, *refs):
    n_in = 8 if has_cache else 6
    io, scr = refs[:n_in], refs[n_in:]
    half = len(scr) // 2
    buf_a, buf_b = scr[:half], scr[half:]
    u = pl.program_id(0)

    @pl.when(u == 0)
    def _():
        for r in buf_b:
            r[...] = jnp.zeros(r.shape, r.dtype)

    @pl.when(u % 2 == 0)
    def _():
        _attn_step(lam_init, has_cache, io, buf_a, buf_b)

    @pl.when(u % 2 == 1)
    def _():
        _attn_step(lam_init, has_cache, io, buf_b, buf_a)


def _diff_attention(layer, n_seq, seq_len, q, k, v, cache_k, cache_v, da_lambda, subln):
    tq = TILE
    nq = seq_len // tq
    n_units = n_seq * HEADS * nq
    has_cache = cache_k is not None
    lam_init = 0.8 - 0.6 * math.exp(-0.3 * layer)

    cur = lambda u: jnp.minimum(u, n_units - 1)
    prev = lambda u: jnp.maximum(u - 1, 0)
    seq_of = lambda w: w // (HEADS * nq)
    head_of = lambda w: (w // nq) % HEADS
    rows_of = lambda w: seq_of(w) * nq + w % nq

    q_spec = pl.BlockSpec((tq, 128), lambda u: (rows_of(cur(u)), head_of(cur(u))))
    k_spec = pl.BlockSpec((seq_len, 128), lambda u: (seq_of(cur(u)), head_of(cur(u))))
    v_spec = pl.BlockSpec((seq_len, 128), lambda u: (seq_of(prev(u)), head_of(prev(u))))
    ck_spec = pl.BlockSpec((None, None, PAST_LEN, 128), lambda u: (seq_of(cur(u)), layer, 0, head_of(cur(u))))
    cv_spec = pl.BlockSpec((None, None, PAST_LEN, 128), lambda u: (seq_of(prev(u)), layer, 0, head_of(prev(u))))
    tail = [pl.BlockSpec((None, 4, HEAD_D), lambda u: (layer, 0, 0)),
            pl.BlockSpec((None, 1, 128), lambda u: (layer, 0, 0))]
    if has_cache:
        in_specs = [q_spec, k_spec, ck_spec, v_spec, cv_spec] + tail
        args = [q, k, cache_k, v, cache_v, da_lambda, subln]
        scratch = [pltpu.VMEM((2, tq, seq_len), F32), pltpu.VMEM((2, tq, PAST_LEN), F32),
                   pltpu.VMEM((2, tq, 1), F32)] * 2
    else:
        in_specs = [q_spec, k_spec, v_spec] + tail
        args = [q, k, v, da_lambda, subln]
        scratch = [pltpu.VMEM((2, tq, seq_len), F32), pltpu.VMEM((2, tq, 1), F32)] * 2
    return pl.pallas_call(
        functools.partial(_attn_kernel, lam_init, has_cache),
        grid=(n_units + 1,),
        in_specs=in_specs,
        out_specs=pl.BlockSpec((tq, 128), lambda u: (rows_of(prev(u)), head_of(prev(u)))),
        out_shape=jax.ShapeDtypeStruct((n_seq * seq_len, DA_W), BF16),
        scratch_shapes=scratch,
        compiler_params=_cparams(1),
        name="diff_attention_lat" if has_cache else "diff_attention_ctx",
    )(*args)


def _route(logits):
    lane = lax.broadcasted_iota(jnp.int32, logits.shape, 1)
    lane_f = lane.astype(F32)
    big = 1e9
    is_g = jnp.logical_and(lane >= N_EXPERTS, lane < N_EXPERTS + N_GROUPS)
    gl = jnp.where(is_g, logits, NEG_BIG)
    gmax = jnp.max(gl, axis=-1, keepdims=True)
    gsel = jnp.min(jnp.where(gl == gmax, lane_f, big), axis=-1, keepdims=True) - N_EXPERTS
    g_w = 1.0 / jnp.sum(jnp.exp(gl - gmax), axis=-1, keepdims=True)
    in_grp = jnp.logical_and(lane < N_EXPERTS, (lane // EPG).astype(F32) == gsel)
    el = jnp.where(in_grp, logits, NEG_BIG)
    v1 = jnp.max(el, axis=-1, keepdims=True)
    i1 = jnp.min(jnp.where(el == v1, lane_f, big), axis=-1, keepdims=True)
    el2 = jnp.where(lane_f == i1, NEG_BIG, el)
    v2 = jnp.max(el2, axis=-1, keepdims=True)
    i2 = jnp.min(jnp.where(el2 == v2, lane_f, big), axis=-1, keepdims=True)
    t = jnp.exp(v2 - v1)
    w1 = g_w / (1.0 + t)
    w2 = w1 * t
    first_lo = i1 < i2
    a = jnp.minimum(i1, i2) - EPG * gsel
    b = jnp.maximum(i1, i2) - EPG * gsel
    bucket = gsel * N_PAIRS + a * (7.0 - a) * 0.5 + (b - a - 1.0)
    w_lo = jnp.where(first_lo, w1, w2)
    w_hi = jnp.where(first_lo, w2, w1)
    return jnp.where(lane == 0, bucket, jnp.where(lane == 1, w_lo, jnp.where(lane == 2, w_hi, 0.0)))


def _outproj_body(mod_ref, n2_ref, streams, hgn_ref, bd_ref, wout_ref, wrh_ref, wrl_ref, br_ref,
                  x1_ref, hp_ref, route_ref):
    x_ref, rof_ref, rob_ref, rsg_ref, da_ref, hof_ref, hob_ref, hsg_ref = streams
    ro = rof_ref[...] + rob_ref[...]
    ro = ro * lax.rsqrt(_group_mean_sq(ro, bd_ref, HEAD_D) + RMS_EPS) * rsg_ref[...].astype(F32)
    ho = hof_ref[...] + hob_ref[...]
    ho = ho * lax.rsqrt(_group_mean_sq(ho, bd_ref, HEAD_D) + RMS_EPS) * hgn_ref[...] * hsg_ref[...].astype(F32)
    mix = jnp.concatenate([ro.astype(BF16), da_ref[...], ho.astype(BF16)], axis=1)
    mixed = _dot(mix, wout_ref[...])
    gate1 = mod_ref[2:3, :]
    shift2 = mod_ref[3:4, :]
    scale2 = mod_ref[4:5, :]
    x1 = x_ref[...] + gate1 * mixed
    ms = jnp.mean(x1 * x1, axis=-1, keepdims=True)
    h2 = x1 * lax.rsqrt(ms + RMS_EPS) * n2_ref[...] * (1.0 + scale2) + shift2
    h_hi = h2.astype(BF16)
    h_lo = (h2 - h_hi.astype(F32)).astype(BF16)
    logits = _dot(h_hi, wrh_ref[...]) + _dot(h_lo, wrh_ref[...]) + _dot(h_hi, wrl_ref[...]) + br_ref[...]
    route = _route(logits)
    x1_ref[...] = x1
    route_ref[...] = route
    hp_ref[...] = jnp.concatenate([h2, route], axis=1)


def _outproj_kernel(mod_ref, n2_ref, *rest):
    ctx_streams, lat_streams, tail = rest[0:8], rest[8:16], rest[16:]
    t = pl.program_id(0)

    @pl.when(t < CTX_PTILES)
    def _():
        _outproj_body(mod_ref, n2_ref, ctx_streams, *tail)

    @pl.when(t >= CTX_PTILES)
    def _():
        _outproj_body(mod_ref, n2_ref, lat_streams, *tail)


def _cond_of_tile(t):
    return jnp.where(t < CTX_TILES, 0, 1 + (t - CTX_TILES) // LAT_TILES_PER_SEQ)


def _out_projection(layer, mod, norm2, ctx_streams, lat_streams, hgn_t, bd256, w_out_bf, w_route_hi, w_route_lo,
                    b_route):
    row = lambda t: (t, 0)
    ctx_row = lambda t: (jnp.minimum(t, CTX_PTILES - 1), 0)
    lat_row = lambda t: (jnp.maximum(t - CTX_PTILES, 0), 0)
    lay = lambda t: (layer, 0, 0)
    cond = lambda t: jnp.where(t < CTX_PTILES, 0, 1 + (t - CTX_PTILES) // LAT_PTILES_PER_SEQ)
    widths = (D_MODEL, MIX_G, MIX_G, MIX_G, DA_W, MIX_G, MIX_G, MIX_G)
    in_specs = [
        pl.BlockSpec((None, None, 6, D_MODEL), lambda t: (layer, cond(t), 0, 0)),
        pl.BlockSpec((None, 1, D_MODEL), lay),
    ]
    in_specs += [pl.BlockSpec((PTILE, w), ctx_row) for w in widths]
    in_specs += [pl.BlockSpec((PTILE, w), lat_row) for w in widths]
    in_specs += [
        pl.BlockSpec((None, 1, MIX_G), lay),
        pl.BlockSpec((MIX_G, MIX_G), lambda t: (0, 0)),
        pl.BlockSpec((None, D_MODEL, D_MODEL), lay),
        pl.BlockSpec((None, D_MODEL, 128), lay),
        pl.BlockSpec((None, D_MODEL, 128), lay),
        pl.BlockSpec((None, 1, 128), lay),
    ]
    return pl.pallas_call(
        _outproj_kernel,
        grid=(N_ROWS // PTILE,),
        in_specs=in_specs,
        out_specs=[pl.BlockSpec((PTILE, D_MODEL), row), pl.BlockSpec((PTILE, ROW_W), row),
                   pl.BlockSpec((PTILE, 128), row)],
        out_shape=[jax.ShapeDtypeStruct((N_ROWS, D_MODEL), F32), jax.ShapeDtypeStruct((N_ROWS, ROW_W), F32),
                   jax.ShapeDtypeStruct((N_ROWS, 128), F32)],
        compiler_params=_cparams(1),
        name="out_projection",
    )(mod, norm2, *ctx_streams, *lat_streams, hgn_t, bd256, w_out_bf, w_route_hi, w_route_lo, b_route)


def _dispatch_kernel(route_ref, dest_ref, meta_ref, cnt_ref):
    ph = pl.program_id(0)
    t = pl.program_id(1)
    lane = lax.broadcasted_iota(jnp.int32, (TILE, 128), 1).astype(F32)
    onehots = [(lane == route_ref[j * TILE:(j + 1) * TILE, 0:1]).astype(F32) for j in range(DISPATCH_TILES)]
    tile_cnts = [jnp.sum(oh, axis=0, keepdims=True) for oh in onehots]

    @pl.when(jnp.logical_and(ph == 0, t == 0))
    def _():
        cnt_ref[...] = jnp.zeros_like(cnt_ref)

    @pl.when(ph == 0)
    def _():
        cnt_ref[0:1, :] += sum(tile_cnts[1:], tile_cnts[0])

    @pl.when(jnp.logical_and(ph == 1, t == 0))
    def _():
        cnt = cnt_ref[0:1, :]
        padded = jnp.floor((cnt + (MOE_TILE - 1.0)) * (1.0 / MOE_TILE)) * MOE_TILE
        r = lax.broadcasted_iota(jnp.int32, (128, 128), 0)
        c = lax.broadcasted_iota(jnp.int32, (128, 128), 1)
        off = jnp.dot(jnp.broadcast_to(padded, (8, 128)), (r < c).astype(F32),
                      preferred_element_type=F32, precision=HI)[0:1, :]
        cnt_ref[1:2, :] = off
        end = off + padded
        end_col = jnp.sum(jnp.where(r == c, jnp.broadcast_to(end, (128, 128)), 0.0), axis=1, keepdims=True)
        ended = jnp.logical_and(end_col <= c.astype(F32) * MOE_TILE, r < N_BUCKETS)
        tile_bucket = jnp.sum(ended.astype(F32), axis=0, keepdims=True)
        lane1 = lax.broadcasted_iota(jnp.int32, (1, 128), 1)
        tile_idx = lane1.astype(F32)
        n_valid = jnp.max(jnp.where(lane1 < N_BUCKETS, end, 0.0), axis=1, keepdims=True) * (1.0 / MOE_TILE)
        last = jnp.max(jnp.where(tile_idx < n_valid, tile_bucket, 0.0), axis=1, keepdims=True)
        tb = jnp.where(tile_idx < n_valid, tile_bucket, last)
        g = jnp.floor(tb * (1.0 / N_PAIRS))
        pid = tb - g * N_PAIRS
        a = (pid >= 3.0).astype(F32) + (pid >= 5.0).astype(F32)
        b = pid + 1.0 - 2.0 * (a >= 1.0).astype(F32) - (a >= 2.0).astype(F32)
        rows = [EPG * g + a, EPG * g + b, jnp.broadcast_to(n_valid, (1, 128)), jnp.zeros((5, 128), F32)]
        meta_ref[...] = jnp.concatenate(rows, axis=0).astype(jnp.int32)

    @pl.when(ph == 1)
    def _():
        base = cnt_ref[1:2, :]
        rr = lax.broadcasted_iota(jnp.int32, (TILE, TILE), 0)
        cc = lax.broadcasted_iota(jnp.int32, (TILE, TILE), 1)
        for j, onehot in enumerate(onehots):
            oh16 = onehot.astype(BF16)
            same = _dot_nt(oh16, oh16)
            rank = jnp.sum(jnp.where(rr < cc, same, 0.0), axis=0, keepdims=True)
            base_row = lax.dot_general(jnp.broadcast_to(base, (8, 128)), onehot, (((1,), (1,)), ((), ())),
                                       preferred_element_type=F32, precision=HI)[0:1, :]
            dest_ref[j] = (base_row + rank).astype(jnp.int32)
            base = base + tile_cnts[j]
        cnt_ref[1:2, :] = base


def _moe_dispatch(route):
    dest, meta = pl.pallas_call(
        _dispatch_kernel,
        grid=(2, N_TILES // DISPATCH_TILES),
        in_specs=[pl.BlockSpec((DISPATCH_TILES * TILE, 128), lambda ph, t: (t, 0))],
        out_specs=[pl.BlockSpec((DISPATCH_TILES, 1, TILE), lambda ph, t: (t * ph, 0, 0)),
                   pl.BlockSpec((8, 128), lambda ph, t: (0, 0))],
        out_shape=[jax.ShapeDtypeStruct((N_TILES, 1, TILE), jnp.int32),
                   jax.ShapeDtypeStruct((8, 128), jnp.int32)],
        scratch_shapes=[pltpu.VMEM((8, 128), F32)],
        compiler_params=_cparams(2),
        name="moe_dispatch",
    )(route)
    return dest.reshape(N_ROWS), meta[0:3].reshape(3 * 128)


def _dma_params():
    return pltpu.CompilerParams(dimension_semantics=("arbitrary",), vmem_limit_bytes=VMEM_LIMIT,
                                disable_bounds_checks=True)


def _moe_scatter_kernel(dest_ref, h_ref, hs_init_ref, hs_ref, sem):
    del hs_init_ref
    base = pl.program_id(0) * TILE

    def issue(i, carry):
        d = dest_ref[base + i]
        pltpu.make_async_copy(h_ref.at[pl.ds(i, 1)], hs_ref.at[pl.ds(d, 1)], sem).start()
        return carry

    lax.fori_loop(0, TILE, issue, 0, unroll=8)
    pltpu.make_async_copy(h_ref, hs_ref.at[pl.ds(0, TILE)], sem).wait()


def _moe_scatter(dest, hp, hs_init):
    return pl.pallas_call(
        _moe_scatter_kernel,
        grid_spec=pltpu.PrefetchScalarGridSpec(
            num_scalar_prefetch=1,
            grid=(N_TILES,),
            in_specs=[pl.BlockSpec((TILE, ROW_W), lambda t, dest: (t, 0)),
                      pl.BlockSpec(memory_space=pl.ANY)],
            out_specs=pl.BlockSpec(memory_space=pl.ANY),
            scratch_shapes=[pltpu.SemaphoreType.DMA(())],
        ),
        out_shape=jax.ShapeDtypeStruct((N_SLOTS, ROW_W), F32),
        input_output_aliases={2: 0},
        compiler_params=_dma_params(),
        name="moe_scatter",
    )(dest, hp, hs_init)


def _moe_ffn_kernel(meta_ref, hs_ref, wgl_ref, wul_ref, wdl_ref, wgh_ref, wuh_ref, wdh_ref, ys_ref):
    t = pl.program_id(0)
    n_valid = meta_ref[2 * 128]

    @pl.when(t < n_valid)
    def _():
        w = hs_ref[...]
        x = w[:, :D_MODEL].astype(BF16)
        rw = w[:, D_MODEL:]

        def expert(wg_ref, wu_ref, wd_ref, weight):
            a = _silu(_dot(x, wg_ref[...])) * _dot(x, wu_ref[...])
            return _dot((a * weight).astype(BF16), wd_ref[...])

        y = expert(wgl_ref, wul_ref, wdl_ref, rw[:, 1:2]) + expert(wgh_ref, wuh_ref, wdh_ref, rw[:, 2:3])
        ys_ref[...] = y

    @pl.when(t >= n_valid)
    def _():
        ys_ref[...] = jnp.zeros_like(ys_ref)


def _moe_ffn(layer, meta, hs, w_gate_bf, w_up_bf, w_down_bf):
    lo = lambda t, m: (layer, m[t], 0, 0)
    hi = lambda t, m: (layer, m[128 + t], 0, 0)
    up = lambda im: pl.BlockSpec((None, None, D_MODEL, EXPERT_FF), im)
    down = lambda im: pl.BlockSpec((None, None, EXPERT_FF, D_MODEL), im)
    return pl.pallas_call(
        _moe_ffn_kernel,
        grid_spec=pltpu.PrefetchScalarGridSpec(
            num_scalar_prefetch=1,
            grid=(MOE_MAX_TILES,),
            in_specs=[pl.BlockSpec((MOE_TILE, ROW_W), lambda t, m: (jnp.maximum(jnp.minimum(t, m[2 * 128] - 1), 0), 0)),
                      up(lo), up(lo), down(lo), up(hi), up(hi), down(hi)],
            out_specs=pl.BlockSpec((MOE_TILE, D_MODEL), lambda t, m: (t, 0)),
        ),
        out_shape=jax.ShapeDtypeStruct((N_SLOTS, D_MODEL), F32),
        compiler_params=_cparams(1),
        name="moe_ffn",
    )(meta, hs, w_gate_bf, w_up_bf, w_down_bf, w_gate_bf, w_up_bf, w_down_bf)


def _moe_combine_kernel(dest_ref, x1_ref, mod_ref, ys_ref, out_ctx_ref, out_lat_ref, buf, sems):
    t = pl.program_id(0)
    nt = pl.num_programs(0)

    def issue(tile, slot):
        def body(i, carry):
            d = dest_ref[tile * TILE + i]
            pltpu.make_async_copy(ys_ref.at[pl.ds(d, 1)], buf.at[slot, pl.ds(i, 1)], sems.at[slot]).start()
            return carry
        lax.fori_loop(0, TILE, body, 0, unroll=8)

    @pl.when(t == 0)
    def _():
        issue(0, 0)

    @pl.when(t + 1 < nt)
    def _():
        issue(t + 1, (t + 1) % 2)

    slot = t % 2
    pltpu.make_async_copy(ys_ref.at[pl.ds(0, TILE)], buf.at[slot], sems.at[slot]).wait()
    x2 = x1_ref[...] + mod_ref[5:6, :] * buf[slot]

    @pl.when(t < CTX_TILES)
    def _():
        out_ctx_ref[...] = x2

    @pl.when(t >= CTX_TILES)
    def _():
        out_lat_ref[...] = x2


def _moe_combine(layer, dest, x1, mod, ys):
    return pl.pallas_call(
        _moe_combine_kernel,
        grid_spec=pltpu.PrefetchScalarGridSpec(
            num_scalar_prefetch=1,
            grid=(N_TILES,),
            in_specs=[pl.BlockSpec((TILE, D_MODEL), lambda t, dest: (t, 0)),
                      pl.BlockSpec((None, None, 6, D_MODEL), lambda t, dest: (layer, _cond_of_tile(t), 0, 0)),
                      pl.BlockSpec(memory_space=pl.ANY)],
            out_specs=[pl.BlockSpec((TILE, D_MODEL), lambda t, dest: (jnp.minimum(t, CTX_TILES - 1), 0)),
                       pl.BlockSpec((TILE, D_MODEL), lambda t, dest: (jnp.maximum(t - CTX_TILES, 0), 0))],
            scratch_shapes=[pltpu.VMEM((2, TILE, D_MODEL), F32), pltpu.SemaphoreType.DMA((2,))],
        ),
        out_shape=[jax.ShapeDtypeStruct((N_CTX_ROWS, D_MODEL), F32), jax.ShapeDtypeStruct((N_LAT_ROWS, D_MODEL), F32)],
        compiler_params=_dma_params(),
        name="moe_combine",
    )(dest, x1, mod, ys)


def _block_diag_ones(n, group):
    i = np.arange(n) // group
    return jnp.asarray((i[:, None] == i[None, :]).astype(np.float32), dtype=BF16)


def _rope_tables():
    pos = np.arange(DEC_SEQ)
    rows = (pos // GRID_W).astype(np.float32)
    cols = (pos % GRID_W).astype(np.float32)
    inv_freq = (ROPE_BASE ** (-(np.arange(ROPE_PAIRS, dtype=np.float32) / ROPE_PAIRS))).astype(np.float32)
    ang_r = rows[:, None] * inv_freq[None, :]
    ang_c = cols[:, None] * inv_freq[None, :]
    cos64 = np.concatenate([np.cos(ang_r), np.cos(ang_r), np.cos(ang_c), np.cos(ang_c)], axis=1)
    sin64 = np.concatenate([-np.sin(ang_r), np.sin(ang_r), -np.sin(ang_c), np.sin(ang_c)], axis=1)
    return (jnp.asarray(np.tile(cos64, (1, 2)), F32), jnp.asarray(np.tile(sin64, (1, 2)), F32))


def _state_to_blockdiag_t(s):
    eye = jnp.eye(HEADS, dtype=F32)
    return jnp.einsum('bzhdv,hg->bzhvgd', s.astype(F32), eye).reshape(s.shape[0], 2, MIX_G, MIX_G)


def _blockdiag_t_to_state(st):
    s6 = st.reshape(st.shape[0], 2, HEADS, HEAD_D, HEADS, HEAD_D)
    diag = jnp.stack([s6[:, :, h, :, h, :] for h in range(HEADS)], axis=2)
    return jnp.swapaxes(diag, -1, -2)


def _state_to_blockdiag(s):
    eye = jnp.eye(HEADS, dtype=F32)
    return jnp.einsum('bzhdv,hg->bzhdgv', s.astype(F32), eye).reshape(s.shape[0], 2, MIX_G, MIX_G)


def _blockdiag_to_state(st):
    s6 = st.reshape(st.shape[0], 2, HEADS, HEAD_D, HEADS, HEAD_D)
    return jnp.stack([s6[:, :, h, :, h, :] for h in range(HEADS)], axis=2)


def _mixers(latent, layer, x, consts, params, ret_s0, hg_s0, cache_k, cache_v):
    (mod, lb_all, bd512, bd256, rope_cos, rope_sin, ret_tables) = consts
    n_seq = DEC_BATCH if latent else BATCH
    seq_len = DEC_SEQ if latent else SEQ
    nt = seq_len // TILE
    proj = _in_projection(latent, layer, x, mod, params['norm1'], params['w_in'], params['da_qn'],
                          params['da_kn'], lb_all, bd512, rope_cos, rope_sin)
    (rq, rk, rv, rsg, dq, dk, dv, hq, hkf, hkb, hv, hgf, hgb, hsg) = proj[:14]
    ret_of, ret_ob, ret_fin = _retention(n_seq, nt, rq, rk, rv, ret_s0, ret_tables,
                                         "retention_lat" if latent else "retention_ctx")
    hg_of, hg_ob, hg_fin = _gated_scan(n_seq, nt, hq, hkf, hkb, hv, hgf, hgb, hg_s0,
                                       "hgrn_scan_lat" if latent else "hgrn_scan_ctx")
    da_o = _diff_attention(layer, n_seq, seq_len, dq, dk, dv, cache_k, cache_v,
                           params['da_lambda'], params['da_subln'])
    streams = (x, ret_of, ret_ob, rsg, da_o, hg_of, hg_ob, hsg)
    extras = None if latent else (proj[14], proj[15], ret_fin, hg_fin)
    return streams, extras


def _trunk_layer(layer, x_ctx, x_lat, sorted_buf, consts, params, lat_ret_s0, lat_hg_s0, zero_state, cache_k, cache_v):
    mod, bd256 = consts[0], consts[3]
    ctx_streams, extras = _mixers(False, layer, x_ctx, consts, params, zero_state, zero_state, None, None)
    lat_streams, _ = _mixers(True, layer, x_lat, consts, params, lat_ret_s0, lat_hg_s0, cache_k, cache_v)
    x1, hp, route = _out_projection(layer, mod, params['norm2'], ctx_streams, lat_streams, params['hg_norm'],
                                    bd256, params['w_out'], params['w_route_hi'], params['w_route_lo'],
                                    params['b_route'])
    dest, meta = _moe_dispatch(route)
    sorted_buf = _moe_scatter(dest, hp, sorted_buf)
    ys = _moe_ffn(layer, meta, sorted_buf, params['w_gate'], params['w_up'], params['w_down'])
    x_ctx, x_lat = _moe_combine(layer, dest, x1, mod, ys)
    return x_ctx, x_lat, sorted_buf, extras


def kernel(x_prompt, x_sample, cache_k, cache_v, state_ret, state_hgrn, c, c_ctx, norm1, norm2, w_mod, b_mod,
           w_in, w_out, da_qn, da_kn, da_lambda, da_subln, hg_lb, hg_norm, w_group, b_group, w_router,
           b_router, w_gate, w_up, w_down):
    cond = jnp.zeros((N_COND, D_MODEL), F32).at[0].set(c_ctx).at[1:1 + DEC_BATCH].set(c)
    mod, lb_all = _modulation(cond, w_mod, b_mod, hg_lb)
    mod = mod.reshape(DEPTH, N_COND, 6, D_MODEL)
    rope_cos, rope_sin = _rope_tables()
    consts = (mod, lb_all.reshape(DEPTH, 1, MIX_G), _block_diag_ones(DA_W, HEAD_D),
              _block_diag_ones(MIX_G, HEAD_D), rope_cos, rope_sin, _retention_tables())
    pad = jnp.zeros((DEPTH, D_MODEL, 128 - N_EXPERTS - N_GROUPS), F32)
    w_route = jnp.concatenate([w_router, w_group, pad], axis=-1)
    params = {
        'norm1': norm1.reshape(DEPTH, 1, D_MODEL), 'norm2': norm2.reshape(DEPTH, 1, D_MODEL),
        'w_in': w_in.astype(BF16), 'w_out': w_out.astype(BF16),
        'da_qn': jnp.tile(da_qn, (1, DA_W // HEAD_D)).reshape(DEPTH, 1, DA_W),
        'da_kn': jnp.tile(da_kn, (1, DA_W // HEAD_D)).reshape(DEPTH, 1, DA_W),
        'da_lambda': da_lambda, 'da_subln': da_subln.reshape(DEPTH, 1, 128),
        'hg_norm': jnp.tile(hg_norm, (1, HEADS)).reshape(DEPTH, 1, MIX_G),
        'w_route_hi': w_route.astype(BF16),
        'w_route_lo': (w_route - w_route.astype(BF16).astype(F32)).astype(BF16),
        'b_route': jnp.concatenate([b_router, b_group, pad[:, 0, :]], axis=-1).reshape(DEPTH, 1, 128),
        'w_gate': w_gate.astype(BF16), 'w_up': w_up.astype(BF16), 'w_down': w_down.astype(BF16),
    }
    ck_bf = cache_k.astype(BF16).reshape(DEC_BATCH, DEPTH, PAST_LEN, DA_W)
    cv_bf = cache_v.astype(BF16).reshape(DEC_BATCH, DEPTH, PAST_LEN, DA_W)
    zero_state = jnp.zeros((BATCH, 2, MIX_G, MIX_G), F32)

    yp = x_prompt.reshape(N_CTX_ROWS, D_MODEL)
    ys = x_sample.reshape(N_LAT_ROWS, D_MODEL)
    sorted_buf = jnp.zeros((N_SLOTS, ROW_W), F32)
    ks_out, vs_out, rets_out, hgs_out = [], [], [], []
    for l in range(DEPTH):
        yp, ys, sorted_buf, (k_new, v_new, ret_fin, hg_fin) = _trunk_layer(
            l, yp, ys, sorted_buf, consts, params, _state_to_blockdiag(state_ret[:, l]),
            _state_to_blockdiag_t(state_hgrn[:, l]), zero_state, ck_bf, cv_bf)
        ks_out.append(k_new.reshape(BATCH, SEQ, HEADS, 2, HEAD_D))
        vs_out.append(v_new.reshape(BATCH, SEQ, HEADS, 2 * HEAD_D))
        rets_out.append(_blockdiag_to_state(ret_fin))
        hgs_out.append(_blockdiag_t_to_state(hg_fin))
    return (yp.reshape(BATCH, SEQ, D_MODEL), ys.reshape(DEC_BATCH, DEC_SEQ, D_MODEL),
            jnp.stack(ks_out, axis=1), jnp.stack(vs_out, axis=1),
            jnp.stack(rets_out, axis=1), jnp.stack(hgs_out, axis=1))
```

```python
import functools
import math

import numpy as np
import jax
import jax.numpy as jnp
from jax import lax
from jax.experimental import pallas as pl
from jax.experimental.pallas import tpu as pltpu

F32 = jnp.float32
BF16 = jnp.bfloat16

D_MODEL = 1024
BATCH = 16
SEQ = 256
DEPTH = 4
DEC_BATCH = 4
DEC_SEQ = 4096
PAST_LEN = 256
GRID_W = 64
HEADS = 4
HEAD_D = 64
MIX_G = HEADS * HEAD_D
DA_W = 512
IN_COLS = 3840
ROPE_BASE = 10000.0
ROPE_PAIRS = 16
RMS_EPS = 1e-6
N_GROUPS = 4
EPG = 4
N_EXPERTS = 16
EXPERT_FF = 512

TILE = 256
N_CTX_ROWS = BATCH * SEQ
N_LAT_ROWS = DEC_BATCH * DEC_SEQ
N_ROWS = N_CTX_ROWS + N_LAT_ROWS
CTX_TILES = N_CTX_ROWS // TILE
LAT_TILES_PER_SEQ = DEC_SEQ // TILE
N_TILES = N_ROWS // TILE
PTILE = 512
CTX_PTILES = N_CTX_ROWS // PTILE
LAT_PTILES_PER_SEQ = DEC_SEQ // PTILE
N_COND = 8

N_PAIRS = EPG * (EPG - 1) // 2
N_BUCKETS = N_GROUPS * N_PAIRS
MOE_TILE = 256
MOE_MAX_TILES = N_TILES + N_BUCKETS
DISPATCH_TILES = 4
N_SLOTS = MOE_MAX_TILES * MOE_TILE
ROW_W = D_MODEL + 128

CHUNK = 64
SUB = 16
NSUB = CHUNK // SUB
NCHUNK = TILE // CHUNK

VMEM_LIMIT = 48 * 1024 * 1024
NEG_BIG = -1e30
ONES_ROWS = 16
ATTN_TQ = 512
ATTN_KEY_CHUNK = 512
LOG2_E = 1.4426950408889634

HI = lax.Precision.HIGHEST


def _cparams(n_axes):
    return pltpu.CompilerParams(dimension_semantics=("arbitrary",) * n_axes,
                                vmem_limit_bytes=VMEM_LIMIT)


def _dot(a, b):
    return jnp.dot(a, b, preferred_element_type=F32)


def _dot_nt(a, b):
    return lax.dot_general(a, b, (((1,), (1,)), ((), ())), preferred_element_type=F32)


def _dot_tn(a, b):
    return lax.dot_general(a, b, (((0,), (0,)), ((), ())), preferred_element_type=F32)


def _silu(x):
    return x * (1.0 / (1.0 + jnp.exp(-x)))


def _sigmoid(x):
    return 1.0 / (1.0 + jnp.exp(-x))


def _mod_kernel(cond_ref, w_ref, b_ref, lb_ref, mod_ref, lbo_ref):
    c = cond_ref[...]
    m = jnp.dot(_silu(c), w_ref[...], preferred_element_type=F32, precision=HI)
    mod_ref[...] = m + b_ref[...]
    z = lb_ref[...]
    zmax = jnp.max(z, axis=0, keepdims=True)
    e = jnp.exp(z - zmax)
    p = e / jnp.sum(e, axis=0, keepdims=True)
    rows = [jnp.zeros_like(p[0:1])]
    for l in range(1, DEPTH):
        rows.append(rows[-1] + p[l:l + 1])
    lbo_ref[...] = jnp.concatenate(rows, axis=0)


def _modulation(cond, w_mod, b_mod, hg_lb):
    nblk = 6
    return pl.pallas_call(
        _mod_kernel,
        grid=(DEPTH, nblk),
        in_specs=[
            pl.BlockSpec((N_COND, D_MODEL), lambda l, j: (0, 0)),
            pl.BlockSpec((None, D_MODEL, D_MODEL), lambda l, j: (l, 0, j)),
            pl.BlockSpec((None, 1, D_MODEL), lambda l, j: (l, 0, j)),
            pl.BlockSpec((DEPTH, MIX_G), lambda l, j: (0, 0)),
        ],
        out_specs=[
            pl.BlockSpec((None, N_COND, D_MODEL), lambda l, j: (l, 0, j)),
            pl.BlockSpec((DEPTH, MIX_G), lambda l, j: (0, 0)),
        ],
        out_shape=[jax.ShapeDtypeStruct((DEPTH, N_COND, 6 * D_MODEL), F32),
                   jax.ShapeDtypeStruct((DEPTH, MIX_G), F32)],
        compiler_params=_cparams(2),
        name="modulation",
    )(cond, w_mod, b_mod.reshape(DEPTH, 1, 6 * D_MODEL), hg_lb)


def _group_mean_sq(x, bd_ref, group):
    return _dot((x * x).astype(BF16), bd_ref[...]) * (1.0 / group)


def _swap16(x):
    w = x.shape[-1]
    lane = lax.broadcasted_iota(jnp.int32, x.shape, x.ndim - 1)
    up = pltpu.roll(x, w - 16, x.ndim - 1)
    dn = pltpu.roll(x, 16, x.ndim - 1)
    return jnp.where((lane % 32) < 16, up, dn)


def _inproj_kernel(latent, x_ref, mod_ref, n1_ref, w_ref, wvt_ref, qn_ref, kn_ref, lb_ref, bd_ref, *rest):
    if latent:
        cos_ref, sin_ref = rest[0], rest[1]
        outs = rest[2:]
    else:
        outs = rest
    (rq_ref, rk_ref, rv_ref, rsg_ref, dq_ref, dk_ref, dv_ref,
     hq_ref, hkf_ref, hkb_ref, hv_ref, hgf_ref, hgb_ref, hsg_ref) = outs[:14]

    x = x_ref[...]
    ms = jnp.mean(x * x, axis=-1, keepdims=True)
    shift1 = mod_ref[0:1, :]
    scale1 = mod_ref[1:2, :]
    h = x * lax.rsqrt(ms + RMS_EPS) * n1_ref[...] * (1.0 + scale1) + shift1
    h16 = h.astype(BF16)
    y = _dot(h16, w_ref[...])

    rq_ref[...] = y[:, 0:256].astype(BF16)
    rk_ref[...] = (y[:, 256:512] * (HEAD_D ** -0.5)).astype(BF16)
    rv_ref[...] = y[:, 512:768].astype(BF16)
    rsg_ref[...] = _silu(y[:, 768:1024]).astype(BF16)

    dq = y[:, 1024:1536]
    dk = y[:, 1536:2048]
    qn = dq * lax.rsqrt(_group_mean_sq(dq, bd_ref, HEAD_D) + RMS_EPS) * qn_ref[...]
    kn = dk * lax.rsqrt(_group_mean_sq(dk, bd_ref, HEAD_D) + RMS_EPS) * kn_ref[...]
    if latent:
        cos = jnp.concatenate([cos_ref[...]] * 4, axis=1)
        sin = jnp.concatenate([sin_ref[...]] * 4, axis=1)
        qr = qn * cos + _swap16(qn) * sin
        kr = kn * cos + _swap16(kn) * sin
    else:
        qr, kr = qn, kn
        ck_ref, cv_ref = outs[14], outs[15]
        ck_ref[...] = kn
        cv_ref[...] = y[:, 2048:2560]
    dq_ref[...] = (qr * (HEAD_D ** -0.5 * LOG2_E)).astype(BF16)
    dk_ref[...] = kr.astype(BF16)
    dv_ref[...] = _dot_nt(wvt_ref[...], h16).astype(BF16)

    lb = lb_ref[...]
    f_f = lb + (1.0 - lb) * _sigmoid(y[:, 2816:3072])
    f_b = lb + (1.0 - lb) * _sigmoid(y[:, 3072:3328])
    hq_ref[...] = y[:, 2560:2816].astype(BF16)
    hkf_ref[...] = (1.0 - f_f).astype(BF16)
    hkb_ref[...] = (1.0 - f_b).astype(BF16)
    hgf_ref[...] = jnp.log2(f_f)
    hgb_ref[...] = jnp.log2(f_b)
    hv_ref[...] = y[:, 3328:3584].astype(BF16)
    hsg_ref[...] = _silu(y[:, 3584:3840]).astype(BF16)


def _in_projection(latent, layer, x, mod, norm1, w_in_bf, w_vt_bf, qn_t, kn_t, lb_all, bd512, rope_cos, rope_sin):
    nrows = x.shape[0]
    ntiles = nrows // PTILE
    cond = (lambda t: 1 + t // LAT_PTILES_PER_SEQ) if latent else (lambda t: 0)

    in_specs = [
        pl.BlockSpec((PTILE, D_MODEL), lambda t: (t, 0)),
        pl.BlockSpec((None, None, 6, D_MODEL), lambda t: (layer, cond(t), 0, 0)),
        pl.BlockSpec((None, 1, D_MODEL), lambda t: (layer, 0, 0)),
        pl.BlockSpec((None, D_MODEL, IN_COLS), lambda t: (layer, 0, 0)),
        pl.BlockSpec((None, DA_W, D_MODEL), lambda t: (layer, 0, 0)),
        pl.BlockSpec((None, 1, DA_W), lambda t: (layer, 0, 0)),
        pl.BlockSpec((None, 1, DA_W), lambda t: (layer, 0, 0)),
        pl.BlockSpec((None, 1, MIX_G), lambda t: (layer, 0, 0)),
        pl.BlockSpec((DA_W, DA_W), lambda t: (0, 0)),
    ]
    args = [x, mod, norm1, w_in_bf, w_vt_bf, qn_t, kn_t, lb_all, bd512]
    if latent:
        in_specs += [pl.BlockSpec((PTILE, 128), lambda t: (t % LAT_PTILES_PER_SEQ, 0))] * 2
        args += [rope_cos, rope_sin]

    def o(width, dtype):
        return jax.ShapeDtypeStruct((nrows, width), dtype), pl.BlockSpec((PTILE, width), lambda t: (t, 0))

    dv_t = (jax.ShapeDtypeStruct((DA_W, nrows), BF16), pl.BlockSpec((DA_W, PTILE), lambda t: (0, t)))
    outs = [o(256, BF16), o(256, BF16), o(256, BF16), o(256, BF16),
            o(512, BF16), o(512, BF16), dv_t,
            o(256, BF16), o(256, BF16), o(256, BF16), o(256, BF16), o(256, F32), o(256, F32), o(256, BF16)]
    if not latent:
        outs += [o(512, F32), o(512, F32)]
    return pl.pallas_call(
        functools.partial(_inproj_kernel, latent),
        grid=(ntiles,),
        in_specs=in_specs,
        out_specs=[s for _, s in outs],
        out_shape=[s for s, _ in outs],
        compiler_params=_cparams(1),
        name="in_projection_lat" if latent else "in_projection_ctx",
    )(*args)


def _head_masks():
    lane = lax.broadcasted_iota(jnp.int32, (1, MIX_G), 1)
    return [(lane // HEAD_D == h).astype(F32) for h in range(HEADS)]


def _scan_chunks(items, st_f, st_b, consts_f, consts_b):
    n = len(items)
    cs = [consts_b if it[4] else consts_f for it in items]
    zero_row = jnp.zeros((1, MIX_G), F32)

    bs = [jnp.dot(cs[x][0], items[x][3], preferred_element_type=F32, precision=HI) for x in range(n)]

    prep = []
    for x in range(n):
        q, k, v, _, rev = items[x]
        hmask = cs[x][4]
        b = bs[x]
        r_rows, e_rows = [], []
        for s in range(NSUB):
            lo, hi = s * SUB, s * SUB + SUB - 1
            if not rev:
                r_rows.append(b[lo - 1:lo, :] if s > 0 else zero_row)
                e_rows.append(b[hi:hi + 1, :])
            else:
                r_rows.append(b[hi + 1:hi + 2, :] if s < NSUB - 1 else zero_row)
                e_rows.append(b[lo:lo + 1, :])
        btot = e_rows[NSUB - 1] if not rev else e_rows[0]
        r_full = jnp.concatenate([jnp.broadcast_to(r, (SUB, MIX_G)) for r in r_rows], axis=0)
        e_full = jnp.concatenate([jnp.broadcast_to(e, (SUB, MIX_G)) for e in e_rows], axis=0)
        bl = b - r_full
        qp = q * jnp.exp2(bl)
        kend = k * jnp.exp2(e_full - b)
        pairs = [(i, j) for i in range(NSUB) for j in range(NSUB) if (j < i if not rev else j > i)]
        lh = jnp.concatenate([qp[i * SUB:(i + 1) * SUB, :] * jnp.exp2(r_rows[i] - e_rows[j])
                              for (i, j) in pairs], axis=0)
        prep.append(dict(
            bl=bl, btot=btot, pairs=pairs,
            q_inter=(qp * jnp.exp2(r_full)).astype(BF16),
            k_state=(kend * jnp.exp2(btot - e_full)).astype(BF16),
            kend=kend.astype(BF16),
            lhs4=jnp.concatenate([lh * hmask[h] for h in range(HEADS)], axis=0).astype(BF16)))

    uts = [_dot_tn(items[x][2], prep[x]['k_state']) for x in range(n)]
    scs = [_dot_nt(prep[x]['lhs4'], prep[x]['kend']) for x in range(n)]

    st_in = []
    for x in range(n):
        rev = items[x][4]
        st = st_b if rev else st_f
        st_in.append(st.astype(BF16))
        st = st * jnp.exp2(prep[x]['btot']) + uts[x] * cs[x][1]
        if rev:
            st_b = st
        else:
            st_f = st
    col = lax.broadcasted_iota(jnp.int32, (SUB, CHUNK), 1) // SUB
    jrow = lax.broadcasted_iota(jnp.int32, (SUB, MIX_G), 0)
    pss, pps, vts = [], [], []
    for x in range(n):
        q, k, v, _, rev = items[x]
        pairs, sc, bl = prep[x]['pairs'], scs[x], prep[x]['bl']
        p_rows = []
        for h in range(HEADS):
            for i in range(NSUB):
                acc = None
                for p, (pi, pj) in enumerate(pairs):
                    if pi != i:
                        continue
                    base = (h * len(pairs) + p) * SUB
                    blk = jnp.where(col == pj, sc[base:base + SUB, :], 0.0)
                    acc = blk if acc is None else acc + blk
                p_rows.append(jnp.zeros((SUB, CHUNK), F32) if acc is None else acc)
        pss.append(jnp.concatenate(p_rows, axis=0).astype(BF16))
        pp, vt = [], []
        for s in range(NSUB):
            sl = slice(s * SUB, (s + 1) * SUB)
            bl_s, q_s, k_s = bl[sl, :], q[sl, :], k[sl, :]
            for i in range(SUB):
                d = bl_s[i:i + 1, :] - bl_s
                keep = (jrow <= i) if not rev else (jrow >= i)
                e = jnp.exp2(jnp.where(keep, d, NEG_BIG))
                pp.append((e * q_s[i:i + 1, :] * k_s).astype(BF16))
            vt.extend([v[sl, :]] * SUB)
        pps.append(jnp.concatenate(pp, axis=0))
        vts.append(jnp.concatenate(vt, axis=0))

    o_int = [_dot_nt(prep[x]['q_inter'], st_in[x]) for x in range(n)]
    fulls = [_dot(pss[x], items[x][2]) for x in range(n)]
    sbs = [_dot(pps[x], cs[x][2]) for x in range(n)]

    ws = [(sbs[x] * vts[x].astype(F32)).astype(BF16) for x in range(n)]
    o_diag = [_dot(cs[x][3], ws[x]) for x in range(n)]

    outs = []
    for x in range(n):
        hmask = cs[x][4]
        o = o_int[x] + o_diag[x]
        for h in range(HEADS):
            o = o + fulls[x][h * CHUNK:(h + 1) * CHUNK, :] * hmask[h]
        outs.append(o)
    return outs, st_f, st_b


def _scan_consts(rev):
    r = lax.broadcasted_iota(jnp.int32, (CHUNK, CHUNK), 0)
    c = lax.broadcasted_iota(jnp.int32, (CHUNK, CHUNK), 1)
    tri = ((c <= r) if not rev else (c >= r)).astype(F32)
    rr = lax.broadcasted_iota(jnp.int32, (MIX_G, MIX_G), 0) // HEAD_D
    cc = lax.broadcasted_iota(jnp.int32, (MIX_G, MIX_G), 1) // HEAD_D
    bd_mask = (rr == cc).astype(F32)
    ones_bd = bd_mask.astype(BF16)
    gi = lax.broadcasted_iota(jnp.int32, (CHUNK, CHUNK * SUB), 0)
    gj = lax.broadcasted_iota(jnp.int32, (CHUNK, CHUNK * SUB), 1) // SUB
    gsel = (gi == gj).astype(BF16)
    return tri, bd_mask, ones_bd, gsel, _head_masks()


def _scan_kernel(nt, qf_ref, kf_ref, vf_ref, gf_ref, qb_ref, kb_ref, vb_ref, gb_ref, s0_ref,
                 of_ref, ob_ref, sfin_ref, st_scr):
    t = pl.program_id(1)

    @pl.when(t == 0)
    def _():
        st_scr[...] = s0_ref[...]

    cf = _scan_consts(False)
    cb = _scan_consts(True)

    items, rows = [], []
    for c in range(NCHUNK):
        rf = slice(c * CHUNK, (c + 1) * CHUNK)
        rb = slice((NCHUNK - 1 - c) * CHUNK, (NCHUNK - c) * CHUNK)
        items.append((qf_ref[rf, :].astype(F32), kf_ref[rf, :].astype(F32), vf_ref[rf, :], gf_ref[rf, :], False))
        items.append((qb_ref[rb, :].astype(F32), kb_ref[rb, :].astype(F32), vb_ref[rb, :], gb_ref[rb, :], True))
        rows += [(of_ref, rf), (ob_ref, rb)]
    outs, st_f, st_b = _scan_chunks(items, st_scr[0], st_scr[1], cf, cb)
    for (ref, sl), o in zip(rows, outs):
        ref[sl, :] = o
    st_scr[0] = st_f
    st_scr[1] = st_b

    @pl.when(t == nt - 1)
    def _():
        sfin_ref[...] = st_scr[...]


def _gated_scan(n_seq, nt, q, k_f, k_b, v, g_f, g_b, s0, name):
    fwd = lambda s, t: (s * nt + t, 0)
    bwd = lambda s, t: (s * nt + (nt - 1 - t), 0)
    blk = lambda im: pl.BlockSpec((TILE, MIX_G), im)
    in_specs = [blk(fwd), blk(fwd), blk(fwd), blk(fwd), blk(bwd), blk(bwd), blk(bwd), blk(bwd),
                pl.BlockSpec((None, 2, MIX_G, MIX_G), lambda s, t: (s, 0, 0, 0))]
    nrows = n_seq * nt * TILE
    return pl.pallas_call(
        functools.partial(_scan_kernel, nt),
        grid=(n_seq, nt),
        in_specs=in_specs,
        out_specs=[pl.BlockSpec((TILE, MIX_G), lambda s, t: (s * nt + t, 0)),
                   pl.BlockSpec((TILE, MIX_G), lambda s, t: (s * nt + (nt - 1 - t), 0)),
                   pl.BlockSpec((None, 2, MIX_G, MIX_G), lambda s, t: (s, 0, 0, 0))],
        out_shape=[jax.ShapeDtypeStruct((nrows, MIX_G), F32),
                   jax.ShapeDtypeStruct((nrows, MIX_G), F32),
                   jax.ShapeDtypeStruct((n_seq, 2, MIX_G, MIX_G), F32)],
        scratch_shapes=[pltpu.VMEM((2, MIX_G, MIX_G), F32)],
        compiler_params=_cparams(2),
        name=name,
    )(q, k_f, v, g_f, q, k_b, v, g_b, s0)


def _retention_tables():
    gam = 1.0 - 2.0 ** (-5.0 - np.arange(HEADS, dtype=np.float64))
    gam_r = gam[::-1]
    i = np.arange(TILE, dtype=np.float64)
    diff = i[:, None] - i[None, :]
    dcomb = np.zeros((HEADS, TILE, TILE))
    for h in range(HEADS):
        lower = np.where(diff > 0, gam[h] ** np.maximum(diff, 0), 0.0)
        upper = np.where(diff < 0, gam_r[h] ** np.maximum(-diff, 0), 0.0)
        dcomb[h] = lower + upper + 2.0 * (diff == 0)
    lanes = lambda per_head: np.repeat(per_head, HEAD_D, axis=-1)
    qd_f = lanes(gam[None, :] ** (i[:, None] + 1.0))
    kd_f = lanes(gam[None, :] ** (TILE - 1.0 - i[:, None]))
    qd_b = lanes(gam_r[None, :] ** (TILE - i[:, None]))
    kd_b = lanes(gam_r[None, :] ** i[:, None])
    blk = (np.arange(MIX_G)[:, None] // HEAD_D) == (np.arange(MIX_G)[None, :] // HEAD_D)
    c_f = np.where(blk, lanes(gam ** TILE)[None, :], 0.0) * np.ones((MIX_G, 1))
    c_b = np.where(blk, lanes(gam_r ** TILE)[None, :], 0.0) * np.ones((MIX_G, 1))
    f = lambda a: jnp.asarray(a, F32)
    return (f(dcomb.reshape(HEADS * TILE, TILE)), f(qd_f), f(kd_f), f(qd_b), f(kd_b), f(c_f), f(c_b))


def _ret_kernel(nt, qf_ref, kf_ref, vf_ref, qb_ref, kb_ref, vb_ref, s0_ref, dcomb_ref, qdf_ref, kdf_ref,
                qdb_ref, kdb_ref, cf_ref, cb_ref, of_ref, ob_ref, sfin_ref, st_scr):
    t = pl.program_id(1)

    @pl.when(t == 0)
    def _():
        st_scr[...] = s0_ref[...]

    lane_head = lax.broadcasted_iota(jnp.int32, (TILE, MIX_G), 1) // HEAD_D
    rr = lax.broadcasted_iota(jnp.int32, (MIX_G, MIX_G), 0) // HEAD_D
    cc = lax.broadcasted_iota(jnp.int32, (MIX_G, MIX_G), 1) // HEAD_D
    same_head = rr == cc

    q = qf_ref[...]
    k = kf_ref[...]
    v = vf_ref[...]
    qs = jnp.concatenate([jnp.where(lane_head == h, q, jnp.zeros_like(q)) for h in range(HEADS)], axis=0)
    p = (_dot_nt(qs, k) * dcomb_ref[...]).astype(BF16)
    full = _dot(p, v)
    o = _dot((q.astype(F32) * qdf_ref[...]).astype(BF16), st_scr[0].astype(BF16))
    for h in range(HEADS):
        o = o + jnp.where(lane_head == h, full[h * TILE:(h + 1) * TILE, :], 0.0)
    of_ref[...] = o
    u = _dot_tn((k.astype(F32) * kdf_ref[...]).astype(BF16), v)
    st_scr[0] = st_scr[0] * cf_ref[...] + jnp.where(same_head, u, 0.0)

    qb = qb_ref[...]
    kb = kb_ref[...]
    vb = vb_ref[...]
    ob_ref[...] = _dot((qb.astype(F32) * qdb_ref[...]).astype(BF16), st_scr[1].astype(BF16))
    ub = _dot_tn((kb.astype(F32) * kdb_ref[...]).astype(BF16), vb)
    st_scr[1] = st_scr[1] * cb_ref[...] + jnp.where(same_head, ub, 0.0)

    @pl.when(t == nt - 1)
    def _():
        sfin_ref[...] = st_scr[...]


def _retention(n_seq, nt, q, k, v, s0, tables, name):
    fwd = lambda s, t: (s * nt + t, 0)
    bwd = lambda s, t: (s * nt + (nt - 1 - t), 0)
    const = lambda s, t: (0, 0)
    blk = lambda im: pl.BlockSpec((TILE, MIX_G), im)
    sq = lambda: pl.BlockSpec((MIX_G, MIX_G), const)
    in_specs = [blk(fwd), blk(fwd), blk(fwd), blk(bwd), blk(bwd), blk(bwd),
                pl.BlockSpec((None, 2, MIX_G, MIX_G), lambda s, t: (s, 0, 0, 0)),
                pl.BlockSpec((HEADS * TILE, TILE), const), sq(), sq(), sq(), sq(), sq(), sq()]
    nrows = n_seq * nt * TILE
    return pl.pallas_call(
        functools.partial(_ret_kernel, nt),
        grid=(n_seq, nt),
        in_specs=in_specs,
        out_specs=[blk(fwd), blk(bwd), pl.BlockSpec((None, 2, MIX_G, MIX_G), lambda s, t: (s, 0, 0, 0))],
        out_shape=[jax.ShapeDtypeStruct((nrows, MIX_G), F32),
                   jax.ShapeDtypeStruct((nrows, MIX_G), F32),
                   jax.ShapeDtypeStruct((n_seq, 2, MIX_G, MIX_G), F32)],
        scratch_shapes=[pltpu.VMEM((2, MIX_G, MIX_G), F32)],
        compiler_params=_cparams(2),
        name=name,
    )(q, k, v, q, k, v, s0, *tables)


def _attn_step(lam_init, has_cache, comp, refs, s_write, s_read, part0_ref):
    if has_cache:
        q_ref, k_ref, ck_ref, v_ref, cv_ref, lam_ref, sub_ref, o_ref = refs
        sw, sw2, mw = s_write
        sr, sr2, mr = s_read
    else:
        q_ref, k_ref, v_ref, lam_ref, sub_ref, o_ref = refs
        sw, mw = s_write
        sr, mr = s_read

    q = q_ref[...]
    lane = lax.broadcasted_iota(jnp.int32, q.shape, 1)
    qc = jnp.where((lane // HEAD_D) == comp, q, jnp.zeros_like(q))
    n_keys = k_ref.shape[0]
    kc = min(ATTN_KEY_CHUNK, n_keys)
    ones_rows = jnp.ones((ONES_ROWS, kc), BF16)
    m_prev = mr[...]
    oe, m_cur = None, None
    for j in range(n_keys // kc):
        ks = slice(j * kc, (j + 1) * kc)
        d = _dot(jnp.concatenate([v_ref[:, ks], ones_rows], axis=0),
                 jnp.exp2((sr[ks, :] - m_prev).astype(BF16)))
        oe = d if oe is None else oe + d
        s = _dot_nt(k_ref[ks, :], qc)
        sw[ks, :] = s
        mj = jnp.max(s, axis=0, keepdims=True)
        m_cur = mj if m_cur is None else jnp.maximum(m_cur, mj)
    if has_cache:
        oe = oe + _dot(jnp.concatenate([cv_ref[...], jnp.ones((ONES_ROWS, cv_ref.shape[1]), BF16)], axis=0),
                       jnp.exp2((sr2[...] - m_prev).astype(BF16)))
        s2 = _dot_nt(ck_ref[...], qc)
        sw2[...] = s2
        m_cur = jnp.maximum(m_cur, jnp.max(s2, axis=0, keepdims=True))
    mw[...] = m_cur
    part = oe[:128, :] / oe[128:129, :]
    if comp == 1:
        part0_ref[...] = part
    else:
        lp = lam_ref[...]
        lam = (jnp.exp(jnp.sum(lp[0:1] * lp[1:2], axis=-1, keepdims=True))
               - jnp.exp(jnp.sum(lp[2:3] * lp[3:4], axis=-1, keepdims=True)) + lam_init)
        o = part0_ref[...] - lam * part
        ms = jnp.mean(o * o, axis=0, keepdims=True)
        o = (o * lax.rsqrt(ms + RMS_EPS)).T
        o_ref[...] = (o * sub_ref[...] * (1.0 - lam_init)).astype(BF16)


def _attn_kernel(lam_init, has_cache, *refs):
    n_in = 8 if has_cache else 6
    io, part0_ref, scr = refs[:n_in], refs[n_in], refs[n_in + 1:]
    half = len(scr) // 2
    buf_a, buf_b = scr[:half], scr[half:]
    u = pl.program_id(0)

    @pl.when(u == 0)
    def _():
        for r in buf_b + (part0_ref,):
            r[...] = jnp.zeros(r.shape, r.dtype)

    @pl.when(u % 2 == 0)
    def _():
        _attn_step(lam_init, has_cache, 0, io, buf_a, buf_b, part0_ref)

    @pl.when(u % 2 == 1)
    def _():
        _attn_step(lam_init, has_cache, 1, io, buf_b, buf_a, part0_ref)


def _diff_attention(layer, n_seq, seq_len, q, k, v_t, cache_k, cache_v_t, da_lambda, subln):
    tq = min(ATTN_TQ, seq_len)
    nq = seq_len // tq
    n_units = n_seq * HEADS * nq * 2
    has_cache = cache_k is not None
    lam_init = 0.8 - 0.6 * math.exp(-0.3 * layer)

    cur = lambda u: jnp.minimum(u, n_units - 1) // 2
    prev = lambda u: jnp.maximum(u - 1, 0) // 2
    seq_of = lambda w: w // (HEADS * nq)
    head_of = lambda w: (w // nq) % HEADS
    rows_of = lambda w: seq_of(w) * nq + w % nq

    q_spec = pl.BlockSpec((tq, 128), lambda u: (rows_of(cur(u)), head_of(cur(u))))
    k_spec = pl.BlockSpec((seq_len, 128), lambda u: (seq_of(cur(u)), head_of(cur(u))))
    v_spec = pl.BlockSpec((128, seq_len), lambda u: (head_of(prev(u)), seq_of(prev(u))))
    ck_spec = pl.BlockSpec((None, None, PAST_LEN, 128), lambda u: (seq_of(cur(u)), layer, 0, head_of(cur(u))))
    cv_spec = pl.BlockSpec((None, None, 128, PAST_LEN), lambda u: (seq_of(prev(u)), layer, head_of(prev(u)), 0))
    tail = [pl.BlockSpec((None, 4, HEAD_D), lambda u: (layer, 0, 0)),
            pl.BlockSpec((None, 1, 128), lambda u: (layer, 0, 0))]
    part0 = [pltpu.VMEM((128, tq), F32)]
    if has_cache:
        in_specs = [q_spec, k_spec, ck_spec, v_spec, cv_spec] + tail
        args = [q, k, cache_k, v_t, cache_v_t, da_lambda, subln]
        scratch = part0 + [pltpu.VMEM((seq_len, tq), F32), pltpu.VMEM((PAST_LEN, tq), F32),
                           pltpu.VMEM((1, tq), F32)] * 2
    else:
        in_specs = [q_spec, k_spec, v_spec] + tail
        args = [q, k, v_t, da_lambda, subln]
        scratch = part0 + [pltpu.VMEM((seq_len, tq), F32), pltpu.VMEM((1, tq), F32)] * 2
    return pl.pallas_call(
        functools.partial(_attn_kernel, lam_init, has_cache),
        grid=(n_units + 1,),
        in_specs=in_specs,
        out_specs=pl.BlockSpec((tq, 128), lambda u: (rows_of(prev(u)), head_of(prev(u)))),
        out_shape=jax.ShapeDtypeStruct((n_seq * seq_len, DA_W), BF16),
        scratch_shapes=scratch,
        compiler_params=_cparams(1),
        name="diff_attention_lat" if has_cache else "diff_attention_ctx",
    )(*args)


def _route(logits):
    lane = lax.broadcasted_iota(jnp.int32, logits.shape, 1)
    lane_f = lane.astype(F32)
    big = 1e9
    is_g = jnp.logical_and(lane >= N_EXPERTS, lane < N_EXPERTS + N_GROUPS)
    gl = jnp.where(is_g, logits, NEG_BIG)
    gmax = jnp.max(gl, axis=-1, keepdims=True)
    gsel = jnp.min(jnp.where(gl == gmax, lane_f, big), axis=-1, keepdims=True) - N_EXPERTS
    g_w = 1.0 / jnp.sum(jnp.exp(gl - gmax), axis=-1, keepdims=True)
    in_grp = jnp.logical_and(lane < N_EXPERTS, (lane // EPG).astype(F32) == gsel)
    el = jnp.where(in_grp, logits, NEG_BIG)
    v1 = jnp.max(el, axis=-1, keepdims=True)
    i1 = jnp.min(jnp.where(el == v1, lane_f, big), axis=-1, keepdims=True)
    el2 = jnp.where(lane_f == i1, NEG_BIG, el)
    v2 = jnp.max(el2, axis=-1, keepdims=True)
    i2 = jnp.min(jnp.where(el2 == v2, lane_f, big), axis=-1, keepdims=True)
    t = jnp.exp(v2 - v1)
    w1 = g_w / (1.0 + t)
    w2 = w1 * t
    first_lo = i1 < i2
    a = jnp.minimum(i1, i2) - EPG * gsel
    b = jnp.maximum(i1, i2) - EPG * gsel
    bucket = gsel * N_PAIRS + a * (7.0 - a) * 0.5 + (b - a - 1.0)
    w_lo = jnp.where(first_lo, w1, w2)
    w_hi = jnp.where(first_lo, w2, w1)
    return jnp.where(lane == 0, bucket, jnp.where(lane == 1, w_lo, jnp.where(lane == 2, w_hi, 0.0)))


def _outproj_body(mod_ref, n2_ref, streams, hgn_ref, bd_ref, wout_ref, wrh_ref, wrl_ref, br_ref,
                  x1_ref, hp_ref, route_ref):
    x_ref, rof_ref, rob_ref, rsg_ref, da_ref, hof_ref, hob_ref, hsg_ref = streams
    ro = rof_ref[...] + rob_ref[...]
    ro = ro * lax.rsqrt(_group_mean_sq(ro, bd_ref, HEAD_D) + RMS_EPS) * rsg_ref[...].astype(F32)
    ho = hof_ref[...] + hob_ref[...]
    ho = ho * lax.rsqrt(_group_mean_sq(ho, bd_ref, HEAD_D) + RMS_EPS) * hgn_ref[...] * hsg_ref[...].astype(F32)
    mix = jnp.concatenate([ro.astype(BF16), da_ref[...], ho.astype(BF16)], axis=1)
    mixed = _dot(mix, wout_ref[...])
    gate1 = mod_ref[2:3, :]
    shift2 = mod_ref[3:4, :]
    scale2 = mod_ref[4:5, :]
    x1 = x_ref[...] + gate1 * mixed
    ms = jnp.mean(x1 * x1, axis=-1, keepdims=True)
    h2 = x1 * lax.rsqrt(ms + RMS_EPS) * n2_ref[...] * (1.0 + scale2) + shift2
    h_hi = h2.astype(BF16)
    h_lo = (h2 - h_hi.astype(F32)).astype(BF16)
    logits = _dot(h_hi, wrh_ref[...]) + _dot(h_lo, wrh_ref[...]) + _dot(h_hi, wrl_ref[...]) + br_ref[...]
    route = _route(logits)
    x1_ref[...] = x1
    route_ref[...] = route
    hp_ref[...] = jnp.concatenate([h2, route], axis=1)


def _outproj_kernel(mod_ref, n2_ref, *rest):
    ctx_streams, lat_streams, tail = rest[0:8], rest[8:16], rest[16:]
    t = pl.program_id(0)

    @pl.when(t < CTX_PTILES)
    def _():
        _outproj_body(mod_ref, n2_ref, ctx_streams, *tail)

    @pl.when(t >= CTX_PTILES)
    def _():
        _outproj_body(mod_ref, n2_ref, lat_streams, *tail)


def _cond_of_tile(t):
    return jnp.where(t < CTX_TILES, 0, 1 + (t - CTX_TILES) // LAT_TILES_PER_SEQ)


def _out_projection(layer, mod, norm2, ctx_streams, lat_streams, hgn_t, bd256, w_out_bf, w_route_hi, w_route_lo,
                    b_route):
    row = lambda t: (t, 0)
    ctx_row = lambda t: (jnp.minimum(t, CTX_PTILES - 1), 0)
    lat_row = lambda t: (jnp.maximum(t - CTX_PTILES, 0), 0)
    lay = lambda t: (layer, 0, 0)
    cond = lambda t: jnp.where(t < CTX_PTILES, 0, 1 + (t - CTX_PTILES) // LAT_PTILES_PER_SEQ)
    widths = (D_MODEL, MIX_G, MIX_G, MIX_G, DA_W, MIX_G, MIX_G, MIX_G)
    in_specs = [
        pl.BlockSpec((None, None, 6, D_MODEL), lambda t: (layer, cond(t), 0, 0)),
        pl.BlockSpec((None, 1, D_MODEL), lay),
    ]
    in_specs += [pl.BlockSpec((PTILE, w), ctx_row) for w in widths]
    in_specs += [pl.BlockSpec((PTILE, w), lat_row) for w in widths]
    in_specs += [
        pl.BlockSpec((None, 1, MIX_G), lay),
        pl.BlockSpec((MIX_G, MIX_G), lambda t: (0, 0)),
        pl.BlockSpec((None, D_MODEL, D_MODEL), lay),
        pl.BlockSpec((None, D_MODEL, 128), lay),
        pl.BlockSpec((None, D_MODEL, 128), lay),
        pl.BlockSpec((None, 1, 128), lay),
    ]
    return pl.pallas_call(
        _outproj_kernel,
        grid=(N_ROWS // PTILE,),
        in_specs=in_specs,
        out_specs=[pl.BlockSpec((PTILE, D_MODEL), row), pl.BlockSpec((PTILE, ROW_W), row),
                   pl.BlockSpec((PTILE, 128), row)],
        out_shape=[jax.ShapeDtypeStruct((N_ROWS, D_MODEL), F32), jax.ShapeDtypeStruct((N_ROWS, ROW_W), F32),
                   jax.ShapeDtypeStruct((N_ROWS, 128), F32)],
        compiler_params=_cparams(1),
        name="out_projection",
    )(mod, norm2, *ctx_streams, *lat_streams, hgn_t, bd256, w_out_bf, w_route_hi, w_route_lo, b_route)


def _dispatch_kernel(route_ref, dest_ref, meta_ref, cnt_ref):
    ph = pl.program_id(0)
    t = pl.program_id(1)
    lane = lax.broadcasted_iota(jnp.int32, (TILE, 128), 1).astype(F32)
    onehots = [(lane == route_ref[j * TILE:(j + 1) * TILE, 0:1]).astype(F32) for j in range(DISPATCH_TILES)]
    tile_cnts = [jnp.sum(oh, axis=0, keepdims=True) for oh in onehots]

    @pl.when(jnp.logical_and(ph == 0, t == 0))
    def _():
        cnt_ref[...] = jnp.zeros_like(cnt_ref)

    @pl.when(ph == 0)
    def _():
        cnt_ref[0:1, :] += sum(tile_cnts[1:], tile_cnts[0])

    @pl.when(jnp.logical_and(ph == 1, t == 0))
    def _():
        cnt = cnt_ref[0:1, :]
        padded = jnp.floor((cnt + (MOE_TILE - 1.0)) * (1.0 / MOE_TILE)) * MOE_TILE
        r = lax.broadcasted_iota(jnp.int32, (128, 128), 0)
        c = lax.broadcasted_iota(jnp.int32, (128, 128), 1)
        off = jnp.dot(jnp.broadcast_to(padded, (8, 128)), (r < c).astype(F32),
                      preferred_element_type=F32, precision=HI)[0:1, :]
        cnt_ref[1:2, :] = off
        end = off + padded
        end_col = jnp.sum(jnp.where(r == c, jnp.broadcast_to(end, (128, 128)), 0.0), axis=1, keepdims=True)
        ended = jnp.logical_and(end_col <= c.astype(F32) * MOE_TILE, r < N_BUCKETS)
        tile_bucket = jnp.sum(ended.astype(F32), axis=0, keepdims=True)
        lane1 = lax.broadcasted_iota(jnp.int32, (1, 128), 1)
        tile_idx = lane1.astype(F32)
        n_valid = jnp.max(jnp.where(lane1 < N_BUCKETS, end, 0.0), axis=1, keepdims=True) * (1.0 / MOE_TILE)
        last = jnp.max(jnp.where(tile_idx < n_valid, tile_bucket, 0.0), axis=1, keepdims=True)
        tb = jnp.where(tile_idx < n_valid, tile_bucket, last)
        g = jnp.floor(tb * (1.0 / N_PAIRS))
        pid = tb - g * N_PAIRS
        a = (pid >= 3.0).astype(F32) + (pid >= 5.0).astype(F32)
        b = pid + 1.0 - 2.0 * (a >= 1.0).astype(F32) - (a >= 2.0).astype(F32)
        rows = [EPG * g + a, EPG * g + b, jnp.broadcast_to(n_valid, (1, 128)), jnp.zeros((5, 128), F32)]
        meta_ref[...] = jnp.concatenate(rows, axis=0).astype(jnp.int32)

    @pl.when(ph == 1)
    def _():
        base = cnt_ref[1:2, :]
        rr = lax.broadcasted_iota(jnp.int32, (TILE, TILE), 0)
        cc = lax.broadcasted_iota(jnp.int32, (TILE, TILE), 1)
        for j, onehot in enumerate(onehots):
            oh16 = onehot.astype(BF16)
            same = _dot_nt(oh16, oh16)
            rank = jnp.sum(jnp.where(rr < cc, same, 0.0), axis=0, keepdims=True)
            base_row = lax.dot_general(jnp.broadcast_to(base, (8, 128)), onehot, (((1,), (1,)), ((), ())),
                                       preferred_element_type=F32, precision=HI)[0:1, :]
            dest_ref[j] = (base_row + rank).astype(jnp.int32)
            base = base + tile_cnts[j]
        cnt_ref[1:2, :] = base


def _moe_dispatch(route):
    dest, meta = pl.pallas_call(
        _dispatch_kernel,
        grid=(2, N_TILES // DISPATCH_TILES),
        in_specs=[pl.BlockSpec((DISPATCH_TILES * TILE, 128), lambda ph, t: (t, 0))],
        out_specs=[pl.BlockSpec((DISPATCH_TILES, 1, TILE), lambda ph, t: (t * ph, 0, 0)),
                   pl.BlockSpec((8, 128), lambda ph, t: (0, 0))],
        out_shape=[jax.ShapeDtypeStruct((N_TILES, 1, TILE), jnp.int32),
                   jax.ShapeDtypeStruct((8, 128), jnp.int32)],
        scratch_shapes=[pltpu.VMEM((8, 128), F32)],
        compiler_params=_cparams(2),
        name="moe_dispatch",
    )(route)
    return dest.reshape(N_ROWS), meta[0:3].reshape(3 * 128)


def _dma_params():
    return pltpu.CompilerParams(dimension_semantics=("arbitrary",), vmem_limit_bytes=VMEM_LIMIT,
                                disable_bounds_checks=True)


def _moe_scatter_kernel(dest_ref, h_ref, hs_init_ref, hs_ref, stage, sems):
    del hs_init_ref
    t = pl.program_id(0)
    base = t * TILE
    slot = t % 2
    stage[slot] = h_ref[...]

    def issue(i, carry):
        d = dest_ref[base + i]
        pltpu.make_async_copy(stage.at[slot, pl.ds(i, 1)], hs_ref.at[pl.ds(d, 1)], sems.at[slot]).start()
        return carry

    lax.fori_loop(0, TILE, issue, 0, unroll=8)

    def wait_tile(s):
        pltpu.make_async_copy(stage.at[s], hs_ref.at[pl.ds(0, TILE)], sems.at[s]).wait()

    @pl.when(t > 0)
    def _():
        wait_tile(1 - slot)

    @pl.when(t == pl.num_programs(0) - 1)
    def _():
        wait_tile(slot)


def _moe_scatter(dest, hp, hs_init):
    return pl.pallas_call(
        _moe_scatter_kernel,
        grid_spec=pltpu.PrefetchScalarGridSpec(
            num_scalar_prefetch=1,
            grid=(N_TILES,),
            in_specs=[pl.BlockSpec((TILE, ROW_W), lambda t, dest: (t, 0)),
                      pl.BlockSpec(memory_space=pl.ANY)],
            out_specs=pl.BlockSpec(memory_space=pl.ANY),
            scratch_shapes=[pltpu.VMEM((2, TILE, ROW_W), F32), pltpu.SemaphoreType.DMA((2,))],
        ),
        out_shape=jax.ShapeDtypeStruct((N_SLOTS, ROW_W), F32),
        input_output_aliases={2: 0},
        compiler_params=_dma_params(),
        name="moe_scatter",
    )(dest, hp, hs_init)


def _moe_ffn_kernel(meta_ref, hs_ref, wgl_ref, wul_ref, wdl_ref, wgh_ref, wuh_ref, wdh_ref, ys_ref):
    t = pl.program_id(0)
    n_valid = meta_ref[2 * 128]

    @pl.when(t < n_valid)
    def _():
        w = hs_ref[...]
        x = w[:, :D_MODEL].astype(BF16)
        rw = w[:, D_MODEL:]

        def expert(wg_ref, wu_ref, wd_ref, weight):
            a = _silu(_dot(x, wg_ref[...])) * _dot(x, wu_ref[...])
            return _dot((a * weight).astype(BF16), wd_ref[...])

        y = expert(wgl_ref, wul_ref, wdl_ref, rw[:, 1:2]) + expert(wgh_ref, wuh_ref, wdh_ref, rw[:, 2:3])
        ys_ref[...] = y

    @pl.when(t >= n_valid)
    def _():
        ys_ref[...] = jnp.zeros_like(ys_ref)


def _moe_ffn(layer, meta, hs, w_gate_bf, w_up_bf, w_down_bf):
    lo = lambda t, m: (layer, m[t], 0, 0)
    hi = lambda t, m: (layer, m[128 + t], 0, 0)
    up = lambda im: pl.BlockSpec((None, None, D_MODEL, EXPERT_FF), im)
    down = lambda im: pl.BlockSpec((None, None, EXPERT_FF, D_MODEL), im)
    return pl.pallas_call(
        _moe_ffn_kernel,
        grid_spec=pltpu.PrefetchScalarGridSpec(
            num_scalar_prefetch=1,
            grid=(MOE_MAX_TILES,),
            in_specs=[pl.BlockSpec((MOE_TILE, ROW_W), lambda t, m: (jnp.maximum(jnp.minimum(t, m[2 * 128] - 1), 0), 0)),
                      up(lo), up(lo), down(lo), up(hi), up(hi), down(hi)],
            out_specs=pl.BlockSpec((MOE_TILE, D_MODEL), lambda t, m: (t, 0)),
        ),
        out_shape=jax.ShapeDtypeStruct((N_SLOTS, D_MODEL), F32),
        compiler_params=_cparams(1),
        name="moe_ffn",
    )(meta, hs, w_gate_bf, w_up_bf, w_down_bf, w_gate_bf, w_up_bf, w_down_bf)


def _moe_combine_kernel(dest_ref, x1_ref, mod_ref, ys_ref, out_ctx_ref, out_lat_ref, buf, sems):
    t = pl.program_id(0)
    nt = pl.num_programs(0)

    def issue(tile, slot):
        def body(i, carry):
            d = dest_ref[tile * TILE + i]
            pltpu.make_async_copy(ys_ref.at[pl.ds(d, 1)], buf.at[slot, pl.ds(i, 1)], sems.at[slot]).start()
            return carry
        lax.fori_loop(0, TILE, body, 0, unroll=8)

    @pl.when(t == 0)
    def _():
        issue(0, 0)

    @pl.when(t + 1 < nt)
    def _():
        issue(t + 1, (t + 1) % 2)

    slot = t % 2
    pltpu.make_async_copy(ys_ref.at[pl.ds(0, TILE)], buf.at[slot], sems.at[slot]).wait()
    x2 = x1_ref[...] + mod_ref[5:6, :] * buf[slot]

    @pl.when(t < CTX_TILES)
    def _():
        out_ctx_ref[...] = x2

    @pl.when(t >= CTX_TILES)
    def _():
        out_lat_ref[...] = x2


def _moe_combine(layer, dest, x1, mod, ys):
    return pl.pallas_call(
        _moe_combine_kernel,
        grid_spec=pltpu.PrefetchScalarGridSpec(
            num_scalar_prefetch=1,
            grid=(N_TILES,),
            in_specs=[pl.BlockSpec((TILE, D_MODEL), lambda t, dest: (t, 0)),
                      pl.BlockSpec((None, None, 6, D_MODEL), lambda t, dest: (layer, _cond_of_tile(t), 0, 0)),
                      pl.BlockSpec(memory_space=pl.ANY)],
            out_specs=[pl.BlockSpec((TILE, D_MODEL), lambda t, dest: (jnp.minimum(t, CTX_TILES - 1), 0)),
                       pl.BlockSpec((TILE, D_MODEL), lambda t, dest: (jnp.maximum(t - CTX_TILES, 0), 0))],
            scratch_shapes=[pltpu.VMEM((2, TILE, D_MODEL), F32), pltpu.SemaphoreType.DMA((2,))],
        ),
        out_shape=[jax.ShapeDtypeStruct((N_CTX_ROWS, D_MODEL), F32), jax.ShapeDtypeStruct((N_LAT_ROWS, D_MODEL), F32)],
        compiler_params=_dma_params(),
        name="moe_combine",
    )(dest, x1, mod, ys)


def _block_diag_ones(n, group):
    i = np.arange(n) // group
    return jnp.asarray((i[:, None] == i[None, :]).astype(np.float32), dtype=BF16)


def _rope_tables():
    pos = np.arange(DEC_SEQ)
    rows = (pos // GRID_W).astype(np.float32)
    cols = (pos % GRID_W).astype(np.float32)
    inv_freq = (ROPE_BASE ** (-(np.arange(ROPE_PAIRS, dtype=np.float32) / ROPE_PAIRS))).astype(np.float32)
    ang_r = rows[:, None] * inv_freq[None, :]
    ang_c = cols[:, None] * inv_freq[None, :]
    cos64 = np.concatenate([np.cos(ang_r), np.cos(ang_r), np.cos(ang_c), np.cos(ang_c)], axis=1)
    sin64 = np.concatenate([-np.sin(ang_r), np.sin(ang_r), -np.sin(ang_c), np.sin(ang_c)], axis=1)
    return (jnp.asarray(np.tile(cos64, (1, 2)), F32), jnp.asarray(np.tile(sin64, (1, 2)), F32))


def _state_to_blockdiag_t(s):
    eye = jnp.eye(HEADS, dtype=F32)
    return jnp.einsum('bzhdv,hg->bzhvgd', s.astype(F32), eye).reshape(s.shape[0], 2, MIX_G, MIX_G)


def _blockdiag_t_to_state(st):
    s6 = st.reshape(st.shape[0], 2, HEADS, HEAD_D, HEADS, HEAD_D)
    diag = jnp.stack([s6[:, :, h, :, h, :] for h in range(HEADS)], axis=2)
    return jnp.swapaxes(diag, -1, -2)


def _state_to_blockdiag(s):
    eye = jnp.eye(HEADS, dtype=F32)
    return jnp.einsum('bzhdv,hg->bzhdgv', s.astype(F32), eye).reshape(s.shape[0], 2, MIX_G, MIX_G)


def _blockdiag_to_state(st):
    s6 = st.reshape(st.shape[0], 2, HEADS, HEAD_D, HEADS, HEAD_D)
    return jnp.stack([s6[:, :, h, :, h, :] for h in range(HEADS)], axis=2)


def _mixers(latent, layer, x, consts, params, ret_s0, hg_s0, cache_k, cache_v):
    (mod, lb_all, bd512, bd256, rope_cos, rope_sin, ret_tables) = consts
    n_seq = DEC_BATCH if latent else BATCH
    seq_len = DEC_SEQ if latent else SEQ
    nt = seq_len // TILE
    proj = _in_projection(latent, layer, x, mod, params['norm1'], params['w_in'], params['w_vt'], params['da_qn'],
                          params['da_kn'], lb_all, bd512, rope_cos, rope_sin)
    (rq, rk, rv, rsg, dq, dk, dv, hq, hkf, hkb, hv, hgf, hgb, hsg) = proj[:14]
    ret_of, ret_ob, ret_fin = _retention(n_seq, nt, rq, rk, rv, ret_s0, ret_tables,
                                         "retention_lat" if latent else "retention_ctx")
    hg_of, hg_ob, hg_fin = _gated_scan(n_seq, nt, hq, hkf, hkb, hv, hgf, hgb, hg_s0,
                                       "hgrn_scan_lat" if latent else "hgrn_scan_ctx")
    da_o = _diff_attention(layer, n_seq, seq_len, dq, dk, dv, cache_k, cache_v,
                           params['da_lambda'], params['da_subln'])
    streams = (x, ret_of, ret_ob, rsg, da_o, hg_of, hg_ob, hsg)
    extras = None if latent else (proj[14], proj[15], ret_fin, hg_fin)
    return streams, extras


def _trunk_layer(layer, x_ctx, x_lat, sorted_buf, consts, params, lat_ret_s0, lat_hg_s0, zero_state, cache_k, cache_v):
    mod, bd256 = consts[0], consts[3]
    ctx_streams, extras = _mixers(False, layer, x_ctx, consts, params, zero_state, zero_state, None, None)
    lat_streams, _ = _mixers(True, layer, x_lat, consts, params, lat_ret_s0, lat_hg_s0, cache_k, cache_v)
    x1, hp, route = _out_projection(layer, mod, params['norm2'], ctx_streams, lat_streams, params['hg_norm'],
                                    bd256, params['w_out'], params['w_route_hi'], params['w_route_lo'],
                                    params['b_route'])
    dest, meta = _moe_dispatch(route)
    sorted_buf = _moe_scatter(dest, hp, sorted_buf)
    ys = _moe_ffn(layer, meta, sorted_buf, params['w_gate'], params['w_up'], params['w_down'])
    x_ctx, x_lat = _moe_combine(layer, dest, x1, mod, ys)
    return x_ctx, x_lat, sorted_buf, extras


def kernel(x_prompt, x_sample, cache_k, cache_v, state_ret, state_hgrn, c, c_ctx, norm1, norm2, w_mod, b_mod,
           w_in, w_out, da_qn, da_kn, da_lambda, da_subln, hg_lb, hg_norm, w_group, b_group, w_router,
           b_router, w_gate, w_up, w_down):
    cond = jnp.zeros((N_COND, D_MODEL), F32).at[0].set(c_ctx).at[1:1 + DEC_BATCH].set(c)
    mod, lb_all = _modulation(cond, w_mod, b_mod, hg_lb)
    mod = mod.reshape(DEPTH, N_COND, 6, D_MODEL)
    rope_cos, rope_sin = _rope_tables()
    consts = (mod, lb_all.reshape(DEPTH, 1, MIX_G), _block_diag_ones(DA_W, HEAD_D),
              _block_diag_ones(MIX_G, HEAD_D), rope_cos, rope_sin, _retention_tables())
    pad = jnp.zeros((DEPTH, D_MODEL, 128 - N_EXPERTS - N_GROUPS), F32)
    w_route = jnp.concatenate([w_router, w_group, pad], axis=-1)
    params = {
        'norm1': norm1.reshape(DEPTH, 1, D_MODEL), 'norm2': norm2.reshape(DEPTH, 1, D_MODEL),
        'w_in': w_in.astype(BF16), 'w_out': w_out.astype(BF16),
        'w_vt': jnp.swapaxes(w_in[:, :, 2048:2560], 1, 2).astype(BF16),
        'da_qn': jnp.tile(da_qn, (1, DA_W // HEAD_D)).reshape(DEPTH, 1, DA_W),
        'da_kn': jnp.tile(da_kn, (1, DA_W // HEAD_D)).reshape(DEPTH, 1, DA_W),
        'da_lambda': da_lambda, 'da_subln': da_subln.reshape(DEPTH, 1, 128),
        'hg_norm': jnp.tile(hg_norm, (1, HEADS)).reshape(DEPTH, 1, MIX_G),
        'w_route_hi': w_route.astype(BF16),
        'w_route_lo': (w_route - w_route.astype(BF16).astype(F32)).astype(BF16),
        'b_route': jnp.concatenate([b_router, b_group, pad[:, 0, :]], axis=-1).reshape(DEPTH, 1, 128),
        'w_gate': w_gate.astype(BF16), 'w_up': w_up.astype(BF16), 'w_down': w_down.astype(BF16),
    }
    ck_bf = cache_k.astype(BF16).reshape(DEC_BATCH, DEPTH, PAST_LEN, DA_W)
    cv_bf = jnp.swapaxes(cache_v.astype(BF16).reshape(DEC_BATCH, DEPTH, PAST_LEN, DA_W), 2, 3)
    zero_state = jnp.zeros((BATCH, 2, MIX_G, MIX_G), F32)

    yp = x_prompt.reshape(N_CTX_ROWS, D_MODEL)
    ys = x_sample.reshape(N_LAT_ROWS, D_MODEL)
    sorted_buf = jnp.zeros((N_SLOTS, ROW_W), F32)
    ks_out, vs_out, rets_out, hgs_out = [], [], [], []
    for l in range(DEPTH):
        yp, ys, sorted_buf, (k_new, v_new, ret_fin, hg_fin) = _trunk_layer(
            l, yp, ys, sorted_buf, consts, params, _state_to_blockdiag(state_ret[:, l]),
            _state_to_blockdiag_t(state_hgrn[:, l]), zero_state, ck_bf, cv_bf)
        ks_out.append(k_new.reshape(BATCH, SEQ, HEADS, 2, HEAD_D))
        vs_out.append(v_new.reshape(BATCH, SEQ, HEADS, 2 * HEAD_D))
        rets_out.append(_blockdiag_to_state(ret_fin))
        hgs_out.append(_blockdiag_t_to_state(hg_fin))
    return (yp.reshape(BATCH, SEQ, D_MODEL), ys.reshape(DEC_BATCH, DEC_SEQ, D_MODEL),
            jnp.stack(ks_out, axis=1), jnp.stack(vs_out, axis=1),
            jnp.stack(rets_out, axis=1), jnp.stack(hgs_out, axis=1))
```

```python
import functools
import math

import numpy as np
import jax
import jax.numpy as jnp
from jax import lax
from jax.experimental import pallas as pl
from jax.experimental.pallas import tpu as pltpu

F32 = jnp.float32
BF16 = jnp.bfloat16

D_MODEL = 1024
BATCH = 16
SEQ = 256
DEPTH = 4
DEC_BATCH = 4
DEC_SEQ = 4096
PAST_LEN = 256
GRID_W = 64
HEADS = 4
HEAD_D = 64
MIX_G = HEADS * HEAD_D
DA_W = 512
IN_COLS = 3840
ROPE_BASE = 10000.0
ROPE_PAIRS = 16
RMS_EPS = 1e-6
N_GROUPS = 4
EPG = 4
N_EXPERTS = 16
EXPERT_FF = 512

TILE = 256
N_CTX_ROWS = BATCH * SEQ
N_LAT_ROWS = DEC_BATCH * DEC_SEQ
N_ROWS = N_CTX_ROWS + N_LAT_ROWS
CTX_TILES = N_CTX_ROWS // TILE
LAT_TILES_PER_SEQ = DEC_SEQ // TILE
N_TILES = N_ROWS // TILE
PTILE = 512
CTX_PTILES = N_CTX_ROWS // PTILE
LAT_PTILES_PER_SEQ = DEC_SEQ // PTILE
N_COND = 8

N_PAIRS = EPG * (EPG - 1) // 2
N_BUCKETS = N_GROUPS * N_PAIRS
MOE_TILE = 256
MOE_MAX_TILES = N_TILES + N_BUCKETS
DISPATCH_TILES = 4
N_SLOTS = MOE_MAX_TILES * MOE_TILE
ROW_W = D_MODEL + 128

CHUNK = 64
SUB = 16
NSUB = CHUNK // SUB
NCHUNK = TILE // CHUNK

VMEM_LIMIT = 48 * 1024 * 1024
NEG_BIG = -1e30
ONES_ROWS = 16
ATTN_TQ = 512
ATTN_KEY_CHUNK = 512
LOG2_E = 1.4426950408889634

HI = lax.Precision.HIGHEST


def _cparams(n_axes):
    return pltpu.CompilerParams(dimension_semantics=("arbitrary",) * n_axes,
                                vmem_limit_bytes=VMEM_LIMIT)


def _dot(a, b):
    return jnp.dot(a, b, preferred_element_type=F32)


def _dot_nt(a, b):
    return lax.dot_general(a, b, (((1,), (1,)), ((), ())), preferred_element_type=F32)


def _dot_tn(a, b):
    return lax.dot_general(a, b, (((0,), (0,)), ((), ())), preferred_element_type=F32)


def _silu(x):
    return x * (1.0 / (1.0 + jnp.exp(-x)))


def _sigmoid(x):
    return 1.0 / (1.0 + jnp.exp(-x))


def _mod_kernel(cond_ref, w_ref, b_ref, lb_ref, mod_ref, lbo_ref):
    c = cond_ref[...]
    m = jnp.dot(_silu(c), w_ref[...], preferred_element_type=F32, precision=HI)
    mod_ref[...] = m + b_ref[...]
    z = lb_ref[...]
    zmax = jnp.max(z, axis=0, keepdims=True)
    e = jnp.exp(z - zmax)
    p = e / jnp.sum(e, axis=0, keepdims=True)
    rows = [jnp.zeros_like(p[0:1])]
    for l in range(1, DEPTH):
        rows.append(rows[-1] + p[l:l + 1])
    lbo_ref[...] = jnp.concatenate(rows, axis=0)


def _modulation(cond, w_mod, b_mod, hg_lb):
    nblk = 6
    return pl.pallas_call(
        _mod_kernel,
        grid=(DEPTH, nblk),
        in_specs=[
            pl.BlockSpec((N_COND, D_MODEL), lambda l, j: (0, 0)),
            pl.BlockSpec((None, D_MODEL, D_MODEL), lambda l, j: (l, 0, j)),
            pl.BlockSpec((None, 1, D_MODEL), lambda l, j: (l, 0, j)),
            pl.BlockSpec((DEPTH, MIX_G), lambda l, j: (0, 0)),
        ],
        out_specs=[
            pl.BlockSpec((None, N_COND, D_MODEL), lambda l, j: (l, 0, j)),
            pl.BlockSpec((DEPTH, MIX_G), lambda l, j: (0, 0)),
        ],
        out_shape=[jax.ShapeDtypeStruct((DEPTH, N_COND, 6 * D_MODEL), F32),
                   jax.ShapeDtypeStruct((DEPTH, MIX_G), F32)],
        compiler_params=_cparams(2),
        name="modulation",
    )(cond, w_mod, b_mod.reshape(DEPTH, 1, 6 * D_MODEL), hg_lb)


def _group_mean_sq(x, bd_ref, group):
    return _dot((x * x).astype(BF16), bd_ref[...]) * (1.0 / group)


def _swap16(x):
    w = x.shape[-1]
    lane = lax.broadcasted_iota(jnp.int32, x.shape, x.ndim - 1)
    up = pltpu.roll(x, w - 16, x.ndim - 1)
    dn = pltpu.roll(x, 16, x.ndim - 1)
    return jnp.where((lane % 32) < 16, up, dn)


def _inproj_kernel(latent, x_ref, mod_ref, n1_ref, w_ref, wvt_ref, qn_ref, kn_ref, lb_ref, bd_ref, *rest):
    if latent:
        cos_ref, sin_ref = rest[0], rest[1]
        outs = rest[2:]
    else:
        outs = rest
    (rq_ref, rk_ref, rv_ref, rsg_ref, dq_ref, dk_ref, dv_ref,
     hq_ref, hkf_ref, hkb_ref, hv_ref, hgf_ref, hgb_ref, hsg_ref) = outs[:14]

    x = x_ref[...]
    ms = jnp.mean(x * x, axis=-1, keepdims=True)
    shift1 = mod_ref[0:1, :]
    scale1 = mod_ref[1:2, :]
    h = x * lax.rsqrt(ms + RMS_EPS) * n1_ref[...] * (1.0 + scale1) + shift1
    h16 = h.astype(BF16)
    y = _dot(h16, w_ref[...])

    rq_ref[...] = y[:, 0:256].astype(BF16)
    rk_ref[...] = (y[:, 256:512] * (HEAD_D ** -0.5)).astype(BF16)
    rv_ref[...] = y[:, 512:768].astype(BF16)
    rsg_ref[...] = _silu(y[:, 768:1024]).astype(BF16)

    dq = y[:, 1024:1536]
    dk = y[:, 1536:2048]
    qn = dq * lax.rsqrt(_group_mean_sq(dq, bd_ref, HEAD_D) + RMS_EPS) * qn_ref[...]
    kn = dk * lax.rsqrt(_group_mean_sq(dk, bd_ref, HEAD_D) + RMS_EPS) * kn_ref[...]
    if latent:
        cos = jnp.concatenate([cos_ref[...]] * 4, axis=1)
        sin = jnp.concatenate([sin_ref[...]] * 4, axis=1)
        qr = qn * cos + _swap16(qn) * sin
        kr = kn * cos + _swap16(kn) * sin
    else:
        qr, kr = qn, kn
        ck_ref, cv_ref = outs[14], outs[15]
        ck_ref[...] = kn
        cv_ref[...] = y[:, 2048:2560]
    dq_ref[...] = (qr * (HEAD_D ** -0.5 * LOG2_E)).astype(BF16)
    dk_ref[...] = kr.astype(BF16)
    dv_ref[...] = _dot_nt(wvt_ref[...], h16).astype(BF16)

    lb = lb_ref[...]
    f_f = lb + (1.0 - lb) * _sigmoid(y[:, 2816:3072])
    f_b = lb + (1.0 - lb) * _sigmoid(y[:, 3072:3328])
    hq_ref[...] = y[:, 2560:2816].astype(BF16)
    hkf_ref[...] = (1.0 - f_f).astype(BF16)
    hkb_ref[...] = (1.0 - f_b).astype(BF16)
    hgf_ref[...] = jnp.log2(f_f)
    hgb_ref[...] = jnp.log2(f_b)
    hv_ref[...] = y[:, 3328:3584].astype(BF16)
    hsg_ref[...] = _silu(y[:, 3584:3840]).astype(BF16)


def _in_projection(latent, layer, x, mod, norm1, w_in_bf, w_vt_bf, qn_t, kn_t, lb_all, bd512, rope_cos, rope_sin):
    nrows = x.shape[0]
    ntiles = nrows // PTILE
    cond = (lambda t: 1 + t // LAT_PTILES_PER_SEQ) if latent else (lambda t: 0)

    in_specs = [
        pl.BlockSpec((PTILE, D_MODEL), lambda t: (t, 0)),
        pl.BlockSpec((None, None, 6, D_MODEL), lambda t: (layer, cond(t), 0, 0)),
        pl.BlockSpec((None, 1, D_MODEL), lambda t: (layer, 0, 0)),
        pl.BlockSpec((None, D_MODEL, IN_COLS), lambda t: (layer, 0, 0)),
        pl.BlockSpec((None, DA_W, D_MODEL), lambda t: (layer, 0, 0)),
        pl.BlockSpec((None, 1, DA_W), lambda t: (layer, 0, 0)),
        pl.BlockSpec((None, 1, DA_W), lambda t: (layer, 0, 0)),
        pl.BlockSpec((None, 1, MIX_G), lambda t: (layer, 0, 0)),
        pl.BlockSpec((DA_W, DA_W), lambda t: (0, 0)),
    ]
    args = [x, mod, norm1, w_in_bf, w_vt_bf, qn_t, kn_t, lb_all, bd512]
    if latent:
        in_specs += [pl.BlockSpec((PTILE, 128), lambda t: (t % LAT_PTILES_PER_SEQ, 0))] * 2
        args += [rope_cos, rope_sin]

    def o(width, dtype):
        return jax.ShapeDtypeStruct((nrows, width), dtype), pl.BlockSpec((PTILE, width), lambda t: (t, 0))

    dv_t = (jax.ShapeDtypeStruct((DA_W, nrows), BF16), pl.BlockSpec((DA_W, PTILE), lambda t: (0, t)))
    outs = [o(256, BF16), o(256, BF16), o(256, BF16), o(256, BF16),
            o(512, BF16), o(512, BF16), dv_t,
            o(256, BF16), o(256, BF16), o(256, BF16), o(256, BF16), o(256, F32), o(256, F32), o(256, BF16)]
    if not latent:
        outs += [o(512, F32), o(512, F32)]
    return pl.pallas_call(
        functools.partial(_inproj_kernel, latent),
        grid=(ntiles,),
        in_specs=in_specs,
        out_specs=[s for _, s in outs],
        out_shape=[s for s, _ in outs],
        compiler_params=_cparams(1),
        name="in_projection_lat" if latent else "in_projection_ctx",
    )(*args)


def _head_masks():
    lane = lax.broadcasted_iota(jnp.int32, (1, MIX_G), 1)
    return [(lane // HEAD_D == h).astype(F32) for h in range(HEADS)]


def _scan_chunks(items, st_f, st_b, consts_f, consts_b):
    n = len(items)
    cs = [consts_b if it[4] else consts_f for it in items]
    zero_row = jnp.zeros((1, MIX_G), F32)

    bs = []
    for x in range(n):
        g = items[x][3]
        g_hi = g.astype(BF16)
        g_lo = (g - g_hi.astype(F32)).astype(BF16)
        bs.append(_dot(cs[x][0], g_hi) + _dot(cs[x][0], g_lo))

    prep = []
    for x in range(n):
        q, k, v, _, rev = items[x]
        hmask = cs[x][4]
        b = bs[x]
        r_rows, e_rows = [], []
        for s in range(NSUB):
            lo, hi = s * SUB, s * SUB + SUB - 1
            if not rev:
                r_rows.append(b[lo - 1:lo, :] if s > 0 else zero_row)
                e_rows.append(b[hi:hi + 1, :])
            else:
                r_rows.append(b[hi + 1:hi + 2, :] if s < NSUB - 1 else zero_row)
                e_rows.append(b[lo:lo + 1, :])
        btot = e_rows[NSUB - 1] if not rev else e_rows[0]
        r_full = jnp.concatenate([jnp.broadcast_to(r, (SUB, MIX_G)) for r in r_rows], axis=0)
        e_full = jnp.concatenate([jnp.broadcast_to(e, (SUB, MIX_G)) for e in e_rows], axis=0)
        bl = b - r_full
        qp = q * jnp.exp2(bl)
        kend = k * jnp.exp2(e_full - b)
        pairs = [(i, j) for i in range(NSUB) for j in range(NSUB) if (j < i if not rev else j > i)]
        lh = jnp.concatenate([qp[i * SUB:(i + 1) * SUB, :] * jnp.exp2(r_rows[i] - e_rows[j])
                              for (i, j) in pairs], axis=0)
        prep.append(dict(
            bl=bl, btot=btot, pairs=pairs,
            q_inter=(qp * jnp.exp2(r_full)).astype(BF16),
            k_state=(kend * jnp.exp2(btot - e_full)).astype(BF16),
            kend=kend.astype(BF16),
            lhs4=jnp.concatenate([lh * hmask[h] for h in range(HEADS)], axis=0).astype(BF16)))

    uts = [_dot_tn(items[x][2], prep[x]['k_state']) for x in range(n)]
    scs = [_dot_nt(prep[x]['lhs4'], prep[x]['kend']) for x in range(n)]

    st_in = []
    for x in range(n):
        rev = items[x][4]
        st = st_b if rev else st_f
        st_in.append(st.astype(BF16))
        st = st * jnp.exp2(prep[x]['btot']) + uts[x] * cs[x][1]
        if rev:
            st_b = st
        else:
            st_f = st
    col = lax.broadcasted_iota(jnp.int32, (SUB, CHUNK), 1) // SUB
    jrow = lax.broadcasted_iota(jnp.int32, (SUB, MIX_G), 0)
    pss, pps, vts = [], [], []
    for x in range(n):
        q, k, v, _, rev = items[x]
        pairs, sc, bl = prep[x]['pairs'], scs[x], prep[x]['bl']
        p_rows = []
        for h in range(HEADS):
            for i in range(NSUB):
                acc = None
                for p, (pi, pj) in enumerate(pairs):
                    if pi != i:
                        continue
                    base = (h * len(pairs) + p) * SUB
                    blk = jnp.where(col == pj, sc[base:base + SUB, :], 0.0)
                    acc = blk if acc is None else acc + blk
                p_rows.append(jnp.zeros((SUB, CHUNK), F32) if acc is None else acc)
        pss.append(jnp.concatenate(p_rows, axis=0).astype(BF16))
        pp, vt = [], []
        for s in range(NSUB):
            sl = slice(s * SUB, (s + 1) * SUB)
            bl_s, q_s, k_s = bl[sl, :], q[sl, :], k[sl, :]
            for i in range(SUB):
                d = bl_s[i:i + 1, :] - bl_s
                keep = (jrow <= i) if not rev else (jrow >= i)
                e = jnp.exp2(jnp.where(keep, d, NEG_BIG))
                pp.append((e * q_s[i:i + 1, :] * k_s).astype(BF16))
            vt.extend([v[sl, :]] * SUB)
        pps.append(jnp.concatenate(pp, axis=0))
        vts.append(jnp.concatenate(vt, axis=0))

    o_int = [_dot_nt(prep[x]['q_inter'], st_in[x]) for x in range(n)]
    fulls = [_dot(pss[x], items[x][2]) for x in range(n)]
    sbs = [_dot(pps[x], cs[x][2]) for x in range(n)]

    ws = [(sbs[x] * vts[x].astype(F32)).astype(BF16) for x in range(n)]
    o_diag = [_dot(cs[x][3], ws[x]) for x in range(n)]

    outs = []
    for x in range(n):
        hmask = cs[x][4]
        o = o_int[x] + o_diag[x]
        for h in range(HEADS):
            o = o + fulls[x][h * CHUNK:(h + 1) * CHUNK, :] * hmask[h]
        outs.append(o)
    return outs, st_f, st_b


def _scan_consts(rev):
    r = lax.broadcasted_iota(jnp.int32, (CHUNK, CHUNK), 0)
    c = lax.broadcasted_iota(jnp.int32, (CHUNK, CHUNK), 1)
    tri = ((c <= r) if not rev else (c >= r)).astype(BF16)
    rr = lax.broadcasted_iota(jnp.int32, (MIX_G, MIX_G), 0) // HEAD_D
    cc = lax.broadcasted_iota(jnp.int32, (MIX_G, MIX_G), 1) // HEAD_D
    bd_mask = (rr == cc).astype(F32)
    ones_bd = bd_mask.astype(BF16)
    gi = lax.broadcasted_iota(jnp.int32, (CHUNK, CHUNK * SUB), 0)
    gj = lax.broadcasted_iota(jnp.int32, (CHUNK, CHUNK * SUB), 1) // SUB
    gsel = (gi == gj).astype(BF16)
    return tri, bd_mask, ones_bd, gsel, _head_masks()


def _scan_kernel(nt, qf_ref, kf_ref, vf_ref, gf_ref, qb_ref, kb_ref, vb_ref, gb_ref, s0_ref,
                 of_ref, ob_ref, sfin_ref, st_scr):
    t = pl.program_id(1)

    @pl.when(t == 0)
    def _():
        st_scr[...] = s0_ref[...]

    cf = _scan_consts(False)
    cb = _scan_consts(True)

    items, rows = [], []
    for c in range(NCHUNK):
        rf = slice(c * CHUNK, (c + 1) * CHUNK)
        rb = slice((NCHUNK - 1 - c) * CHUNK, (NCHUNK - c) * CHUNK)
        items.append((qf_ref[rf, :].astype(F32), kf_ref[rf, :].astype(F32), vf_ref[rf, :], gf_ref[rf, :], False))
        items.append((qb_ref[rb, :].astype(F32), kb_ref[rb, :].astype(F32), vb_ref[rb, :], gb_ref[rb, :], True))
        rows += [(of_ref, rf), (ob_ref, rb)]
    outs, st_f, st_b = _scan_chunks(items, st_scr[0], st_scr[1], cf, cb)
    for (ref, sl), o in zip(rows, outs):
        ref[sl, :] = o
    st_scr[0] = st_f
    st_scr[1] = st_b

    @pl.when(t == nt - 1)
    def _():
        sfin_ref[...] = st_scr[...]


def _gated_scan(n_seq, nt, q, k_f, k_b, v, g_f, g_b, s0, name):
    fwd = lambda s, t: (s * nt + t, 0)
    bwd = lambda s, t: (s * nt + (nt - 1 - t), 0)
    blk = lambda im: pl.BlockSpec((TILE, MIX_G), im)
    in_specs = [blk(fwd), blk(fwd), blk(fwd), blk(fwd), blk(bwd), blk(bwd), blk(bwd), blk(bwd),
                pl.BlockSpec((None, 2, MIX_G, MIX_G), lambda s, t: (s, 0, 0, 0))]
    nrows = n_seq * nt * TILE
    return pl.pallas_call(
        functools.partial(_scan_kernel, nt),
        grid=(n_seq, nt),
        in_specs=in_specs,
        out_specs=[pl.BlockSpec((TILE, MIX_G), lambda s, t: (s * nt + t, 0)),
                   pl.BlockSpec((TILE, MIX_G), lambda s, t: (s * nt + (nt - 1 - t), 0)),
                   pl.BlockSpec((None, 2, MIX_G, MIX_G), lambda s, t: (s, 0, 0, 0))],
        out_shape=[jax.ShapeDtypeStruct((nrows, MIX_G), F32),
                   jax.ShapeDtypeStruct((nrows, MIX_G), F32),
                   jax.ShapeDtypeStruct((n_seq, 2, MIX_G, MIX_G), F32)],
        scratch_shapes=[pltpu.VMEM((2, MIX_G, MIX_G), F32)],
        compiler_params=_cparams(2),
        name=name,
    )(q, k_f, v, g_f, q, k_b, v, g_b, s0)


def _retention_tables():
    gam = 1.0 - 2.0 ** (-5.0 - np.arange(HEADS, dtype=np.float64))
    gam_r = gam[::-1]
    i = np.arange(TILE, dtype=np.float64)
    diff = i[:, None] - i[None, :]
    dcomb = np.zeros((HEADS, TILE, TILE))
    for h in range(HEADS):
        lower = np.where(diff > 0, gam[h] ** np.maximum(diff, 0), 0.0)
        upper = np.where(diff < 0, gam_r[h] ** np.maximum(-diff, 0), 0.0)
        dcomb[h] = lower + upper + 2.0 * (diff == 0)
    lanes = lambda per_head: np.repeat(per_head, HEAD_D, axis=-1)
    qd_f = lanes(gam[None, :] ** (i[:, None] + 1.0))
    kd_f = lanes(gam[None, :] ** (TILE - 1.0 - i[:, None]))
    qd_b = lanes(gam_r[None, :] ** (TILE - i[:, None]))
    kd_b = lanes(gam_r[None, :] ** i[:, None])
    blk = (np.arange(MIX_G)[:, None] // HEAD_D) == (np.arange(MIX_G)[None, :] // HEAD_D)
    c_f = np.where(blk, lanes(gam ** TILE)[None, :], 0.0) * np.ones((MIX_G, 1))
    c_b = np.where(blk, lanes(gam_r ** TILE)[None, :], 0.0) * np.ones((MIX_G, 1))
    f = lambda a: jnp.asarray(a, F32)
    return (f(dcomb.reshape(HEADS * TILE, TILE)), f(qd_f), f(kd_f), f(qd_b), f(kd_b), f(c_f), f(c_b))


def _ret_kernel(nt, qf_ref, kf_ref, vf_ref, qb_ref, kb_ref, vb_ref, s0_ref, dcomb_ref, qdf_ref, kdf_ref,
                qdb_ref, kdb_ref, cf_ref, cb_ref, of_ref, ob_ref, sfin_ref, st_scr):
    t = pl.program_id(1)

    @pl.when(t == 0)
    def _():
        st_scr[...] = s0_ref[...]

    lane_head = lax.broadcasted_iota(jnp.int32, (TILE, MIX_G), 1) // HEAD_D
    rr = lax.broadcasted_iota(jnp.int32, (MIX_G, MIX_G), 0) // HEAD_D
    cc = lax.broadcasted_iota(jnp.int32, (MIX_G, MIX_G), 1) // HEAD_D
    same_head = rr == cc

    q = qf_ref[...]
    k = kf_ref[...]
    v = vf_ref[...]
    qs = jnp.concatenate([jnp.where(lane_head == h, q, jnp.zeros_like(q)) for h in range(HEADS)], axis=0)
    p = (_dot_nt(qs, k) * dcomb_ref[...]).astype(BF16)
    full = _dot(p, v)
    o = _dot((q.astype(F32) * qdf_ref[...]).astype(BF16), st_scr[0].astype(BF16))
    for h in range(HEADS):
        o = o + jnp.where(lane_head == h, full[h * TILE:(h + 1) * TILE, :], 0.0)
    of_ref[...] = o
    u = _dot_tn((k.astype(F32) * kdf_ref[...]).astype(BF16), v)
    st_scr[0] = st_scr[0] * cf_ref[...] + jnp.where(same_head, u, 0.0)

    qb = qb_ref[...]
    kb = kb_ref[...]
    vb = vb_ref[...]
    ob_ref[...] = _dot((qb.astype(F32) * qdb_ref[...]).astype(BF16), st_scr[1].astype(BF16))
    ub = _dot_tn((kb.astype(F32) * kdb_ref[...]).astype(BF16), vb)
    st_scr[1] = st_scr[1] * cb_ref[...] + jnp.where(same_head, ub, 0.0)

    @pl.when(t == nt - 1)
    def _():
        sfin_ref[...] = st_scr[...]


def _retention(n_seq, nt, q, k, v, s0, tables, name):
    fwd = lambda s, t: (s * nt + t, 0)
    bwd = lambda s, t: (s * nt + (nt - 1 - t), 0)
    const = lambda s, t: (0, 0)
    blk = lambda im: pl.BlockSpec((TILE, MIX_G), im)
    sq = lambda: pl.BlockSpec((MIX_G, MIX_G), const)
    in_specs = [blk(fwd), blk(fwd), blk(fwd), blk(bwd), blk(bwd), blk(bwd),
                pl.BlockSpec((None, 2, MIX_G, MIX_G), lambda s, t: (s, 0, 0, 0)),
                pl.BlockSpec((HEADS * TILE, TILE), const), sq(), sq(), sq(), sq(), sq(), sq()]
    nrows = n_seq * nt * TILE
    return pl.pallas_call(
        functools.partial(_ret_kernel, nt),
        grid=(n_seq, nt),
        in_specs=in_specs,
        out_specs=[blk(fwd), blk(bwd), pl.BlockSpec((None, 2, MIX_G, MIX_G), lambda s, t: (s, 0, 0, 0))],
        out_shape=[jax.ShapeDtypeStruct((nrows, MIX_G), F32),
                   jax.ShapeDtypeStruct((nrows, MIX_G), F32),
                   jax.ShapeDtypeStruct((n_seq, 2, MIX_G, MIX_G), F32)],
        scratch_shapes=[pltpu.VMEM((2, MIX_G, MIX_G), F32)],
        compiler_params=_cparams(2),
        name=name,
    )(q, k, v, q, k, v, s0, *tables)


def _attn_step(lam_init, comp, refs, s_write, s_read, part0_ref):
    q_ref, k_ref, ck_ref, v_ref, cv_ref, lam_ref, sub_ref, o_ref = refs
    sw, sw2, mw = s_write
    sr, sr2, mr = s_read

    q = q_ref[...]
    lane = lax.broadcasted_iota(jnp.int32, q.shape, 1)
    qc = jnp.where((lane // HEAD_D) == comp, q, jnp.zeros_like(q))
    n_keys = k_ref.shape[0]
    kc = min(ATTN_KEY_CHUNK, n_keys)
    ones_rows = jnp.ones((ONES_ROWS, kc), BF16)
    m_prev = mr[...]
    oe, m_cur = None, None
    for j in range(n_keys // kc):
        ks = slice(j * kc, (j + 1) * kc)
        d = _dot(jnp.concatenate([v_ref[:, ks], ones_rows], axis=0),
                 jnp.exp2((sr[ks, :] - m_prev).astype(BF16)))
        oe = d if oe is None else oe + d
        s = _dot_nt(k_ref[ks, :], qc)
        sw[ks, :] = s
        mj = jnp.max(s, axis=0, keepdims=True)
        m_cur = mj if m_cur is None else jnp.maximum(m_cur, mj)
    oe = oe + _dot(jnp.concatenate([cv_ref[...], jnp.ones((ONES_ROWS, cv_ref.shape[1]), BF16)], axis=0),
                   jnp.exp2((sr2[...] - m_prev).astype(BF16)))
    s2 = _dot_nt(ck_ref[...], qc)
    sw2[...] = s2
    m_cur = jnp.maximum(m_cur, jnp.max(s2, axis=0, keepdims=True))
    mw[...] = m_cur
    part = oe[:128, :] / oe[128:129, :]
    if comp == 1:
        part0_ref[...] = part
    else:
        lp = lam_ref[...]
        lam = (jnp.exp(jnp.sum(lp[0:1] * lp[1:2], axis=-1, keepdims=True))
               - jnp.exp(jnp.sum(lp[2:3] * lp[3:4], axis=-1, keepdims=True)) + lam_init)
        o = part0_ref[...] - lam * part
        ms = jnp.mean(o * o, axis=0, keepdims=True)
        o = (o * lax.rsqrt(ms + RMS_EPS)).T
        o_ref[...] = (o * sub_ref[...] * (1.0 - lam_init)).astype(BF16)


def _attn_kernel(lam_init, *refs):
    n_in = 8
    io, part0_ref, scr = refs[:n_in], refs[n_in], refs[n_in + 1:]
    half = len(scr) // 2
    buf_a, buf_b = scr[:half], scr[half:]
    u = pl.program_id(0)

    @pl.when(u == 0)
    def _():
        for r in buf_b + (part0_ref,):
            r[...] = jnp.zeros(r.shape, r.dtype)

    @pl.when(u % 2 == 0)
    def _():
        _attn_step(lam_init, 0, io, buf_a, buf_b, part0_ref)

    @pl.when(u % 2 == 1)
    def _():
        _attn_step(lam_init, 1, io, buf_b, buf_a, part0_ref)


def _attn_short_kernel(lam_init, q_ref, k_ref, v_ref, lam_ref, sub_ref, o_ref):
    lp = lam_ref[...]
    lam = (jnp.exp(jnp.sum(lp[0:1] * lp[1:2], axis=-1, keepdims=True))
           - jnp.exp(jnp.sum(lp[2:3] * lp[3:4], axis=-1, keepdims=True)) + lam_init)
    n_keys = k_ref.shape[0]
    ones_rows = jnp.ones((ONES_ROWS, n_keys), BF16)
    lane = lax.broadcasted_iota(jnp.int32, (q_ref.shape[0], 128), 1)
    units = [(h, c) for h in range(HEADS) for c in range(2)]
    scores = []
    for h, c in units:
        q = q_ref[:, h * 128:(h + 1) * 128]
        qc = jnp.where((lane // HEAD_D) == c, q, jnp.zeros_like(q))
        scores.append(_dot_nt(k_ref[:, h * 128:(h + 1) * 128], qc))
    probs = [jnp.exp2((s - jnp.max(s, axis=0, keepdims=True)).astype(BF16)) for s in scores]
    oes = [_dot(jnp.concatenate([v_ref[h * 128:(h + 1) * 128, :], ones_rows], axis=0), p)
           for (h, c), p in zip(units, probs)]
    parts = [oe[:128, :] / oe[128:129, :] for oe in oes]
    outs = []
    for h in range(HEADS):
        o = parts[2 * h] - lam * parts[2 * h + 1]
        ms = jnp.mean(o * o, axis=0, keepdims=True)
        outs.append((o * lax.rsqrt(ms + RMS_EPS)).T * sub_ref[...] * (1.0 - lam_init))
    o_ref[...] = jnp.concatenate(outs, axis=1).astype(BF16)


def _diff_attention_short(layer, n_seq, seq_len, q, k, v_t, da_lambda, subln):
    lam_init = 0.8 - 0.6 * math.exp(-0.3 * layer)
    return pl.pallas_call(
        functools.partial(_attn_short_kernel, lam_init),
        grid=(n_seq,),
        in_specs=[pl.BlockSpec((seq_len, DA_W), lambda b: (b, 0)),
                  pl.BlockSpec((seq_len, DA_W), lambda b: (b, 0)),
                  pl.BlockSpec((DA_W, seq_len), lambda b: (0, b)),
                  pl.BlockSpec((None, 4, HEAD_D), lambda b: (layer, 0, 0)),
                  pl.BlockSpec((None, 1, 128), lambda b: (layer, 0, 0))],
        out_specs=pl.BlockSpec((seq_len, DA_W), lambda b: (b, 0)),
        out_shape=jax.ShapeDtypeStruct((n_seq * seq_len, DA_W), BF16),
        compiler_params=_cparams(1),
        name="diff_attention_ctx",
    )(q, k, v_t, da_lambda, subln)


def _diff_attention(layer, n_seq, seq_len, q, k, v_t, cache_k, cache_v_t, da_lambda, subln):
    tq = min(ATTN_TQ, seq_len)
    nq = seq_len // tq
    n_units = n_seq * HEADS * nq * 2
    lam_init = 0.8 - 0.6 * math.exp(-0.3 * layer)

    cur = lambda u: jnp.minimum(u, n_units - 1) // 2
    prev = lambda u: jnp.maximum(u - 1, 0) // 2
    seq_of = lambda w: w // (HEADS * nq)
    head_of = lambda w: (w // nq) % HEADS
    rows_of = lambda w: seq_of(w) * nq + w % nq

    q_spec = pl.BlockSpec((tq, 128), lambda u: (rows_of(cur(u)), head_of(cur(u))))
    k_spec = pl.BlockSpec((seq_len, 128), lambda u: (seq_of(cur(u)), head_of(cur(u))))
    v_spec = pl.BlockSpec((128, seq_len), lambda u: (head_of(prev(u)), seq_of(prev(u))))
    ck_spec = pl.BlockSpec((None, None, PAST_LEN, 128), lambda u: (seq_of(cur(u)), layer, 0, head_of(cur(u))))
    cv_spec = pl.BlockSpec((None, None, 128, PAST_LEN), lambda u: (seq_of(prev(u)), layer, head_of(prev(u)), 0))
    tail = [pl.BlockSpec((None, 4, HEAD_D), lambda u: (layer, 0, 0)),
            pl.BlockSpec((None, 1, 128), lambda u: (layer, 0, 0))]
    in_specs = [q_spec, k_spec, ck_spec, v_spec, cv_spec] + tail
    args = [q, k, cache_k, v_t, cache_v_t, da_lambda, subln]
    scratch = [pltpu.VMEM((128, tq), F32)] + [pltpu.VMEM((seq_len, tq), F32), pltpu.VMEM((PAST_LEN, tq), F32),
                                              pltpu.VMEM((1, tq), F32)] * 2
    return pl.pallas_call(
        functools.partial(_attn_kernel, lam_init),
        grid=(n_units + 1,),
        in_specs=in_specs,
        out_specs=pl.BlockSpec((tq, 128), lambda u: (rows_of(prev(u)), head_of(prev(u)))),
        out_shape=jax.ShapeDtypeStruct((n_seq * seq_len, DA_W), BF16),
        scratch_shapes=scratch,
        compiler_params=_cparams(1),
        name="diff_attention_lat",
    )(*args)


def _route(logits):
    lane = lax.broadcasted_iota(jnp.int32, logits.shape, 1)
    lane_f = lane.astype(F32)
    big = 1e9
    is_g = jnp.logical_and(lane >= N_EXPERTS, lane < N_EXPERTS + N_GROUPS)
    gl = jnp.where(is_g, logits, NEG_BIG)
    gmax = jnp.max(gl, axis=-1, keepdims=True)
    gsel = jnp.min(jnp.where(gl == gmax, lane_f, big), axis=-1, keepdims=True) - N_EXPERTS
    g_w = 1.0 / jnp.sum(jnp.exp(gl - gmax), axis=-1, keepdims=True)
    in_grp = jnp.logical_and(lane < N_EXPERTS, (lane // EPG).astype(F32) == gsel)
    el = jnp.where(in_grp, logits, NEG_BIG)
    v1 = jnp.max(el, axis=-1, keepdims=True)
    i1 = jnp.min(jnp.where(el == v1, lane_f, big), axis=-1, keepdims=True)
    el2 = jnp.where(lane_f == i1, NEG_BIG, el)
    v2 = jnp.max(el2, axis=-1, keepdims=True)
    i2 = jnp.min(jnp.where(el2 == v2, lane_f, big), axis=-1, keepdims=True)
    t = jnp.exp(v2 - v1)
    w1 = g_w / (1.0 + t)
    w2 = w1 * t
    first_lo = i1 < i2
    a = jnp.minimum(i1, i2) - EPG * gsel
    b = jnp.maximum(i1, i2) - EPG * gsel
    bucket = gsel * N_PAIRS + a * (7.0 - a) * 0.5 + (b - a - 1.0)
    w_lo = jnp.where(first_lo, w1, w2)
    w_hi = jnp.where(first_lo, w2, w1)
    return jnp.where(lane == 0, bucket, jnp.where(lane == 1, w_lo, jnp.where(lane == 2, w_hi, 0.0)))


def _outproj_body(mod_ref, n2_ref, streams, hgn_ref, bd_ref, wout_ref, wrh_ref, wrl_ref, br_ref,
                  x1_ref, hp_ref, route_ref):
    x_ref, rof_ref, rob_ref, rsg_ref, da_ref, hof_ref, hob_ref, hsg_ref = streams
    ro = rof_ref[...] + rob_ref[...]
    ro = ro * lax.rsqrt(_group_mean_sq(ro, bd_ref, HEAD_D) + RMS_EPS) * rsg_ref[...].astype(F32)
    ho = hof_ref[...] + hob_ref[...]
    ho = ho * lax.rsqrt(_group_mean_sq(ho, bd_ref, HEAD_D) + RMS_EPS) * hgn_ref[...] * hsg_ref[...].astype(F32)
    mix = jnp.concatenate([ro.astype(BF16), da_ref[...], ho.astype(BF16)], axis=1)
    mixed = _dot(mix, wout_ref[...])
    gate1 = mod_ref[2:3, :]
    shift2 = mod_ref[3:4, :]
    scale2 = mod_ref[4:5, :]
    x1 = x_ref[...] + gate1 * mixed
    ms = jnp.mean(x1 * x1, axis=-1, keepdims=True)
    h2 = x1 * lax.rsqrt(ms + RMS_EPS) * n2_ref[...] * (1.0 + scale2) + shift2
    h_hi = h2.astype(BF16)
    h_lo = (h2 - h_hi.astype(F32)).astype(BF16)
    logits = _dot(h_hi, wrh_ref[...]) + _dot(h_lo, wrh_ref[...]) + _dot(h_hi, wrl_ref[...]) + br_ref[...]
    route = _route(logits)
    x1_ref[...] = x1
    route_ref[...] = route
    hp_ref[...] = jnp.concatenate([h2, route], axis=1)


def _outproj_kernel(mod_ref, n2_ref, *rest):
    ctx_streams, lat_streams, tail = rest[0:8], rest[8:16], rest[16:]
    t = pl.program_id(0)

    @pl.when(t < CTX_PTILES)
    def _():
        _outproj_body(mod_ref, n2_ref, ctx_streams, *tail)

    @pl.when(t >= CTX_PTILES)
    def _():
        _outproj_body(mod_ref, n2_ref, lat_streams, *tail)


def _cond_of_tile(t):
    return jnp.where(t < CTX_TILES, 0, 1 + (t - CTX_TILES) // LAT_TILES_PER_SEQ)


def _out_projection(layer, mod, norm2, ctx_streams, lat_streams, hgn_t, bd256, w_out_bf, w_route_hi, w_route_lo,
                    b_route):
    row = lambda t: (t, 0)
    ctx_row = lambda t: (jnp.minimum(t, CTX_PTILES - 1), 0)
    lat_row = lambda t: (jnp.maximum(t - CTX_PTILES, 0), 0)
    lay = lambda t: (layer, 0, 0)
    cond = lambda t: jnp.where(t < CTX_PTILES, 0, 1 + (t - CTX_PTILES) // LAT_PTILES_PER_SEQ)
    widths = (D_MODEL, MIX_G, MIX_G, MIX_G, DA_W, MIX_G, MIX_G, MIX_G)
    in_specs = [
        pl.BlockSpec((None, None, 6, D_MODEL), lambda t: (layer, cond(t), 0, 0)),
        pl.BlockSpec((None, 1, D_MODEL), lay),
    ]
    in_specs += [pl.BlockSpec((PTILE, w), ctx_row) for w in widths]
    in_specs += [pl.BlockSpec((PTILE, w), lat_row) for w in widths]
    in_specs += [
        pl.BlockSpec((None, 1, MIX_G), lay),
        pl.BlockSpec((MIX_G, MIX_G), lambda t: (0, 0)),
        pl.BlockSpec((None, D_MODEL, D_MODEL), lay),
        pl.BlockSpec((None, D_MODEL, 128), lay),
        pl.BlockSpec((None, D_MODEL, 128), lay),
        pl.BlockSpec((None, 1, 128), lay),
    ]
    return pl.pallas_call(
        _outproj_kernel,
        grid=(N_ROWS // PTILE,),
        in_specs=in_specs,
        out_specs=[pl.BlockSpec((PTILE, D_MODEL), row), pl.BlockSpec((PTILE, ROW_W), row),
                   pl.BlockSpec((PTILE, 128), row)],
        out_shape=[jax.ShapeDtypeStruct((N_ROWS, D_MODEL), F32), jax.ShapeDtypeStruct((N_ROWS, ROW_W), F32),
                   jax.ShapeDtypeStruct((N_ROWS, 128), F32)],
        compiler_params=_cparams(1),
        name="out_projection",
    )(mod, norm2, *ctx_streams, *lat_streams, hgn_t, bd256, w_out_bf, w_route_hi, w_route_lo, b_route)


def _dispatch_kernel(route_ref, dest_ref, meta_ref, cnt_ref):
    ph = pl.program_id(0)
    t = pl.program_id(1)
    lane = lax.broadcasted_iota(jnp.int32, (TILE, 128), 1).astype(F32)
    onehots = [(lane == route_ref[j * TILE:(j + 1) * TILE, 0:1]).astype(F32) for j in range(DISPATCH_TILES)]
    tile_cnts = [jnp.sum(oh, axis=0, keepdims=True) for oh in onehots]

    @pl.when(jnp.logical_and(ph == 0, t == 0))
    def _():
        cnt_ref[...] = jnp.zeros_like(cnt_ref)

    @pl.when(ph == 0)
    def _():
        cnt_ref[0:1, :] += sum(tile_cnts[1:], tile_cnts[0])

    @pl.when(jnp.logical_and(ph == 1, t == 0))
    def _():
        cnt = cnt_ref[0:1, :]
        padded = jnp.floor((cnt + (MOE_TILE - 1.0)) * (1.0 / MOE_TILE)) * MOE_TILE
        r = lax.broadcasted_iota(jnp.int32, (128, 128), 0)
        c = lax.broadcasted_iota(jnp.int32, (128, 128), 1)
        off = jnp.dot(jnp.broadcast_to(padded, (8, 128)), (r < c).astype(F32),
                      preferred_element_type=F32, precision=HI)[0:1, :]
        cnt_ref[1:2, :] = off
        end = off + padded
        end_col = jnp.sum(jnp.where(r == c, jnp.broadcast_to(end, (128, 128)), 0.0), axis=1, keepdims=True)
        ended = jnp.logical_and(end_col <= c.astype(F32) * MOE_TILE, r < N_BUCKETS)
        tile_bucket = jnp.sum(ended.astype(F32), axis=0, keepdims=True)
        lane1 = lax.broadcasted_iota(jnp.int32, (1, 128), 1)
        tile_idx = lane1.astype(F32)
        n_valid = jnp.max(jnp.where(lane1 < N_BUCKETS, end, 0.0), axis=1, keepdims=True) * (1.0 / MOE_TILE)
        last = jnp.max(jnp.where(tile_idx < n_valid, tile_bucket, 0.0), axis=1, keepdims=True)
        tb = jnp.where(tile_idx < n_valid, tile_bucket, last)
        g = jnp.floor(tb * (1.0 / N_PAIRS))
        pid = tb - g * N_PAIRS
        a = (pid >= 3.0).astype(F32) + (pid >= 5.0).astype(F32)
        b = pid + 1.0 - 2.0 * (a >= 1.0).astype(F32) - (a >= 2.0).astype(F32)
        rows = [EPG * g + a, EPG * g + b, jnp.broadcast_to(n_valid, (1, 128)), jnp.zeros((5, 128), F32)]
        meta_ref[...] = jnp.concatenate(rows, axis=0).astype(jnp.int32)

    @pl.when(ph == 1)
    def _():
        base = cnt_ref[1:2, :]
        rr = lax.broadcasted_iota(jnp.int32, (TILE, TILE), 0)
        cc = lax.broadcasted_iota(jnp.int32, (TILE, TILE), 1)
        for j, onehot in enumerate(onehots):
            oh16 = onehot.astype(BF16)
            same = _dot_nt(oh16, oh16)
            rank = jnp.sum(jnp.where(rr < cc, same, 0.0), axis=0, keepdims=True)
            base_row = lax.dot_general(jnp.broadcast_to(base, (8, 128)), onehot, (((1,), (1,)), ((), ())),
                                       preferred_element_type=F32, precision=HI)[0:1, :]
            dest_ref[j] = (base_row + rank).astype(jnp.int32)
            base = base + tile_cnts[j]
        cnt_ref[1:2, :] = base


def _moe_dispatch(route):
    dest, meta = pl.pallas_call(
        _dispatch_kernel,
        grid=(2, N_TILES // DISPATCH_TILES),
        in_specs=[pl.BlockSpec((DISPATCH_TILES * TILE, 128), lambda ph, t: (t, 0))],
        out_specs=[pl.BlockSpec((DISPATCH_TILES, 1, TILE), lambda ph, t: (t * ph, 0, 0)),
                   pl.BlockSpec((8, 128), lambda ph, t: (0, 0))],
        out_shape=[jax.ShapeDtypeStruct((N_TILES, 1, TILE), jnp.int32),
                   jax.ShapeDtypeStruct((8, 128), jnp.int32)],
        scratch_shapes=[pltpu.VMEM((8, 128), F32)],
        compiler_params=_cparams(2),
        name="moe_dispatch",
    )(route)
    return dest.reshape(N_ROWS), meta[0:3].reshape(3 * 128)


def _dma_params():
    return pltpu.CompilerParams(dimension_semantics=("arbitrary",), vmem_limit_bytes=VMEM_LIMIT,
                                disable_bounds_checks=True)


def _moe_scatter_kernel(dest_ref, h_ref, hs_init_ref, hs_ref, stage, sems):
    del hs_init_ref
    t = pl.program_id(0)
    base = t * TILE
    slot = t % 2
    stage[slot] = h_ref[...]

    def issue(i, carry):
        d = dest_ref[base + i]
        pltpu.make_async_copy(stage.at[slot, pl.ds(i, 1)], hs_ref.at[pl.ds(d, 1)], sems.at[slot]).start()
        return carry

    lax.fori_loop(0, TILE, issue, 0, unroll=8)

    def wait_tile(s):
        pltpu.make_async_copy(stage.at[s], hs_ref.at[pl.ds(0, TILE)], sems.at[s]).wait()

    @pl.when(t > 0)
    def _():
        wait_tile(1 - slot)

    @pl.when(t == pl.num_programs(0) - 1)
    def _():
        wait_tile(slot)


def _moe_scatter(dest, hp, hs_init):
    return pl.pallas_call(
        _moe_scatter_kernel,
        grid_spec=pltpu.PrefetchScalarGridSpec(
            num_scalar_prefetch=1,
            grid=(N_TILES,),
            in_specs=[pl.BlockSpec((TILE, ROW_W), lambda t, dest: (t, 0)),
                      pl.BlockSpec(memory_space=pl.ANY)],
            out_specs=pl.BlockSpec(memory_space=pl.ANY),
            scratch_shapes=[pltpu.VMEM((2, TILE, ROW_W), F32), pltpu.SemaphoreType.DMA((2,))],
        ),
        out_shape=jax.ShapeDtypeStruct((N_SLOTS, ROW_W), F32),
        input_output_aliases={2: 0},
        compiler_params=_dma_params(),
        name="moe_scatter",
    )(dest, hp, hs_init)


def _moe_ffn_kernel(meta_ref, hs_ref, wgl_ref, wul_ref, wdl_ref, wgh_ref, wuh_ref, wdh_ref, ys_ref):
    t = pl.program_id(0)
    n_valid = meta_ref[2 * 128]

    @pl.when(t < n_valid)
    def _():
        w = hs_ref[...]
        x = w[:, :D_MODEL].astype(BF16)
        rw = w[:, D_MODEL:]

        def expert(wg_ref, wu_ref, wd_ref, weight):
            a = _silu(_dot(x, wg_ref[...])) * _dot(x, wu_ref[...])
            return _dot((a * weight).astype(BF16), wd_ref[...])

        y = expert(wgl_ref, wul_ref, wdl_ref, rw[:, 1:2]) + expert(wgh_ref, wuh_ref, wdh_ref, rw[:, 2:3])
        ys_ref[...] = y

    @pl.when(t >= n_valid)
    def _():
        ys_ref[...] = jnp.zeros_like(ys_ref)


def _moe_ffn(layer, meta, hs, w_gate_bf, w_up_bf, w_down_bf):
    lo = lambda t, m: (layer, m[t], 0, 0)
    hi = lambda t, m: (layer, m[128 + t], 0, 0)
    up = lambda im: pl.BlockSpec((None, None, D_MODEL, EXPERT_FF), im)
    down = lambda im: pl.BlockSpec((None, None, EXPERT_FF, D_MODEL), im)
    return pl.pallas_call(
        _moe_ffn_kernel,
        grid_spec=pltpu.PrefetchScalarGridSpec(
            num_scalar_prefetch=1,
            grid=(MOE_MAX_TILES,),
            in_specs=[pl.BlockSpec((MOE_TILE, ROW_W), lambda t, m: (jnp.maximum(jnp.minimum(t, m[2 * 128] - 1), 0), 0)),
                      up(lo), up(lo), down(lo), up(hi), up(hi), down(hi)],
            out_specs=pl.BlockSpec((MOE_TILE, D_MODEL), lambda t, m: (t, 0)),
        ),
        out_shape=jax.ShapeDtypeStruct((N_SLOTS, D_MODEL), F32),
        compiler_params=_cparams(1),
        name="moe_ffn",
    )(meta, hs, w_gate_bf, w_up_bf, w_down_bf, w_gate_bf, w_up_bf, w_down_bf)


def _moe_combine_kernel(dest_ref, x1_ref, mod_ref, ys_ref, out_ctx_ref, out_lat_ref, buf, sems):
    t = pl.program_id(0)
    nt = pl.num_programs(0)

    def issue(tile, slot):
        def body(i, carry):
            d = dest_ref[tile * TILE + i]
            pltpu.make_async_copy(ys_ref.at[pl.ds(d, 1)], buf.at[slot, pl.ds(i, 1)], sems.at[slot]).start()
            return carry
        lax.fori_loop(0, TILE, body, 0, unroll=8)

    @pl.when(t == 0)
    def _():
        issue(0, 0)

    @pl.when(t + 1 < nt)
    def _():
        issue(t + 1, (t + 1) % 2)

    slot = t % 2
    pltpu.make_async_copy(ys_ref.at[pl.ds(0, TILE)], buf.at[slot], sems.at[slot]).wait()
    x2 = x1_ref[...] + mod_ref[5:6, :] * buf[slot]

    @pl.when(t < CTX_TILES)
    def _():
        out_ctx_ref[...] = x2

    @pl.when(t >= CTX_TILES)
    def _():
        out_lat_ref[...] = x2


def _moe_combine(layer, dest, x1, mod, ys):
    return pl.pallas_call(
        _moe_combine_kernel,
        grid_spec=pltpu.PrefetchScalarGridSpec(
            num_scalar_prefetch=1,
            grid=(N_TILES,),
            in_specs=[pl.BlockSpec((TILE, D_MODEL), lambda t, dest: (t, 0)),
                      pl.BlockSpec((None, None, 6, D_MODEL), lambda t, dest: (layer, _cond_of_tile(t), 0, 0)),
                      pl.BlockSpec(memory_space=pl.ANY)],
            out_specs=[pl.BlockSpec((TILE, D_MODEL), lambda t, dest: (jnp.minimum(t, CTX_TILES - 1), 0)),
                       pl.BlockSpec((TILE, D_MODEL), lambda t, dest: (jnp.maximum(t - CTX_TILES, 0), 0))],
            scratch_shapes=[pltpu.VMEM((2, TILE, D_MODEL), F32), pltpu.SemaphoreType.DMA((2,))],
        ),
        out_shape=[jax.ShapeDtypeStruct((N_CTX_ROWS, D_MODEL), F32), jax.ShapeDtypeStruct((N_LAT_ROWS, D_MODEL), F32)],
        compiler_params=_dma_params(),
        name="moe_combine",
    )(dest, x1, mod, ys)


def _block_diag_ones(n, group):
    i = np.arange(n) // group
    return jnp.asarray((i[:, None] == i[None, :]).astype(np.float32), dtype=BF16)


def _rope_tables():
    pos = np.arange(DEC_SEQ)
    rows = (pos // GRID_W).astype(np.float32)
    cols = (pos % GRID_W).astype(np.float32)
    inv_freq = (ROPE_BASE ** (-(np.arange(ROPE_PAIRS, dtype=np.float32) / ROPE_PAIRS))).astype(np.float32)
    ang_r = rows[:, None] * inv_freq[None, :]
    ang_c = cols[:, None] * inv_freq[None, :]
    cos64 = np.concatenate([np.cos(ang_r), np.cos(ang_r), np.cos(ang_c), np.cos(ang_c)], axis=1)
    sin64 = np.concatenate([-np.sin(ang_r), np.sin(ang_r), -np.sin(ang_c), np.sin(ang_c)], axis=1)
    return (jnp.asarray(np.tile(cos64, (1, 2)), F32), jnp.asarray(np.tile(sin64, (1, 2)), F32))


def _state_to_blockdiag_t(s):
    eye = jnp.eye(HEADS, dtype=F32)
    return jnp.einsum('bzhdv,hg->bzhvgd', s.astype(F32), eye).reshape(s.shape[0], 2, MIX_G, MIX_G)


def _blockdiag_t_to_state(st):
    s6 = st.reshape(st.shape[0], 2, HEADS, HEAD_D, HEADS, HEAD_D)
    diag = jnp.stack([s6[:, :, h, :, h, :] for h in range(HEADS)], axis=2)
    return jnp.swapaxes(diag, -1, -2)


def _state_to_blockdiag(s):
    eye = jnp.eye(HEADS, dtype=F32)
    return jnp.einsum('bzhdv,hg->bzhdgv', s.astype(F32), eye).reshape(s.shape[0], 2, MIX_G, MIX_G)


def _blockdiag_to_state(st):
    s6 = st.reshape(st.shape[0], 2, HEADS, HEAD_D, HEADS, HEAD_D)
    return jnp.stack([s6[:, :, h, :, h, :] for h in range(HEADS)], axis=2)


def _mixers(latent, layer, x, consts, params, ret_s0, hg_s0, cache_k, cache_v):
    (mod, lb_all, bd512, bd256, rope_cos, rope_sin, ret_tables) = consts
    n_seq = DEC_BATCH if latent else BATCH
    seq_len = DEC_SEQ if latent else SEQ
    nt = seq_len // TILE
    proj = _in_projection(latent, layer, x, mod, params['norm1'], params['w_in'], params['w_vt'], params['da_qn'],
                          params['da_kn'], lb_all, bd512, rope_cos, rope_sin)
    (rq, rk, rv, rsg, dq, dk, dv, hq, hkf, hkb, hv, hgf, hgb, hsg) = proj[:14]
    ret_of, ret_ob, ret_fin = _retention(n_seq, nt, rq, rk, rv, ret_s0, ret_tables,
                                         "retention_lat" if latent else "retention_ctx")
    hg_of, hg_ob, hg_fin = _gated_scan(n_seq, nt, hq, hkf, hkb, hv, hgf, hgb, hg_s0,
                                       "hgrn_scan_lat" if latent else "hgrn_scan_ctx")
    if latent:
        da_o = _diff_attention(layer, n_seq, seq_len, dq, dk, dv, cache_k, cache_v,
                               params['da_lambda'], params['da_subln'])
    else:
        da_o = _diff_attention_short(layer, n_seq, seq_len, dq, dk, dv, params['da_lambda'], params['da_subln'])
    streams = (x, ret_of, ret_ob, rsg, da_o, hg_of, hg_ob, hsg)
    extras = None if latent else (proj[14], proj[15], ret_fin, hg_fin)
    return streams, extras


def _trunk_layer(layer, x_ctx, x_lat, sorted_buf, consts, params, lat_ret_s0, lat_hg_s0, zero_state, cache_k, cache_v):
    mod, bd256 = consts[0], consts[3]
    ctx_streams, extras = _mixers(False, layer, x_ctx, consts, params, zero_state, zero_state, None, None)
    lat_streams, _ = _mixers(True, layer, x_lat, consts, params, lat_ret_s0, lat_hg_s0, cache_k, cache_v)
    x1, hp, route = _out_projection(layer, mod, params['norm2'], ctx_streams, lat_streams, params['hg_norm'],
                                    bd256, params['w_out'], params['w_route_hi'], params['w_route_lo'],
                                    params['b_route'])
    dest, meta = _moe_dispatch(route)
    sorted_buf = _moe_scatter(dest, hp, sorted_buf)
    ys = _moe_ffn(layer, meta, sorted_buf, params['w_gate'], params['w_up'], params['w_down'])
    x_ctx, x_lat = _moe_combine(layer, dest, x1, mod, ys)
    return x_ctx, x_lat, sorted_buf, extras


def kernel(x_prompt, x_sample, cache_k, cache_v, state_ret, state_hgrn, c, c_ctx, norm1, norm2, w_mod, b_mod,
           w_in, w_out, da_qn, da_kn, da_lambda, da_subln, hg_lb, hg_norm, w_group, b_group, w_router,
           b_router, w_gate, w_up, w_down):
    cond = jnp.zeros((N_COND, D_MODEL), F32).at[0].set(c_ctx).at[1:1 + DEC_BATCH].set(c)
    mod, lb_all = _modulation(cond, w_mod, b_mod, hg_lb)
    mod = mod.reshape(DEPTH, N_COND, 6, D_MODEL)
    rope_cos, rope_sin = _rope_tables()
    consts = (mod, lb_all.reshape(DEPTH, 1, MIX_G), _block_diag_ones(DA_W, HEAD_D),
              _block_diag_ones(MIX_G, HEAD_D), rope_cos, rope_sin, _retention_tables())
    pad = jnp.zeros((DEPTH, D_MODEL, 128 - N_EXPERTS - N_GROUPS), F32)
    w_route = jnp.concatenate([w_router, w_group, pad], axis=-1)
    params = {
        'norm1': norm1.reshape(DEPTH, 1, D_MODEL), 'norm2': norm2.reshape(DEPTH, 1, D_MODEL),
        'w_in': w_in.astype(BF16), 'w_out': w_out.astype(BF16),
        'w_vt': jnp.swapaxes(w_in[:, :, 2048:2560].astype(BF16), 1, 2),
        'da_qn': jnp.tile(da_qn, (1, DA_W // HEAD_D)).reshape(DEPTH, 1, DA_W),
        'da_kn': jnp.tile(da_kn, (1, DA_W // HEAD_D)).reshape(DEPTH, 1, DA_W),
        'da_lambda': da_lambda, 'da_subln': da_subln.reshape(DEPTH, 1, 128),
        'hg_norm': jnp.tile(hg_norm, (1, HEADS)).reshape(DEPTH, 1, MIX_G),
        'w_route_hi': w_route.astype(BF16),
        'w_route_lo': (w_route - w_route.astype(BF16).astype(F32)).astype(BF16),
        'b_route': jnp.concatenate([b_router, b_group, pad[:, 0, :]], axis=-1).reshape(DEPTH, 1, 128),
        'w_gate': w_gate.astype(BF16), 'w_up': w_up.astype(BF16), 'w_down': w_down.astype(BF16),
    }
    ck_bf = cache_k.astype(BF16).reshape(DEC_BATCH, DEPTH, PAST_LEN, DA_W)
    cv_bf = jnp.swapaxes(cache_v.astype(BF16).reshape(DEC_BATCH, DEPTH, PAST_LEN, DA_W), 2, 3)
    zero_state = jnp.zeros((BATCH, 2, MIX_G, MIX_G), F32)

    yp = x_prompt.reshape(N_CTX_ROWS, D_MODEL)
    ys = x_sample.reshape(N_LAT_ROWS, D_MODEL)
    sorted_buf = jnp.zeros((N_SLOTS, ROW_W), F32)
    ks_out, vs_out, rets_out, hgs_out = [], [], [], []
    for l in range(DEPTH):
        yp, ys, sorted_buf, (k_new, v_new, ret_fin, hg_fin) = _trunk_layer(
            l, yp, ys, sorted_buf, consts, params, _state_to_blockdiag(state_ret[:, l]),
            _state_to_blockdiag_t(state_hgrn[:, l]), zero_state, ck_bf, cv_bf)
        ks_out.append(k_new.reshape(BATCH, SEQ, HEADS, 2, HEAD_D))
        vs_out.append(v_new.reshape(BATCH, SEQ, HEADS, 2 * HEAD_D))
        rets_out.append(_blockdiag_to_state(ret_fin))
        hgs_out.append(_blockdiag_t_to_state(hg_fin))
    return (yp.reshape(BATCH, SEQ, D_MODEL), ys.reshape(DEC_BATCH, DEC_SEQ, D_MODEL),
            jnp.stack(ks_out, axis=1), jnp.stack(vs_out, axis=1),
            jnp.stack(rets_out, axis=1), jnp.stack(hgs_out, axis=1))
```

```python
import functools
import math

import numpy as np
import jax
import jax.numpy as jnp
from jax import lax
from jax.experimental import pallas as pl
from jax.experimental.pallas import tpu as pltpu

F32 = jnp.float32
BF16 = jnp.bfloat16

D_MODEL = 1024
BATCH = 16
SEQ = 256
DEPTH = 4
DEC_BATCH = 4
DEC_SEQ = 4096
PAST_LEN = 256
GRID_W = 64
HEADS = 4
HEAD_D = 64
MIX_G = HEADS * HEAD_D
DA_W = 512
IN_COLS = 3840
ROPE_BASE = 10000.0
ROPE_PAIRS = 16
RMS_EPS = 1e-6
N_GROUPS = 4
EPG = 4
N_EXPERTS = 16
EXPERT_FF = 512

TILE = 256
N_CTX_ROWS = BATCH * SEQ
N_LAT_ROWS = DEC_BATCH * DEC_SEQ
N_ROWS = N_CTX_ROWS + N_LAT_ROWS
CTX_TILES = N_CTX_ROWS // TILE
LAT_TILES_PER_SEQ = DEC_SEQ // TILE
N_TILES = N_ROWS // TILE
PTILE = 512
CTX_PTILES = N_CTX_ROWS // PTILE
LAT_PTILES_PER_SEQ = DEC_SEQ // PTILE
N_COND = 8

N_PAIRS = EPG * (EPG - 1) // 2
N_BUCKETS = N_GROUPS * N_PAIRS
MOE_TILE = 256
MOE_MAX_TILES = N_ROWS // MOE_TILE + N_BUCKETS
DISPATCH_TILES = 4
N_SLOTS = MOE_MAX_TILES * MOE_TILE
ROW_W = D_MODEL + 128

CHUNK = 64
SUB = 16
NSUB = CHUNK // SUB
NCHUNK = TILE // CHUNK

VMEM_LIMIT = 48 * 1024 * 1024
NEG_BIG = -1e30
ONES_ROWS = 16
ATTN_TQ = 512
ATTN_KEY_CHUNK = 512
LOG2_E = 1.4426950408889634

HI = lax.Precision.HIGHEST


def _cparams(n_axes):
    return pltpu.CompilerParams(dimension_semantics=("arbitrary",) * n_axes,
                                vmem_limit_bytes=VMEM_LIMIT)


def _dot(a, b):
    return jnp.dot(a, b, preferred_element_type=F32)


def _dot_nt(a, b):
    return lax.dot_general(a, b, (((1,), (1,)), ((), ())), preferred_element_type=F32)


def _dot_tn(a, b):
    return lax.dot_general(a, b, (((0,), (0,)), ((), ())), preferred_element_type=F32)


def _silu(x):
    return x * (1.0 / (1.0 + jnp.exp(-x)))


def _sigmoid(x):
    return 1.0 / (1.0 + jnp.exp(-x))


def _mod_kernel(cond_ref, w_ref, b_ref, lb_ref, mod_ref, lbo_ref):
    c = cond_ref[...]
    m = jnp.dot(_silu(c), w_ref[...], preferred_element_type=F32, precision=HI)
    mod_ref[...] = m + b_ref[...]
    z = lb_ref[...]
    zmax = jnp.max(z, axis=0, keepdims=True)
    e = jnp.exp(z - zmax)
    p = e / jnp.sum(e, axis=0, keepdims=True)
    rows = [jnp.zeros_like(p[0:1])]
    for l in range(1, DEPTH):
        rows.append(rows[-1] + p[l:l + 1])
    lbo_ref[...] = jnp.concatenate(rows, axis=0)


def _modulation(cond, w_mod, b_mod, hg_lb):
    nblk = 6
    return pl.pallas_call(
        _mod_kernel,
        grid=(DEPTH, nblk),
        in_specs=[
            pl.BlockSpec((N_COND, D_MODEL), lambda l, j: (0, 0)),
            pl.BlockSpec((None, D_MODEL, D_MODEL), lambda l, j: (l, 0, j)),
            pl.BlockSpec((None, 1, D_MODEL), lambda l, j: (l, 0, j)),
            pl.BlockSpec((DEPTH, MIX_G), lambda l, j: (0, 0)),
        ],
        out_specs=[
            pl.BlockSpec((None, N_COND, D_MODEL), lambda l, j: (l, 0, j)),
            pl.BlockSpec((DEPTH, MIX_G), lambda l, j: (0, 0)),
        ],
        out_shape=[jax.ShapeDtypeStruct((DEPTH, N_COND, 6 * D_MODEL), F32),
                   jax.ShapeDtypeStruct((DEPTH, MIX_G), F32)],
        compiler_params=_cparams(2),
        name="modulation",
    )(cond, w_mod, b_mod.reshape(DEPTH, 1, 6 * D_MODEL), hg_lb)


def _group_mean_sq(x, bd_ref, group):
    return _dot((x * x).astype(BF16), bd_ref[...]) * (1.0 / group)


def _swap16(x):
    w = x.shape[-1]
    lane = lax.broadcasted_iota(jnp.int32, x.shape, x.ndim - 1)
    up = pltpu.roll(x, w - 16, x.ndim - 1)
    dn = pltpu.roll(x, 16, x.ndim - 1)
    return jnp.where((lane % 32) < 16, up, dn)


def _inproj_kernel(latent, x_ref, mod_ref, n1_ref, w_ref, wvt_ref, qn_ref, kn_ref, lb_ref, bd_ref, *rest):
    if latent:
        cos_ref, sin_ref = rest[0], rest[1]
        outs = rest[2:]
    else:
        outs = rest
    (rq_ref, rk_ref, rv_ref, rsg_ref, dq_ref, dk_ref, dv_ref,
     hq_ref, hkf_ref, hkb_ref, hv_ref, hgf_ref, hgb_ref, hsg_ref) = outs[:14]

    x = x_ref[...]
    ms = jnp.mean(x * x, axis=-1, keepdims=True)
    shift1 = mod_ref[0:1, :]
    scale1 = mod_ref[1:2, :]
    h = x * lax.rsqrt(ms + RMS_EPS) * n1_ref[...] * (1.0 + scale1) + shift1
    h16 = h.astype(BF16)
    y = _dot(h16, w_ref[...])

    rq_ref[...] = y[:, 0:256].astype(BF16)
    rk_ref[...] = (y[:, 256:512] * (HEAD_D ** -0.5)).astype(BF16)
    rv_ref[...] = y[:, 512:768].astype(BF16)
    rsg_ref[...] = _silu(y[:, 768:1024]).astype(BF16)

    dq = y[:, 1024:1536]
    dk = y[:, 1536:2048]
    qn = dq * lax.rsqrt(_group_mean_sq(dq, bd_ref, HEAD_D) + RMS_EPS) * qn_ref[...]
    kn = dk * lax.rsqrt(_group_mean_sq(dk, bd_ref, HEAD_D) + RMS_EPS) * kn_ref[...]
    if latent:
        cos = jnp.concatenate([cos_ref[...]] * 4, axis=1)
        sin = jnp.concatenate([sin_ref[...]] * 4, axis=1)
        qr = qn * cos + _swap16(qn) * sin
        kr = kn * cos + _swap16(kn) * sin
    else:
        qr, kr = qn, kn
        ck_ref, cv_ref = outs[14], outs[15]
        ck_ref[...] = kn
        cv_ref[...] = y[:, 2048:2560]
    dq_ref[...] = (qr * (HEAD_D ** -0.5 * LOG2_E)).astype(BF16)
    dk_ref[...] = kr.astype(BF16)
    dv_ref[...] = _dot_nt(wvt_ref[...], h16).astype(BF16)

    lb = lb_ref[...]
    f_f = lb + (1.0 - lb) * _sigmoid(y[:, 2816:3072])
    f_b = lb + (1.0 - lb) * _sigmoid(y[:, 3072:3328])
    hq_ref[...] = y[:, 2560:2816].astype(BF16)
    hkf_ref[...] = (1.0 - f_f).astype(BF16)
    hkb_ref[...] = (1.0 - f_b).astype(BF16)
    hgf_ref[...] = jnp.log2(f_f)
    hgb_ref[...] = jnp.log2(f_b)
    hv_ref[...] = y[:, 3328:3584].astype(BF16)
    hsg_ref[...] = _silu(y[:, 3584:3840]).astype(BF16)


def _in_projection(latent, layer, x, mod, norm1, w_in_bf, w_vt_bf, qn_t, kn_t, lb_all, bd512, rope_cos, rope_sin):
    nrows = x.shape[0]
    ntiles = nrows // PTILE
    cond = (lambda t: 1 + t // LAT_PTILES_PER_SEQ) if latent else (lambda t: 0)

    in_specs = [
        pl.BlockSpec((PTILE, D_MODEL), lambda t: (t, 0)),
        pl.BlockSpec((None, None, 6, D_MODEL), lambda t: (layer, cond(t), 0, 0)),
        pl.BlockSpec((None, 1, D_MODEL), lambda t: (layer, 0, 0)),
        pl.BlockSpec((None, D_MODEL, IN_COLS), lambda t: (layer, 0, 0)),
        pl.BlockSpec((None, DA_W, D_MODEL), lambda t: (layer, 0, 0)),
        pl.BlockSpec((None, 1, DA_W), lambda t: (layer, 0, 0)),
        pl.BlockSpec((None, 1, DA_W), lambda t: (layer, 0, 0)),
        pl.BlockSpec((None, 1, MIX_G), lambda t: (layer, 0, 0)),
        pl.BlockSpec((DA_W, DA_W), lambda t: (0, 0)),
    ]
    args = [x, mod, norm1, w_in_bf, w_vt_bf, qn_t, kn_t, lb_all, bd512]
    if latent:
        in_specs += [pl.BlockSpec((PTILE, 128), lambda t: (t % LAT_PTILES_PER_SEQ, 0))] * 2
        args += [rope_cos, rope_sin]

    def o(width, dtype):
        return jax.ShapeDtypeStruct((nrows, width), dtype), pl.BlockSpec((PTILE, width), lambda t: (t, 0))

    dv_t = (jax.ShapeDtypeStruct((DA_W, nrows), BF16), pl.BlockSpec((DA_W, PTILE), lambda t: (0, t)))
    outs = [o(256, BF16), o(256, BF16), o(256, BF16), o(256, BF16),
            o(512, BF16), o(512, BF16), dv_t,
            o(256, BF16), o(256, BF16), o(256, BF16), o(256, BF16), o(256, F32), o(256, F32), o(256, BF16)]
    if not latent:
        outs += [o(512, F32), o(512, F32)]
    return pl.pallas_call(
        functools.partial(_inproj_kernel, latent),
        grid=(ntiles,),
        in_specs=in_specs,
        out_specs=[s for _, s in outs],
        out_shape=[s for s, _ in outs],
        compiler_params=_cparams(1),
        name="in_projection_lat" if latent else "in_projection_ctx",
    )(*args)


def _head_masks():
    lane = lax.broadcasted_iota(jnp.int32, (1, MIX_G), 1)
    return [(lane // HEAD_D == h).astype(F32) for h in range(HEADS)]


def _scan_chunks(items, st_f, st_b, consts_f, consts_b):
    n = len(items)
    cs = [consts_b if it[4] else consts_f for it in items]
    zero_row = jnp.zeros((1, MIX_G), F32)

    bs = []
    for x in range(n):
        g = items[x][3]
        g_hi = g.astype(BF16)
        g_lo = (g - g_hi.astype(F32)).astype(BF16)
        bs.append(_dot(cs[x][0], g_hi) + _dot(cs[x][0], g_lo))

    prep = []
    for x in range(n):
        q, k, v, _, rev = items[x]
        hmask = cs[x][4]
        b = bs[x]
        r_rows, e_rows = [], []
        for s in range(NSUB):
            lo, hi = s * SUB, s * SUB + SUB - 1
            if not rev:
                r_rows.append(b[lo - 1:lo, :] if s > 0 else zero_row)
                e_rows.append(b[hi:hi + 1, :])
            else:
                r_rows.append(b[hi + 1:hi + 2, :] if s < NSUB - 1 else zero_row)
                e_rows.append(b[lo:lo + 1, :])
        btot = e_rows[NSUB - 1] if not rev else e_rows[0]
        r_full = jnp.concatenate([jnp.broadcast_to(r, (SUB, MIX_G)) for r in r_rows], axis=0)
        e_full = jnp.concatenate([jnp.broadcast_to(e, (SUB, MIX_G)) for e in e_rows], axis=0)
        bl = b - r_full
        qp = q * jnp.exp2(bl)
        kend = k * jnp.exp2(e_full - b)
        pairs = [(i, j) for i in range(NSUB) for j in range(NSUB) if (j < i if not rev else j > i)]
        lh = jnp.concatenate([qp[i * SUB:(i + 1) * SUB, :] * jnp.exp2(r_rows[i] - e_rows[j])
                              for (i, j) in pairs], axis=0)
        prep.append(dict(
            bl=bl, btot=btot, pairs=pairs,
            q_inter=(qp * jnp.exp2(r_full)).astype(BF16),
            k_state=(kend * jnp.exp2(btot - e_full)).astype(BF16),
            kend=kend.astype(BF16),
            lhs4=jnp.concatenate([lh * hmask[h] for h in range(HEADS)], axis=0).astype(BF16)))

    uts = [_dot_tn(items[x][2], prep[x]['k_state']) for x in range(n)]
    scs = [_dot_nt(prep[x]['lhs4'], prep[x]['kend']) for x in range(n)]

    st_in = []
    for x in range(n):
        rev = items[x][4]
        st = st_b if rev else st_f
        st_in.append(st.astype(BF16))
        st = st * jnp.exp2(prep[x]['btot']) + uts[x] * cs[x][1]
        if rev:
            st_b = st
        else:
            st_f = st
    col = lax.broadcasted_iota(jnp.int32, (SUB, CHUNK), 1) // SUB
    jrow = lax.broadcasted_iota(jnp.int32, (SUB, MIX_G), 0)
    pss, pps, vts = [], [], []
    for x in range(n):
        q, k, v, _, rev = items[x]
        pairs, sc, bl = prep[x]['pairs'], scs[x], prep[x]['bl']
        p_rows = []
        for h in range(HEADS):
            for i in range(NSUB):
                acc = None
                for p, (pi, pj) in enumerate(pairs):
                    if pi != i:
                        continue
                    base = (h * len(pairs) + p) * SUB
                    blk = jnp.where(col == pj, sc[base:base + SUB, :], 0.0)
                    acc = blk if acc is None else acc + blk
                p_rows.append(jnp.zeros((SUB, CHUNK), F32) if acc is None else acc)
        pss.append(jnp.concatenate(p_rows, axis=0).astype(BF16))
        pp, vt = [], []
        for s in range(NSUB):
            sl = slice(s * SUB, (s + 1) * SUB)
            bl_s, q_s, k_s = bl[sl, :], q[sl, :].astype(BF16), k[sl, :].astype(BF16)
            for i in range(SUB):
                d = bl_s[i:i + 1, :] - bl_s
                keep = (jrow <= i) if not rev else (jrow >= i)
                e = jnp.exp2(jnp.where(keep, d, NEG_BIG).astype(BF16))
                pp.append(e * q_s[i:i + 1, :] * k_s)
            vt.extend([v[sl, :]] * SUB)
        pps.append(jnp.concatenate(pp, axis=0))
        vts.append(jnp.concatenate(vt, axis=0))

    o_int = [_dot_nt(prep[x]['q_inter'], st_in[x]) for x in range(n)]
    fulls = [_dot(pss[x], items[x][2]) for x in range(n)]
    sbs = [_dot(pps[x], cs[x][2]) for x in range(n)]

    ws = [(sbs[x] * vts[x].astype(F32)).astype(BF16) for x in range(n)]
    o_diag = [_dot(cs[x][3], ws[x]) for x in range(n)]

    outs = []
    for x in range(n):
        hmask = cs[x][4]
        o = o_int[x] + o_diag[x]
        for h in range(HEADS):
            o = o + fulls[x][h * CHUNK:(h + 1) * CHUNK, :] * hmask[h]
        outs.append(o)
    return outs, st_f, st_b


def _scan_consts(rev):
    r = lax.broadcasted_iota(jnp.int32, (CHUNK, CHUNK), 0)
    c = lax.broadcasted_iota(jnp.int32, (CHUNK, CHUNK), 1)
    tri = ((c <= r) if not rev else (c >= r)).astype(BF16)
    rr = lax.broadcasted_iota(jnp.int32, (MIX_G, MIX_G), 0) // HEAD_D
    cc = lax.broadcasted_iota(jnp.int32, (MIX_G, MIX_G), 1) // HEAD_D
    bd_mask = (rr == cc).astype(F32)
    ones_bd = bd_mask.astype(BF16)
    gi = lax.broadcasted_iota(jnp.int32, (CHUNK, CHUNK * SUB), 0)
    gj = lax.broadcasted_iota(jnp.int32, (CHUNK, CHUNK * SUB), 1) // SUB
    gsel = (gi == gj).astype(BF16)
    return tri, bd_mask, ones_bd, gsel, _head_masks()


def _scan_kernel(nt, qf_ref, kf_ref, vf_ref, gf_ref, qb_ref, kb_ref, vb_ref, gb_ref, s0_ref,
                 of_ref, ob_ref, sfin_ref, st_scr):
    t = pl.program_id(1)

    @pl.when(t == 0)
    def _():
        st_scr[...] = s0_ref[...]

    cf = _scan_consts(False)
    cb = _scan_consts(True)

    items, rows = [], []
    for c in range(NCHUNK):
        rf = slice(c * CHUNK, (c + 1) * CHUNK)
        rb = slice((NCHUNK - 1 - c) * CHUNK, (NCHUNK - c) * CHUNK)
        items.append((qf_ref[rf, :].astype(F32), kf_ref[rf, :].astype(F32), vf_ref[rf, :], gf_ref[rf, :], False))
        items.append((qb_ref[rb, :].astype(F32), kb_ref[rb, :].astype(F32), vb_ref[rb, :], gb_ref[rb, :], True))
        rows += [(of_ref, rf), (ob_ref, rb)]
    outs, st_f, st_b = _scan_chunks(items, st_scr[0], st_scr[1], cf, cb)
    for (ref, sl), o in zip(rows, outs):
        ref[sl, :] = o
    st_scr[0] = st_f
    st_scr[1] = st_b

    @pl.when(t == nt - 1)
    def _():
        sfin_ref[...] = st_scr[...]


def _gated_scan(n_seq, nt, q, k_f, k_b, v, g_f, g_b, s0, name):
    fwd = lambda s, t: (s * nt + t, 0)
    bwd = lambda s, t: (s * nt + (nt - 1 - t), 0)
    blk = lambda im: pl.BlockSpec((TILE, MIX_G), im)
    in_specs = [blk(fwd), blk(fwd), blk(fwd), blk(fwd), blk(bwd), blk(bwd), blk(bwd), blk(bwd),
                pl.BlockSpec((None, 2, MIX_G, MIX_G), lambda s, t: (s, 0, 0, 0))]
    nrows = n_seq * nt * TILE
    return pl.pallas_call(
        functools.partial(_scan_kernel, nt),
        grid=(n_seq, nt),
        in_specs=in_specs,
        out_specs=[pl.BlockSpec((TILE, MIX_G), lambda s, t: (s * nt + t, 0)),
                   pl.BlockSpec((TILE, MIX_G), lambda s, t: (s * nt + (nt - 1 - t), 0)),
                   pl.BlockSpec((None, 2, MIX_G, MIX_G), lambda s, t: (s, 0, 0, 0))],
        out_shape=[jax.ShapeDtypeStruct((nrows, MIX_G), F32),
                   jax.ShapeDtypeStruct((nrows, MIX_G), F32),
                   jax.ShapeDtypeStruct((n_seq, 2, MIX_G, MIX_G), F32)],
        scratch_shapes=[pltpu.VMEM((2, MIX_G, MIX_G), F32)],
        compiler_params=_cparams(2),
        name=name,
    )(q, k_f, v, g_f, q, k_b, v, g_b, s0)


def _retention_tables():
    gam = 1.0 - 2.0 ** (-5.0 - np.arange(HEADS, dtype=np.float64))
    gam_r = gam[::-1]
    i = np.arange(TILE, dtype=np.float64)
    diff = i[:, None] - i[None, :]
    dcomb = np.zeros((HEADS, TILE, TILE))
    for h in range(HEADS):
        lower = np.where(diff > 0, gam[h] ** np.maximum(diff, 0), 0.0)
        upper = np.where(diff < 0, gam_r[h] ** np.maximum(-diff, 0), 0.0)
        dcomb[h] = lower + upper + 2.0 * (diff == 0)
    lanes = lambda per_head: np.repeat(per_head, HEAD_D, axis=-1)
    qd_f = lanes(gam[None, :] ** (i[:, None] + 1.0))
    kd_f = lanes(gam[None, :] ** (TILE - 1.0 - i[:, None]))
    qd_b = lanes(gam_r[None, :] ** (TILE - i[:, None]))
    kd_b = lanes(gam_r[None, :] ** i[:, None])
    blk = (np.arange(MIX_G)[:, None] // HEAD_D) == (np.arange(MIX_G)[None, :] // HEAD_D)
    c_f = np.where(blk, lanes(gam ** TILE)[None, :], 0.0) * np.ones((MIX_G, 1))
    c_b = np.where(blk, lanes(gam_r ** TILE)[None, :], 0.0) * np.ones((MIX_G, 1))
    f = lambda a: jnp.asarray(a, F32)
    return (f(dcomb.reshape(HEADS * TILE, TILE)), f(qd_f), f(kd_f), f(qd_b), f(kd_b), f(c_f), f(c_b))


def _ret_kernel(nt, qf_ref, kf_ref, vf_ref, qb_ref, kb_ref, vb_ref, s0_ref, dcomb_ref, qdf_ref, kdf_ref,
                qdb_ref, kdb_ref, cf_ref, cb_ref, of_ref, ob_ref, sfin_ref, st_scr):
    t = pl.program_id(1)

    @pl.when(t == 0)
    def _():
        st_scr[...] = s0_ref[...]

    lane_head = lax.broadcasted_iota(jnp.int32, (TILE, MIX_G), 1) // HEAD_D
    rr = lax.broadcasted_iota(jnp.int32, (MIX_G, MIX_G), 0) // HEAD_D
    cc = lax.broadcasted_iota(jnp.int32, (MIX_G, MIX_G), 1) // HEAD_D
    same_head = rr == cc

    q = qf_ref[...]
    k = kf_ref[...]
    v = vf_ref[...]
    qs = jnp.concatenate([jnp.where(lane_head == h, q, jnp.zeros_like(q)) for h in range(HEADS)], axis=0)
    p = (_dot_nt(qs, k) * dcomb_ref[...]).astype(BF16)
    full = _dot(p, v)
    o = _dot((q.astype(F32) * qdf_ref[...]).astype(BF16), st_scr[0].astype(BF16))
    for h in range(HEADS):
        o = o + jnp.where(lane_head == h, full[h * TILE:(h + 1) * TILE, :], 0.0)
    of_ref[...] = o
    u = _dot_tn((k.astype(F32) * kdf_ref[...]).astype(BF16), v)
    st_scr[0] = st_scr[0] * cf_ref[...] + jnp.where(same_head, u, 0.0)

    qb = qb_ref[...]
    kb = kb_ref[...]
    vb = vb_ref[...]
    ob_ref[...] = _dot((qb.astype(F32) * qdb_ref[...]).astype(BF16), st_scr[1].astype(BF16))
    ub = _dot_tn((kb.astype(F32) * kdb_ref[...]).astype(BF16), vb)
    st_scr[1] = st_scr[1] * cb_ref[...] + jnp.where(same_head, ub, 0.0)

    @pl.when(t == nt - 1)
    def _():
        sfin_ref[...] = st_scr[...]


def _retention(n_seq, nt, q, k, v, s0, tables, name):
    fwd = lambda s, t: (s * nt + t, 0)
    bwd = lambda s, t: (s * nt + (nt - 1 - t), 0)
    const = lambda s, t: (0, 0)
    blk = lambda im: pl.BlockSpec((TILE, MIX_G), im)
    sq = lambda: pl.BlockSpec((MIX_G, MIX_G), const)
    in_specs = [blk(fwd), blk(fwd), blk(fwd), blk(bwd), blk(bwd), blk(bwd),
                pl.BlockSpec((None, 2, MIX_G, MIX_G), lambda s, t: (s, 0, 0, 0)),
                pl.BlockSpec((HEADS * TILE, TILE), const), sq(), sq(), sq(), sq(), sq(), sq()]
    nrows = n_seq * nt * TILE
    return pl.pallas_call(
        functools.partial(_ret_kernel, nt),
        grid=(n_seq, nt),
        in_specs=in_specs,
        out_specs=[blk(fwd), blk(bwd), pl.BlockSpec((None, 2, MIX_G, MIX_G), lambda s, t: (s, 0, 0, 0))],
        out_shape=[jax.ShapeDtypeStruct((nrows, MIX_G), F32),
                   jax.ShapeDtypeStruct((nrows, MIX_G), F32),
                   jax.ShapeDtypeStruct((n_seq, 2, MIX_G, MIX_G), F32)],
        scratch_shapes=[pltpu.VMEM((2, MIX_G, MIX_G), F32)],
        compiler_params=_cparams(2),
        name=name,
    )(q, k, v, q, k, v, s0, *tables)


def _attn_step(lam_init, comp, refs, s_write, s_read, part0_ref):
    q_ref, k_ref, ck_ref, v_ref, cv_ref, lam_ref, sub_ref, o_ref = refs
    sw, sw2, mw = s_write
    sr, sr2, mr = s_read

    q = q_ref[...]
    lane = lax.broadcasted_iota(jnp.int32, q.shape, 1)
    qc = jnp.where((lane // HEAD_D) == comp, q, jnp.zeros_like(q))
    n_keys = k_ref.shape[0]
    kc = min(ATTN_KEY_CHUNK, n_keys)
    ones_rows = jnp.ones((ONES_ROWS, kc), BF16)
    m_prev = mr[...]
    oe, m_cur = None, None
    for j in range(n_keys // kc):
        ks = slice(j * kc, (j + 1) * kc)
        d = _dot(jnp.concatenate([v_ref[:, ks], ones_rows], axis=0),
                 jnp.exp2((sr[ks, :] - m_prev).astype(BF16)))
        oe = d if oe is None else oe + d
        s = _dot_nt(k_ref[ks, :], qc)
        sw[ks, :] = s
        mj = jnp.max(s, axis=0, keepdims=True)
        m_cur = mj if m_cur is None else jnp.maximum(m_cur, mj)
    oe = oe + _dot(jnp.concatenate([cv_ref[...], jnp.ones((ONES_ROWS, cv_ref.shape[1]), BF16)], axis=0),
                   jnp.exp2((sr2[...] - m_prev).astype(BF16)))
    s2 = _dot_nt(ck_ref[...], qc)
    sw2[...] = s2
    m_cur = jnp.maximum(m_cur, jnp.max(s2, axis=0, keepdims=True))
    mw[...] = m_cur
    part = oe[:128, :] / oe[128:129, :]
    if comp == 1:
        part0_ref[...] = part
    else:
        lp = lam_ref[...]
        lam = (jnp.exp(jnp.sum(lp[0:1] * lp[1:2], axis=-1, keepdims=True))
               - jnp.exp(jnp.sum(lp[2:3] * lp[3:4], axis=-1, keepdims=True)) + lam_init)
        o = part0_ref[...] - lam * part
        ms = jnp.mean(o * o, axis=0, keepdims=True)
        o = (o * lax.rsqrt(ms + RMS_EPS)).T
        o_ref[...] = (o * sub_ref[...] * (1.0 - lam_init)).astype(BF16)


def _attn_kernel(lam_init, *refs):
    n_in = 8
    io, part0_ref, scr = refs[:n_in], refs[n_in], refs[n_in + 1:]
    half = len(scr) // 2
    buf_a, buf_b = scr[:half], scr[half:]
    u = pl.program_id(0)

    @pl.when(u == 0)
    def _():
        for r in buf_b + (part0_ref,):
            r[...] = jnp.zeros(r.shape, r.dtype)

    @pl.when(u % 2 == 0)
    def _():
        _attn_step(lam_init, 0, io, buf_a, buf_b, part0_ref)

    @pl.when(u % 2 == 1)
    def _():
        _attn_step(lam_init, 1, io, buf_b, buf_a, part0_ref)


def _attn_short_kernel(lam_init, q_ref, k_ref, v_ref, lam_ref, sub_ref, o_ref):
    lp = lam_ref[...]
    lam = (jnp.exp(jnp.sum(lp[0:1] * lp[1:2], axis=-1, keepdims=True))
           - jnp.exp(jnp.sum(lp[2:3] * lp[3:4], axis=-1, keepdims=True)) + lam_init)
    n_keys = k_ref.shape[0]
    ones_rows = jnp.ones((ONES_ROWS, n_keys), BF16)
    lane = lax.broadcasted_iota(jnp.int32, (q_ref.shape[0], 128), 1)
    units = [(h, c) for h in range(HEADS) for c in range(2)]
    scores = []
    for h, c in units:
        q = q_ref[:, h * 128:(h + 1) * 128]
        qc = jnp.where((lane // HEAD_D) == c, q, jnp.zeros_like(q))
        scores.append(_dot_nt(k_ref[:, h * 128:(h + 1) * 128], qc))
    probs = [jnp.exp2((s - jnp.max(s, axis=0, keepdims=True)).astype(BF16)) for s in scores]
    oes = [_dot(jnp.concatenate([v_ref[h * 128:(h + 1) * 128, :], ones_rows], axis=0), p)
           for (h, c), p in zip(units, probs)]
    parts = [oe[:128, :] / oe[128:129, :] for oe in oes]
    outs = []
    for h in range(HEADS):
        o = parts[2 * h] - lam * parts[2 * h + 1]
        ms = jnp.mean(o * o, axis=0, keepdims=True)
        outs.append((o * lax.rsqrt(ms + RMS_EPS)).T * sub_ref[...] * (1.0 - lam_init))
    o_ref[...] = jnp.concatenate(outs, axis=1).astype(BF16)


def _diff_attention_short(layer, n_seq, seq_len, q, k, v_t, da_lambda, subln):
    lam_init = 0.8 - 0.6 * math.exp(-0.3 * layer)
    return pl.pallas_call(
        functools.partial(_attn_short_kernel, lam_init),
        grid=(n_seq,),
        in_specs=[pl.BlockSpec((seq_len, DA_W), lambda b: (b, 0)),
                  pl.BlockSpec((seq_len, DA_W), lambda b: (b, 0)),
                  pl.BlockSpec((DA_W, seq_len), lambda b: (0, b)),
                  pl.BlockSpec((None, 4, HEAD_D), lambda b: (layer, 0, 0)),
                  pl.BlockSpec((None, 1, 128), lambda b: (layer, 0, 0))],
        out_specs=pl.BlockSpec((seq_len, DA_W), lambda b: (b, 0)),
        out_shape=jax.ShapeDtypeStruct((n_seq * seq_len, DA_W), BF16),
        compiler_params=_cparams(1),
        name="diff_attention_ctx",
    )(q, k, v_t, da_lambda, subln)


def _diff_attention(layer, n_seq, seq_len, q, k, v_t, cache_k, cache_v_t, da_lambda, subln):
    tq = min(ATTN_TQ, seq_len)
    nq = seq_len // tq
    n_units = n_seq * HEADS * nq * 2
    lam_init = 0.8 - 0.6 * math.exp(-0.3 * layer)

    cur = lambda u: jnp.minimum(u, n_units - 1) // 2
    prev = lambda u: jnp.maximum(u - 1, 0) // 2
    seq_of = lambda w: w // (HEADS * nq)
    head_of = lambda w: (w // nq) % HEADS
    rows_of = lambda w: seq_of(w) * nq + w % nq

    q_spec = pl.BlockSpec((tq, 128), lambda u: (rows_of(cur(u)), head_of(cur(u))))
    k_spec = pl.BlockSpec((seq_len, 128), lambda u: (seq_of(cur(u)), head_of(cur(u))))
    v_spec = pl.BlockSpec((128, seq_len), lambda u: (head_of(prev(u)), seq_of(prev(u))))
    ck_spec = pl.BlockSpec((None, None, PAST_LEN, 128), lambda u: (seq_of(cur(u)), layer, 0, head_of(cur(u))))
    cv_spec = pl.BlockSpec((None, None, 128, PAST_LEN), lambda u: (seq_of(prev(u)), layer, head_of(prev(u)), 0))
    tail = [pl.BlockSpec((None, 4, HEAD_D), lambda u: (layer, 0, 0)),
            pl.BlockSpec((None, 1, 128), lambda u: (layer, 0, 0))]
    in_specs = [q_spec, k_spec, ck_spec, v_spec, cv_spec] + tail
    args = [q, k, cache_k, v_t, cache_v_t, da_lambda, subln]
    scratch = [pltpu.VMEM((128, tq), F32)] + [pltpu.VMEM((seq_len, tq), F32), pltpu.VMEM((PAST_LEN, tq), F32),
                                              pltpu.VMEM((1, tq), F32)] * 2
    return pl.pallas_call(
        functools.partial(_attn_kernel, lam_init),
        grid=(n_units + 1,),
        in_specs=in_specs,
        out_specs=pl.BlockSpec((tq, 128), lambda u: (rows_of(prev(u)), head_of(prev(u)))),
        out_shape=jax.ShapeDtypeStruct((n_seq * seq_len, DA_W), BF16),
        scratch_shapes=scratch,
        compiler_params=_cparams(1),
        name="diff_attention_lat",
    )(*args)


def _route(logits):
    lane = lax.broadcasted_iota(jnp.int32, logits.shape, 1)
    lane_f = lane.astype(F32)
    big = 1e9
    is_g = jnp.logical_and(lane >= N_EXPERTS, lane < N_EXPERTS + N_GROUPS)
    gl = jnp.where(is_g, logits, NEG_BIG)
    gmax = jnp.max(gl, axis=-1, keepdims=True)
    gsel = jnp.min(jnp.where(gl == gmax, lane_f, big), axis=-1, keepdims=True) - N_EXPERTS
    g_w = 1.0 / jnp.sum(jnp.exp(gl - gmax), axis=-1, keepdims=True)
    in_grp = jnp.logical_and(lane < N_EXPERTS, (lane // EPG).astype(F32) == gsel)
    el = jnp.where(in_grp, logits, NEG_BIG)
    v1 = jnp.max(el, axis=-1, keepdims=True)
    i1 = jnp.min(jnp.where(el == v1, lane_f, big), axis=-1, keepdims=True)
    el2 = jnp.where(lane_f == i1, NEG_BIG, el)
    v2 = jnp.max(el2, axis=-1, keepdims=True)
    i2 = jnp.min(jnp.where(el2 == v2, lane_f, big), axis=-1, keepdims=True)
    t = jnp.exp(v2 - v1)
    w1 = g_w / (1.0 + t)
    w2 = w1 * t
    first_lo = i1 < i2
    a = jnp.minimum(i1, i2) - EPG * gsel
    b = jnp.maximum(i1, i2) - EPG * gsel
    bucket = gsel * N_PAIRS + a * (7.0 - a) * 0.5 + (b - a - 1.0)
    w_lo = jnp.where(first_lo, w1, w2)
    w_hi = jnp.where(first_lo, w2, w1)
    return jnp.where(lane == 0, bucket, jnp.where(lane == 1, w_lo, jnp.where(lane == 2, w_hi, 0.0)))


def _outproj_body(mod_ref, n2_ref, streams, hgn_ref, bd_ref, wout_ref, wrh_ref, wrl_ref, br_ref,
                  x1_ref, hp_ref, route_ref):
    x_ref, rof_ref, rob_ref, rsg_ref, da_ref, hof_ref, hob_ref, hsg_ref = streams
    ro = rof_ref[...] + rob_ref[...]
    ro = ro * lax.rsqrt(_group_mean_sq(ro, bd_ref, HEAD_D) + RMS_EPS) * rsg_ref[...].astype(F32)
    ho = hof_ref[...] + hob_ref[...]
    ho = ho * lax.rsqrt(_group_mean_sq(ho, bd_ref, HEAD_D) + RMS_EPS) * hgn_ref[...] * hsg_ref[...].astype(F32)
    mix = jnp.concatenate([ro.astype(BF16), da_ref[...], ho.astype(BF16)], axis=1)
    mixed = _dot(mix, wout_ref[...])
    gate1 = mod_ref[2:3, :]
    shift2 = mod_ref[3:4, :]
    scale2 = mod_ref[4:5, :]
    x1 = x_ref[...] + gate1 * mixed
    ms = jnp.mean(x1 * x1, axis=-1, keepdims=True)
    h2 = x1 * lax.rsqrt(ms + RMS_EPS) * n2_ref[...] * (1.0 + scale2) + shift2
    h_hi = h2.astype(BF16)
    h_lo = (h2 - h_hi.astype(F32)).astype(BF16)
    logits = _dot(h_hi, wrh_ref[...]) + _dot(h_lo, wrh_ref[...]) + _dot(h_hi, wrl_ref[...]) + br_ref[...]
    route = _route(logits)
    x1_ref[...] = x1
    route_ref[...] = route
    hp_ref[...] = jnp.concatenate([h2, route], axis=1)


def _outproj_kernel(mod_ref, n2_ref, *rest):
    ctx_streams, lat_streams, tail = rest[0:8], rest[8:16], rest[16:]
    t = pl.program_id(0)

    @pl.when(t < CTX_PTILES)
    def _():
        _outproj_body(mod_ref, n2_ref, ctx_streams, *tail)

    @pl.when(t >= CTX_PTILES)
    def _():
        _outproj_body(mod_ref, n2_ref, lat_streams, *tail)


def _cond_of_tile(t):
    return jnp.where(t < CTX_TILES, 0, 1 + (t - CTX_TILES) // LAT_TILES_PER_SEQ)


def _out_projection(layer, mod, norm2, ctx_streams, lat_streams, hgn_t, bd256, w_out_bf, w_route_hi, w_route_lo,
                    b_route):
    row = lambda t: (t, 0)
    ctx_row = lambda t: (jnp.minimum(t, CTX_PTILES - 1), 0)
    lat_row = lambda t: (jnp.maximum(t - CTX_PTILES, 0), 0)
    lay = lambda t: (layer, 0, 0)
    cond = lambda t: jnp.where(t < CTX_PTILES, 0, 1 + (t - CTX_PTILES) // LAT_PTILES_PER_SEQ)
    widths = (D_MODEL, MIX_G, MIX_G, MIX_G, DA_W, MIX_G, MIX_G, MIX_G)
    in_specs = [
        pl.BlockSpec((None, None, 6, D_MODEL), lambda t: (layer, cond(t), 0, 0)),
        pl.BlockSpec((None, 1, D_MODEL), lay),
    ]
    in_specs += [pl.BlockSpec((PTILE, w), ctx_row) for w in widths]
    in_specs += [pl.BlockSpec((PTILE, w), lat_row) for w in widths]
    in_specs += [
        pl.BlockSpec((None, 1, MIX_G), lay),
        pl.BlockSpec((MIX_G, MIX_G), lambda t: (0, 0)),
        pl.BlockSpec((None, D_MODEL, D_MODEL), lay),
        pl.BlockSpec((None, D_MODEL, 128), lay),
        pl.BlockSpec((None, D_MODEL, 128), lay),
        pl.BlockSpec((None, 1, 128), lay),
    ]
    return pl.pallas_call(
        _outproj_kernel,
        grid=(N_ROWS // PTILE,),
        in_specs=in_specs,
        out_specs=[pl.BlockSpec((PTILE, D_MODEL), row), pl.BlockSpec((PTILE, ROW_W), row),
                   pl.BlockSpec((PTILE, 128), row)],
        out_shape=[jax.ShapeDtypeStruct((N_ROWS, D_MODEL), F32), jax.ShapeDtypeStruct((N_ROWS, ROW_W), F32),
                   jax.ShapeDtypeStruct((N_ROWS, 128), F32)],
        compiler_params=_cparams(1),
        name="out_projection",
    )(mod, norm2, *ctx_streams, *lat_streams, hgn_t, bd256, w_out_bf, w_route_hi, w_route_lo, b_route)


def _dispatch_kernel(route_ref, dest_ref, meta_ref, cnt_ref):
    ph = pl.program_id(0)
    t = pl.program_id(1)
    lane = lax.broadcasted_iota(jnp.int32, (TILE, 128), 1).astype(F32)
    onehots = [(lane == route_ref[j * TILE:(j + 1) * TILE, 0:1]).astype(F32) for j in range(DISPATCH_TILES)]
    tile_cnts = [jnp.sum(oh, axis=0, keepdims=True) for oh in onehots]

    @pl.when(jnp.logical_and(ph == 0, t == 0))
    def _():
        cnt_ref[...] = jnp.zeros_like(cnt_ref)

    @pl.when(ph == 0)
    def _():
        cnt_ref[0:1, :] += sum(tile_cnts[1:], tile_cnts[0])

    @pl.when(jnp.logical_and(ph == 1, t == 0))
    def _():
        cnt = cnt_ref[0:1, :]
        padded = jnp.floor((cnt + (MOE_TILE - 1.0)) * (1.0 / MOE_TILE)) * MOE_TILE
        r = lax.broadcasted_iota(jnp.int32, (128, 128), 0)
        c = lax.broadcasted_iota(jnp.int32, (128, 128), 1)
        off = jnp.dot(jnp.broadcast_to(padded, (8, 128)), (r < c).astype(F32),
                      preferred_element_type=F32, precision=HI)[0:1, :]
        cnt_ref[1:2, :] = off
        end = off + padded
        end_col = jnp.sum(jnp.where(r == c, jnp.broadcast_to(end, (128, 128)), 0.0), axis=1, keepdims=True)
        ended = jnp.logical_and(end_col <= c.astype(F32) * MOE_TILE, r < N_BUCKETS)
        tile_bucket = jnp.sum(ended.astype(F32), axis=0, keepdims=True)
        lane1 = lax.broadcasted_iota(jnp.int32, (1, 128), 1)
        tile_idx = lane1.astype(F32)
        n_valid = jnp.max(jnp.where(lane1 < N_BUCKETS, end, 0.0), axis=1, keepdims=True) * (1.0 / MOE_TILE)
        last = jnp.max(jnp.where(tile_idx < n_valid, tile_bucket, 0.0), axis=1, keepdims=True)
        tb = jnp.where(tile_idx < n_valid, tile_bucket, last)
        g = jnp.floor(tb * (1.0 / N_PAIRS))
        pid = tb - g * N_PAIRS
        a = (pid >= 3.0).astype(F32) + (pid >= 5.0).astype(F32)
        b = pid + 1.0 - 2.0 * (a >= 1.0).astype(F32) - (a >= 2.0).astype(F32)
        rows = [EPG * g + a, EPG * g + b, jnp.broadcast_to(n_valid, (1, 128)), jnp.zeros((5, 128), F32)]
        meta_ref[...] = jnp.concatenate(rows, axis=0).astype(jnp.int32)

    @pl.when(ph == 1)
    def _():
        base = cnt_ref[1:2, :]
        rr = lax.broadcasted_iota(jnp.int32, (TILE, TILE), 0)
        cc = lax.broadcasted_iota(jnp.int32, (TILE, TILE), 1)
        for j, onehot in enumerate(onehots):
            oh16 = onehot.astype(BF16)
            same = _dot_nt(oh16, oh16)
            rank = jnp.sum(jnp.where(rr < cc, same, 0.0), axis=0, keepdims=True)
            base_row = lax.dot_general(jnp.broadcast_to(base, (8, 128)), onehot, (((1,), (1,)), ((), ())),
                                       preferred_element_type=F32, precision=HI)[0:1, :]
            dest_ref[j] = (base_row + rank).astype(jnp.int32)
            base = base + tile_cnts[j]
        cnt_ref[1:2, :] = base


def _moe_dispatch(route):
    dest, meta = pl.pallas_call(
        _dispatch_kernel,
        grid=(2, N_TILES // DISPATCH_TILES),
        in_specs=[pl.BlockSpec((DISPATCH_TILES * TILE, 128), lambda ph, t: (t, 0))],
        out_specs=[pl.BlockSpec((DISPATCH_TILES, 1, TILE), lambda ph, t: (t * ph, 0, 0)),
                   pl.BlockSpec((8, 128), lambda ph, t: (0, 0))],
        out_shape=[jax.ShapeDtypeStruct((N_TILES, 1, TILE), jnp.int32),
                   jax.ShapeDtypeStruct((8, 128), jnp.int32)],
        scratch_shapes=[pltpu.VMEM((8, 128), F32)],
        compiler_params=_cparams(2),
        name="moe_dispatch",
    )(route)
    return dest.reshape(N_ROWS), meta[0:3].reshape(3 * 128)


def _dma_params():
    return pltpu.CompilerParams(dimension_semantics=("arbitrary",), vmem_limit_bytes=VMEM_LIMIT,
                                disable_bounds_checks=True)


def _moe_scatter_kernel(dest_ref, h_ref, hs_init_ref, hs_ref, stage, sems):
    del hs_init_ref
    t = pl.program_id(0)
    base = t * TILE
    slot = t % 2
    stage[slot] = h_ref[...]

    def issue(i, carry):
        d = dest_ref[base + i]
        pltpu.make_async_copy(stage.at[slot, pl.ds(i, 1)], hs_ref.at[pl.ds(d, 1)], sems.at[slot]).start()
        return carry

    lax.fori_loop(0, TILE, issue, 0, unroll=8)

    def wait_tile(s):
        pltpu.make_async_copy(stage.at[s], hs_ref.at[pl.ds(0, TILE)], sems.at[s]).wait()

    @pl.when(t > 0)
    def _():
        wait_tile(1 - slot)

    @pl.when(t == pl.num_programs(0) - 1)
    def _():
        wait_tile(slot)


def _moe_scatter(dest, hp, hs_init):
    return pl.pallas_call(
        _moe_scatter_kernel,
        grid_spec=pltpu.PrefetchScalarGridSpec(
            num_scalar_prefetch=1,
            grid=(N_TILES,),
            in_specs=[pl.BlockSpec((TILE, ROW_W), lambda t, dest: (t, 0)),
                      pl.BlockSpec(memory_space=pl.ANY)],
            out_specs=pl.BlockSpec(memory_space=pl.ANY),
            scratch_shapes=[pltpu.VMEM((2, TILE, ROW_W), F32), pltpu.SemaphoreType.DMA((2,))],
        ),
        out_shape=jax.ShapeDtypeStruct((N_SLOTS, ROW_W), F32),
        input_output_aliases={2: 0},
        compiler_params=_dma_params(),
        name="moe_scatter",
    )(dest, hp, hs_init)


def _moe_ffn_kernel(meta_ref, hs_ref, wgl_ref, wul_ref, wdl_ref, wgh_ref, wuh_ref, wdh_ref, ys_ref):
    t = pl.program_id(0)
    n_valid = meta_ref[2 * 128]

    @pl.when(t < n_valid)
    def _():
        w = hs_ref[...]
        x = w[:, :D_MODEL].astype(BF16)
        rw = w[:, D_MODEL:]

        def gated(wg_ref, wu_ref, weight):
            return (_silu(_dot(x, wg_ref[...])) * _dot(x, wu_ref[...]) * weight).astype(BF16)

        a_lo = gated(wgl_ref, wul_ref, rw[:, 1:2])
        a_hi = gated(wgh_ref, wuh_ref, rw[:, 2:3])
        ys_ref[...] = _dot(a_lo, wdl_ref[...]) + _dot(a_hi, wdh_ref[...])

    @pl.when(t >= n_valid)
    def _():
        ys_ref[...] = jnp.zeros_like(ys_ref)


def _moe_ffn(layer, meta, hs, w_gate_bf, w_up_bf, w_down_bf):
    lo = lambda t, m: (layer, m[t], 0, 0)
    hi = lambda t, m: (layer, m[128 + t], 0, 0)
    up = lambda im: pl.BlockSpec((None, None, D_MODEL, EXPERT_FF), im)
    down = lambda im: pl.BlockSpec((None, None, EXPERT_FF, D_MODEL), im)
    return pl.pallas_call(
        _moe_ffn_kernel,
        grid_spec=pltpu.PrefetchScalarGridSpec(
            num_scalar_prefetch=1,
            grid=(MOE_MAX_TILES,),
            in_specs=[pl.BlockSpec((MOE_TILE, ROW_W), lambda t, m: (jnp.maximum(jnp.minimum(t, m[2 * 128] - 1), 0), 0)),
                      up(lo), up(lo), down(lo), up(hi), up(hi), down(hi)],
            out_specs=pl.BlockSpec((MOE_TILE, D_MODEL), lambda t, m: (t, 0)),
        ),
        out_shape=jax.ShapeDtypeStruct((N_SLOTS, D_MODEL), F32),
        compiler_params=_cparams(1),
        name="moe_ffn",
    )(meta, hs, w_gate_bf, w_up_bf, w_down_bf, w_gate_bf, w_up_bf, w_down_bf)


def _moe_combine_kernel(dest_ref, x1_ref, mod_ref, ys_ref, out_ctx_ref, out_lat_ref, buf, sems):
    t = pl.program_id(0)
    nt = pl.num_programs(0)

    def issue(tile, slot):
        def body(i, carry):
            d = dest_ref[tile * TILE + i]
            pltpu.make_async_copy(ys_ref.at[pl.ds(d, 1)], buf.at[slot, pl.ds(i, 1)], sems.at[slot]).start()
            return carry
        lax.fori_loop(0, TILE, body, 0, unroll=8)

    @pl.when(t == 0)
    def _():
        issue(0, 0)

    @pl.when(t + 1 < nt)
    def _():
        issue(t + 1, (t + 1) % 2)

    slot = t % 2
    pltpu.make_async_copy(ys_ref.at[pl.ds(0, TILE)], buf.at[slot], sems.at[slot]).wait()
    x2 = x1_ref[...] + mod_ref[5:6, :] * buf[slot]

    @pl.when(t < CTX_TILES)
    def _():
        out_ctx_ref[...] = x2

    @pl.when(t >= CTX_TILES)
    def _():
        out_lat_ref[...] = x2


def _moe_combine(layer, dest, x1, mod, ys):
    return pl.pallas_call(
        _moe_combine_kernel,
        grid_spec=pltpu.PrefetchScalarGridSpec(
            num_scalar_prefetch=1,
            grid=(N_TILES,),
            in_specs=[pl.BlockSpec((TILE, D_MODEL), lambda t, dest: (t, 0)),
                      pl.BlockSpec((None, None, 6, D_MODEL), lambda t, dest: (layer, _cond_of_tile(t), 0, 0)),
                      pl.BlockSpec(memory_space=pl.ANY)],
            out_specs=[pl.BlockSpec((TILE, D_MODEL), lambda t, dest: (jnp.minimum(t, CTX_TILES - 1), 0)),
                       pl.BlockSpec((TILE, D_MODEL), lambda t, dest: (jnp.maximum(t - CTX_TILES, 0), 0))],
            scratch_shapes=[pltpu.VMEM((2, TILE, D_MODEL), F32), pltpu.SemaphoreType.DMA((2,))],
        ),
        out_shape=[jax.ShapeDtypeStruct((N_CTX_ROWS, D_MODEL), F32), jax.ShapeDtypeStruct((N_LAT_ROWS, D_MODEL), F32)],
        compiler_params=_dma_params(),
        name="moe_combine",
    )(dest, x1, mod, ys)


def _block_diag_ones(n, group):
    i = np.arange(n) // group
    return jnp.asarray((i[:, None] == i[None, :]).astype(np.float32), dtype=BF16)


def _rope_tables():
    pos = np.arange(DEC_SEQ)
    rows = (pos // GRID_W).astype(np.float32)
    cols = (pos % GRID_W).astype(np.float32)
    inv_freq = (ROPE_BASE ** (-(np.arange(ROPE_PAIRS, dtype=np.float32) / ROPE_PAIRS))).astype(np.float32)
    ang_r = rows[:, None] * inv_freq[None, :]
    ang_c = cols[:, None] * inv_freq[None, :]
    cos64 = np.concatenate([np.cos(ang_r), np.cos(ang_r), np.cos(ang_c), np.cos(ang_c)], axis=1)
    sin64 = np.concatenate([-np.sin(ang_r), np.sin(ang_r), -np.sin(ang_c), np.sin(ang_c)], axis=1)
    return (jnp.asarray(np.tile(cos64, (1, 2)), F32), jnp.asarray(np.tile(sin64, (1, 2)), F32))


def _state_to_blockdiag_t(s):
    eye = jnp.eye(HEADS, dtype=F32)
    return jnp.einsum('...hdv,hg->...hvgd', s.astype(F32), eye).reshape(s.shape[:-3] + (MIX_G, MIX_G))


def _blockdiag_t_to_state(st):
    return jnp.swapaxes(_blockdiag_to_state(st), -1, -2)


def _state_to_blockdiag(s):
    eye = jnp.eye(HEADS, dtype=F32)
    return jnp.einsum('...hdv,hg->...hdgv', s.astype(F32), eye).reshape(s.shape[:-3] + (MIX_G, MIX_G))


def _blockdiag_to_state(st):
    s6 = st.reshape(st.shape[:-2] + (HEADS, HEAD_D, HEADS, HEAD_D))
    return jnp.stack([s6[..., h, :, h, :] for h in range(HEADS)], axis=-3)


def _mixers(latent, layer, x, consts, params, ret_s0, hg_s0, cache_k, cache_v):
    (mod, lb_all, bd512, bd256, rope_cos, rope_sin, ret_tables) = consts
    n_seq = DEC_BATCH if latent else BATCH
    seq_len = DEC_SEQ if latent else SEQ
    nt = seq_len // TILE
    proj = _in_projection(latent, layer, x, mod, params['norm1'], params['w_in'], params['w_vt'], params['da_qn'],
                          params['da_kn'], lb_all, bd512, rope_cos, rope_sin)
    (rq, rk, rv, rsg, dq, dk, dv, hq, hkf, hkb, hv, hgf, hgb, hsg) = proj[:14]
    ret_of, ret_ob, ret_fin = _retention(n_seq, nt, rq, rk, rv, ret_s0, ret_tables,
                                         "retention_lat" if latent else "retention_ctx")
    hg_of, hg_ob, hg_fin = _gated_scan(n_seq, nt, hq, hkf, hkb, hv, hgf, hgb, hg_s0,
                                       "hgrn_scan_lat" if latent else "hgrn_scan_ctx")
    if latent:
        da_o = _diff_attention(layer, n_seq, seq_len, dq, dk, dv, cache_k, cache_v,
                               params['da_lambda'], params['da_subln'])
    else:
        da_o = _diff_attention_short(layer, n_seq, seq_len, dq, dk, dv, params['da_lambda'], params['da_subln'])
    streams = (x, ret_of, ret_ob, rsg, da_o, hg_of, hg_ob, hsg)
    extras = None if latent else (proj[14], proj[15], ret_fin, hg_fin)
    return streams, extras


def _trunk_layer(layer, x_ctx, x_lat, sorted_buf, consts, params, lat_ret_s0, lat_hg_s0, zero_state, cache_k, cache_v):
    mod, bd256 = consts[0], consts[3]
    ctx_streams, extras = _mixers(False, layer, x_ctx, consts, params, zero_state, zero_state, None, None)
    lat_streams, _ = _mixers(True, layer, x_lat, consts, params, lat_ret_s0, lat_hg_s0, cache_k, cache_v)
    x1, hp, route = _out_projection(layer, mod, params['norm2'], ctx_streams, lat_streams, params['hg_norm'],
                                    bd256, params['w_out'], params['w_route_hi'], params['w_route_lo'],
                                    params['b_route'])
    dest, meta = _moe_dispatch(route)
    sorted_buf = _moe_scatter(dest, hp, sorted_buf)
    ys = _moe_ffn(layer, meta, sorted_buf, params['w_gate'], params['w_up'], params['w_down'])
    x_ctx, x_lat = _moe_combine(layer, dest, x1, mod, ys)
    return x_ctx, x_lat, sorted_buf, extras


def kernel(x_prompt, x_sample, cache_k, cache_v, state_ret, state_hgrn, c, c_ctx, norm1, norm2, w_mod, b_mod,
           w_in, w_out, da_qn, da_kn, da_lambda, da_subln, hg_lb, hg_norm, w_group, b_group, w_router,
           b_router, w_gate, w_up, w_down):
    cond = jnp.zeros((N_COND, D_MODEL), F32).at[0].set(c_ctx).at[1:1 + DEC_BATCH].set(c)
    mod, lb_all = _modulation(cond, w_mod, b_mod, hg_lb)
    mod = mod.reshape(DEPTH, N_COND, 6, D_MODEL)
    rope_cos, rope_sin = _rope_tables()
    consts = (mod, lb_all.reshape(DEPTH, 1, MIX_G), _block_diag_ones(DA_W, HEAD_D),
              _block_diag_ones(MIX_G, HEAD_D), rope_cos, rope_sin, _retention_tables())
    pad = jnp.zeros((DEPTH, D_MODEL, 128 - N_EXPERTS - N_GROUPS), F32)
    w_route = jnp.concatenate([w_router, w_group, pad], axis=-1)
    params = {
        'norm1': norm1.reshape(DEPTH, 1, D_MODEL), 'norm2': norm2.reshape(DEPTH, 1, D_MODEL),
        'w_in': w_in.astype(BF16), 'w_out': w_out.astype(BF16),
        'w_vt': jnp.swapaxes(w_in[:, :, 2048:2560].astype(BF16), 1, 2),
        'da_qn': jnp.tile(da_qn, (1, DA_W // HEAD_D)).reshape(DEPTH, 1, DA_W),
        'da_kn': jnp.tile(da_kn, (1, DA_W // HEAD_D)).reshape(DEPTH, 1, DA_W),
        'da_lambda': da_lambda, 'da_subln': da_subln.reshape(DEPTH, 1, 128),
        'hg_norm': jnp.tile(hg_norm, (1, HEADS)).reshape(DEPTH, 1, MIX_G),
        'w_route_hi': w_route.astype(BF16),
        'w_route_lo': (w_route - w_route.astype(BF16).astype(F32)).astype(BF16),
        'b_route': jnp.concatenate([b_router, b_group, pad[:, 0, :]], axis=-1).reshape(DEPTH, 1, 128),
        'w_gate': w_gate.astype(BF16), 'w_up': w_up.astype(BF16), 'w_down': w_down.astype(BF16),
    }
    ck_bf = cache_k.astype(BF16).reshape(DEC_BATCH, DEPTH, PAST_LEN, DA_W)
    cv_bf = jnp.swapaxes(cache_v.astype(BF16).reshape(DEC_BATCH, DEPTH, PAST_LEN, DA_W), 2, 3)
    zero_state = jnp.zeros((BATCH, 2, MIX_G, MIX_G), F32)

    yp = x_prompt.reshape(N_CTX_ROWS, D_MODEL)
    ys = x_sample.reshape(N_LAT_ROWS, D_MODEL)
    sorted_buf = jnp.zeros((N_SLOTS, ROW_W), F32)
    ret_s0 = _state_to_blockdiag(state_ret)
    hg_s0 = _state_to_blockdiag_t(state_hgrn)
    ks_out, vs_out, rets_out, hgs_out = [], [], [], []
    for l in range(DEPTH):
        yp, ys, sorted_buf, (k_new, v_new, ret_fin, hg_fin) = _trunk_layer(
            l, yp, ys, sorted_buf, consts, params, ret_s0[:, l], hg_s0[:, l], zero_state, ck_bf, cv_bf)
        ks_out.append(k_new.reshape(BATCH, SEQ, HEADS, 2, HEAD_D))
        vs_out.append(v_new.reshape(BATCH, SEQ, HEADS, 2 * HEAD_D))
        rets_out.append(ret_fin)
        hgs_out.append(hg_fin)
    return (yp.reshape(BATCH, SEQ, D_MODEL), ys.reshape(DEC_BATCH, DEC_SEQ, D_MODEL),
            jnp.stack(ks_out, axis=1), jnp.stack(vs_out, axis=1),
            _blockdiag_to_state(jnp.stack(rets_out, axis=1)), _blockdiag_t_to_state(jnp.stack(hgs_out, axis=1)))
```

```python
import functools
import math

import numpy as np
import jax
import jax.numpy as jnp
from jax import lax
from jax.experimental import pallas as pl
from jax.experimental.pallas import tpu as pltpu

F32 = jnp.float32
BF16 = jnp.bfloat16

D_MODEL = 1024
BATCH = 16
SEQ = 256
DEPTH = 4
DEC_BATCH = 4
DEC_SEQ = 4096
PAST_LEN = 256
GRID_W = 64
HEADS = 4
HEAD_D = 64
MIX_G = HEADS * HEAD_D
DA_W = 512
IN_COLS = 3840
ROPE_BASE = 10000.0
ROPE_PAIRS = 16
RMS_EPS = 1e-6
N_GROUPS = 4
EPG = 4
N_EXPERTS = 16
EXPERT_FF = 512

TILE = 256
N_CTX_ROWS = BATCH * SEQ
N_LAT_ROWS = DEC_BATCH * DEC_SEQ
N_ROWS = N_CTX_ROWS + N_LAT_ROWS
CTX_TILES = N_CTX_ROWS // TILE
LAT_TILES_PER_SEQ = DEC_SEQ // TILE
N_TILES = N_ROWS // TILE
PTILE = 512
CTX_PTILES = N_CTX_ROWS // PTILE
LAT_PTILES_PER_SEQ = DEC_SEQ // PTILE
N_COND = 8

N_PAIRS = EPG * (EPG - 1) // 2
N_BUCKETS = N_GROUPS * N_PAIRS
MOE_TILE = 256
MOE_MAX_TILES = N_ROWS // MOE_TILE + N_BUCKETS
DISPATCH_TILES = 4
N_SLOTS = MOE_MAX_TILES * MOE_TILE
ROW_W = D_MODEL + 128

CHUNK = 64
SUB = 16
NSUB = CHUNK // SUB
NCHUNK = TILE // CHUNK

VMEM_LIMIT = 48 * 1024 * 1024
NEG_BIG = -1e30
ONES_ROWS = 16
ATTN_TQ = 512
ATTN_KEY_CHUNK = 512
LOG2_E = 1.4426950408889634

HI = lax.Precision.HIGHEST


def _cparams(n_axes):
    return pltpu.CompilerParams(dimension_semantics=("arbitrary",) * n_axes,
                                vmem_limit_bytes=VMEM_LIMIT)


def _dot(a, b):
    return jnp.dot(a, b, preferred_element_type=F32)


def _dot_nt(a, b):
    return lax.dot_general(a, b, (((1,), (1,)), ((), ())), preferred_element_type=F32)


def _dot_tn(a, b):
    return lax.dot_general(a, b, (((0,), (0,)), ((), ())), preferred_element_type=F32)


def _silu(x):
    return x * (1.0 / (1.0 + jnp.exp(-x)))


def _sigmoid(x):
    return 1.0 / (1.0 + jnp.exp(-x))


def _mod_kernel(cond_ref, w_ref, b_ref, lb_ref, mod_ref, lbo_ref):
    c = cond_ref[...]
    m = jnp.dot(_silu(c), w_ref[...], preferred_element_type=F32, precision=HI)
    mod_ref[...] = m + b_ref[...]
    z = lb_ref[...]
    zmax = jnp.max(z, axis=0, keepdims=True)
    e = jnp.exp(z - zmax)
    p = e / jnp.sum(e, axis=0, keepdims=True)
    rows = [jnp.zeros_like(p[0:1])]
    for l in range(1, DEPTH):
        rows.append(rows[-1] + p[l:l + 1])
    lbo_ref[...] = jnp.concatenate(rows, axis=0)


def _modulation(cond, w_mod, b_mod, hg_lb):
    nblk = 6
    return pl.pallas_call(
        _mod_kernel,
        grid=(DEPTH, nblk),
        in_specs=[
            pl.BlockSpec((N_COND, D_MODEL), lambda l, j: (0, 0)),
            pl.BlockSpec((None, D_MODEL, D_MODEL), lambda l, j: (l, 0, j)),
            pl.BlockSpec((None, 1, D_MODEL), lambda l, j: (l, 0, j)),
            pl.BlockSpec((DEPTH, MIX_G), lambda l, j: (0, 0)),
        ],
        out_specs=[
            pl.BlockSpec((None, N_COND, D_MODEL), lambda l, j: (l, 0, j)),
            pl.BlockSpec((DEPTH, MIX_G), lambda l, j: (0, 0)),
        ],
        out_shape=[jax.ShapeDtypeStruct((DEPTH, N_COND, 6 * D_MODEL), F32),
                   jax.ShapeDtypeStruct((DEPTH, MIX_G), F32)],
        compiler_params=_cparams(2),
        name="modulation",
    )(cond, w_mod, b_mod.reshape(DEPTH, 1, 6 * D_MODEL), hg_lb)


def _group_mean_sq(x, bd_ref, group):
    return _dot((x * x).astype(BF16), bd_ref[...]) * (1.0 / group)


def _swap16(x):
    w = x.shape[-1]
    lane = lax.broadcasted_iota(jnp.int32, x.shape, x.ndim - 1)
    up = pltpu.roll(x, w - 16, x.ndim - 1)
    dn = pltpu.roll(x, 16, x.ndim - 1)
    return jnp.where((lane % 32) < 16, up, dn)


def _inproj_kernel(latent, x_ref, mod_ref, n1_ref, w_ref, wvt_ref, qn_ref, kn_ref, lb_ref, bd_ref, *rest):
    if latent:
        cos_ref, sin_ref = rest[0], rest[1]
        outs = rest[2:]
    else:
        outs = rest
    (rq_ref, rk_ref, rv_ref, rsg_ref, dq_ref, dk_ref, dv_ref,
     hq_ref, hkf_ref, hkb_ref, hv_ref, hgf_ref, hgb_ref, hsg_ref) = outs[:14]

    x = x_ref[...]
    ms = jnp.mean(x * x, axis=-1, keepdims=True)
    shift1 = mod_ref[0:1, :]
    scale1 = mod_ref[1:2, :]
    h = x * lax.rsqrt(ms + RMS_EPS) * n1_ref[...] * (1.0 + scale1) + shift1
    h16 = h.astype(BF16)
    y = _dot(h16, w_ref[...])

    rq_ref[...] = y[:, 0:256].astype(BF16)
    rk_ref[...] = (y[:, 256:512] * (HEAD_D ** -0.5)).astype(BF16)
    rv_ref[...] = y[:, 512:768].astype(BF16)
    rsg_ref[...] = _silu(y[:, 768:1024]).astype(BF16)

    dq = y[:, 1024:1536]
    dk = y[:, 1536:2048]
    qn = dq * lax.rsqrt(_group_mean_sq(dq, bd_ref, HEAD_D) + RMS_EPS) * qn_ref[...]
    kn = dk * lax.rsqrt(_group_mean_sq(dk, bd_ref, HEAD_D) + RMS_EPS) * kn_ref[...]
    if latent:
        cos = jnp.concatenate([cos_ref[...]] * 4, axis=1)
        sin = jnp.concatenate([sin_ref[...]] * 4, axis=1)
        qr = qn * cos + _swap16(qn) * sin
        kr = kn * cos + _swap16(kn) * sin
    else:
        qr, kr = qn, kn
        ck_ref, cv_ref = outs[14], outs[15]
        ck_ref[...] = kn
        cv_ref[...] = y[:, 2048:2560]
    dq_ref[...] = (qr * (HEAD_D ** -0.5 * LOG2_E)).astype(BF16)
    dk_ref[...] = kr.astype(BF16)
    dv_ref[...] = _dot_nt(wvt_ref[...], h16).astype(BF16)

    lb = lb_ref[...]
    f_f = lb + (1.0 - lb) * _sigmoid(y[:, 2816:3072])
    f_b = lb + (1.0 - lb) * _sigmoid(y[:, 3072:3328])
    hq_ref[...] = y[:, 2560:2816].astype(BF16)
    hkf_ref[...] = (1.0 - f_f).astype(BF16)
    hkb_ref[...] = (1.0 - f_b).astype(BF16)
    hgf_ref[...] = jnp.log2(f_f)
    hgb_ref[...] = jnp.log2(f_b)
    hv_ref[...] = y[:, 3328:3584].astype(BF16)
    hsg_ref[...] = _silu(y[:, 3584:3840]).astype(BF16)


def _in_projection(latent, layer, x, mod, norm1, w_in_bf, w_vt_bf, qn_t, kn_t, lb_all, bd512, rope_cos, rope_sin):
    nrows = x.shape[0]
    ntiles = nrows // PTILE
    cond = (lambda t: 1 + t // LAT_PTILES_PER_SEQ) if latent else (lambda t: 0)

    in_specs = [
        pl.BlockSpec((PTILE, D_MODEL), lambda t: (t, 0)),
        pl.BlockSpec((None, None, 6, D_MODEL), lambda t: (layer, cond(t), 0, 0)),
        pl.BlockSpec((None, 1, D_MODEL), lambda t: (layer, 0, 0)),
        pl.BlockSpec((None, D_MODEL, IN_COLS), lambda t: (layer, 0, 0)),
        pl.BlockSpec((None, DA_W, D_MODEL), lambda t: (layer, 0, 0)),
        pl.BlockSpec((None, 1, DA_W), lambda t: (layer, 0, 0)),
        pl.BlockSpec((None, 1, DA_W), lambda t: (layer, 0, 0)),
        pl.BlockSpec((None, 1, MIX_G), lambda t: (layer, 0, 0)),
        pl.BlockSpec((DA_W, DA_W), lambda t: (0, 0)),
    ]
    args = [x, mod, norm1, w_in_bf, w_vt_bf, qn_t, kn_t, lb_all, bd512]
    if latent:
        in_specs += [pl.BlockSpec((PTILE, 128), lambda t: (t % LAT_PTILES_PER_SEQ, 0))] * 2
        args += [rope_cos, rope_sin]

    def o(width, dtype):
        return jax.ShapeDtypeStruct((nrows, width), dtype), pl.BlockSpec((PTILE, width), lambda t: (t, 0))

    dv_t = (jax.ShapeDtypeStruct((DA_W, nrows), BF16), pl.BlockSpec((DA_W, PTILE), lambda t: (0, t)))
    outs = [o(256, BF16), o(256, BF16), o(256, BF16), o(256, BF16),
            o(512, BF16), o(512, BF16), dv_t,
            o(256, BF16), o(256, BF16), o(256, BF16), o(256, BF16), o(256, F32), o(256, F32), o(256, BF16)]
    if not latent:
        outs += [o(512, F32), o(512, F32)]
    return pl.pallas_call(
        functools.partial(_inproj_kernel, latent),
        grid=(ntiles,),
        in_specs=in_specs,
        out_specs=[s for _, s in outs],
        out_shape=[s for s, _ in outs],
        compiler_params=_cparams(1),
        name="in_projection_lat" if latent else "in_projection_ctx",
    )(*args)


def _load_states(s0_ref, st_scr, transposed):
    st_scr[...] = jnp.zeros_like(st_scr)
    for d in range(2):
        for h in range(HEADS):
            blk = s0_ref[d, h]
            st_scr[d, h * HEAD_D:(h + 1) * HEAD_D, h * HEAD_D:(h + 1) * HEAD_D] = blk.T if transposed else blk


def _store_states(st_scr, sfin_ref, transposed):
    for d in range(2):
        for h in range(HEADS):
            blk = st_scr[d, h * HEAD_D:(h + 1) * HEAD_D, h * HEAD_D:(h + 1) * HEAD_D]
            sfin_ref[d, h] = blk.T if transposed else blk


def _head_masks():
    lane = lax.broadcasted_iota(jnp.int32, (1, MIX_G), 1)
    return [(lane // HEAD_D == h).astype(F32) for h in range(HEADS)]


def _scan_chunks(items, st_f, st_b, consts_f, consts_b):
    n = len(items)
    cs = [consts_b if it[4] else consts_f for it in items]
    zero_row = jnp.zeros((1, MIX_G), F32)

    bs = []
    for x in range(n):
        g = items[x][3]
        g_hi = g.astype(BF16)
        g_lo = (g - g_hi.astype(F32)).astype(BF16)
        bs.append(_dot(cs[x][0], g_hi) + _dot(cs[x][0], g_lo))

    prep = []
    for x in range(n):
        q, k, v, _, rev = items[x]
        hmask = cs[x][4]
        b = bs[x]
        r_rows, e_rows = [], []
        for s in range(NSUB):
            lo, hi = s * SUB, s * SUB + SUB - 1
            if not rev:
                r_rows.append(b[lo - 1:lo, :] if s > 0 else zero_row)
                e_rows.append(b[hi:hi + 1, :])
            else:
                r_rows.append(b[hi + 1:hi + 2, :] if s < NSUB - 1 else zero_row)
                e_rows.append(b[lo:lo + 1, :])
        btot = e_rows[NSUB - 1] if not rev else e_rows[0]
        r_full = jnp.concatenate([jnp.broadcast_to(r, (SUB, MIX_G)) for r in r_rows], axis=0)
        e_full = jnp.concatenate([jnp.broadcast_to(e, (SUB, MIX_G)) for e in e_rows], axis=0)
        bl = b - r_full
        qp = q * jnp.exp2(bl)
        kend = k * jnp.exp2(e_full - b)
        pairs = [(i, j) for i in range(NSUB) for j in range(NSUB) if (j < i if not rev else j > i)]
        lh = jnp.concatenate([qp[i * SUB:(i + 1) * SUB, :] * jnp.exp2(r_rows[i] - e_rows[j])
                              for (i, j) in pairs], axis=0)
        prep.append(dict(
            bl=bl, btot=btot, pairs=pairs,
            q_inter=(qp * jnp.exp2(r_full)).astype(BF16),
            k_state=(kend * jnp.exp2(btot - e_full)).astype(BF16),
            kend=kend.astype(BF16),
            lhs4=jnp.concatenate([lh * hmask[h] for h in range(HEADS)], axis=0).astype(BF16)))

    uts = [_dot_tn(items[x][2], prep[x]['k_state']) for x in range(n)]
    scs = [_dot_nt(prep[x]['lhs4'], prep[x]['kend']) for x in range(n)]

    st_in = []
    for x in range(n):
        rev = items[x][4]
        st = st_b if rev else st_f
        st_in.append(st.astype(BF16))
        st = st * jnp.exp2(prep[x]['btot']) + uts[x] * cs[x][1]
        if rev:
            st_b = st
        else:
            st_f = st
    col = lax.broadcasted_iota(jnp.int32, (SUB, CHUNK), 1) // SUB
    jrow = lax.broadcasted_iota(jnp.int32, (SUB, MIX_G), 0)
    pss, pps, vts = [], [], []
    for x in range(n):
        q, k, v, _, rev = items[x]
        pairs, sc, bl = prep[x]['pairs'], scs[x], prep[x]['bl']
        p_rows = []
        for h in range(HEADS):
            for i in range(NSUB):
                acc = None
                for p, (pi, pj) in enumerate(pairs):
                    if pi != i:
                        continue
                    base = (h * len(pairs) + p) * SUB
                    blk = jnp.where(col == pj, sc[base:base + SUB, :], 0.0)
                    acc = blk if acc is None else acc + blk
                p_rows.append(jnp.zeros((SUB, CHUNK), F32) if acc is None else acc)
        pss.append(jnp.concatenate(p_rows, axis=0).astype(BF16))
        pp, vt = [], []
        for s in range(NSUB):
            sl = slice(s * SUB, (s + 1) * SUB)
            bl_s, q_s, k_s = bl[sl, :], q[sl, :].astype(BF16), k[sl, :].astype(BF16)
            for i in range(SUB):
                d = bl_s[i:i + 1, :] - bl_s
                keep = (jrow <= i) if not rev else (jrow >= i)
                e = jnp.exp2(jnp.where(keep, d, NEG_BIG).astype(BF16))
                pp.append(e * q_s[i:i + 1, :] * k_s)
            vt.extend([v[sl, :]] * SUB)
        pps.append(jnp.concatenate(pp, axis=0))
        vts.append(jnp.concatenate(vt, axis=0))

    o_int = [_dot_nt(prep[x]['q_inter'], st_in[x]) for x in range(n)]
    fulls = [_dot(pss[x], items[x][2]) for x in range(n)]
    sbs = [_dot(pps[x], cs[x][2]) for x in range(n)]

    ws = [(sbs[x] * vts[x].astype(F32)).astype(BF16) for x in range(n)]
    o_diag = [_dot(cs[x][3], ws[x]) for x in range(n)]

    outs = []
    for x in range(n):
        hmask = cs[x][4]
        o = o_int[x] + o_diag[x]
        for h in range(HEADS):
            o = o + fulls[x][h * CHUNK:(h + 1) * CHUNK, :] * hmask[h]
        outs.append(o)
    return outs, st_f, st_b


def _scan_consts(rev):
    r = lax.broadcasted_iota(jnp.int32, (CHUNK, CHUNK), 0)
    c = lax.broadcasted_iota(jnp.int32, (CHUNK, CHUNK), 1)
    tri = ((c <= r) if not rev else (c >= r)).astype(BF16)
    rr = lax.broadcasted_iota(jnp.int32, (MIX_G, MIX_G), 0) // HEAD_D
    cc = lax.broadcasted_iota(jnp.int32, (MIX_G, MIX_G), 1) // HEAD_D
    bd_mask = (rr == cc).astype(F32)
    ones_bd = bd_mask.astype(BF16)
    gi = lax.broadcasted_iota(jnp.int32, (CHUNK, CHUNK * SUB), 0)
    gj = lax.broadcasted_iota(jnp.int32, (CHUNK, CHUNK * SUB), 1) // SUB
    gsel = (gi == gj).astype(BF16)
    return tri, bd_mask, ones_bd, gsel, _head_masks()


def _scan_kernel(nt, qf_ref, kf_ref, vf_ref, gf_ref, qb_ref, kb_ref, vb_ref, gb_ref, s0_ref,
                 of_ref, ob_ref, sfin_ref, st_scr):
    t = pl.program_id(1)

    @pl.when(t == 0)
    def _():
        _load_states(s0_ref, st_scr, True)

    cf = _scan_consts(False)
    cb = _scan_consts(True)

    items, rows = [], []
    for c in range(NCHUNK):
        rf = slice(c * CHUNK, (c + 1) * CHUNK)
        rb = slice((NCHUNK - 1 - c) * CHUNK, (NCHUNK - c) * CHUNK)
        items.append((qf_ref[rf, :].astype(F32), kf_ref[rf, :].astype(F32), vf_ref[rf, :], gf_ref[rf, :], False))
        items.append((qb_ref[rb, :].astype(F32), kb_ref[rb, :].astype(F32), vb_ref[rb, :], gb_ref[rb, :], True))
        rows += [(of_ref, rf), (ob_ref, rb)]
    outs, st_f, st_b = _scan_chunks(items, st_scr[0], st_scr[1], cf, cb)
    for (ref, sl), o in zip(rows, outs):
        ref[sl, :] = o
    st_scr[0] = st_f
    st_scr[1] = st_b

    @pl.when(t == nt - 1)
    def _():
        _store_states(st_scr, sfin_ref, True)


def _gated_scan(n_seq, nt, q, k_f, k_b, v, g_f, g_b, s0, name):
    fwd = lambda s, t: (s * nt + t, 0)
    bwd = lambda s, t: (s * nt + (nt - 1 - t), 0)
    blk = lambda im: pl.BlockSpec((TILE, MIX_G), im)
    in_specs = [blk(fwd), blk(fwd), blk(fwd), blk(fwd), blk(bwd), blk(bwd), blk(bwd), blk(bwd),
                pl.BlockSpec((None, 2, HEADS, HEAD_D, HEAD_D), lambda s, t: (s, 0, 0, 0, 0))]
    nrows = n_seq * nt * TILE
    return pl.pallas_call(
        functools.partial(_scan_kernel, nt),
        grid=(n_seq, nt),
        in_specs=in_specs,
        out_specs=[pl.BlockSpec((TILE, MIX_G), lambda s, t: (s * nt + t, 0)),
                   pl.BlockSpec((TILE, MIX_G), lambda s, t: (s * nt + (nt - 1 - t), 0)),
                   pl.BlockSpec((None, 2, HEADS, HEAD_D, HEAD_D), lambda s, t: (s, 0, 0, 0, 0))],
        out_shape=[jax.ShapeDtypeStruct((nrows, MIX_G), F32),
                   jax.ShapeDtypeStruct((nrows, MIX_G), F32),
                   jax.ShapeDtypeStruct((n_seq, 2, HEADS, HEAD_D, HEAD_D), F32)],
        scratch_shapes=[pltpu.VMEM((2, MIX_G, MIX_G), F32)],
        compiler_params=_cparams(2),
        name=name,
    )(q, k_f, v, g_f, q, k_b, v, g_b, s0)


def _retention_tables():
    gam = 1.0 - 2.0 ** (-5.0 - np.arange(HEADS, dtype=np.float64))
    gam_r = gam[::-1]
    i = np.arange(TILE, dtype=np.float64)
    diff = i[:, None] - i[None, :]
    dcomb = np.zeros((HEADS, TILE, TILE))
    for h in range(HEADS):
        lower = np.where(diff > 0, gam[h] ** np.maximum(diff, 0), 0.0)
        upper = np.where(diff < 0, gam_r[h] ** np.maximum(-diff, 0), 0.0)
        dcomb[h] = lower + upper + 2.0 * (diff == 0)
    lanes = lambda per_head: np.repeat(per_head, HEAD_D, axis=-1)
    qd_f = lanes(gam[None, :] ** (i[:, None] + 1.0))
    kd_f = lanes(gam[None, :] ** (TILE - 1.0 - i[:, None]))
    qd_b = lanes(gam_r[None, :] ** (TILE - i[:, None]))
    kd_b = lanes(gam_r[None, :] ** i[:, None])
    blk = (np.arange(MIX_G)[:, None] // HEAD_D) == (np.arange(MIX_G)[None, :] // HEAD_D)
    c_f = np.where(blk, lanes(gam ** TILE)[None, :], 0.0) * np.ones((MIX_G, 1))
    c_b = np.where(blk, lanes(gam_r ** TILE)[None, :], 0.0) * np.ones((MIX_G, 1))
    f = lambda a: jnp.asarray(a, F32)
    return (f(dcomb.reshape(HEADS * TILE, TILE)), f(qd_f), f(kd_f), f(qd_b), f(kd_b), f(c_f), f(c_b))


def _ret_kernel(nt, qf_ref, kf_ref, vf_ref, qb_ref, kb_ref, vb_ref, s0_ref, dcomb_ref, qdf_ref, kdf_ref,
                qdb_ref, kdb_ref, cf_ref, cb_ref, of_ref, ob_ref, sfin_ref, st_scr):
    t = pl.program_id(1)

    @pl.when(t == 0)
    def _():
        _load_states(s0_ref, st_scr, False)

    lane_head = lax.broadcasted_iota(jnp.int32, (TILE, MIX_G), 1) // HEAD_D
    rr = lax.broadcasted_iota(jnp.int32, (MIX_G, MIX_G), 0) // HEAD_D
    cc = lax.broadcasted_iota(jnp.int32, (MIX_G, MIX_G), 1) // HEAD_D
    same_head = rr == cc

    q = qf_ref[...]
    k = kf_ref[...]
    v = vf_ref[...]
    qs = jnp.concatenate([jnp.where(lane_head == h, q, jnp.zeros_like(q)) for h in range(HEADS)], axis=0)
    p = (_dot_nt(qs, k) * dcomb_ref[...]).astype(BF16)
    full = _dot(p, v)
    o = _dot((q.astype(F32) * qdf_ref[...]).astype(BF16), st_scr[0].astype(BF16))
    for h in range(HEADS):
        o = o + jnp.where(lane_head == h, full[h * TILE:(h + 1) * TILE, :], 0.0)
    of_ref[...] = o
    u = _dot_tn((k.astype(F32) * kdf_ref[...]).astype(BF16), v)
    st_scr[0] = st_scr[0] * cf_ref[...] + jnp.where(same_head, u, 0.0)

    qb = qb_ref[...]
    kb = kb_ref[...]
    vb = vb_ref[...]
    ob_ref[...] = _dot((qb.astype(F32) * qdb_ref[...]).astype(BF16), st_scr[1].astype(BF16))
    ub = _dot_tn((kb.astype(F32) * kdb_ref[...]).astype(BF16), vb)
    st_scr[1] = st_scr[1] * cb_ref[...] + jnp.where(same_head, ub, 0.0)

    @pl.when(t == nt - 1)
    def _():
        _store_states(st_scr, sfin_ref, False)


def _retention(n_seq, nt, q, k, v, s0, tables, name):
    fwd = lambda s, t: (s * nt + t, 0)
    bwd = lambda s, t: (s * nt + (nt - 1 - t), 0)
    const = lambda s, t: (0, 0)
    blk = lambda im: pl.BlockSpec((TILE, MIX_G), im)
    sq = lambda: pl.BlockSpec((MIX_G, MIX_G), const)
    in_specs = [blk(fwd), blk(fwd), blk(fwd), blk(bwd), blk(bwd), blk(bwd),
                pl.BlockSpec((None, 2, HEADS, HEAD_D, HEAD_D), lambda s, t: (s, 0, 0, 0, 0)),
                pl.BlockSpec((HEADS * TILE, TILE), const), sq(), sq(), sq(), sq(), sq(), sq()]
    nrows = n_seq * nt * TILE
    return pl.pallas_call(
        functools.partial(_ret_kernel, nt),
        grid=(n_seq, nt),
        in_specs=in_specs,
        out_specs=[blk(fwd), blk(bwd), pl.BlockSpec((None, 2, HEADS, HEAD_D, HEAD_D), lambda s, t: (s, 0, 0, 0, 0))],
        out_shape=[jax.ShapeDtypeStruct((nrows, MIX_G), F32),
                   jax.ShapeDtypeStruct((nrows, MIX_G), F32),
                   jax.ShapeDtypeStruct((n_seq, 2, HEADS, HEAD_D, HEAD_D), F32)],
        scratch_shapes=[pltpu.VMEM((2, MIX_G, MIX_G), F32)],
        compiler_params=_cparams(2),
        name=name,
    )(q, k, v, q, k, v, s0, *tables)


def _attn_step(lam_init, comp, refs, s_write, s_read, part0_ref):
    q_ref, k_ref, ck_ref, v_ref, cv_ref, lam_ref, sub_ref, o_ref = refs
    sw, sw2, mw = s_write
    sr, sr2, mr = s_read

    q = q_ref[...]
    lane = lax.broadcasted_iota(jnp.int32, q.shape, 1)
    qc = jnp.where((lane // HEAD_D) == comp, q, jnp.zeros_like(q))
    n_keys = k_ref.shape[0]
    kc = min(ATTN_KEY_CHUNK, n_keys)
    ones_rows = jnp.ones((ONES_ROWS, kc), BF16)
    m_prev = mr[...]
    oe, m_cur = None, None
    for j in range(n_keys // kc):
        ks = slice(j * kc, (j + 1) * kc)
        d = _dot(jnp.concatenate([v_ref[:, ks], ones_rows], axis=0),
                 jnp.exp2((sr[ks, :] - m_prev).astype(BF16)))
        oe = d if oe is None else oe + d
        s = _dot_nt(k_ref[ks, :], qc)
        sw[ks, :] = s
        mj = jnp.max(s, axis=0, keepdims=True)
        m_cur = mj if m_cur is None else jnp.maximum(m_cur, mj)
    oe = oe + _dot(jnp.concatenate([cv_ref[...], jnp.ones((ONES_ROWS, cv_ref.shape[1]), BF16)], axis=0),
                   jnp.exp2((sr2[...] - m_prev).astype(BF16)))
    s2 = _dot_nt(ck_ref[...], qc)
    sw2[...] = s2
    m_cur = jnp.maximum(m_cur, jnp.max(s2, axis=0, keepdims=True))
    mw[...] = m_cur
    part = oe[:128, :] / oe[128:129, :]
    if comp == 1:
        part0_ref[...] = part
    else:
        lp = lam_ref[...]
        lam = (jnp.exp(jnp.sum(lp[0:1] * lp[1:2], axis=-1, keepdims=True))
               - jnp.exp(jnp.sum(lp[2:3] * lp[3:4], axis=-1, keepdims=True)) + lam_init)
        o = part0_ref[...] - lam * part
        ms = jnp.mean(o * o, axis=0, keepdims=True)
        o = (o * lax.rsqrt(ms + RMS_EPS)).T
        o_ref[...] = (o * sub_ref[...] * (1.0 - lam_init)).astype(BF16)


def _attn_kernel(lam_init, *refs):
    n_in = 8
    io, part0_ref, scr = refs[:n_in], refs[n_in], refs[n_in + 1:]
    half = len(scr) // 2
    buf_a, buf_b = scr[:half], scr[half:]
    u = pl.program_id(0)

    @pl.when(u == 0)
    def _():
        for r in buf_b + (part0_ref,):
            r[...] = jnp.zeros(r.shape, r.dtype)

    @pl.when(u % 2 == 0)
    def _():
        _attn_step(lam_init, 0, io, buf_a, buf_b, part0_ref)

    @pl.when(u % 2 == 1)
    def _():
        _attn_step(lam_init, 1, io, buf_b, buf_a, part0_ref)


def _attn_short_kernel(lam_init, q_ref, k_ref, v_ref, lam_ref, sub_ref, o_ref):
    lp = lam_ref[...]
    lam = (jnp.exp(jnp.sum(lp[0:1] * lp[1:2], axis=-1, keepdims=True))
           - jnp.exp(jnp.sum(lp[2:3] * lp[3:4], axis=-1, keepdims=True)) + lam_init)
    n_keys = k_ref.shape[0]
    ones_rows = jnp.ones((ONES_ROWS, n_keys), BF16)
    lane = lax.broadcasted_iota(jnp.int32, (q_ref.shape[0], 128), 1)
    units = [(h, c) for h in range(HEADS) for c in range(2)]
    scores = []
    for h, c in units:
        q = q_ref[:, h * 128:(h + 1) * 128]
        qc = jnp.where((lane // HEAD_D) == c, q, jnp.zeros_like(q))
        scores.append(_dot_nt(k_ref[:, h * 128:(h + 1) * 128], qc))
    probs = [jnp.exp2((s - jnp.max(s, axis=0, keepdims=True)).astype(BF16)) for s in scores]
    oes = [_dot(jnp.concatenate([v_ref[h * 128:(h + 1) * 128, :], ones_rows], axis=0), p)
           for (h, c), p in zip(units, probs)]
    parts = [oe[:128, :] / oe[128:129, :] for oe in oes]
    outs = []
    for h in range(HEADS):
        o = parts[2 * h] - lam * parts[2 * h + 1]
        ms = jnp.mean(o * o, axis=0, keepdims=True)
        outs.append((o * lax.rsqrt(ms + RMS_EPS)).T * sub_ref[...] * (1.0 - lam_init))
    o_ref[...] = jnp.concatenate(outs, axis=1).astype(BF16)


def _diff_attention_short(layer, n_seq, seq_len, q, k, v_t, da_lambda, subln):
    lam_init = 0.8 - 0.6 * math.exp(-0.3 * layer)
    return pl.pallas_call(
        functools.partial(_attn_short_kernel, lam_init),
        grid=(n_seq,),
        in_specs=[pl.BlockSpec((seq_len, DA_W), lambda b: (b, 0)),
                  pl.BlockSpec((seq_len, DA_W), lambda b: (b, 0)),
                  pl.BlockSpec((DA_W, seq_len), lambda b: (0, b)),
                  pl.BlockSpec((None, 4, HEAD_D), lambda b: (layer, 0, 0)),
                  pl.BlockSpec((None, 1, 128), lambda b: (layer, 0, 0))],
        out_specs=pl.BlockSpec((seq_len, DA_W), lambda b: (b, 0)),
        out_shape=jax.ShapeDtypeStruct((n_seq * seq_len, DA_W), BF16),
        compiler_params=_cparams(1),
        name="diff_attention_ctx",
    )(q, k, v_t, da_lambda, subln)


def _diff_attention(layer, n_seq, seq_len, q, k, v_t, cache_k, cache_v_t, da_lambda, subln):
    tq = min(ATTN_TQ, seq_len)
    nq = seq_len // tq
    n_units = n_seq * HEADS * nq * 2
    lam_init = 0.8 - 0.6 * math.exp(-0.3 * layer)

    cur = lambda u: jnp.minimum(u, n_units - 1) // 2
    prev = lambda u: jnp.maximum(u - 1, 0) // 2
    seq_of = lambda w: w // (HEADS * nq)
    head_of = lambda w: (w // nq) % HEADS
    rows_of = lambda w: seq_of(w) * nq + w % nq

    q_spec = pl.BlockSpec((tq, 128), lambda u: (rows_of(cur(u)), head_of(cur(u))))
    k_spec = pl.BlockSpec((seq_len, 128), lambda u: (seq_of(cur(u)), head_of(cur(u))))
    v_spec = pl.BlockSpec((128, seq_len), lambda u: (head_of(prev(u)), seq_of(prev(u))))
    ck_spec = pl.BlockSpec((None, None, PAST_LEN, 128), lambda u: (seq_of(cur(u)), layer, 0, head_of(cur(u))))
    cv_spec = pl.BlockSpec((None, None, 128, PAST_LEN), lambda u: (seq_of(prev(u)), layer, head_of(prev(u)), 0))
    tail = [pl.BlockSpec((None, 4, HEAD_D), lambda u: (layer, 0, 0)),
            pl.BlockSpec((None, 1, 128), lambda u: (layer, 0, 0))]
    in_specs = [q_spec, k_spec, ck_spec, v_spec, cv_spec] + tail
    args = [q, k, cache_k, v_t, cache_v_t, da_lambda, subln]
    scratch = [pltpu.VMEM((128, tq), F32)] + [pltpu.VMEM((seq_len, tq), F32), pltpu.VMEM((PAST_LEN, tq), F32),
                                              pltpu.VMEM((1, tq), F32)] * 2
    return pl.pallas_call(
        functools.partial(_attn_kernel, lam_init),
        grid=(n_units + 1,),
        in_specs=in_specs,
        out_specs=pl.BlockSpec((tq, 128), lambda u: (rows_of(prev(u)), head_of(prev(u)))),
        out_shape=jax.ShapeDtypeStruct((n_seq * seq_len, DA_W), BF16),
        scratch_shapes=scratch,
        compiler_params=_cparams(1),
        name="diff_attention_lat",
    )(*args)


def _route(logits):
    lane = lax.broadcasted_iota(jnp.int32, logits.shape, 1)
    lane_f = lane.astype(F32)
    big = 1e9
    is_g = jnp.logical_and(lane >= N_EXPERTS, lane < N_EXPERTS + N_GROUPS)
    gl = jnp.where(is_g, logits, NEG_BIG)
    gmax = jnp.max(gl, axis=-1, keepdims=True)
    gsel = jnp.min(jnp.where(gl == gmax, lane_f, big), axis=-1, keepdims=True) - N_EXPERTS
    g_w = 1.0 / jnp.sum(jnp.exp(gl - gmax), axis=-1, keepdims=True)
    in_grp = jnp.logical_and(lane < N_EXPERTS, (lane // EPG).astype(F32) == gsel)
    el = jnp.where(in_grp, logits, NEG_BIG)
    v1 = jnp.max(el, axis=-1, keepdims=True)
    i1 = jnp.min(jnp.where(el == v1, lane_f, big), axis=-1, keepdims=True)
    el2 = jnp.where(lane_f == i1, NEG_BIG, el)
    v2 = jnp.max(el2, axis=-1, keepdims=True)
    i2 = jnp.min(jnp.where(el2 == v2, lane_f, big), axis=-1, keepdims=True)
    t = jnp.exp(v2 - v1)
    w1 = g_w / (1.0 + t)
    w2 = w1 * t
    first_lo = i1 < i2
    a = jnp.minimum(i1, i2) - EPG * gsel
    b = jnp.maximum(i1, i2) - EPG * gsel
    bucket = gsel * N_PAIRS + a * (7.0 - a) * 0.5 + (b - a - 1.0)
    w_lo = jnp.where(first_lo, w1, w2)
    w_hi = jnp.where(first_lo, w2, w1)
    return jnp.where(lane == 0, bucket, jnp.where(lane == 1, w_lo, jnp.where(lane == 2, w_hi, 0.0)))


def _outproj_body(mod_ref, n2_ref, streams, hgn_ref, bd_ref, wout_ref, wrh_ref, wrl_ref, br_ref,
                  x1_ref, hp_ref, route_ref):
    x_ref, rof_ref, rob_ref, rsg_ref, da_ref, hof_ref, hob_ref, hsg_ref = streams
    ro = rof_ref[...] + rob_ref[...]
    ro = ro * lax.rsqrt(_group_mean_sq(ro, bd_ref, HEAD_D) + RMS_EPS) * rsg_ref[...].astype(F32)
    ho = hof_ref[...] + hob_ref[...]
    ho = ho * lax.rsqrt(_group_mean_sq(ho, bd_ref, HEAD_D) + RMS_EPS) * hgn_ref[...] * hsg_ref[...].astype(F32)
    mix = jnp.concatenate([ro.astype(BF16), da_ref[...], ho.astype(BF16)], axis=1)
    mixed = _dot(mix, wout_ref[...])
    gate1 = mod_ref[2:3, :]
    shift2 = mod_ref[3:4, :]
    scale2 = mod_ref[4:5, :]
    x1 = x_ref[...] + gate1 * mixed
    ms = jnp.mean(x1 * x1, axis=-1, keepdims=True)
    h2 = x1 * lax.rsqrt(ms + RMS_EPS) * n2_ref[...] * (1.0 + scale2) + shift2
    h_hi = h2.astype(BF16)
    h_lo = (h2 - h_hi.astype(F32)).astype(BF16)
    logits = _dot(h_hi, wrh_ref[...]) + _dot(h_lo, wrh_ref[...]) + _dot(h_hi, wrl_ref[...]) + br_ref[...]
    route = _route(logits)
    x1_ref[...] = x1
    route_ref[...] = route
    hp_ref[...] = jnp.concatenate([h2, route], axis=1)


def _outproj_kernel(mod_ref, n2_ref, *rest):
    ctx_streams, lat_streams, tail = rest[0:8], rest[8:16], rest[16:]
    t = pl.program_id(0)

    @pl.when(t < CTX_PTILES)
    def _():
        _outproj_body(mod_ref, n2_ref, ctx_streams, *tail)

    @pl.when(t >= CTX_PTILES)
    def _():
        _outproj_body(mod_ref, n2_ref, lat_streams, *tail)


def _cond_of_tile(t):
    return jnp.where(t < CTX_TILES, 0, 1 + (t - CTX_TILES) // LAT_TILES_PER_SEQ)


def _out_projection(layer, mod, norm2, ctx_streams, lat_streams, hgn_t, bd256, w_out_bf, w_route_hi, w_route_lo,
                    b_route):
    row = lambda t: (t, 0)
    ctx_row = lambda t: (jnp.minimum(t, CTX_PTILES - 1), 0)
    lat_row = lambda t: (jnp.maximum(t - CTX_PTILES, 0), 0)
    lay = lambda t: (layer, 0, 0)
    cond = lambda t: jnp.where(t < CTX_PTILES, 0, 1 + (t - CTX_PTILES) // LAT_PTILES_PER_SEQ)
    widths = (D_MODEL, MIX_G, MIX_G, MIX_G, DA_W, MIX_G, MIX_G, MIX_G)
    in_specs = [
        pl.BlockSpec((None, None, 6, D_MODEL), lambda t: (layer, cond(t), 0, 0)),
        pl.BlockSpec((None, 1, D_MODEL), lay),
    ]
    in_specs += [pl.BlockSpec((PTILE, w), ctx_row) for w in widths]
    in_specs += [pl.BlockSpec((PTILE, w), lat_row) for w in widths]
    in_specs += [
        pl.BlockSpec((None, 1, MIX_G), lay),
        pl.BlockSpec((MIX_G, MIX_G), lambda t: (0, 0)),
        pl.BlockSpec((None, D_MODEL, D_MODEL), lay),
        pl.BlockSpec((None, D_MODEL, 128), lay),
        pl.BlockSpec((None, D_MODEL, 128), lay),
        pl.BlockSpec((None, 1, 128), lay),
    ]
    return pl.pallas_call(
        _outproj_kernel,
        grid=(N_ROWS // PTILE,),
        in_specs=in_specs,
        out_specs=[pl.BlockSpec((PTILE, D_MODEL), row), pl.BlockSpec((PTILE, ROW_W), row),
                   pl.BlockSpec((PTILE, 128), row)],
        out_shape=[jax.ShapeDtypeStruct((N_ROWS, D_MODEL), F32), jax.ShapeDtypeStruct((N_ROWS, ROW_W), F32),
                   jax.ShapeDtypeStruct((N_ROWS, 128), F32)],
        compiler_params=_cparams(1),
        name="out_projection",
    )(mod, norm2, *ctx_streams, *lat_streams, hgn_t, bd256, w_out_bf, w_route_hi, w_route_lo, b_route)


def _dispatch_kernel(route_ref, dest_ref, meta_ref, cnt_ref):
    ph = pl.program_id(0)
    t = pl.program_id(1)
    lane = lax.broadcasted_iota(jnp.int32, (TILE, 128), 1).astype(F32)
    onehots = [(lane == route_ref[j * TILE:(j + 1) * TILE, 0:1]).astype(F32) for j in range(DISPATCH_TILES)]
    tile_cnts = [jnp.sum(oh, axis=0, keepdims=True) for oh in onehots]

    @pl.when(jnp.logical_and(ph == 0, t == 0))
    def _():
        cnt_ref[...] = jnp.zeros_like(cnt_ref)

    @pl.when(ph == 0)
    def _():
        cnt_ref[0:1, :] += sum(tile_cnts[1:], tile_cnts[0])

    @pl.when(jnp.logical_and(ph == 1, t == 0))
    def _():
        cnt = cnt_ref[0:1, :]
        padded = jnp.floor((cnt + (MOE_TILE - 1.0)) * (1.0 / MOE_TILE)) * MOE_TILE
        r = lax.broadcasted_iota(jnp.int32, (128, 128), 0)
        c = lax.broadcasted_iota(jnp.int32, (128, 128), 1)
        off = jnp.dot(jnp.broadcast_to(padded, (8, 128)), (r < c).astype(F32),
                      preferred_element_type=F32, precision=HI)[0:1, :]
        cnt_ref[1:2, :] = off
        end = off + padded
        end_col = jnp.sum(jnp.where(r == c, jnp.broadcast_to(end, (128, 128)), 0.0), axis=1, keepdims=True)
        ended = jnp.logical_and(end_col <= c.astype(F32) * MOE_TILE, r < N_BUCKETS)
        tile_bucket = jnp.sum(ended.astype(F32), axis=0, keepdims=True)
        lane1 = lax.broadcasted_iota(jnp.int32, (1, 128), 1)
        tile_idx = lane1.astype(F32)
        n_valid = jnp.max(jnp.where(lane1 < N_BUCKETS, end, 0.0), axis=1, keepdims=True) * (1.0 / MOE_TILE)
        last = jnp.max(jnp.where(tile_idx < n_valid, tile_bucket, 0.0), axis=1, keepdims=True)
        tb = jnp.where(tile_idx < n_valid, tile_bucket, last)
        g = jnp.floor(tb * (1.0 / N_PAIRS))
        pid = tb - g * N_PAIRS
        a = (pid >= 3.0).astype(F32) + (pid >= 5.0).astype(F32)
        b = pid + 1.0 - 2.0 * (a >= 1.0).astype(F32) - (a >= 2.0).astype(F32)
        rows = [EPG * g + a, EPG * g + b, jnp.broadcast_to(n_valid, (1, 128)), jnp.zeros((5, 128), F32)]
        meta_ref[...] = jnp.concatenate(rows, axis=0).astype(jnp.int32)

    @pl.when(ph == 1)
    def _():
        base = cnt_ref[1:2, :]
        rr = lax.broadcasted_iota(jnp.int32, (TILE, TILE), 0)
        cc = lax.broadcasted_iota(jnp.int32, (TILE, TILE), 1)
        for j, onehot in enumerate(onehots):
            oh16 = onehot.astype(BF16)
            same = _dot_nt(oh16, oh16)
            rank = jnp.sum(jnp.where(rr < cc, same, 0.0), axis=0, keepdims=True)
            base_row = lax.dot_general(jnp.broadcast_to(base, (8, 128)), onehot, (((1,), (1,)), ((), ())),
                                       preferred_element_type=F32, precision=HI)[0:1, :]
            dest_ref[j] = (base_row + rank).astype(jnp.int32)
            base = base + tile_cnts[j]
        cnt_ref[1:2, :] = base


def _moe_dispatch(route):
    dest, meta = pl.pallas_call(
        _dispatch_kernel,
        grid=(2, N_TILES // DISPATCH_TILES),
        in_specs=[pl.BlockSpec((DISPATCH_TILES * TILE, 128), lambda ph, t: (t, 0))],
        out_specs=[pl.BlockSpec((DISPATCH_TILES, 1, TILE), lambda ph, t: (t * ph, 0, 0)),
                   pl.BlockSpec((8, 128), lambda ph, t: (0, 0))],
        out_shape=[jax.ShapeDtypeStruct((N_TILES, 1, TILE), jnp.int32),
                   jax.ShapeDtypeStruct((8, 128), jnp.int32)],
        scratch_shapes=[pltpu.VMEM((8, 128), F32)],
        compiler_params=_cparams(2),
        name="moe_dispatch",
    )(route)
    return dest.reshape(N_ROWS), meta[0:3].reshape(3 * 128)


def _dma_params():
    return pltpu.CompilerParams(dimension_semantics=("arbitrary",), vmem_limit_bytes=VMEM_LIMIT,
                                disable_bounds_checks=True)


def _moe_scatter_kernel(dest_ref, h_ref, hs_init_ref, hs_ref, stage, sems):
    del hs_init_ref
    t = pl.program_id(0)
    base = t * TILE
    slot = t % 2
    stage[slot] = h_ref[...]

    def issue(i, carry):
        d = dest_ref[base + i]
        pltpu.make_async_copy(stage.at[slot, pl.ds(i, 1)], hs_ref.at[pl.ds(d, 1)], sems.at[slot]).start()
        return carry

    lax.fori_loop(0, TILE, issue, 0, unroll=8)

    def wait_tile(s):
        pltpu.make_async_copy(stage.at[s], hs_ref.at[pl.ds(0, TILE)], sems.at[s]).wait()

    @pl.when(t > 0)
    def _():
        wait_tile(1 - slot)

    @pl.when(t == pl.num_programs(0) - 1)
    def _():
        wait_tile(slot)


def _moe_scatter(dest, hp, hs_init):
    return pl.pallas_call(
        _moe_scatter_kernel,
        grid_spec=pltpu.PrefetchScalarGridSpec(
            num_scalar_prefetch=1,
            grid=(N_TILES,),
            in_specs=[pl.BlockSpec((TILE, ROW_W), lambda t, dest: (t, 0)),
                      pl.BlockSpec(memory_space=pl.ANY)],
            out_specs=pl.BlockSpec(memory_space=pl.ANY),
            scratch_shapes=[pltpu.VMEM((2, TILE, ROW_W), F32), pltpu.SemaphoreType.DMA((2,))],
        ),
        out_shape=jax.ShapeDtypeStruct((N_SLOTS, ROW_W), F32),
        input_output_aliases={2: 0},
        compiler_params=_dma_params(),
        name="moe_scatter",
    )(dest, hp, hs_init)


def _moe_ffn_kernel(meta_ref, hs_ref, wgl_ref, wul_ref, wdl_ref, wgh_ref, wuh_ref, wdh_ref, ys_ref):
    t = pl.program_id(0)
    n_valid = meta_ref[2 * 128]

    @pl.when(t < n_valid)
    def _():
        w = hs_ref[...]
        x = w[:, :D_MODEL].astype(BF16)
        rw = w[:, D_MODEL:]

        def gated(wg_ref, wu_ref, weight):
            return (_silu(_dot(x, wg_ref[...])) * _dot(x, wu_ref[...]) * weight).astype(BF16)

        a_lo = gated(wgl_ref, wul_ref, rw[:, 1:2])
        a_hi = gated(wgh_ref, wuh_ref, rw[:, 2:3])
        ys_ref[...] = _dot(a_lo, wdl_ref[...]) + _dot(a_hi, wdh_ref[...])

    @pl.when(t >= n_valid)
    def _():
        ys_ref[...] = jnp.zeros_like(ys_ref)


def _moe_ffn(layer, meta, hs, w_gate_bf, w_up_bf, w_down_bf):
    lo = lambda t, m: (layer, m[t], 0, 0)
    hi = lambda t, m: (layer, m[128 + t], 0, 0)
    up = lambda im: pl.BlockSpec((None, None, D_MODEL, EXPERT_FF), im)
    down = lambda im: pl.BlockSpec((None, None, EXPERT_FF, D_MODEL), im)
    return pl.pallas_call(
        _moe_ffn_kernel,
        grid_spec=pltpu.PrefetchScalarGridSpec(
            num_scalar_prefetch=1,
            grid=(MOE_MAX_TILES,),
            in_specs=[pl.BlockSpec((MOE_TILE, ROW_W), lambda t, m: (jnp.maximum(jnp.minimum(t, m[2 * 128] - 1), 0), 0)),
                      up(lo), up(lo), down(lo), up(hi), up(hi), down(hi)],
            out_specs=pl.BlockSpec((MOE_TILE, D_MODEL), lambda t, m: (t, 0)),
        ),
        out_shape=jax.ShapeDtypeStruct((N_SLOTS, D_MODEL), F32),
        compiler_params=_cparams(1),
        name="moe_ffn",
    )(meta, hs, w_gate_bf, w_up_bf, w_down_bf, w_gate_bf, w_up_bf, w_down_bf)


def _moe_combine_kernel(dest_ref, x1_ref, mod_ref, ys_ref, out_ctx_ref, out_lat_ref, buf, sems):
    t = pl.program_id(0)
    nt = pl.num_programs(0)

    def issue(tile, slot):
        def body(i, carry):
            d = dest_ref[tile * TILE + i]
            pltpu.make_async_copy(ys_ref.at[pl.ds(d, 1)], buf.at[slot, pl.ds(i, 1)], sems.at[slot]).start()
            return carry
        lax.fori_loop(0, TILE, body, 0, unroll=8)

    @pl.when(t == 0)
    def _():
        issue(0, 0)

    @pl.when(t + 1 < nt)
    def _():
        issue(t + 1, (t + 1) % 2)

    slot = t % 2
    pltpu.make_async_copy(ys_ref.at[pl.ds(0, TILE)], buf.at[slot], sems.at[slot]).wait()
    x2 = x1_ref[...] + mod_ref[5:6, :] * buf[slot]

    @pl.when(t < CTX_TILES)
    def _():
        out_ctx_ref[...] = x2

    @pl.when(t >= CTX_TILES)
    def _():
        out_lat_ref[...] = x2


def _moe_combine(layer, dest, x1, mod, ys):
    return pl.pallas_call(
        _moe_combine_kernel,
        grid_spec=pltpu.PrefetchScalarGridSpec(
            num_scalar_prefetch=1,
            grid=(N_TILES,),
            in_specs=[pl.BlockSpec((TILE, D_MODEL), lambda t, dest: (t, 0)),
                      pl.BlockSpec((None, None, 6, D_MODEL), lambda t, dest: (layer, _cond_of_tile(t), 0, 0)),
                      pl.BlockSpec(memory_space=pl.ANY)],
            out_specs=[pl.BlockSpec((TILE, D_MODEL), lambda t, dest: (jnp.minimum(t, CTX_TILES - 1), 0)),
                       pl.BlockSpec((TILE, D_MODEL), lambda t, dest: (jnp.maximum(t - CTX_TILES, 0), 0))],
            scratch_shapes=[pltpu.VMEM((2, TILE, D_MODEL), F32), pltpu.SemaphoreType.DMA((2,))],
        ),
        out_shape=[jax.ShapeDtypeStruct((N_CTX_ROWS, D_MODEL), F32), jax.ShapeDtypeStruct((N_LAT_ROWS, D_MODEL), F32)],
        compiler_params=_dma_params(),
        name="moe_combine",
    )(dest, x1, mod, ys)


def _block_diag_ones(n, group):
    i = np.arange(n) // group
    return jnp.asarray((i[:, None] == i[None, :]).astype(np.float32), dtype=BF16)


def _rope_tables():
    pos = np.arange(DEC_SEQ)
    rows = (pos // GRID_W).astype(np.float32)
    cols = (pos % GRID_W).astype(np.float32)
    inv_freq = (ROPE_BASE ** (-(np.arange(ROPE_PAIRS, dtype=np.float32) / ROPE_PAIRS))).astype(np.float32)
    ang_r = rows[:, None] * inv_freq[None, :]
    ang_c = cols[:, None] * inv_freq[None, :]
    cos64 = np.concatenate([np.cos(ang_r), np.cos(ang_r), np.cos(ang_c), np.cos(ang_c)], axis=1)
    sin64 = np.concatenate([-np.sin(ang_r), np.sin(ang_r), -np.sin(ang_c), np.sin(ang_c)], axis=1)
    return (jnp.asarray(np.tile(cos64, (1, 2)), F32), jnp.asarray(np.tile(sin64, (1, 2)), F32))


def _mixers(latent, layer, x, consts, params, ret_s0, hg_s0, cache_k, cache_v):
    (mod, lb_all, bd512, bd256, rope_cos, rope_sin, ret_tables) = consts
    n_seq = DEC_BATCH if latent else BATCH
    seq_len = DEC_SEQ if latent else SEQ
    nt = seq_len // TILE
    proj = _in_projection(latent, layer, x, mod, params['norm1'], params['w_in'], params['w_vt'], params['da_qn'],
                          params['da_kn'], lb_all, bd512, rope_cos, rope_sin)
    (rq, rk, rv, rsg, dq, dk, dv, hq, hkf, hkb, hv, hgf, hgb, hsg) = proj[:14]
    ret_of, ret_ob, ret_fin = _retention(n_seq, nt, rq, rk, rv, ret_s0, ret_tables,
                                         "retention_lat" if latent else "retention_ctx")
    hg_of, hg_ob, hg_fin = _gated_scan(n_seq, nt, hq, hkf, hkb, hv, hgf, hgb, hg_s0,
                                       "hgrn_scan_lat" if latent else "hgrn_scan_ctx")
    if latent:
        da_o = _diff_attention(layer, n_seq, seq_len, dq, dk, dv, cache_k, cache_v,
                               params['da_lambda'], params['da_subln'])
    else:
        da_o = _diff_attention_short(layer, n_seq, seq_len, dq, dk, dv, params['da_lambda'], params['da_subln'])
    streams = (x, ret_of, ret_ob, rsg, da_o, hg_of, hg_ob, hsg)
    extras = None if latent else (proj[14], proj[15], ret_fin, hg_fin)
    return streams, extras


def _trunk_layer(layer, x_ctx, x_lat, sorted_buf, consts, params, lat_ret_s0, lat_hg_s0, zero_state, cache_k, cache_v):
    mod, bd256 = consts[0], consts[3]
    ctx_streams, extras = _mixers(False, layer, x_ctx, consts, params, zero_state, zero_state, None, None)
    lat_streams, _ = _mixers(True, layer, x_lat, consts, params, lat_ret_s0, lat_hg_s0, cache_k, cache_v)
    x1, hp, route = _out_projection(layer, mod, params['norm2'], ctx_streams, lat_streams, params['hg_norm'],
                                    bd256, params['w_out'], params['w_route_hi'], params['w_route_lo'],
                                    params['b_route'])
    dest, meta = _moe_dispatch(route)
    sorted_buf = _moe_scatter(dest, hp, sorted_buf)
    ys = _moe_ffn(layer, meta, sorted_buf, params['w_gate'], params['w_up'], params['w_down'])
    x_ctx, x_lat = _moe_combine(layer, dest, x1, mod, ys)
    return x_ctx, x_lat, sorted_buf, extras


def kernel(x_prompt, x_sample, cache_k, cache_v, state_ret, state_hgrn, c, c_ctx, norm1, norm2, w_mod, b_mod,
           w_in, w_out, da_qn, da_kn, da_lambda, da_subln, hg_lb, hg_norm, w_group, b_group, w_router,
           b_router, w_gate, w_up, w_down):
    cond = jnp.zeros((N_COND, D_MODEL), F32).at[0].set(c_ctx).at[1:1 + DEC_BATCH].set(c)
    mod, lb_all = _modulation(cond, w_mod, b_mod, hg_lb)
    mod = mod.reshape(DEPTH, N_COND, 6, D_MODEL)
    rope_cos, rope_sin = _rope_tables()
    consts = (mod, lb_all.reshape(DEPTH, 1, MIX_G), _block_diag_ones(DA_W, HEAD_D),
              _block_diag_ones(MIX_G, HEAD_D), rope_cos, rope_sin, _retention_tables())
    pad = jnp.zeros((DEPTH, D_MODEL, 128 - N_EXPERTS - N_GROUPS), F32)
    w_route = jnp.concatenate([w_router, w_group, pad], axis=-1)
    params = {
        'norm1': norm1.reshape(DEPTH, 1, D_MODEL), 'norm2': norm2.reshape(DEPTH, 1, D_MODEL),
        'w_in': w_in.astype(BF16), 'w_out': w_out.astype(BF16),
        'w_vt': jnp.swapaxes(w_in[:, :, 2048:2560].astype(BF16), 1, 2),
        'da_qn': jnp.tile(da_qn, (1, DA_W // HEAD_D)).reshape(DEPTH, 1, DA_W),
        'da_kn': jnp.tile(da_kn, (1, DA_W // HEAD_D)).reshape(DEPTH, 1, DA_W),
        'da_lambda': da_lambda, 'da_subln': da_subln.reshape(DEPTH, 1, 128),
        'hg_norm': jnp.tile(hg_norm, (1, HEADS)).reshape(DEPTH, 1, MIX_G),
        'w_route_hi': w_route.astype(BF16),
        'w_route_lo': (w_route - w_route.astype(BF16).astype(F32)).astype(BF16),
        'b_route': jnp.concatenate([b_router, b_group, pad[:, 0, :]], axis=-1).reshape(DEPTH, 1, 128),
        'w_gate': w_gate.astype(BF16), 'w_up': w_up.astype(BF16), 'w_down': w_down.astype(BF16),
    }
    ck_bf = cache_k.astype(BF16).reshape(DEC_BATCH, DEPTH, PAST_LEN, DA_W)
    cv_bf = jnp.swapaxes(cache_v.astype(BF16).reshape(DEC_BATCH, DEPTH, PAST_LEN, DA_W), 2, 3)
    zero_state = jnp.zeros((BATCH, 2, HEADS, HEAD_D, HEAD_D), F32)

    yp = x_prompt.reshape(N_CTX_ROWS, D_MODEL)
    ys = x_sample.reshape(N_LAT_ROWS, D_MODEL)
    sorted_buf = jnp.zeros((N_SLOTS, ROW_W), F32)
    ks_out, vs_out, rets_out, hgs_out = [], [], [], []
    for l in range(DEPTH):
        yp, ys, sorted_buf, (k_new, v_new, ret_fin, hg_fin) = _trunk_layer(
            l, yp, ys, sorted_buf, consts, params, state_ret[:, l], state_hgrn[:, l], zero_state, ck_bf, cv_bf)
        ks_out.append(k_new.reshape(BATCH, SEQ, HEADS, 2, HEAD_D))
        vs_out.append(v_new.reshape(BATCH, SEQ, HEADS, 2 * HEAD_D))
        rets_out.append(ret_fin)
        hgs_out.append(hg_fin)
    return (yp.reshape(BATCH, SEQ, D_MODEL), ys.reshape(DEC_BATCH, DEC_SEQ, D_MODEL),
            jnp.stack(ks_out, axis=1), jnp.stack(vs_out, axis=1),
            jnp.stack(rets_out, axis=1), jnp.stack(hgs_out, axis=1))
```

```python
import functools
import math

import numpy as np
import jax
import jax.numpy as jnp
from jax import lax
from jax.experimental import pallas as pl
from jax.experimental.pallas import tpu as pltpu

F32 = jnp.float32
BF16 = jnp.bfloat16

D_MODEL = 1024
BATCH = 16
SEQ = 256
DEPTH = 4
DEC_BATCH = 4
DEC_SEQ = 4096
PAST_LEN = 256
GRID_W = 64
HEADS = 4
HEAD_D = 64
MIX_G = HEADS * HEAD_D
DA_W = 512
IN_COLS = 3840
ROPE_BASE = 10000.0
ROPE_PAIRS = 16
RMS_EPS = 1e-6
N_GROUPS = 4
EPG = 4
N_EXPERTS = 16
EXPERT_FF = 512

TILE = 256
N_CTX_ROWS = BATCH * SEQ
N_LAT_ROWS = DEC_BATCH * DEC_SEQ
N_ROWS = N_CTX_ROWS + N_LAT_ROWS
CTX_TILES = N_CTX_ROWS // TILE
LAT_TILES_PER_SEQ = DEC_SEQ // TILE
N_TILES = N_ROWS // TILE
PTILE = 512
CTX_PTILES = N_CTX_ROWS // PTILE
LAT_PTILES_PER_SEQ = DEC_SEQ // PTILE
N_COND = 8

N_PAIRS = EPG * (EPG - 1) // 2
N_BUCKETS = N_GROUPS * N_PAIRS
MOE_TILE = 256
MOE_MAX_TILES = N_ROWS // MOE_TILE + N_BUCKETS
DISPATCH_TILES = 4
DMA_GROUP = 8
N_SLOTS = MOE_MAX_TILES * MOE_TILE
ROW_W = D_MODEL + 128

CHUNK = 64
SUB = 16
NSUB = CHUNK // SUB
NCHUNK = TILE // CHUNK

VMEM_LIMIT = 48 * 1024 * 1024
NEG_BIG = -1e30
ONES_ROWS = 16
ATTN_TQ = 512
ATTN_KEY_CHUNK = 512
LOG2_E = 1.4426950408889634

HI = lax.Precision.HIGHEST


def _cparams(n_axes):
    return pltpu.CompilerParams(dimension_semantics=("arbitrary",) * n_axes,
                                vmem_limit_bytes=VMEM_LIMIT)


def _dot(a, b):
    return jnp.dot(a, b, preferred_element_type=F32)


def _dot_nt(a, b):
    return lax.dot_general(a, b, (((1,), (1,)), ((), ())), preferred_element_type=F32)


def _dot_tn(a, b):
    return lax.dot_general(a, b, (((0,), (0,)), ((), ())), preferred_element_type=F32)


def _silu(x):
    return x * (1.0 / (1.0 + jnp.exp(-x)))


def _sigmoid(x):
    return 1.0 / (1.0 + jnp.exp(-x))


def _mod_kernel(cond_ref, w_ref, b_ref, lb_ref, mod_ref, lbo_ref):
    c = cond_ref[...]
    m = jnp.dot(_silu(c), w_ref[...], preferred_element_type=F32, precision=HI)
    mod_ref[...] = m + b_ref[...]
    z = lb_ref[...]
    zmax = jnp.max(z, axis=0, keepdims=True)
    e = jnp.exp(z - zmax)
    p = e / jnp.sum(e, axis=0, keepdims=True)
    rows = [jnp.zeros_like(p[0:1])]
    for l in range(1, DEPTH):
        rows.append(rows[-1] + p[l:l + 1])
    lbo_ref[...] = jnp.concatenate(rows, axis=0)


def _modulation(cond, w_mod, b_mod, hg_lb):
    nblk = 6
    return pl.pallas_call(
        _mod_kernel,
        grid=(DEPTH, nblk),
        in_specs=[
            pl.BlockSpec((N_COND, D_MODEL), lambda l, j: (0, 0)),
            pl.BlockSpec((None, D_MODEL, D_MODEL), lambda l, j: (l, 0, j)),
            pl.BlockSpec((None, 1, D_MODEL), lambda l, j: (l, 0, j)),
            pl.BlockSpec((DEPTH, MIX_G), lambda l, j: (0, 0)),
        ],
        out_specs=[
            pl.BlockSpec((None, N_COND, D_MODEL), lambda l, j: (l, 0, j)),
            pl.BlockSpec((DEPTH, MIX_G), lambda l, j: (0, 0)),
        ],
        out_shape=[jax.ShapeDtypeStruct((DEPTH, N_COND, 6 * D_MODEL), F32),
                   jax.ShapeDtypeStruct((DEPTH, MIX_G), F32)],
        compiler_params=_cparams(2),
        name="modulation",
    )(cond, w_mod, b_mod.reshape(DEPTH, 1, 6 * D_MODEL), hg_lb)


def _group_mean_sq(x, bd_ref, group):
    return _dot((x * x).astype(BF16), bd_ref[...]) * (1.0 / group)


def _swap16(x):
    w = x.shape[-1]
    lane = lax.broadcasted_iota(jnp.int32, x.shape, x.ndim - 1)
    up = pltpu.roll(x, w - 16, x.ndim - 1)
    dn = pltpu.roll(x, 16, x.ndim - 1)
    return jnp.where((lane % 32) < 16, up, dn)


def _inproj_kernel(latent, x_ref, mod_ref, n1_ref, w_ref, wvt_ref, qn_ref, kn_ref, lb_ref, bd_ref, *rest):
    if latent:
        cos_ref, sin_ref = rest[0], rest[1]
        outs = rest[2:]
    else:
        outs = rest
    (rq_ref, rk_ref, rv_ref, rsg_ref, dq_ref, dk_ref, dv_ref,
     hq_ref, hkf_ref, hkb_ref, hv_ref, hgf_ref, hgb_ref, hsg_ref) = outs[:14]

    x = x_ref[...]
    ms = jnp.mean(x * x, axis=-1, keepdims=True)
    shift1 = mod_ref[0:1, :]
    scale1 = mod_ref[1:2, :]
    h = x * lax.rsqrt(ms + RMS_EPS) * n1_ref[...] * (1.0 + scale1) + shift1
    h16 = h.astype(BF16)
    y = _dot(h16, w_ref[...])

    rq_ref[...] = y[:, 0:256].astype(BF16)
    rk_ref[...] = (y[:, 256:512] * (HEAD_D ** -0.5)).astype(BF16)
    rv_ref[...] = y[:, 512:768].astype(BF16)
    rsg_ref[...] = _silu(y[:, 768:1024]).astype(BF16)

    dq = y[:, 1024:1536]
    dk = y[:, 1536:2048]
    qn = dq * lax.rsqrt(_group_mean_sq(dq, bd_ref, HEAD_D) + RMS_EPS) * qn_ref[...]
    kn = dk * lax.rsqrt(_group_mean_sq(dk, bd_ref, HEAD_D) + RMS_EPS) * kn_ref[...]
    if latent:
        cos = jnp.concatenate([cos_ref[...]] * 4, axis=1)
        sin = jnp.concatenate([sin_ref[...]] * 4, axis=1)
        qr = qn * cos + _swap16(qn) * sin
        kr = kn * cos + _swap16(kn) * sin
    else:
        qr, kr = qn, kn
        ck_ref, cv_ref = outs[14], outs[15]
        ck_ref[...] = kn
        cv_ref[...] = y[:, 2048:2560]
    dq_ref[...] = (qr * (HEAD_D ** -0.5 * LOG2_E)).astype(BF16)
    dk_ref[...] = kr.astype(BF16)
    dv_ref[...] = _dot_nt(wvt_ref[...], h16).astype(BF16)

    lb = lb_ref[...]
    f_f = lb + (1.0 - lb) * _sigmoid(y[:, 2816:3072])
    f_b = lb + (1.0 - lb) * _sigmoid(y[:, 3072:3328])
    hq_ref[...] = y[:, 2560:2816].astype(BF16)
    hkf_ref[...] = (1.0 - f_f).astype(BF16)
    hkb_ref[...] = (1.0 - f_b).astype(BF16)
    hgf_ref[...] = jnp.log2(f_f)
    hgb_ref[...] = jnp.log2(f_b)
    hv_ref[...] = y[:, 3328:3584].astype(BF16)
    hsg_ref[...] = _silu(y[:, 3584:3840]).astype(BF16)


def _in_projection(latent, layer, x, mod, norm1, w_in_bf, w_vt_bf, qn_t, kn_t, lb_all, bd512, rope_cos, rope_sin):
    nrows = x.shape[0]
    ntiles = nrows // PTILE
    cond = (lambda t: 1 + t // LAT_PTILES_PER_SEQ) if latent else (lambda t: 0)

    in_specs = [
        pl.BlockSpec((PTILE, D_MODEL), lambda t: (t, 0)),
        pl.BlockSpec((None, None, 6, D_MODEL), lambda t: (layer, cond(t), 0, 0)),
        pl.BlockSpec((None, 1, D_MODEL), lambda t: (layer, 0, 0)),
        pl.BlockSpec((None, D_MODEL, IN_COLS), lambda t: (layer, 0, 0)),
        pl.BlockSpec((None, DA_W, D_MODEL), lambda t: (layer, 0, 0)),
        pl.BlockSpec((None, 1, DA_W), lambda t: (layer, 0, 0)),
        pl.BlockSpec((None, 1, DA_W), lambda t: (layer, 0, 0)),
        pl.BlockSpec((None, 1, MIX_G), lambda t: (layer, 0, 0)),
        pl.BlockSpec((DA_W, DA_W), lambda t: (0, 0)),
    ]
    args = [x, mod, norm1, w_in_bf, w_vt_bf, qn_t, kn_t, lb_all, bd512]
    if latent:
        in_specs += [pl.BlockSpec((PTILE, 128), lambda t: (t % LAT_PTILES_PER_SEQ, 0))] * 2
        args += [rope_cos, rope_sin]

    def o(width, dtype):
        return jax.ShapeDtypeStruct((nrows, width), dtype), pl.BlockSpec((PTILE, width), lambda t: (t, 0))

    dv_t = (jax.ShapeDtypeStruct((DA_W, nrows), BF16), pl.BlockSpec((DA_W, PTILE), lambda t: (0, t)))
    outs = [o(256, BF16), o(256, BF16), o(256, BF16), o(256, BF16),
            o(512, BF16), o(512, BF16), dv_t,
            o(256, BF16), o(256, BF16), o(256, BF16), o(256, BF16), o(256, F32), o(256, F32), o(256, BF16)]
    if not latent:
        outs += [o(512, F32), o(512, F32)]
    return pl.pallas_call(
        functools.partial(_inproj_kernel, latent),
        grid=(ntiles,),
        in_specs=in_specs,
        out_specs=[s for _, s in outs],
        out_shape=[s for s, _ in outs],
        compiler_params=_cparams(1),
        name="in_projection_lat" if latent else "in_projection_ctx",
    )(*args)


def _load_states(s0_ref, st_scr, transposed):
    st_scr[...] = jnp.zeros_like(st_scr)
    for d in range(2):
        for h in range(HEADS):
            blk = s0_ref[d, h]
            st_scr[d, h * HEAD_D:(h + 1) * HEAD_D, h * HEAD_D:(h + 1) * HEAD_D] = blk.T if transposed else blk


def _store_states(st_scr, sfin_ref, transposed):
    for d in range(2):
        for h in range(HEADS):
            blk = st_scr[d, h * HEAD_D:(h + 1) * HEAD_D, h * HEAD_D:(h + 1) * HEAD_D]
            sfin_ref[d, h] = blk.T if transposed else blk


def _head_masks():
    lane = lax.broadcasted_iota(jnp.int32, (1, MIX_G), 1)
    return [(lane // HEAD_D == h).astype(F32) for h in range(HEADS)]


def _scan_chunks(items, st_f, st_b, consts_f, consts_b):
    n = len(items)
    cs = [consts_b if it[4] else consts_f for it in items]
    zero_row = jnp.zeros((1, MIX_G), F32)

    bs = []
    for x in range(n):
        g = items[x][3]
        g_hi = g.astype(BF16)
        g_lo = (g - g_hi.astype(F32)).astype(BF16)
        bs.append(_dot(cs[x][0], g_hi) + _dot(cs[x][0], g_lo))

    prep = []
    for x in range(n):
        q, k, v, _, rev = items[x]
        hmask = cs[x][4]
        b = bs[x]
        r_rows, e_rows = [], []
        for s in range(NSUB):
            lo, hi = s * SUB, s * SUB + SUB - 1
            if not rev:
                r_rows.append(b[lo - 1:lo, :] if s > 0 else zero_row)
                e_rows.append(b[hi:hi + 1, :])
            else:
                r_rows.append(b[hi + 1:hi + 2, :] if s < NSUB - 1 else zero_row)
                e_rows.append(b[lo:lo + 1, :])
        btot = e_rows[NSUB - 1] if not rev else e_rows[0]
        r_full = jnp.concatenate([jnp.broadcast_to(r, (SUB, MIX_G)) for r in r_rows], axis=0)
        e_full = jnp.concatenate([jnp.broadcast_to(e, (SUB, MIX_G)) for e in e_rows], axis=0)
        bl = b - r_full
        qp = q * jnp.exp2(bl)
        kend = k * jnp.exp2(e_full - b)
        pairs = [(i, j) for i in range(NSUB) for j in range(NSUB) if (j < i if not rev else j > i)]
        lh = jnp.concatenate([qp[i * SUB:(i + 1) * SUB, :] * jnp.exp2(r_rows[i] - e_rows[j])
                              for (i, j) in pairs], axis=0)
        prep.append(dict(
            bl=bl, btot=btot, pairs=pairs,
            q_inter=(qp * jnp.exp2(r_full)).astype(BF16),
            k_state=(kend * jnp.exp2(btot - e_full)).astype(BF16),
            kend=kend.astype(BF16),
            lhs4=jnp.concatenate([lh * hmask[h] for h in range(HEADS)], axis=0).astype(BF16)))

    uts = [_dot_tn(items[x][2], prep[x]['k_state']) for x in range(n)]
    scs = [_dot_nt(prep[x]['lhs4'], prep[x]['kend']) for x in range(n)]

    st_in = []
    for x in range(n):
        rev = items[x][4]
        st = st_b if rev else st_f
        st_in.append(st.astype(BF16))
        st = st * jnp.exp2(prep[x]['btot']) + uts[x] * cs[x][1]
        if rev:
            st_b = st
        else:
            st_f = st
    col = lax.broadcasted_iota(jnp.int32, (SUB, CHUNK), 1) // SUB
    jrow = lax.broadcasted_iota(jnp.int32, (SUB, MIX_G), 0)
    pss, pps, vts = [], [], []
    for x in range(n):
        q, k, v, _, rev = items[x]
        pairs, sc, bl = prep[x]['pairs'], scs[x], prep[x]['bl']
        p_rows = []
        for h in range(HEADS):
            for i in range(NSUB):
                acc = None
                for p, (pi, pj) in enumerate(pairs):
                    if pi != i:
                        continue
                    base = (h * len(pairs) + p) * SUB
                    blk = jnp.where(col == pj, sc[base:base + SUB, :], 0.0)
                    acc = blk if acc is None else acc + blk
                p_rows.append(jnp.zeros((SUB, CHUNK), F32) if acc is None else acc)
        pss.append(jnp.concatenate(p_rows, axis=0).astype(BF16))
        pp, vt = [], []
        for s in range(NSUB):
            sl = slice(s * SUB, (s + 1) * SUB)
            bl_s, q_s, k_s = bl[sl, :], q[sl, :].astype(BF16), k[sl, :].astype(BF16)
            for i in range(SUB):
                d = bl_s[i:i + 1, :] - bl_s
                keep = (jrow <= i) if not rev else (jrow >= i)
                e = jnp.exp2(jnp.where(keep, d, NEG_BIG).astype(BF16))
                pp.append(e * q_s[i:i + 1, :] * k_s)
            vt.extend([v[sl, :]] * SUB)
        pps.append(jnp.concatenate(pp, axis=0))
        vts.append(jnp.concatenate(vt, axis=0))

    o_int = [_dot_nt(prep[x]['q_inter'], st_in[x]) for x in range(n)]
    fulls = [_dot(pss[x], items[x][2]) for x in range(n)]
    sbs = [_dot(pps[x], cs[x][2]) for x in range(n)]

    ws = [(sbs[x] * vts[x].astype(F32)).astype(BF16) for x in range(n)]
    o_diag = [_dot(cs[x][3], ws[x]) for x in range(n)]

    outs = []
    for x in range(n):
        hmask = cs[x][4]
        o = o_int[x] + o_diag[x]
        for h in range(HEADS):
            o = o + fulls[x][h * CHUNK:(h + 1) * CHUNK, :] * hmask[h]
        outs.append(o)
    return outs, st_f, st_b


def _scan_consts(rev):
    r = lax.broadcasted_iota(jnp.int32, (CHUNK, CHUNK), 0)
    c = lax.broadcasted_iota(jnp.int32, (CHUNK, CHUNK), 1)
    tri = ((c <= r) if not rev else (c >= r)).astype(BF16)
    rr = lax.broadcasted_iota(jnp.int32, (MIX_G, MIX_G), 0) // HEAD_D
    cc = lax.broadcasted_iota(jnp.int32, (MIX_G, MIX_G), 1) // HEAD_D
    bd_mask = (rr == cc).astype(F32)
    ones_bd = bd_mask.astype(BF16)
    gi = lax.broadcasted_iota(jnp.int32, (CHUNK, CHUNK * SUB), 0)
    gj = lax.broadcasted_iota(jnp.int32, (CHUNK, CHUNK * SUB), 1) // SUB
    gsel = (gi == gj).astype(BF16)
    return tri, bd_mask, ones_bd, gsel, _head_masks()


def _scan_kernel(nt, qf_ref, kf_ref, vf_ref, gf_ref, qb_ref, kb_ref, vb_ref, gb_ref, s0_ref,
                 of_ref, ob_ref, sfin_ref, st_scr):
    t = pl.program_id(1)

    @pl.when(t == 0)
    def _():
        _load_states(s0_ref, st_scr, True)

    cf = _scan_consts(False)
    cb = _scan_consts(True)

    items, rows = [], []
    for c in range(NCHUNK):
        rf = slice(c * CHUNK, (c + 1) * CHUNK)
        rb = slice((NCHUNK - 1 - c) * CHUNK, (NCHUNK - c) * CHUNK)
        items.append((qf_ref[rf, :].astype(F32), kf_ref[rf, :].astype(F32), vf_ref[rf, :], gf_ref[rf, :], False))
        items.append((qb_ref[rb, :].astype(F32), kb_ref[rb, :].astype(F32), vb_ref[rb, :], gb_ref[rb, :], True))
        rows += [(of_ref, rf), (ob_ref, rb)]
    outs, st_f, st_b = _scan_chunks(items, st_scr[0], st_scr[1], cf, cb)
    for (ref, sl), o in zip(rows, outs):
        ref[sl, :] = o
    st_scr[0] = st_f
    st_scr[1] = st_b

    @pl.when(t == nt - 1)
    def _():
        _store_states(st_scr, sfin_ref, True)


def _gated_scan(n_seq, nt, q, k_f, k_b, v, g_f, g_b, s0, name):
    fwd = lambda s, t: (s * nt + t, 0)
    bwd = lambda s, t: (s * nt + (nt - 1 - t), 0)
    blk = lambda im: pl.BlockSpec((TILE, MIX_G), im)
    in_specs = [blk(fwd), blk(fwd), blk(fwd), blk(fwd), blk(bwd), blk(bwd), blk(bwd), blk(bwd),
                pl.BlockSpec((None, 2, HEADS, HEAD_D, HEAD_D), lambda s, t: (s, 0, 0, 0, 0))]
    nrows = n_seq * nt * TILE
    return pl.pallas_call(
        functools.partial(_scan_kernel, nt),
        grid=(n_seq, nt),
        in_specs=in_specs,
        out_specs=[pl.BlockSpec((TILE, MIX_G), lambda s, t: (s * nt + t, 0)),
                   pl.BlockSpec((TILE, MIX_G), lambda s, t: (s * nt + (nt - 1 - t), 0)),
                   pl.BlockSpec((None, 2, HEADS, HEAD_D, HEAD_D), lambda s, t: (s, 0, 0, 0, 0))],
        out_shape=[jax.ShapeDtypeStruct((nrows, MIX_G), F32),
                   jax.ShapeDtypeStruct((nrows, MIX_G), F32),
                   jax.ShapeDtypeStruct((n_seq, 2, HEADS, HEAD_D, HEAD_D), F32)],
        scratch_shapes=[pltpu.VMEM((2, MIX_G, MIX_G), F32)],
        compiler_params=_cparams(2),
        name=name,
    )(q, k_f, v, g_f, q, k_b, v, g_b, s0)


def _retention_tables():
    gam = 1.0 - 2.0 ** (-5.0 - np.arange(HEADS, dtype=np.float64))
    gam_r = gam[::-1]
    i = np.arange(TILE, dtype=np.float64)
    diff = i[:, None] - i[None, :]
    dcomb = np.zeros((HEADS, TILE, TILE))
    for h in range(HEADS):
        lower = np.where(diff > 0, gam[h] ** np.maximum(diff, 0), 0.0)
        upper = np.where(diff < 0, gam_r[h] ** np.maximum(-diff, 0), 0.0)
        dcomb[h] = lower + upper + 2.0 * (diff == 0)
    lanes = lambda per_head: np.repeat(per_head, HEAD_D, axis=-1)
    qd_f = lanes(gam[None, :] ** (i[:, None] + 1.0))
    kd_f = lanes(gam[None, :] ** (TILE - 1.0 - i[:, None]))
    qd_b = lanes(gam_r[None, :] ** (TILE - i[:, None]))
    kd_b = lanes(gam_r[None, :] ** i[:, None])
    blk = (np.arange(MIX_G)[:, None] // HEAD_D) == (np.arange(MIX_G)[None, :] // HEAD_D)
    c_f = np.where(blk, lanes(gam ** TILE)[None, :], 0.0) * np.ones((MIX_G, 1))
    c_b = np.where(blk, lanes(gam_r ** TILE)[None, :], 0.0) * np.ones((MIX_G, 1))
    f = lambda a: jnp.asarray(a, F32)
    return (f(dcomb.reshape(HEADS * TILE, TILE)), f(qd_f), f(kd_f), f(qd_b), f(kd_b), f(c_f), f(c_b))


def _ret_kernel(nt, qf_ref, kf_ref, vf_ref, qb_ref, kb_ref, vb_ref, s0_ref, dcomb_ref, qdf_ref, kdf_ref,
                qdb_ref, kdb_ref, cf_ref, cb_ref, of_ref, ob_ref, sfin_ref, st_scr):
    t = pl.program_id(1)

    @pl.when(t == 0)
    def _():
        _load_states(s0_ref, st_scr, False)

    lane_head = lax.broadcasted_iota(jnp.int32, (TILE, MIX_G), 1) // HEAD_D
    rr = lax.broadcasted_iota(jnp.int32, (MIX_G, MIX_G), 0) // HEAD_D
    cc = lax.broadcasted_iota(jnp.int32, (MIX_G, MIX_G), 1) // HEAD_D
    same_head = rr == cc

    q = qf_ref[...]
    k = kf_ref[...]
    v = vf_ref[...]
    qs = jnp.concatenate([jnp.where(lane_head == h, q, jnp.zeros_like(q)) for h in range(HEADS)], axis=0)
    p = (_dot_nt(qs, k) * dcomb_ref[...]).astype(BF16)
    full = _dot(p, v)
    o = _dot((q.astype(F32) * qdf_ref[...]).astype(BF16), st_scr[0].astype(BF16))
    for h in range(HEADS):
        o = o + jnp.where(lane_head == h, full[h * TILE:(h + 1) * TILE, :], 0.0)
    of_ref[...] = o
    u = _dot_tn((k.astype(F32) * kdf_ref[...]).astype(BF16), v)
    st_scr[0] = st_scr[0] * cf_ref[...] + jnp.where(same_head, u, 0.0)

    qb = qb_ref[...]
    kb = kb_ref[...]
    vb = vb_ref[...]
    ob_ref[...] = _dot((qb.astype(F32) * qdb_ref[...]).astype(BF16), st_scr[1].astype(BF16))
    ub = _dot_tn((kb.astype(F32) * kdb_ref[...]).astype(BF16), vb)
    st_scr[1] = st_scr[1] * cb_ref[...] + jnp.where(same_head, ub, 0.0)

    @pl.when(t == nt - 1)
    def _():
        _store_states(st_scr, sfin_ref, False)


def _retention(n_seq, nt, q, k, v, s0, tables, name):
    fwd = lambda s, t: (s * nt + t, 0)
    bwd = lambda s, t: (s * nt + (nt - 1 - t), 0)
    const = lambda s, t: (0, 0)
    blk = lambda im: pl.BlockSpec((TILE, MIX_G), im)
    sq = lambda: pl.BlockSpec((MIX_G, MIX_G), const)
    in_specs = [blk(fwd), blk(fwd), blk(fwd), blk(bwd), blk(bwd), blk(bwd),
                pl.BlockSpec((None, 2, HEADS, HEAD_D, HEAD_D), lambda s, t: (s, 0, 0, 0, 0)),
                pl.BlockSpec((HEADS * TILE, TILE), const), sq(), sq(), sq(), sq(), sq(), sq()]
    nrows = n_seq * nt * TILE
    return pl.pallas_call(
        functools.partial(_ret_kernel, nt),
        grid=(n_seq, nt),
        in_specs=in_specs,
        out_specs=[blk(fwd), blk(bwd), pl.BlockSpec((None, 2, HEADS, HEAD_D, HEAD_D), lambda s, t: (s, 0, 0, 0, 0))],
        out_shape=[jax.ShapeDtypeStruct((nrows, MIX_G), F32),
                   jax.ShapeDtypeStruct((nrows, MIX_G), F32),
                   jax.ShapeDtypeStruct((n_seq, 2, HEADS, HEAD_D, HEAD_D), F32)],
        scratch_shapes=[pltpu.VMEM((2, MIX_G, MIX_G), F32)],
        compiler_params=_cparams(2),
        name=name,
    )(q, k, v, q, k, v, s0, *tables)


def _attn_step(lam_init, comp, refs, s_write, s_read, part0_ref):
    q_ref, k_ref, ck_ref, v_ref, cv_ref, lam_ref, sub_ref, o_ref = refs
    sw, sw2, mw = s_write
    sr, sr2, mr = s_read

    q = q_ref[...]
    lane = lax.broadcasted_iota(jnp.int32, q.shape, 1)
    qc = jnp.where((lane // HEAD_D) == comp, q, jnp.zeros_like(q))
    n_keys = k_ref.shape[0]
    kc = min(ATTN_KEY_CHUNK, n_keys)
    ones_rows = jnp.ones((ONES_ROWS, kc), BF16)
    m_prev = mr[...]
    oe, m_cur = None, None
    for j in range(n_keys // kc):
        ks = slice(j * kc, (j + 1) * kc)
        d = _dot(jnp.concatenate([v_ref[:, ks], ones_rows], axis=0),
                 jnp.exp2((sr[ks, :] - m_prev).astype(BF16)))
        oe = d if oe is None else oe + d
        s = _dot_nt(k_ref[ks, :], qc)
        sw[ks, :] = s
        mj = jnp.max(s, axis=0, keepdims=True)
        m_cur = mj if m_cur is None else jnp.maximum(m_cur, mj)
    oe = oe + _dot(jnp.concatenate([cv_ref[...], jnp.ones((ONES_ROWS, cv_ref.shape[1]), BF16)], axis=0),
                   jnp.exp2((sr2[...] - m_prev).astype(BF16)))
    s2 = _dot_nt(ck_ref[...], qc)
    sw2[...] = s2
    m_cur = jnp.maximum(m_cur, jnp.max(s2, axis=0, keepdims=True))
    mw[...] = m_cur
    part = oe[:128, :] / oe[128:129, :]
    if comp == 1:
        part0_ref[...] = part
    else:
        lp = lam_ref[...]
        lam = (jnp.exp(jnp.sum(lp[0:1] * lp[1:2], axis=-1, keepdims=True))
               - jnp.exp(jnp.sum(lp[2:3] * lp[3:4], axis=-1, keepdims=True)) + lam_init)
        o = part0_ref[...] - lam * part
        ms = jnp.mean(o * o, axis=0, keepdims=True)
        o = (o * lax.rsqrt(ms + RMS_EPS)).T
        o_ref[...] = (o * sub_ref[...] * (1.0 - lam_init)).astype(BF16)


def _attn_kernel(lam_init, *refs):
    n_in = 8
    io, part0_ref, scr = refs[:n_in], refs[n_in], refs[n_in + 1:]
    half = len(scr) // 2
    buf_a, buf_b = scr[:half], scr[half:]
    u = pl.program_id(0)

    @pl.when(u == 0)
    def _():
        for r in buf_b + (part0_ref,):
            r[...] = jnp.zeros(r.shape, r.dtype)

    @pl.when(u % 2 == 0)
    def _():
        _attn_step(lam_init, 0, io, buf_a, buf_b, part0_ref)

    @pl.when(u % 2 == 1)
    def _():
        _attn_step(lam_init, 1, io, buf_b, buf_a, part0_ref)


def _attn_short_kernel(lam_init, q_ref, k_ref, v_ref, lam_ref, sub_ref, o_ref):
    lp = lam_ref[...]
    lam = (jnp.exp(jnp.sum(lp[0:1] * lp[1:2], axis=-1, keepdims=True))
           - jnp.exp(jnp.sum(lp[2:3] * lp[3:4], axis=-1, keepdims=True)) + lam_init)
    n_keys = k_ref.shape[0]
    ones_rows = jnp.ones((ONES_ROWS, n_keys), BF16)
    lane = lax.broadcasted_iota(jnp.int32, (q_ref.shape[0], 128), 1)
    units = [(h, c) for h in range(HEADS) for c in range(2)]
    scores = []
    for h, c in units:
        q = q_ref[:, h * 128:(h + 1) * 128]
        qc = jnp.where((lane // HEAD_D) == c, q, jnp.zeros_like(q))
        scores.append(_dot_nt(k_ref[:, h * 128:(h + 1) * 128], qc))
    probs = [jnp.exp2((s - jnp.max(s, axis=0, keepdims=True)).astype(BF16)) for s in scores]
    oes = [_dot(jnp.concatenate([v_ref[h * 128:(h + 1) * 128, :], ones_rows], axis=0), p)
           for (h, c), p in zip(units, probs)]
    parts = [oe[:128, :] / oe[128:129, :] for oe in oes]
    outs = []
    for h in range(HEADS):
        o = parts[2 * h] - lam * parts[2 * h + 1]
        ms = jnp.mean(o * o, axis=0, keepdims=True)
        outs.append((o * lax.rsqrt(ms + RMS_EPS)).T * sub_ref[...] * (1.0 - lam_init))
    o_ref[...] = jnp.concatenate(outs, axis=1).astype(BF16)


def _diff_attention_short(layer, n_seq, seq_len, q, k, v_t, da_lambda, subln):
    lam_init = 0.8 - 0.6 * math.exp(-0.3 * layer)
    return pl.pallas_call(
        functools.partial(_attn_short_kernel, lam_init),
        grid=(n_seq,),
        in_specs=[pl.BlockSpec((seq_len, DA_W), lambda b: (b, 0)),
                  pl.BlockSpec((seq_len, DA_W), lambda b: (b, 0)),
                  pl.BlockSpec((DA_W, seq_len), lambda b: (0, b)),
                  pl.BlockSpec((None, 4, HEAD_D), lambda b: (layer, 0, 0)),
                  pl.BlockSpec((None, 1, 128), lambda b: (layer, 0, 0))],
        out_specs=pl.BlockSpec((seq_len, DA_W), lambda b: (b, 0)),
        out_shape=jax.ShapeDtypeStruct((n_seq * seq_len, DA_W), BF16),
        compiler_params=_cparams(1),
        name="diff_attention_ctx",
    )(q, k, v_t, da_lambda, subln)


def _diff_attention(layer, n_seq, seq_len, q, k, v_t, cache_k, cache_v_t, da_lambda, subln):
    tq = min(ATTN_TQ, seq_len)
    nq = seq_len // tq
    n_units = n_seq * HEADS * nq * 2
    lam_init = 0.8 - 0.6 * math.exp(-0.3 * layer)

    cur = lambda u: jnp.minimum(u, n_units - 1) // 2
    prev = lambda u: jnp.maximum(u - 1, 0) // 2
    seq_of = lambda w: w // (HEADS * nq)
    head_of = lambda w: (w // nq) % HEADS
    rows_of = lambda w: seq_of(w) * nq + w % nq

    q_spec = pl.BlockSpec((tq, 128), lambda u: (rows_of(cur(u)), head_of(cur(u))))
    k_spec = pl.BlockSpec((seq_len, 128), lambda u: (seq_of(cur(u)), head_of(cur(u))))
    v_spec = pl.BlockSpec((128, seq_len), lambda u: (head_of(prev(u)), seq_of(prev(u))))
    ck_spec = pl.BlockSpec((None, None, PAST_LEN, 128), lambda u: (seq_of(cur(u)), layer, 0, head_of(cur(u))))
    cv_spec = pl.BlockSpec((None, None, 128, PAST_LEN), lambda u: (seq_of(prev(u)), layer, head_of(prev(u)), 0))
    tail = [pl.BlockSpec((None, 4, HEAD_D), lambda u: (layer, 0, 0)),
            pl.BlockSpec((None, 1, 128), lambda u: (layer, 0, 0))]
    in_specs = [q_spec, k_spec, ck_spec, v_spec, cv_spec] + tail
    args = [q, k, cache_k, v_t, cache_v_t, da_lambda, subln]
    scratch = [pltpu.VMEM((128, tq), F32)] + [pltpu.VMEM((seq_len, tq), F32), pltpu.VMEM((PAST_LEN, tq), F32),
                                              pltpu.VMEM((1, tq), F32)] * 2
    return pl.pallas_call(
        functools.partial(_attn_kernel, lam_init),
        grid=(n_units + 1,),
        in_specs=in_specs,
        out_specs=pl.BlockSpec((tq, 128), lambda u: (rows_of(prev(u)), head_of(prev(u)))),
        out_shape=jax.ShapeDtypeStruct((n_seq * seq_len, DA_W), BF16),
        scratch_shapes=scratch,
        compiler_params=_cparams(1),
        name="diff_attention_lat",
    )(*args)


def _route(logits):
    lane = lax.broadcasted_iota(jnp.int32, logits.shape, 1)
    lane_f = lane.astype(F32)
    big = 1e9
    is_g = jnp.logical_and(lane >= N_EXPERTS, lane < N_EXPERTS + N_GROUPS)
    gl = jnp.where(is_g, logits, NEG_BIG)
    gmax = jnp.max(gl, axis=-1, keepdims=True)
    gsel = jnp.min(jnp.where(gl == gmax, lane_f, big), axis=-1, keepdims=True) - N_EXPERTS
    g_w = 1.0 / jnp.sum(jnp.exp(gl - gmax), axis=-1, keepdims=True)
    in_grp = jnp.logical_and(lane < N_EXPERTS, (lane // EPG).astype(F32) == gsel)
    el = jnp.where(in_grp, logits, NEG_BIG)
    v1 = jnp.max(el, axis=-1, keepdims=True)
    i1 = jnp.min(jnp.where(el == v1, lane_f, big), axis=-1, keepdims=True)
    el2 = jnp.where(lane_f == i1, NEG_BIG, el)
    v2 = jnp.max(el2, axis=-1, keepdims=True)
    i2 = jnp.min(jnp.where(el2 == v2, lane_f, big), axis=-1, keepdims=True)
    t = jnp.exp(v2 - v1)
    w1 = g_w / (1.0 + t)
    w2 = w1 * t
    first_lo = i1 < i2
    a = jnp.minimum(i1, i2) - EPG * gsel
    b = jnp.maximum(i1, i2) - EPG * gsel
    bucket = gsel * N_PAIRS + a * (7.0 - a) * 0.5 + (b - a - 1.0)
    w_lo = jnp.where(first_lo, w1, w2)
    w_hi = jnp.where(first_lo, w2, w1)
    return jnp.where(lane == 0, bucket, jnp.where(lane == 1, w_lo, jnp.where(lane == 2, w_hi, 0.0)))


def _outproj_body(mod_ref, n2_ref, streams, hgn_ref, bd_ref, wout_ref, wrh_ref, wrl_ref, br_ref,
                  x1_ref, hp_ref, route_ref):
    x_ref, rof_ref, rob_ref, rsg_ref, da_ref, hof_ref, hob_ref, hsg_ref = streams
    ro = rof_ref[...] + rob_ref[...]
    ro = ro * lax.rsqrt(_group_mean_sq(ro, bd_ref, HEAD_D) + RMS_EPS) * rsg_ref[...].astype(F32)
    ho = hof_ref[...] + hob_ref[...]
    ho = ho * lax.rsqrt(_group_mean_sq(ho, bd_ref, HEAD_D) + RMS_EPS) * hgn_ref[...] * hsg_ref[...].astype(F32)
    mix = jnp.concatenate([ro.astype(BF16), da_ref[...], ho.astype(BF16)], axis=1)
    mixed = _dot(mix, wout_ref[...])
    gate1 = mod_ref[2:3, :]
    shift2 = mod_ref[3:4, :]
    scale2 = mod_ref[4:5, :]
    x1 = x_ref[...] + gate1 * mixed
    ms = jnp.mean(x1 * x1, axis=-1, keepdims=True)
    h2 = x1 * lax.rsqrt(ms + RMS_EPS) * n2_ref[...] * (1.0 + scale2) + shift2
    h_hi = h2.astype(BF16)
    h_lo = (h2 - h_hi.astype(F32)).astype(BF16)
    logits = _dot(h_hi, wrh_ref[...]) + _dot(h_lo, wrh_ref[...]) + _dot(h_hi, wrl_ref[...]) + br_ref[...]
    route = _route(logits)
    x1_ref[...] = x1
    route_ref[...] = route
    hp_ref[...] = jnp.concatenate([h2, route], axis=1)


def _outproj_kernel(mod_ref, n2_ref, *rest):
    ctx_streams, lat_streams, tail = rest[0:8], rest[8:16], rest[16:]
    t = pl.program_id(0)

    @pl.when(t < CTX_PTILES)
    def _():
        _outproj_body(mod_ref, n2_ref, ctx_streams, *tail)

    @pl.when(t >= CTX_PTILES)
    def _():
        _outproj_body(mod_ref, n2_ref, lat_streams, *tail)


def _cond_of_tile(t):
    return jnp.where(t < CTX_TILES, 0, 1 + (t - CTX_TILES) // LAT_TILES_PER_SEQ)


def _out_projection(layer, mod, norm2, ctx_streams, lat_streams, hgn_t, bd256, w_out_bf, w_route_hi, w_route_lo,
                    b_route):
    row = lambda t: (t, 0)
    ctx_row = lambda t: (jnp.minimum(t, CTX_PTILES - 1), 0)
    lat_row = lambda t: (jnp.maximum(t - CTX_PTILES, 0), 0)
    lay = lambda t: (layer, 0, 0)
    cond = lambda t: jnp.where(t < CTX_PTILES, 0, 1 + (t - CTX_PTILES) // LAT_PTILES_PER_SEQ)
    widths = (D_MODEL, MIX_G, MIX_G, MIX_G, DA_W, MIX_G, MIX_G, MIX_G)
    in_specs = [
        pl.BlockSpec((None, None, 6, D_MODEL), lambda t: (layer, cond(t), 0, 0)),
        pl.BlockSpec((None, 1, D_MODEL), lay),
    ]
    in_specs += [pl.BlockSpec((PTILE, w), ctx_row) for w in widths]
    in_specs += [pl.BlockSpec((PTILE, w), lat_row) for w in widths]
    in_specs += [
        pl.BlockSpec((None, 1, MIX_G), lay),
        pl.BlockSpec((MIX_G, MIX_G), lambda t: (0, 0)),
        pl.BlockSpec((None, D_MODEL, D_MODEL), lay),
        pl.BlockSpec((None, D_MODEL, 128), lay),
        pl.BlockSpec((None, D_MODEL, 128), lay),
        pl.BlockSpec((None, 1, 128), lay),
    ]
    return pl.pallas_call(
        _outproj_kernel,
        grid=(N_ROWS // PTILE,),
        in_specs=in_specs,
        out_specs=[pl.BlockSpec((PTILE, D_MODEL), row), pl.BlockSpec((PTILE, ROW_W), row),
                   pl.BlockSpec((PTILE, 128), row)],
        out_shape=[jax.ShapeDtypeStruct((N_ROWS, D_MODEL), F32), jax.ShapeDtypeStruct((N_ROWS, ROW_W), F32),
                   jax.ShapeDtypeStruct((N_ROWS, 128), F32)],
        compiler_params=_cparams(1),
        name="out_projection",
    )(mod, norm2, *ctx_streams, *lat_streams, hgn_t, bd256, w_out_bf, w_route_hi, w_route_lo, b_route)


def _dispatch_kernel(route_ref, dest_ref, meta_ref, cnt_ref):
    ph = pl.program_id(0)
    t = pl.program_id(1)
    lane = lax.broadcasted_iota(jnp.int32, (TILE, 128), 1).astype(F32)
    onehots = [(lane == route_ref[j * TILE:(j + 1) * TILE, 0:1]).astype(F32) for j in range(DISPATCH_TILES)]
    tile_cnts = [jnp.sum(oh, axis=0, keepdims=True) for oh in onehots]

    @pl.when(jnp.logical_and(ph == 0, t == 0))
    def _():
        cnt_ref[...] = jnp.zeros_like(cnt_ref)

    @pl.when(ph == 0)
    def _():
        cnt_ref[0:1, :] += sum(tile_cnts[1:], tile_cnts[0])

    @pl.when(jnp.logical_and(ph == 1, t == 0))
    def _():
        cnt = cnt_ref[0:1, :]
        padded = jnp.floor((cnt + (MOE_TILE - 1.0)) * (1.0 / MOE_TILE)) * MOE_TILE
        r = lax.broadcasted_iota(jnp.int32, (128, 128), 0)
        c = lax.broadcasted_iota(jnp.int32, (128, 128), 1)
        off = jnp.dot(jnp.broadcast_to(padded, (8, 128)), (r < c).astype(F32),
                      preferred_element_type=F32, precision=HI)[0:1, :]
        cnt_ref[1:2, :] = off
        end = off + padded
        end_col = jnp.sum(jnp.where(r == c, jnp.broadcast_to(end, (128, 128)), 0.0), axis=1, keepdims=True)
        ended = jnp.logical_and(end_col <= c.astype(F32) * MOE_TILE, r < N_BUCKETS)
        tile_bucket = jnp.sum(ended.astype(F32), axis=0, keepdims=True)
        lane1 = lax.broadcasted_iota(jnp.int32, (1, 128), 1)
        tile_idx = lane1.astype(F32)
        n_valid = jnp.max(jnp.where(lane1 < N_BUCKETS, end, 0.0), axis=1, keepdims=True) * (1.0 / MOE_TILE)
        last = jnp.max(jnp.where(tile_idx < n_valid, tile_bucket, 0.0), axis=1, keepdims=True)
        tb = jnp.where(tile_idx < n_valid, tile_bucket, last)
        g = jnp.floor(tb * (1.0 / N_PAIRS))
        pid = tb - g * N_PAIRS
        a = (pid >= 3.0).astype(F32) + (pid >= 5.0).astype(F32)
        b = pid + 1.0 - 2.0 * (a >= 1.0).astype(F32) - (a >= 2.0).astype(F32)
        rows = [EPG * g + a, EPG * g + b, jnp.broadcast_to(n_valid, (1, 128)), jnp.zeros((5, 128), F32)]
        meta_ref[...] = jnp.concatenate(rows, axis=0).astype(jnp.int32)

    @pl.when(ph == 1)
    def _():
        base = cnt_ref[1:2, :]
        rr = lax.broadcasted_iota(jnp.int32, (TILE, TILE), 0)
        cc = lax.broadcasted_iota(jnp.int32, (TILE, TILE), 1)
        for j, onehot in enumerate(onehots):
            oh16 = onehot.astype(BF16)
            same = _dot_nt(oh16, oh16)
            rank = jnp.sum(jnp.where(rr < cc, same, 0.0), axis=0, keepdims=True)
            base_row = lax.dot_general(jnp.broadcast_to(base, (8, 128)), onehot, (((1,), (1,)), ((), ())),
                                       preferred_element_type=F32, precision=HI)[0:1, :]
            dest_ref[j] = (base_row + rank).astype(jnp.int32)
            base = base + tile_cnts[j]
        cnt_ref[1:2, :] = base


def _moe_dispatch(route):
    dest, meta = pl.pallas_call(
        _dispatch_kernel,
        grid=(2, N_TILES // DISPATCH_TILES),
        in_specs=[pl.BlockSpec((DISPATCH_TILES * TILE, 128), lambda ph, t: (t, 0))],
        out_specs=[pl.BlockSpec((DISPATCH_TILES, 1, TILE), lambda ph, t: (t * ph, 0, 0)),
                   pl.BlockSpec((8, 128), lambda ph, t: (0, 0))],
        out_shape=[jax.ShapeDtypeStruct((N_TILES, 1, TILE), jnp.int32),
                   jax.ShapeDtypeStruct((8, 128), jnp.int32)],
        scratch_shapes=[pltpu.VMEM((8, 128), F32)],
        compiler_params=_cparams(2),
        name="moe_dispatch",
    )(route)
    return dest.reshape(N_ROWS), meta[0:3].reshape(3 * 128)


def _dma_params():
    return pltpu.CompilerParams(dimension_semantics=("arbitrary",), vmem_limit_bytes=VMEM_LIMIT,
                                disable_bounds_checks=True)


def _moe_scatter_kernel(dest_ref, h_ref, hs_init_ref, hs_ref, stage, sems):
    del hs_init_ref
    t = pl.program_id(0)
    base = t * TILE
    slot = t % 2
    stage[slot] = h_ref[...]

    def issue(g, carry):
        for j in range(DMA_GROUP):
            i = g * DMA_GROUP + j
            d = dest_ref[base + i]
            pltpu.make_async_copy(stage.at[slot, pl.ds(i, 1)], hs_ref.at[pl.ds(d, 1)],
                                  sems.at[slot]).start(priority=j % 2)
        return carry

    lax.fori_loop(0, TILE // DMA_GROUP, issue, 0)

    def wait_tile(s):
        pltpu.make_async_copy(stage.at[s], hs_ref.at[pl.ds(0, TILE)], sems.at[s]).wait()

    @pl.when(t > 0)
    def _():
        wait_tile(1 - slot)

    @pl.when(t == pl.num_programs(0) - 1)
    def _():
        wait_tile(slot)


def _moe_scatter(dest, hp, hs_init):
    return pl.pallas_call(
        _moe_scatter_kernel,
        grid_spec=pltpu.PrefetchScalarGridSpec(
            num_scalar_prefetch=1,
            grid=(N_TILES,),
            in_specs=[pl.BlockSpec((TILE, ROW_W), lambda t, dest: (t, 0)),
                      pl.BlockSpec(memory_space=pl.ANY)],
            out_specs=pl.BlockSpec(memory_space=pl.ANY),
            scratch_shapes=[pltpu.VMEM((2, TILE, ROW_W), F32), pltpu.SemaphoreType.DMA((2,))],
        ),
        out_shape=jax.ShapeDtypeStruct((N_SLOTS, ROW_W), F32),
        input_output_aliases={2: 0},
        compiler_params=_dma_params(),
        name="moe_scatter",
    )(dest, hp, hs_init)


def _moe_ffn_kernel(meta_ref, hs_ref, wgl_ref, wul_ref, wdl_ref, wgh_ref, wuh_ref, wdh_ref, ys_ref):
    t = pl.program_id(0)
    n_valid = meta_ref[2 * 128]

    @pl.when(t < n_valid)
    def _():
        w = hs_ref[...]
        x = w[:, :D_MODEL].astype(BF16)
        rw = w[:, D_MODEL:]

        def gated(wg_ref, wu_ref, weight):
            return (_silu(_dot(x, wg_ref[...])) * _dot(x, wu_ref[...]) * weight).astype(BF16)

        a_lo = gated(wgl_ref, wul_ref, rw[:, 1:2])
        a_hi = gated(wgh_ref, wuh_ref, rw[:, 2:3])
        ys_ref[...] = _dot(a_lo, wdl_ref[...]) + _dot(a_hi, wdh_ref[...])

    @pl.when(t >= n_valid)
    def _():
        ys_ref[...] = jnp.zeros_like(ys_ref)


def _moe_ffn(layer, meta, hs, w_gate_bf, w_up_bf, w_down_bf):
    lo = lambda t, m: (layer, m[t], 0, 0)
    hi = lambda t, m: (layer, m[128 + t], 0, 0)
    up = lambda im: pl.BlockSpec((None, None, D_MODEL, EXPERT_FF), im)
    down = lambda im: pl.BlockSpec((None, None, EXPERT_FF, D_MODEL), im)
    return pl.pallas_call(
        _moe_ffn_kernel,
        grid_spec=pltpu.PrefetchScalarGridSpec(
            num_scalar_prefetch=1,
            grid=(MOE_MAX_TILES,),
            in_specs=[pl.BlockSpec((MOE_TILE, ROW_W), lambda t, m: (jnp.maximum(jnp.minimum(t, m[2 * 128] - 1), 0), 0)),
                      up(lo), up(lo), down(lo), up(hi), up(hi), down(hi)],
            out_specs=pl.BlockSpec((MOE_TILE, D_MODEL), lambda t, m: (t, 0)),
        ),
        out_shape=jax.ShapeDtypeStruct((N_SLOTS, D_MODEL), F32),
        compiler_params=_cparams(1),
        name="moe_ffn",
    )(meta, hs, w_gate_bf, w_up_bf, w_down_bf, w_gate_bf, w_up_bf, w_down_bf)


def _moe_combine_kernel(dest_ref, x1_ref, mod_ref, ys_ref, out_ctx_ref, out_lat_ref, buf, sems):
    t = pl.program_id(0)
    nt = pl.num_programs(0)

    def issue(tile, slot):
        def body(g, carry):
            for j in range(DMA_GROUP):
                i = g * DMA_GROUP + j
                d = dest_ref[tile * TILE + i]
                pltpu.make_async_copy(ys_ref.at[pl.ds(d, 1)], buf.at[slot, pl.ds(i, 1)],
                                      sems.at[slot]).start(priority=j % 2)
            return carry
        lax.fori_loop(0, TILE // DMA_GROUP, body, 0)

    @pl.when(t == 0)
    def _():
        issue(0, 0)

    @pl.when(t + 1 < nt)
    def _():
        issue(t + 1, (t + 1) % 2)

    slot = t % 2
    pltpu.make_async_copy(ys_ref.at[pl.ds(0, TILE)], buf.at[slot], sems.at[slot]).wait()
    x2 = x1_ref[...] + mod_ref[5:6, :] * buf[slot]

    @pl.when(t < CTX_TILES)
    def _():
        out_ctx_ref[...] = x2

    @pl.when(t >= CTX_TILES)
    def _():
        out_lat_ref[...] = x2


def _moe_combine(layer, dest, x1, mod, ys):
    return pl.pallas_call(
        _moe_combine_kernel,
        grid_spec=pltpu.PrefetchScalarGridSpec(
            num_scalar_prefetch=1,
            grid=(N_TILES,),
            in_specs=[pl.BlockSpec((TILE, D_MODEL), lambda t, dest: (t, 0)),
                      pl.BlockSpec((None, None, 6, D_MODEL), lambda t, dest: (layer, _cond_of_tile(t), 0, 0)),
                      pl.BlockSpec(memory_space=pl.ANY)],
            out_specs=[pl.BlockSpec((TILE, D_MODEL), lambda t, dest: (jnp.minimum(t, CTX_TILES - 1), 0)),
                       pl.BlockSpec((TILE, D_MODEL), lambda t, dest: (jnp.maximum(t - CTX_TILES, 0), 0))],
            scratch_shapes=[pltpu.VMEM((2, TILE, D_MODEL), F32), pltpu.SemaphoreType.DMA((2,))],
        ),
        out_shape=[jax.ShapeDtypeStruct((N_CTX_ROWS, D_MODEL), F32), jax.ShapeDtypeStruct((N_LAT_ROWS, D_MODEL), F32)],
        compiler_params=_dma_params(),
        name="moe_combine",
    )(dest, x1, mod, ys)


def _block_diag_ones(n, group):
    i = np.arange(n) // group
    return jnp.asarray((i[:, None] == i[None, :]).astype(np.float32), dtype=BF16)


def _rope_tables():
    pos = np.arange(DEC_SEQ)
    rows = (pos // GRID_W).astype(np.float32)
    cols = (pos % GRID_W).astype(np.float32)
    inv_freq = (ROPE_BASE ** (-(np.arange(ROPE_PAIRS, dtype=np.float32) / ROPE_PAIRS))).astype(np.float32)
    ang_r = rows[:, None] * inv_freq[None, :]
    ang_c = cols[:, None] * inv_freq[None, :]
    cos64 = np.concatenate([np.cos(ang_r), np.cos(ang_r), np.cos(ang_c), np.cos(ang_c)], axis=1)
    sin64 = np.concatenate([-np.sin(ang_r), np.sin(ang_r), -np.sin(ang_c), np.sin(ang_c)], axis=1)
    return (jnp.asarray(np.tile(cos64, (1, 2)), F32), jnp.asarray(np.tile(sin64, (1, 2)), F32))


def _mixers(latent, layer, x, consts, params, ret_s0, hg_s0, cache_k, cache_v):
    (mod, lb_all, bd512, bd256, rope_cos, rope_sin, ret_tables) = consts
    n_seq = DEC_BATCH if latent else BATCH
    seq_len = DEC_SEQ if latent else SEQ
    nt = seq_len // TILE
    proj = _in_projection(latent, layer, x, mod, params['norm1'], params['w_in'], params['w_vt'], params['da_qn'],
                          params['da_kn'], lb_all, bd512, rope_cos, rope_sin)
    (rq, rk, rv, rsg, dq, dk, dv, hq, hkf, hkb, hv, hgf, hgb, hsg) = proj[:14]
    ret_of, ret_ob, ret_fin = _retention(n_seq, nt, rq, rk, rv, ret_s0, ret_tables,
                                         "retention_lat" if latent else "retention_ctx")
    hg_of, hg_ob, hg_fin = _gated_scan(n_seq, nt, hq, hkf, hkb, hv, hgf, hgb, hg_s0,
                                       "hgrn_scan_lat" if latent else "hgrn_scan_ctx")
    if latent:
        da_o = _diff_attention(layer, n_seq, seq_len, dq, dk, dv, cache_k, cache_v,
                               params['da_lambda'], params['da_subln'])
    else:
        da_o = _diff_attention_short(layer, n_seq, seq_len, dq, dk, dv, params['da_lambda'], params['da_subln'])
    streams = (x, ret_of, ret_ob, rsg, da_o, hg_of, hg_ob, hsg)
    extras = None if latent else (proj[14], proj[15], ret_fin, hg_fin)
    return streams, extras


def _trunk_layer(layer, x_ctx, x_lat, sorted_buf, consts, params, lat_ret_s0, lat_hg_s0, zero_state, cache_k, cache_v):
    mod, bd256 = consts[0], consts[3]
    ctx_streams, extras = _mixers(False, layer, x_ctx, consts, params, zero_state, zero_state, None, None)
    lat_streams, _ = _mixers(True, layer, x_lat, consts, params, lat_ret_s0, lat_hg_s0, cache_k, cache_v)
    x1, hp, route = _out_projection(layer, mod, params['norm2'], ctx_streams, lat_streams, params['hg_norm'],
                                    bd256, params['w_out'], params['w_route_hi'], params['w_route_lo'],
                                    params['b_route'])
    dest, meta = _moe_dispatch(route)
    sorted_buf = _moe_scatter(dest, hp, sorted_buf)
    ys = _moe_ffn(layer, meta, sorted_buf, params['w_gate'], params['w_up'], params['w_down'])
    x_ctx, x_lat = _moe_combine(layer, dest, x1, mod, ys)
    return x_ctx, x_lat, sorted_buf, extras


def kernel(x_prompt, x_sample, cache_k, cache_v, state_ret, state_hgrn, c, c_ctx, norm1, norm2, w_mod, b_mod,
           w_in, w_out, da_qn, da_kn, da_lambda, da_subln, hg_lb, hg_norm, w_group, b_group, w_router,
           b_router, w_gate, w_up, w_down):
    cond = jnp.zeros((N_COND, D_MODEL), F32).at[0].set(c_ctx).at[1:1 + DEC_BATCH].set(c)
    mod, lb_all = _modulation(cond, w_mod, b_mod, hg_lb)
    mod = mod.reshape(DEPTH, N_COND, 6, D_MODEL)
    rope_cos, rope_sin = _rope_tables()
    consts = (mod, lb_all.reshape(DEPTH, 1, MIX_G), _block_diag_ones(DA_W, HEAD_D),
              _block_diag_ones(MIX_G, HEAD_D), rope_cos, rope_sin, _retention_tables())
    pad = jnp.zeros((DEPTH, D_MODEL, 128 - N_EXPERTS - N_GROUPS), F32)
    w_route = jnp.concatenate([w_router, w_group, pad], axis=-1)
    params = {
        'norm1': norm1.reshape(DEPTH, 1, D_MODEL), 'norm2': norm2.reshape(DEPTH, 1, D_MODEL),
        'w_in': w_in.astype(BF16), 'w_out': w_out.astype(BF16),
        'w_vt': jnp.swapaxes(w_in[:, :, 2048:2560].astype(BF16), 1, 2),
        'da_qn': jnp.tile(da_qn, (1, DA_W // HEAD_D)).reshape(DEPTH, 1, DA_W),
        'da_kn': jnp.tile(da_kn, (1, DA_W // HEAD_D)).reshape(DEPTH, 1, DA_W),
        'da_lambda': da_lambda, 'da_subln': da_subln.reshape(DEPTH, 1, 128),
        'hg_norm': jnp.tile(hg_norm, (1, HEADS)).reshape(DEPTH, 1, MIX_G),
        'w_route_hi': w_route.astype(BF16),
        'w_route_lo': (w_route - w_route.astype(BF16).astype(F32)).astype(BF16),
        'b_route': jnp.concatenate([b_router, b_group, pad[:, 0, :]], axis=-1).reshape(DEPTH, 1, 128),
        'w_gate': w_gate.astype(BF16), 'w_up': w_up.astype(BF16), 'w_down': w_down.astype(BF16),
    }
    ck_bf = cache_k.astype(BF16).reshape(DEC_BATCH, DEPTH, PAST_LEN, DA_W)
    cv_bf = jnp.swapaxes(cache_v.astype(BF16).reshape(DEC_BATCH, DEPTH, PAST_LEN, DA_W), 2, 3)
    zero_state = jnp.zeros((BATCH, 2, HEADS, HEAD_D, HEAD_D), F32)

    yp = x_prompt.reshape(N_CTX_ROWS, D_MODEL)
    ys = x_sample.reshape(N_LAT_ROWS, D_MODEL)
    sorted_buf = jnp.zeros((N_SLOTS, ROW_W), F32)
    ks_out, vs_out, rets_out, hgs_out = [], [], [], []
    for l in range(DEPTH):
        yp, ys, sorted_buf, (k_new, v_new, ret_fin, hg_fin) = _trunk_layer(
            l, yp, ys, sorted_buf, consts, params, state_ret[:, l], state_hgrn[:, l], zero_state, ck_bf, cv_bf)
        ks_out.append(k_new.reshape(BATCH, SEQ, DA_W))
        vs_out.append(v_new.reshape(BATCH, SEQ, DA_W))
        rets_out.append(ret_fin)
        hgs_out.append(hg_fin)
    return (yp.reshape(BATCH, SEQ, D_MODEL), ys.reshape(DEC_BATCH, DEC_SEQ, D_MODEL),
            jnp.stack(ks_out, axis=1).reshape(BATCH, DEPTH, SEQ, HEADS, 2, HEAD_D),
            jnp.stack(vs_out, axis=1).reshape(BATCH, DEPTH, SEQ, HEADS, 2 * HEAD_D),
            jnp.stack(rets_out, axis=1), jnp.stack(hgs_out, axis=1))
```

```python
import functools
import math

import numpy as np
import jax
import jax.numpy as jnp
from jax import lax
from jax.experimental import pallas as pl
from jax.experimental.pallas import tpu as pltpu

F32 = jnp.float32
BF16 = jnp.bfloat16

D_MODEL = 1024
BATCH = 16
SEQ = 256
DEPTH = 4
DEC_BATCH = 4
DEC_SEQ = 4096
PAST_LEN = 256
GRID_W = 64
HEADS = 4
HEAD_D = 64
MIX_G = HEADS * HEAD_D
DA_W = 512
IN_COLS = 3840
ROPE_BASE = 10000.0
ROPE_PAIRS = 16
RMS_EPS = 1e-6
N_GROUPS = 4
EPG = 4
N_EXPERTS = 16
EXPERT_FF = 512

TILE = 256
N_CTX_ROWS = BATCH * SEQ
N_LAT_ROWS = DEC_BATCH * DEC_SEQ
N_ROWS = N_CTX_ROWS + N_LAT_ROWS
CTX_TILES = N_CTX_ROWS // TILE
LAT_TILES_PER_SEQ = DEC_SEQ // TILE
N_TILES = N_ROWS // TILE
PTILE = 512
CTX_PTILES = N_CTX_ROWS // PTILE
LAT_PTILES_PER_SEQ = DEC_SEQ // PTILE
N_COND = 8

N_PAIRS = EPG * (EPG - 1) // 2
N_BUCKETS = N_GROUPS * N_PAIRS
MOE_TILE = 256
MOE_MAX_TILES = N_ROWS // MOE_TILE + N_BUCKETS
DISPATCH_TILES = 4
DMA_GROUP = 8
N_SLOTS = MOE_MAX_TILES * MOE_TILE
ROW_W = D_MODEL + 128

CHUNK = 64
SUB = 16
NSUB = CHUNK // SUB
NCHUNK = TILE // CHUNK

VMEM_LIMIT = 48 * 1024 * 1024
NEG_BIG = -1e30
ONES_ROWS = 16
ATTN_TQ = 512
ATTN_KEY_CHUNK = 512
LOG2_E = 1.4426950408889634

HI = lax.Precision.HIGHEST


def _cparams(n_axes):
    return pltpu.CompilerParams(dimension_semantics=("arbitrary",) * n_axes,
                                vmem_limit_bytes=VMEM_LIMIT)


def _dot(a, b):
    return jnp.dot(a, b, preferred_element_type=F32)


def _dot_nt(a, b):
    return lax.dot_general(a, b, (((1,), (1,)), ((), ())), preferred_element_type=F32)


def _dot_tn(a, b):
    return lax.dot_general(a, b, (((0,), (0,)), ((), ())), preferred_element_type=F32)


def _silu(x):
    return x * (1.0 / (1.0 + jnp.exp(-x)))


def _sigmoid(x):
    return 1.0 / (1.0 + jnp.exp(-x))


def _mod_kernel(cond_ref, w_ref, b_ref, lb_ref, mod_ref, lbo_ref):
    c = cond_ref[...]
    m = jnp.dot(_silu(c), w_ref[...], preferred_element_type=F32, precision=HI)
    mod_ref[...] = m + b_ref[...]
    z = lb_ref[...]
    zmax = jnp.max(z, axis=0, keepdims=True)
    e = jnp.exp(z - zmax)
    p = e / jnp.sum(e, axis=0, keepdims=True)
    rows = [jnp.zeros_like(p[0:1])]
    for l in range(1, DEPTH):
        rows.append(rows[-1] + p[l:l + 1])
    lbo_ref[...] = jnp.concatenate(rows, axis=0)


def _modulation(cond, w_mod, b_mod, hg_lb):
    nblk = 6
    return pl.pallas_call(
        _mod_kernel,
        grid=(DEPTH, nblk),
        in_specs=[
            pl.BlockSpec((N_COND, D_MODEL), lambda l, j: (0, 0)),
            pl.BlockSpec((None, D_MODEL, D_MODEL), lambda l, j: (l, 0, j)),
            pl.BlockSpec((None, 1, D_MODEL), lambda l, j: (l, 0, j)),
            pl.BlockSpec((DEPTH, MIX_G), lambda l, j: (0, 0)),
        ],
        out_specs=[
            pl.BlockSpec((None, N_COND, D_MODEL), lambda l, j: (l, 0, j)),
            pl.BlockSpec((DEPTH, MIX_G), lambda l, j: (0, 0)),
        ],
        out_shape=[jax.ShapeDtypeStruct((DEPTH, N_COND, 6 * D_MODEL), F32),
                   jax.ShapeDtypeStruct((DEPTH, MIX_G), F32)],
        compiler_params=_cparams(2),
        name="modulation",
    )(cond, w_mod, b_mod.reshape(DEPTH, 1, 6 * D_MODEL), hg_lb)


def _group_mean_sq(x, bd_ref, group):
    return _dot((x * x).astype(BF16), bd_ref[...]) * (1.0 / group)


def _swap16(x):
    w = x.shape[-1]
    lane = lax.broadcasted_iota(jnp.int32, x.shape, x.ndim - 1)
    up = pltpu.roll(x, w - 16, x.ndim - 1)
    dn = pltpu.roll(x, 16, x.ndim - 1)
    return jnp.where((lane % 32) < 16, up, dn)


def _inproj_kernel(latent, x_ref, mod_ref, n1_ref, w_ref, wvt_ref, qn_ref, kn_ref, lb_ref, bd_ref, *rest):
    if latent:
        cos_ref, sin_ref = rest[0], rest[1]
        outs = rest[2:]
    else:
        outs = rest
    (rq_ref, rk_ref, rv_ref, rsg_ref, dq_ref, dk_ref, dv_ref,
     hq_ref, hkf_ref, hkb_ref, hv_ref, hgf_ref, hgb_ref, hsg_ref) = outs[:14]

    x = x_ref[...]
    ms = jnp.mean(x * x, axis=-1, keepdims=True)
    shift1 = mod_ref[0:1, :]
    scale1 = mod_ref[1:2, :]
    h = x * lax.rsqrt(ms + RMS_EPS) * n1_ref[...] * (1.0 + scale1) + shift1
    h16 = h.astype(BF16)
    y = _dot(h16, w_ref[...])

    rq_ref[...] = y[:, 0:256].astype(BF16)
    rk_ref[...] = (y[:, 256:512] * (HEAD_D ** -0.5)).astype(BF16)
    rv_ref[...] = y[:, 512:768].astype(BF16)
    rsg_ref[...] = _silu(y[:, 768:1024]).astype(BF16)

    dq = y[:, 1024:1536]
    dk = y[:, 1536:2048]
    qn = dq * lax.rsqrt(_group_mean_sq(dq, bd_ref, HEAD_D) + RMS_EPS) * qn_ref[...]
    kn = dk * lax.rsqrt(_group_mean_sq(dk, bd_ref, HEAD_D) + RMS_EPS) * kn_ref[...]
    if latent:
        cos = jnp.concatenate([cos_ref[...]] * 4, axis=1)
        sin = jnp.concatenate([sin_ref[...]] * 4, axis=1)
        qr = qn * cos + _swap16(qn) * sin
        kr = kn * cos + _swap16(kn) * sin
    else:
        qr, kr = qn, kn
        ck_ref, cv_ref = outs[14], outs[15]
        ck_ref[...] = kn
        cv_ref[...] = y[:, 2048:2560]
    dq_ref[...] = (qr * (HEAD_D ** -0.5 * LOG2_E)).astype(BF16)
    dk_ref[...] = kr.astype(BF16)
    dv_ref[...] = _dot_nt(wvt_ref[...], h16).astype(BF16)

    lb = lb_ref[...]
    f_f = lb + (1.0 - lb) * _sigmoid(y[:, 2816:3072])
    f_b = lb + (1.0 - lb) * _sigmoid(y[:, 3072:3328])
    hq_ref[...] = y[:, 2560:2816].astype(BF16)
    hkf_ref[...] = (1.0 - f_f).astype(BF16)
    hkb_ref[...] = (1.0 - f_b).astype(BF16)
    hgf_ref[...] = jnp.log2(f_f)
    hgb_ref[...] = jnp.log2(f_b)
    hv_ref[...] = y[:, 3328:3584].astype(BF16)
    hsg_ref[...] = _silu(y[:, 3584:3840]).astype(BF16)


def _in_projection(latent, layer, x, mod, norm1, w_in_bf, w_vt_bf, qn_t, kn_t, lb_all, bd512, rope_cos, rope_sin):
    nrows = x.shape[0]
    ntiles = nrows // PTILE
    cond = (lambda t: 1 + t // LAT_PTILES_PER_SEQ) if latent else (lambda t: 0)

    in_specs = [
        pl.BlockSpec((PTILE, D_MODEL), lambda t: (t, 0)),
        pl.BlockSpec((None, None, 6, D_MODEL), lambda t: (layer, cond(t), 0, 0)),
        pl.BlockSpec((None, 1, D_MODEL), lambda t: (layer, 0, 0)),
        pl.BlockSpec((None, D_MODEL, IN_COLS), lambda t: (layer, 0, 0)),
        pl.BlockSpec((None, DA_W, D_MODEL), lambda t: (layer, 0, 0)),
        pl.BlockSpec((None, 1, DA_W), lambda t: (layer, 0, 0)),
        pl.BlockSpec((None, 1, DA_W), lambda t: (layer, 0, 0)),
        pl.BlockSpec((None, 1, MIX_G), lambda t: (layer, 0, 0)),
        pl.BlockSpec((DA_W, DA_W), lambda t: (0, 0)),
    ]
    args = [x, mod, norm1, w_in_bf, w_vt_bf, qn_t, kn_t, lb_all, bd512]
    if latent:
        in_specs += [pl.BlockSpec((PTILE, 128), lambda t: (t % LAT_PTILES_PER_SEQ, 0))] * 2
        args += [rope_cos, rope_sin]

    def o(width, dtype):
        return jax.ShapeDtypeStruct((nrows, width), dtype), pl.BlockSpec((PTILE, width), lambda t: (t, 0))

    dv_t = (jax.ShapeDtypeStruct((DA_W, nrows), BF16), pl.BlockSpec((DA_W, PTILE), lambda t: (0, t)))
    outs = [o(256, BF16), o(256, BF16), o(256, BF16), o(256, BF16),
            o(512, BF16), o(512, BF16), dv_t,
            o(256, BF16), o(256, BF16), o(256, BF16), o(256, BF16), o(256, F32), o(256, F32), o(256, BF16)]
    if not latent:
        outs += [o(512, F32), o(512, F32)]
    return pl.pallas_call(
        functools.partial(_inproj_kernel, latent),
        grid=(ntiles,),
        in_specs=in_specs,
        out_specs=[s for _, s in outs],
        out_shape=[s for s, _ in outs],
        compiler_params=_cparams(1),
        name="in_projection_lat" if latent else "in_projection_ctx",
    )(*args)


def _load_states(s0_ref, st_scr, transposed):
    st_scr[...] = jnp.zeros_like(st_scr)
    for d in range(2):
        for h in range(HEADS):
            blk = s0_ref[d, h]
            st_scr[d, h * HEAD_D:(h + 1) * HEAD_D, h * HEAD_D:(h + 1) * HEAD_D] = blk.T if transposed else blk


def _store_states(st_scr, sfin_ref, transposed):
    for d in range(2):
        for h in range(HEADS):
            blk = st_scr[d, h * HEAD_D:(h + 1) * HEAD_D, h * HEAD_D:(h + 1) * HEAD_D]
            sfin_ref[d, h] = blk.T if transposed else blk


def _head_masks():
    lane = lax.broadcasted_iota(jnp.int32, (1, MIX_G), 1)
    return [(lane // HEAD_D == h).astype(F32) for h in range(HEADS)]


def _scan_chunks(items, st_f, st_b, consts_f, consts_b):
    n = len(items)
    cs = [consts_b if it[4] else consts_f for it in items]
    zero_row = jnp.zeros((1, MIX_G), F32)

    bs = []
    for x in range(n):
        g = items[x][3]
        g_hi = g.astype(BF16)
        g_lo = (g - g_hi.astype(F32)).astype(BF16)
        bs.append(_dot(cs[x][0], g_hi) + _dot(cs[x][0], g_lo))

    prep = []
    for x in range(n):
        q, k, v, _, rev = items[x]
        hmask = cs[x][4]
        b = bs[x]
        r_rows, e_rows = [], []
        for s in range(NSUB):
            lo, hi = s * SUB, s * SUB + SUB - 1
            if not rev:
                r_rows.append(b[lo - 1:lo, :] if s > 0 else zero_row)
                e_rows.append(b[hi:hi + 1, :])
            else:
                r_rows.append(b[hi + 1:hi + 2, :] if s < NSUB - 1 else zero_row)
                e_rows.append(b[lo:lo + 1, :])
        btot = e_rows[NSUB - 1] if not rev else e_rows[0]
        r_full = jnp.concatenate([jnp.broadcast_to(r, (SUB, MIX_G)) for r in r_rows], axis=0)
        e_full = jnp.concatenate([jnp.broadcast_to(e, (SUB, MIX_G)) for e in e_rows], axis=0)
        bl = b - r_full
        qp = q * jnp.exp2(bl)
        kend = k * jnp.exp2(e_full - b)
        pairs = [(i, j) for i in range(NSUB) for j in range(NSUB) if (j < i if not rev else j > i)]
        lh = jnp.concatenate([qp[i * SUB:(i + 1) * SUB, :] * jnp.exp2(r_rows[i] - e_rows[j])
                              for (i, j) in pairs], axis=0)
        prep.append(dict(
            bl=bl, btot=btot, pairs=pairs,
            q_inter=(qp * jnp.exp2(r_full)).astype(BF16),
            k_state=(kend * jnp.exp2(btot - e_full)).astype(BF16),
            kend=kend.astype(BF16),
            lhs4=jnp.concatenate([lh * hmask[h] for h in range(HEADS)], axis=0).astype(BF16)))

    uts = [_dot_tn(items[x][2], prep[x]['k_state']) for x in range(n)]
    scs = [_dot_nt(prep[x]['lhs4'], prep[x]['kend']) for x in range(n)]

    st_in = []
    for x in range(n):
        rev = items[x][4]
        st = st_b if rev else st_f
        st_in.append(st.astype(BF16))
        st = st * jnp.exp2(prep[x]['btot']) + uts[x] * cs[x][1]
        if rev:
            st_b = st
        else:
            st_f = st
    col = lax.broadcasted_iota(jnp.int32, (SUB, CHUNK), 1) // SUB
    jrow = lax.broadcasted_iota(jnp.int32, (SUB, MIX_G), 0)
    pss, pps, vts = [], [], []
    for x in range(n):
        q, k, v, _, rev = items[x]
        pairs, sc, bl = prep[x]['pairs'], scs[x], prep[x]['bl']
        p_rows = []
        for h in range(HEADS):
            for i in range(NSUB):
                acc = None
                for p, (pi, pj) in enumerate(pairs):
                    if pi != i:
                        continue
                    base = (h * len(pairs) + p) * SUB
                    blk = jnp.where(col == pj, sc[base:base + SUB, :], 0.0)
                    acc = blk if acc is None else acc + blk
                p_rows.append(jnp.zeros((SUB, CHUNK), F32) if acc is None else acc)
        pss.append(jnp.concatenate(p_rows, axis=0).astype(BF16))
        pp, vt = [], []
        for s in range(NSUB):
            sl = slice(s * SUB, (s + 1) * SUB)
            bl_s, q_s, k_s = bl[sl, :], q[sl, :].astype(BF16), k[sl, :].astype(BF16)
            for i in range(SUB):
                d = bl_s[i:i + 1, :] - bl_s
                keep = (jrow <= i) if not rev else (jrow >= i)
                e = jnp.exp2(jnp.where(keep, d, NEG_BIG).astype(BF16))
                pp.append(e * q_s[i:i + 1, :] * k_s)
            vt.extend([v[sl, :]] * SUB)
        pps.append(jnp.concatenate(pp, axis=0))
        vts.append(jnp.concatenate(vt, axis=0))

    o_int = [_dot_nt(prep[x]['q_inter'], st_in[x]) for x in range(n)]
    fulls = [_dot(pss[x], items[x][2]) for x in range(n)]
    sbs = [_dot(pps[x], cs[x][2]) for x in range(n)]

    ws = [(sbs[x] * vts[x].astype(F32)).astype(BF16) for x in range(n)]
    o_diag = [_dot(cs[x][3], ws[x]) for x in range(n)]

    outs = []
    for x in range(n):
        hmask = cs[x][4]
        o = o_int[x] + o_diag[x]
        for h in range(HEADS):
            o = o + fulls[x][h * CHUNK:(h + 1) * CHUNK, :] * hmask[h]
        outs.append(o)
    return outs, st_f, st_b


def _scan_consts(rev):
    r = lax.broadcasted_iota(jnp.int32, (CHUNK, CHUNK), 0)
    c = lax.broadcasted_iota(jnp.int32, (CHUNK, CHUNK), 1)
    tri = ((c <= r) if not rev else (c >= r)).astype(BF16)
    rr = lax.broadcasted_iota(jnp.int32, (MIX_G, MIX_G), 0) // HEAD_D
    cc = lax.broadcasted_iota(jnp.int32, (MIX_G, MIX_G), 1) // HEAD_D
    bd_mask = (rr == cc).astype(F32)
    ones_bd = bd_mask.astype(BF16)
    gi = lax.broadcasted_iota(jnp.int32, (CHUNK, CHUNK * SUB), 0)
    gj = lax.broadcasted_iota(jnp.int32, (CHUNK, CHUNK * SUB), 1) // SUB
    gsel = (gi == gj).astype(BF16)
    return tri, bd_mask, ones_bd, gsel, _head_masks()


def _scan_kernel(nt, qf_ref, kf_ref, vf_ref, gf_ref, qb_ref, kb_ref, vb_ref, gb_ref, s0_ref,
                 of_ref, ob_ref, sfin_ref, st_scr):
    t = pl.program_id(1)

    @pl.when(t == 0)
    def _():
        _load_states(s0_ref, st_scr, True)

    cf = _scan_consts(False)
    cb = _scan_consts(True)

    items, rows = [], []
    for c in range(NCHUNK):
        rf = slice(c * CHUNK, (c + 1) * CHUNK)
        rb = slice((NCHUNK - 1 - c) * CHUNK, (NCHUNK - c) * CHUNK)
        items.append((qf_ref[rf, :].astype(F32), kf_ref[rf, :].astype(F32), vf_ref[rf, :], gf_ref[rf, :], False))
        items.append((qb_ref[rb, :].astype(F32), kb_ref[rb, :].astype(F32), vb_ref[rb, :], gb_ref[rb, :], True))
        rows += [(of_ref, rf), (ob_ref, rb)]
    outs, st_f, st_b = _scan_chunks(items, st_scr[0], st_scr[1], cf, cb)
    for (ref, sl), o in zip(rows, outs):
        ref[sl, :] = o
    st_scr[0] = st_f
    st_scr[1] = st_b

    @pl.when(t == nt - 1)
    def _():
        _store_states(st_scr, sfin_ref, True)


def _gated_scan(n_seq, nt, q, k_f, k_b, v, g_f, g_b, s0, name):
    fwd = lambda s, t: (s * nt + t, 0)
    bwd = lambda s, t: (s * nt + (nt - 1 - t), 0)
    blk = lambda im: pl.BlockSpec((TILE, MIX_G), im)
    in_specs = [blk(fwd), blk(fwd), blk(fwd), blk(fwd), blk(bwd), blk(bwd), blk(bwd), blk(bwd),
                pl.BlockSpec((None, 2, HEADS, HEAD_D, HEAD_D), lambda s, t: (s, 0, 0, 0, 0))]
    nrows = n_seq * nt * TILE
    return pl.pallas_call(
        functools.partial(_scan_kernel, nt),
        grid=(n_seq, nt),
        in_specs=in_specs,
        out_specs=[pl.BlockSpec((TILE, MIX_G), lambda s, t: (s * nt + t, 0)),
                   pl.BlockSpec((TILE, MIX_G), lambda s, t: (s * nt + (nt - 1 - t), 0)),
                   pl.BlockSpec((None, 2, HEADS, HEAD_D, HEAD_D), lambda s, t: (s, 0, 0, 0, 0))],
        out_shape=[jax.ShapeDtypeStruct((nrows, MIX_G), F32),
                   jax.ShapeDtypeStruct((nrows, MIX_G), F32),
                   jax.ShapeDtypeStruct((n_seq, 2, HEADS, HEAD_D, HEAD_D), F32)],
        scratch_shapes=[pltpu.VMEM((2, MIX_G, MIX_G), F32)],
        compiler_params=_cparams(2),
        name=name,
    )(q, k_f, v, g_f, q, k_b, v, g_b, s0)


def _retention_tables():
    gam = 1.0 - 2.0 ** (-5.0 - np.arange(HEADS, dtype=np.float64))
    gam_r = gam[::-1]
    i = np.arange(TILE, dtype=np.float64)
    diff = i[:, None] - i[None, :]
    dcomb = np.zeros((HEADS, TILE, TILE))
    for h in range(HEADS):
        lower = np.where(diff > 0, gam[h] ** np.maximum(diff, 0), 0.0)
        upper = np.where(diff < 0, gam_r[h] ** np.maximum(-diff, 0), 0.0)
        dcomb[h] = lower + upper + 2.0 * (diff == 0)
    lanes = lambda per_head: np.repeat(per_head, HEAD_D, axis=-1)
    qd_f = lanes(gam[None, :] ** (i[:, None] + 1.0))
    kd_f = lanes(gam[None, :] ** (TILE - 1.0 - i[:, None]))
    qd_b = lanes(gam_r[None, :] ** (TILE - i[:, None]))
    kd_b = lanes(gam_r[None, :] ** i[:, None])
    blk = (np.arange(MIX_G)[:, None] // HEAD_D) == (np.arange(MIX_G)[None, :] // HEAD_D)
    c_f = np.where(blk, lanes(gam ** TILE)[None, :], 0.0) * np.ones((MIX_G, 1))
    c_b = np.where(blk, lanes(gam_r ** TILE)[None, :], 0.0) * np.ones((MIX_G, 1))
    f = lambda a: jnp.asarray(a, F32)
    return (f(dcomb.reshape(HEADS * TILE, TILE)), f(qd_f), f(kd_f), f(qd_b), f(kd_b), f(c_f), f(c_b))


def _ret_kernel(nt, qf_ref, kf_ref, vf_ref, qb_ref, kb_ref, vb_ref, s0_ref, dcomb_ref, qdf_ref, kdf_ref,
                qdb_ref, kdb_ref, cf_ref, cb_ref, of_ref, ob_ref, sfin_ref, st_scr):
    t = pl.program_id(1)

    @pl.when(t == 0)
    def _():
        _load_states(s0_ref, st_scr, False)

    lane_head = lax.broadcasted_iota(jnp.int32, (TILE, MIX_G), 1) // HEAD_D
    rr = lax.broadcasted_iota(jnp.int32, (MIX_G, MIX_G), 0) // HEAD_D
    cc = lax.broadcasted_iota(jnp.int32, (MIX_G, MIX_G), 1) // HEAD_D
    same_head = rr == cc

    q = qf_ref[...]
    k = kf_ref[...]
    v = vf_ref[...]
    qs = jnp.concatenate([jnp.where(lane_head == h, q, jnp.zeros_like(q)) for h in range(HEADS)], axis=0)
    p = (_dot_nt(qs, k) * dcomb_ref[...]).astype(BF16)
    full = _dot(p, v)
    o = _dot((q.astype(F32) * qdf_ref[...]).astype(BF16), st_scr[0].astype(BF16))
    for h in range(HEADS):
        o = o + jnp.where(lane_head == h, full[h * TILE:(h + 1) * TILE, :], 0.0)
    of_ref[...] = o
    u = _dot_tn((k.astype(F32) * kdf_ref[...]).astype(BF16), v)
    st_scr[0] = st_scr[0] * cf_ref[...] + jnp.where(same_head, u, 0.0)

    qb = qb_ref[...]
    kb = kb_ref[...]
    vb = vb_ref[...]
    ob_ref[...] = _dot((qb.astype(F32) * qdb_ref[...]).astype(BF16), st_scr[1].astype(BF16))
    ub = _dot_tn((kb.astype(F32) * kdb_ref[...]).astype(BF16), vb)
    st_scr[1] = st_scr[1] * cb_ref[...] + jnp.where(same_head, ub, 0.0)

    @pl.when(t == nt - 1)
    def _():
        _store_states(st_scr, sfin_ref, False)


def _retention(n_seq, nt, q, k, v, s0, tables, name):
    fwd = lambda s, t: (s * nt + t, 0)
    bwd = lambda s, t: (s * nt + (nt - 1 - t), 0)
    const = lambda s, t: (0, 0)
    blk = lambda im: pl.BlockSpec((TILE, MIX_G), im)
    sq = lambda: pl.BlockSpec((MIX_G, MIX_G), const)
    in_specs = [blk(fwd), blk(fwd), blk(fwd), blk(bwd), blk(bwd), blk(bwd),
                pl.BlockSpec((None, 2, HEADS, HEAD_D, HEAD_D), lambda s, t: (s, 0, 0, 0, 0)),
                pl.BlockSpec((HEADS * TILE, TILE), const), sq(), sq(), sq(), sq(), sq(), sq()]
    nrows = n_seq * nt * TILE
    return pl.pallas_call(
        functools.partial(_ret_kernel, nt),
        grid=(n_seq, nt),
        in_specs=in_specs,
        out_specs=[blk(fwd), blk(bwd), pl.BlockSpec((None, 2, HEADS, HEAD_D, HEAD_D), lambda s, t: (s, 0, 0, 0, 0))],
        out_shape=[jax.ShapeDtypeStruct((nrows, MIX_G), F32),
                   jax.ShapeDtypeStruct((nrows, MIX_G), F32),
                   jax.ShapeDtypeStruct((n_seq, 2, HEADS, HEAD_D, HEAD_D), F32)],
        scratch_shapes=[pltpu.VMEM((2, MIX_G, MIX_G), F32)],
        compiler_params=_cparams(2),
        name=name,
    )(q, k, v, q, k, v, s0, *tables)


def _attn_step(lam_init, comp, refs, s_write, s_read, part0_ref):
    q_ref, k_ref, ck_ref, v_ref, cv_ref, lam_ref, sub_ref, o_ref = refs
    sw, sw2, mw = s_write
    sr, sr2, mr = s_read

    q = q_ref[...]
    lane = lax.broadcasted_iota(jnp.int32, q.shape, 1)
    qc = jnp.where((lane // HEAD_D) == comp, q, jnp.zeros_like(q))
    n_keys = k_ref.shape[0]
    kc = min(ATTN_KEY_CHUNK, n_keys)
    ones_rows = jnp.ones((ONES_ROWS, kc), BF16)
    m_prev = mr[...]
    oe, m_cur = None, None
    for j in range(n_keys // kc):
        ks = slice(j * kc, (j + 1) * kc)
        d = _dot(jnp.concatenate([v_ref[:, ks], ones_rows], axis=0),
                 jnp.exp2((sr[ks, :] - m_prev).astype(BF16)))
        oe = d if oe is None else oe + d
        s = _dot_nt(k_ref[ks, :], qc)
        sw[ks, :] = s
        mj = jnp.max(s, axis=0, keepdims=True)
        m_cur = mj if m_cur is None else jnp.maximum(m_cur, mj)
    oe = oe + _dot(jnp.concatenate([cv_ref[...], jnp.ones((ONES_ROWS, cv_ref.shape[1]), BF16)], axis=0),
                   jnp.exp2((sr2[...] - m_prev).astype(BF16)))
    s2 = _dot_nt(ck_ref[...], qc)
    sw2[...] = s2
    m_cur = jnp.maximum(m_cur, jnp.max(s2, axis=0, keepdims=True))
    mw[...] = m_cur
    part = oe[:128, :] / oe[128:129, :]
    if comp == 1:
        part0_ref[...] = part
    else:
        lp = lam_ref[...]
        lam = (jnp.exp(jnp.sum(lp[0:1] * lp[1:2], axis=-1, keepdims=True))
               - jnp.exp(jnp.sum(lp[2:3] * lp[3:4], axis=-1, keepdims=True)) + lam_init)
        o = part0_ref[...] - lam * part
        ms = jnp.mean(o * o, axis=0, keepdims=True)
        o = (o * lax.rsqrt(ms + RMS_EPS)).T
        o_ref[...] = (o * sub_ref[...] * (1.0 - lam_init)).astype(BF16)


def _attn_kernel(lam_init, *refs):
    n_in = 8
    io, part0_ref, scr = refs[:n_in], refs[n_in], refs[n_in + 1:]
    half = len(scr) // 2
    buf_a, buf_b = scr[:half], scr[half:]
    u = pl.program_id(0)

    @pl.when(u == 0)
    def _():
        for r in buf_b + (part0_ref,):
            r[...] = jnp.zeros(r.shape, r.dtype)

    @pl.when(u % 2 == 0)
    def _():
        _attn_step(lam_init, 0, io, buf_a, buf_b, part0_ref)

    @pl.when(u % 2 == 1)
    def _():
        _attn_step(lam_init, 1, io, buf_b, buf_a, part0_ref)


def _attn_short_kernel(lam_init, q_ref, k_ref, v_ref, lam_ref, sub_ref, o_ref):
    lp = lam_ref[...]
    lam = (jnp.exp(jnp.sum(lp[0:1] * lp[1:2], axis=-1, keepdims=True))
           - jnp.exp(jnp.sum(lp[2:3] * lp[3:4], axis=-1, keepdims=True)) + lam_init)
    n_keys = k_ref.shape[0]
    ones_rows = jnp.ones((ONES_ROWS, n_keys), BF16)
    lane = lax.broadcasted_iota(jnp.int32, (q_ref.shape[0], 128), 1)
    units = [(h, c) for h in range(HEADS) for c in range(2)]
    scores = []
    for h, c in units:
        q = q_ref[:, h * 128:(h + 1) * 128]
        qc = jnp.where((lane // HEAD_D) == c, q, jnp.zeros_like(q))
        scores.append(_dot_nt(k_ref[:, h * 128:(h + 1) * 128], qc))
    probs = [jnp.exp2((s - jnp.max(s, axis=0, keepdims=True)).astype(BF16)) for s in scores]
    oes = [_dot(jnp.concatenate([v_ref[h * 128:(h + 1) * 128, :], ones_rows], axis=0), p)
           for (h, c), p in zip(units, probs)]
    parts = [oe[:128, :] / oe[128:129, :] for oe in oes]
    outs = []
    for h in range(HEADS):
        o = parts[2 * h] - lam * parts[2 * h + 1]
        ms = jnp.mean(o * o, axis=0, keepdims=True)
        outs.append((o * lax.rsqrt(ms + RMS_EPS)).T * sub_ref[...] * (1.0 - lam_init))
    o_ref[...] = jnp.concatenate(outs, axis=1).astype(BF16)


def _diff_attention_short(layer, n_seq, seq_len, q, k, v_t, da_lambda, subln):
    lam_init = 0.8 - 0.6 * math.exp(-0.3 * layer)
    return pl.pallas_call(
        functools.partial(_attn_short_kernel, lam_init),
        grid=(n_seq,),
        in_specs=[pl.BlockSpec((seq_len, DA_W), lambda b: (b, 0)),
                  pl.BlockSpec((seq_len, DA_W), lambda b: (b, 0)),
                  pl.BlockSpec((DA_W, seq_len), lambda b: (0, b)),
                  pl.BlockSpec((None, 4, HEAD_D), lambda b: (layer, 0, 0)),
                  pl.BlockSpec((None, 1, 128), lambda b: (layer, 0, 0))],
        out_specs=pl.BlockSpec((seq_len, DA_W), lambda b: (b, 0)),
        out_shape=jax.ShapeDtypeStruct((n_seq * seq_len, DA_W), BF16),
        compiler_params=_cparams(1),
        name="diff_attention_ctx",
    )(q, k, v_t, da_lambda, subln)


def _diff_attention(layer, n_seq, seq_len, q, k, v_t, cache_k, cache_v_t, da_lambda, subln):
    tq = min(ATTN_TQ, seq_len)
    nq = seq_len // tq
    n_units = n_seq * HEADS * nq * 2
    lam_init = 0.8 - 0.6 * math.exp(-0.3 * layer)

    cur = lambda u: jnp.minimum(u, n_units - 1) // 2
    prev = lambda u: jnp.maximum(u - 1, 0) // 2
    seq_of = lambda w: w // (HEADS * nq)
    head_of = lambda w: (w // nq) % HEADS
    rows_of = lambda w: seq_of(w) * nq + w % nq

    q_spec = pl.BlockSpec((tq, 128), lambda u: (rows_of(cur(u)), head_of(cur(u))))
    k_spec = pl.BlockSpec((seq_len, 128), lambda u: (seq_of(cur(u)), head_of(cur(u))))
    v_spec = pl.BlockSpec((128, seq_len), lambda u: (head_of(prev(u)), seq_of(prev(u))))
    ck_spec = pl.BlockSpec((None, None, PAST_LEN, 128), lambda u: (seq_of(cur(u)), layer, 0, head_of(cur(u))))
    cv_spec = pl.BlockSpec((None, None, 128, PAST_LEN), lambda u: (seq_of(prev(u)), layer, head_of(prev(u)), 0))
    tail = [pl.BlockSpec((None, 4, HEAD_D), lambda u: (layer, 0, 0)),
            pl.BlockSpec((None, 1, 128), lambda u: (layer, 0, 0))]
    in_specs = [q_spec, k_spec, ck_spec, v_spec, cv_spec] + tail
    args = [q, k, cache_k, v_t, cache_v_t, da_lambda, subln]
    scratch = [pltpu.VMEM((128, tq), F32)] + [pltpu.VMEM((seq_len, tq), F32), pltpu.VMEM((PAST_LEN, tq), F32),
                                              pltpu.VMEM((1, tq), F32)] * 2
    return pl.pallas_call(
        functools.partial(_attn_kernel, lam_init),
        grid=(n_units + 1,),
        in_specs=in_specs,
        out_specs=pl.BlockSpec((tq, 128), lambda u: (rows_of(prev(u)), head_of(prev(u)))),
        out_shape=jax.ShapeDtypeStruct((n_seq * seq_len, DA_W), BF16),
        scratch_shapes=scratch,
        compiler_params=_cparams(1),
        name="diff_attention_lat",
    )(*args)


def _route(logits):
    lane = lax.broadcasted_iota(jnp.int32, logits.shape, 1)
    lane_f = lane.astype(F32)
    big = 1e9
    is_g = jnp.logical_and(lane >= N_EXPERTS, lane < N_EXPERTS + N_GROUPS)
    gl = jnp.where(is_g, logits, NEG_BIG)
    gmax = jnp.max(gl, axis=-1, keepdims=True)
    gsel = jnp.min(jnp.where(gl == gmax, lane_f, big), axis=-1, keepdims=True) - N_EXPERTS
    g_w = 1.0 / jnp.sum(jnp.exp(gl - gmax), axis=-1, keepdims=True)
    in_grp = jnp.logical_and(lane < N_EXPERTS, (lane // EPG).astype(F32) == gsel)
    el = jnp.where(in_grp, logits, NEG_BIG)
    v1 = jnp.max(el, axis=-1, keepdims=True)
    i1 = jnp.min(jnp.where(el == v1, lane_f, big), axis=-1, keepdims=True)
    el2 = jnp.where(lane_f == i1, NEG_BIG, el)
    v2 = jnp.max(el2, axis=-1, keepdims=True)
    i2 = jnp.min(jnp.where(el2 == v2, lane_f, big), axis=-1, keepdims=True)
    t = jnp.exp(v2 - v1)
    w1 = g_w / (1.0 + t)
    w2 = w1 * t
    first_lo = i1 < i2
    a = jnp.minimum(i1, i2) - EPG * gsel
    b = jnp.maximum(i1, i2) - EPG * gsel
    bucket = gsel * N_PAIRS + a * (7.0 - a) * 0.5 + (b - a - 1.0)
    w_lo = jnp.where(first_lo, w1, w2)
    w_hi = jnp.where(first_lo, w2, w1)
    return jnp.where(lane == 0, bucket, jnp.where(lane == 1, w_lo, jnp.where(lane == 2, w_hi, 0.0)))


def _outproj_body(mod_ref, n2_ref, streams, hgn_ref, bd_ref, wout_ref, wrh_ref, wrl_ref, br_ref,
                  x1_ref, hp_ref, route_ref):
    x_ref, rof_ref, rob_ref, rsg_ref, da_ref, hof_ref, hob_ref, hsg_ref = streams
    ro = rof_ref[...] + rob_ref[...]
    ro = ro * lax.rsqrt(_group_mean_sq(ro, bd_ref, HEAD_D) + RMS_EPS) * rsg_ref[...].astype(F32)
    ho = hof_ref[...] + hob_ref[...]
    ho = ho * lax.rsqrt(_group_mean_sq(ho, bd_ref, HEAD_D) + RMS_EPS) * hgn_ref[...] * hsg_ref[...].astype(F32)
    mix = jnp.concatenate([ro.astype(BF16), da_ref[...], ho.astype(BF16)], axis=1)
    mixed = _dot(mix, wout_ref[...])
    gate1 = mod_ref[2:3, :]
    shift2 = mod_ref[3:4, :]
    scale2 = mod_ref[4:5, :]
    x1 = x_ref[...] + gate1 * mixed
    ms = jnp.mean(x1 * x1, axis=-1, keepdims=True)
    h2 = x1 * lax.rsqrt(ms + RMS_EPS) * n2_ref[...] * (1.0 + scale2) + shift2
    h_hi = h2.astype(BF16)
    h_lo = (h2 - h_hi.astype(F32)).astype(BF16)
    logits = _dot(h_hi, wrh_ref[...]) + _dot(h_lo, wrh_ref[...]) + _dot(h_hi, wrl_ref[...]) + br_ref[...]
    route = _route(logits)
    x1_ref[...] = x1
    route_ref[...] = route
    hp_ref[...] = jnp.concatenate([h2, route], axis=1)


def _outproj_kernel(mod_ref, n2_ref, *rest):
    ctx_streams, lat_streams, tail = rest[0:8], rest[8:16], rest[16:]
    t = pl.program_id(0)

    @pl.when(t < CTX_PTILES)
    def _():
        _outproj_body(mod_ref, n2_ref, ctx_streams, *tail)

    @pl.when(t >= CTX_PTILES)
    def _():
        _outproj_body(mod_ref, n2_ref, lat_streams, *tail)


def _cond_of_tile(t):
    return jnp.where(t < CTX_TILES, 0, 1 + (t - CTX_TILES) // LAT_TILES_PER_SEQ)


def _out_projection(layer, mod, norm2, ctx_streams, lat_streams, hgn_t, bd256, w_out_bf, w_route_hi, w_route_lo,
                    b_route):
    row = lambda t: (t, 0)
    ctx_row = lambda t: (jnp.minimum(t, CTX_PTILES - 1), 0)
    lat_row = lambda t: (jnp.maximum(t - CTX_PTILES, 0), 0)
    lay = lambda t: (layer, 0, 0)
    cond = lambda t: jnp.where(t < CTX_PTILES, 0, 1 + (t - CTX_PTILES) // LAT_PTILES_PER_SEQ)
    widths = (D_MODEL, MIX_G, MIX_G, MIX_G, DA_W, MIX_G, MIX_G, MIX_G)
    in_specs = [
        pl.BlockSpec((None, None, 6, D_MODEL), lambda t: (layer, cond(t), 0, 0)),
        pl.BlockSpec((None, 1, D_MODEL), lay),
    ]
    in_specs += [pl.BlockSpec((PTILE, w), ctx_row) for w in widths]
    in_specs += [pl.BlockSpec((PTILE, w), lat_row) for w in widths]
    in_specs += [
        pl.BlockSpec((None, 1, MIX_G), lay),
        pl.BlockSpec((MIX_G, MIX_G), lambda t: (0, 0)),
        pl.BlockSpec((None, D_MODEL, D_MODEL), lay),
        pl.BlockSpec((None, D_MODEL, 128), lay),
        pl.BlockSpec((None, D_MODEL, 128), lay),
        pl.BlockSpec((None, 1, 128), lay),
    ]
    return pl.pallas_call(
        _outproj_kernel,
        grid=(N_ROWS // PTILE,),
        in_specs=in_specs,
        out_specs=[pl.BlockSpec((PTILE, D_MODEL), row), pl.BlockSpec((PTILE, ROW_W), row),
                   pl.BlockSpec((PTILE, 128), row)],
        out_shape=[jax.ShapeDtypeStruct((N_ROWS, D_MODEL), F32), jax.ShapeDtypeStruct((N_ROWS, ROW_W), F32),
                   jax.ShapeDtypeStruct((N_ROWS, 128), F32)],
        compiler_params=_cparams(1),
        name="out_projection",
    )(mod, norm2, *ctx_streams, *lat_streams, hgn_t, bd256, w_out_bf, w_route_hi, w_route_lo, b_route)


def _dispatch_kernel(route_ref, dest_ref, meta_ref, cnt_ref):
    ph = pl.program_id(0)
    t = pl.program_id(1)
    lane = lax.broadcasted_iota(jnp.int32, (TILE, 128), 1).astype(F32)
    onehots = [(lane == route_ref[j * TILE:(j + 1) * TILE, 0:1]).astype(F32) for j in range(DISPATCH_TILES)]
    tile_cnts = [jnp.sum(oh, axis=0, keepdims=True) for oh in onehots]

    @pl.when(jnp.logical_and(ph == 0, t == 0))
    def _():
        cnt_ref[...] = jnp.zeros_like(cnt_ref)

    @pl.when(ph == 0)
    def _():
        cnt_ref[0:1, :] += sum(tile_cnts[1:], tile_cnts[0])

    @pl.when(jnp.logical_and(ph == 1, t == 0))
    def _():
        cnt = cnt_ref[0:1, :]
        padded = jnp.floor((cnt + (MOE_TILE - 1.0)) * (1.0 / MOE_TILE)) * MOE_TILE
        r = lax.broadcasted_iota(jnp.int32, (128, 128), 0)
        c = lax.broadcasted_iota(jnp.int32, (128, 128), 1)
        off = jnp.dot(jnp.broadcast_to(padded, (8, 128)), (r < c).astype(F32),
                      preferred_element_type=F32, precision=HI)[0:1, :]
        cnt_ref[1:2, :] = off
        end = off + padded
        end_col = jnp.sum(jnp.where(r == c, jnp.broadcast_to(end, (128, 128)), 0.0), axis=1, keepdims=True)
        ended = jnp.logical_and(end_col <= c.astype(F32) * MOE_TILE, r < N_BUCKETS)
        tile_bucket = jnp.sum(ended.astype(F32), axis=0, keepdims=True)
        lane1 = lax.broadcasted_iota(jnp.int32, (1, 128), 1)
        tile_idx = lane1.astype(F32)
        n_valid = jnp.max(jnp.where(lane1 < N_BUCKETS, end, 0.0), axis=1, keepdims=True) * (1.0 / MOE_TILE)
        last = jnp.max(jnp.where(tile_idx < n_valid, tile_bucket, 0.0), axis=1, keepdims=True)
        tb = jnp.where(tile_idx < n_valid, tile_bucket, last)
        g = jnp.floor(tb * (1.0 / N_PAIRS))
        pid = tb - g * N_PAIRS
        a = (pid >= 3.0).astype(F32) + (pid >= 5.0).astype(F32)
        b = pid + 1.0 - 2.0 * (a >= 1.0).astype(F32) - (a >= 2.0).astype(F32)
        rows = [EPG * g + a, EPG * g + b, jnp.broadcast_to(n_valid, (1, 128)), jnp.zeros((5, 128), F32)]
        meta_ref[...] = jnp.concatenate(rows, axis=0).astype(jnp.int32)

    @pl.when(ph == 1)
    def _():
        base = cnt_ref[1:2, :]
        rr = lax.broadcasted_iota(jnp.int32, (TILE, TILE), 0)
        cc = lax.broadcasted_iota(jnp.int32, (TILE, TILE), 1)
        for j, onehot in enumerate(onehots):
            oh16 = onehot.astype(BF16)
            same = _dot_nt(oh16, oh16)
            rank = jnp.sum(jnp.where(rr < cc, same, 0.0), axis=0, keepdims=True)
            base_row = lax.dot_general(jnp.broadcast_to(base, (8, 128)), onehot, (((1,), (1,)), ((), ())),
                                       preferred_element_type=F32, precision=HI)[0:1, :]
            dest_ref[j] = (base_row + rank).astype(jnp.int32)
            base = base + tile_cnts[j]
        cnt_ref[1:2, :] = base


def _moe_dispatch(route):
    dest, meta = pl.pallas_call(
        _dispatch_kernel,
        grid=(2, N_TILES // DISPATCH_TILES),
        in_specs=[pl.BlockSpec((DISPATCH_TILES * TILE, 128), lambda ph, t: (t, 0))],
        out_specs=[pl.BlockSpec((DISPATCH_TILES, 1, TILE), lambda ph, t: (t * ph, 0, 0)),
                   pl.BlockSpec((8, 128), lambda ph, t: (0, 0))],
        out_shape=[jax.ShapeDtypeStruct((N_TILES, 1, TILE), jnp.int32),
                   jax.ShapeDtypeStruct((8, 128), jnp.int32)],
        scratch_shapes=[pltpu.VMEM((8, 128), F32)],
        compiler_params=_cparams(2),
        name="moe_dispatch",
    )(route)
    return dest.reshape(N_ROWS), meta[0:3].reshape(3 * 128)


def _dma_params():
    return pltpu.CompilerParams(dimension_semantics=("arbitrary",), vmem_limit_bytes=VMEM_LIMIT,
                                disable_bounds_checks=True)


def _moe_scatter_kernel(dest_ref, h_ref, hs_init_ref, hs_ref, stage, sems):
    del hs_init_ref
    t = pl.program_id(0)
    base = t * TILE

    def wait_tile(s):
        pltpu.make_async_copy(stage.at[s], hs_ref.at[pl.ds(0, TILE)], sems.at[s]).wait()

    def step(slot):
        stage[slot] = h_ref[...]

        def issue(g, carry):
            row0 = pl.multiple_of(g * DMA_GROUP, DMA_GROUP)
            for j in range(DMA_GROUP):
                d = dest_ref[base + row0 + j]
                pltpu.make_async_copy(stage.at[slot, pl.ds(row0 + j, 1)], hs_ref.at[pl.ds(d, 1)],
                                      sems.at[slot]).start(priority=j % 2)
            return carry

        lax.fori_loop(0, TILE // DMA_GROUP, issue, 0)

        @pl.when(t > 0)
        def _():
            wait_tile(1 - slot)

        @pl.when(t == pl.num_programs(0) - 1)
        def _():
            wait_tile(slot)

    for parity in range(2):
        pl.when(t % 2 == parity)(functools.partial(step, parity))


def _moe_scatter(dest, hp, hs_init):
    return pl.pallas_call(
        _moe_scatter_kernel,
        grid_spec=pltpu.PrefetchScalarGridSpec(
            num_scalar_prefetch=1,
            grid=(N_TILES,),
            in_specs=[pl.BlockSpec((TILE, ROW_W), lambda t, dest: (t, 0)),
                      pl.BlockSpec(memory_space=pl.ANY)],
            out_specs=pl.BlockSpec(memory_space=pl.ANY),
            scratch_shapes=[pltpu.VMEM((2, TILE, ROW_W), F32), pltpu.SemaphoreType.DMA((2,))],
        ),
        out_shape=jax.ShapeDtypeStruct((N_SLOTS, ROW_W), F32),
        input_output_aliases={2: 0},
        compiler_params=_dma_params(),
        name="moe_scatter",
    )(dest, hp, hs_init)


def _moe_ffn_kernel(meta_ref, hs_ref, wgl_ref, wul_ref, wdl_ref, wgh_ref, wuh_ref, wdh_ref, ys_ref):
    t = pl.program_id(0)
    n_valid = meta_ref[2 * 128]

    @pl.when(t < n_valid)
    def _():
        w = hs_ref[...]
        x = w[:, :D_MODEL].astype(BF16)
        rw = w[:, D_MODEL:]

        def gated(wg_ref, wu_ref, weight):
            return (_silu(_dot(x, wg_ref[...])) * _dot(x, wu_ref[...]) * weight).astype(BF16)

        a_lo = gated(wgl_ref, wul_ref, rw[:, 1:2])
        a_hi = gated(wgh_ref, wuh_ref, rw[:, 2:3])
        ys_ref[...] = _dot(a_lo, wdl_ref[...]) + _dot(a_hi, wdh_ref[...])

    @pl.when(t >= n_valid)
    def _():
        ys_ref[...] = jnp.zeros_like(ys_ref)


def _moe_ffn(layer, meta, hs, w_gate_bf, w_up_bf, w_down_bf):
    lo = lambda t, m: (layer, m[t], 0, 0)
    hi = lambda t, m: (layer, m[128 + t], 0, 0)
    up = lambda im: pl.BlockSpec((None, None, D_MODEL, EXPERT_FF), im)
    down = lambda im: pl.BlockSpec((None, None, EXPERT_FF, D_MODEL), im)
    return pl.pallas_call(
        _moe_ffn_kernel,
        grid_spec=pltpu.PrefetchScalarGridSpec(
            num_scalar_prefetch=1,
            grid=(MOE_MAX_TILES,),
            in_specs=[pl.BlockSpec((MOE_TILE, ROW_W), lambda t, m: (jnp.maximum(jnp.minimum(t, m[2 * 128] - 1), 0), 0)),
                      up(lo), up(lo), down(lo), up(hi), up(hi), down(hi)],
            out_specs=pl.BlockSpec((MOE_TILE, D_MODEL), lambda t, m: (t, 0)),
        ),
        out_shape=jax.ShapeDtypeStruct((N_SLOTS, D_MODEL), F32),
        compiler_params=_cparams(1),
        name="moe_ffn",
    )(meta, hs, w_gate_bf, w_up_bf, w_down_bf, w_gate_bf, w_up_bf, w_down_bf)


def _moe_combine_kernel(dest_ref, x1_ref, mod_ref, ys_ref, out_ctx_ref, out_lat_ref, buf, sems):
    t = pl.program_id(0)
    nt = pl.num_programs(0)

    def issue(tile, slot):
        def body(g, carry):
            row0 = pl.multiple_of(g * DMA_GROUP, DMA_GROUP)
            for j in range(DMA_GROUP):
                d = dest_ref[tile * TILE + row0 + j]
                pltpu.make_async_copy(ys_ref.at[pl.ds(d, 1)], buf.at[slot, pl.ds(row0 + j, 1)],
                                      sems.at[slot]).start(priority=j % 2)
            return carry
        lax.fori_loop(0, TILE // DMA_GROUP, body, 0)

    @pl.when(t == 0)
    def _():
        issue(0, 0)

    for parity in range(2):
        pl.when(jnp.logical_and(t + 1 < nt, (t + 1) % 2 == parity))(functools.partial(issue, t + 1, parity))

    slot = t % 2
    pltpu.make_async_copy(ys_ref.at[pl.ds(0, TILE)], buf.at[slot], sems.at[slot]).wait()
    x2 = x1_ref[...] + mod_ref[5:6, :] * buf[slot]

    @pl.when(t < CTX_TILES)
    def _():
        out_ctx_ref[...] = x2

    @pl.when(t >= CTX_TILES)
    def _():
        out_lat_ref[...] = x2


def _moe_combine(layer, dest, x1, mod, ys):
    return pl.pallas_call(
        _moe_combine_kernel,
        grid_spec=pltpu.PrefetchScalarGridSpec(
            num_scalar_prefetch=1,
            grid=(N_TILES,),
            in_specs=[pl.BlockSpec((TILE, D_MODEL), lambda t, dest: (t, 0)),
                      pl.BlockSpec((None, None, 6, D_MODEL), lambda t, dest: (layer, _cond_of_tile(t), 0, 0)),
                      pl.BlockSpec(memory_space=pl.ANY)],
            out_specs=[pl.BlockSpec((TILE, D_MODEL), lambda t, dest: (jnp.minimum(t, CTX_TILES - 1), 0)),
                       pl.BlockSpec((TILE, D_MODEL), lambda t, dest: (jnp.maximum(t - CTX_TILES, 0), 0))],
            scratch_shapes=[pltpu.VMEM((2, TILE, D_MODEL), F32), pltpu.SemaphoreType.DMA((2,))],
        ),
        out_shape=[jax.ShapeDtypeStruct((N_CTX_ROWS, D_MODEL), F32), jax.ShapeDtypeStruct((N_LAT_ROWS, D_MODEL), F32)],
        compiler_params=_dma_params(),
        name="moe_combine",
    )(dest, x1, mod, ys)


def _block_diag_ones(n, group):
    i = np.arange(n) // group
    return jnp.asarray((i[:, None] == i[None, :]).astype(np.float32), dtype=BF16)


def _rope_tables():
    pos = np.arange(DEC_SEQ)
    rows = (pos // GRID_W).astype(np.float32)
    cols = (pos % GRID_W).astype(np.float32)
    inv_freq = (ROPE_BASE ** (-(np.arange(ROPE_PAIRS, dtype=np.float32) / ROPE_PAIRS))).astype(np.float32)
    ang_r = rows[:, None] * inv_freq[None, :]
    ang_c = cols[:, None] * inv_freq[None, :]
    cos64 = np.concatenate([np.cos(ang_r), np.cos(ang_r), np.cos(ang_c), np.cos(ang_c)], axis=1)
    sin64 = np.concatenate([-np.sin(ang_r), np.sin(ang_r), -np.sin(ang_c), np.sin(ang_c)], axis=1)
    return (jnp.asarray(np.tile(cos64, (1, 2)), F32), jnp.asarray(np.tile(sin64, (1, 2)), F32))


def _mixers(latent, layer, x, consts, params, ret_s0, hg_s0, cache_k, cache_v):
    (mod, lb_all, bd512, bd256, rope_cos, rope_sin, ret_tables) = consts
    n_seq = DEC_BATCH if latent else BATCH
    seq_len = DEC_SEQ if latent else SEQ
    nt = seq_len // TILE
    proj = _in_projection(latent, layer, x, mod, params['norm1'], params['w_in'], params['w_vt'], params['da_qn'],
                          params['da_kn'], lb_all, bd512, rope_cos, rope_sin)
    (rq, rk, rv, rsg, dq, dk, dv, hq, hkf, hkb, hv, hgf, hgb, hsg) = proj[:14]
    ret_of, ret_ob, ret_fin = _retention(n_seq, nt, rq, rk, rv, ret_s0, ret_tables,
                                         "retention_lat" if latent else "retention_ctx")
    hg_of, hg_ob, hg_fin = _gated_scan(n_seq, nt, hq, hkf, hkb, hv, hgf, hgb, hg_s0,
                                       "hgrn_scan_lat" if latent else "hgrn_scan_ctx")
    if latent:
        da_o = _diff_attention(layer, n_seq, seq_len, dq, dk, dv, cache_k, cache_v,
                               params['da_lambda'], params['da_subln'])
    else:
        da_o = _diff_attention_short(layer, n_seq, seq_len, dq, dk, dv, params['da_lambda'], params['da_subln'])
    streams = (x, ret_of, ret_ob, rsg, da_o, hg_of, hg_ob, hsg)
    extras = None if latent else (proj[14], proj[15], ret_fin, hg_fin)
    return streams, extras


def _trunk_layer(layer, x_ctx, x_lat, sorted_buf, consts, params, lat_ret_s0, lat_hg_s0, zero_state, cache_k, cache_v):
    mod, bd256 = consts[0], consts[3]
    ctx_streams, extras = _mixers(False, layer, x_ctx, consts, params, zero_state, zero_state, None, None)
    lat_streams, _ = _mixers(True, layer, x_lat, consts, params, lat_ret_s0, lat_hg_s0, cache_k, cache_v)
    x1, hp, route = _out_projection(layer, mod, params['norm2'], ctx_streams, lat_streams, params['hg_norm'],
                                    bd256, params['w_out'], params['w_route_hi'], params['w_route_lo'],
                                    params['b_route'])
    dest, meta = _moe_dispatch(route)
    sorted_buf = _moe_scatter(dest, hp, sorted_buf)
    ys = _moe_ffn(layer, meta, sorted_buf, params['w_gate'], params['w_up'], params['w_down'])
    x_ctx, x_lat = _moe_combine(layer, dest, x1, mod, ys)
    return x_ctx, x_lat, sorted_buf, extras


def kernel(x_prompt, x_sample, cache_k, cache_v, state_ret, state_hgrn, c, c_ctx, norm1, norm2, w_mod, b_mod,
           w_in, w_out, da_qn, da_kn, da_lambda, da_subln, hg_lb, hg_norm, w_group, b_group, w_router,
           b_router, w_gate, w_up, w_down):
    cond = jnp.zeros((N_COND, D_MODEL), F32).at[0].set(c_ctx).at[1:1 + DEC_BATCH].set(c)
    mod, lb_all = _modulation(cond, w_mod, b_mod, hg_lb)
    mod = mod.reshape(DEPTH, N_COND, 6, D_MODEL)
    rope_cos, rope_sin = _rope_tables()
    consts = (mod, lb_all.reshape(DEPTH, 1, MIX_G), _block_diag_ones(DA_W, HEAD_D),
              _block_diag_ones(MIX_G, HEAD_D), rope_cos, rope_sin, _retention_tables())
    pad = jnp.zeros((DEPTH, D_MODEL, 128 - N_EXPERTS - N_GROUPS), F32)
    w_route = jnp.concatenate([w_router, w_group, pad], axis=-1)
    params = {
        'norm1': norm1.reshape(DEPTH, 1, D_MODEL), 'norm2': norm2.reshape(DEPTH, 1, D_MODEL),
        'w_in': w_in.astype(BF16), 'w_out': w_out.astype(BF16),
        'w_vt': jnp.swapaxes(w_in[:, :, 2048:2560].astype(BF16), 1, 2),
        'da_qn': jnp.tile(da_qn, (1, DA_W // HEAD_D)).reshape(DEPTH, 1, DA_W),
        'da_kn': jnp.tile(da_kn, (1, DA_W // HEAD_D)).reshape(DEPTH, 1, DA_W),
        'da_lambda': da_lambda, 'da_subln': da_subln.reshape(DEPTH, 1, 128),
        'hg_norm': jnp.tile(hg_norm, (1, HEADS)).reshape(DEPTH, 1, MIX_G),
        'w_route_hi': w_route.astype(BF16),
        'w_route_lo': (w_route - w_route.astype(BF16).astype(F32)).astype(BF16),
        'b_route': jnp.concatenate([b_router, b_group, pad[:, 0, :]], axis=-1).reshape(DEPTH, 1, 128),
        'w_gate': w_gate.astype(BF16), 'w_up': w_up.astype(BF16), 'w_down': w_down.astype(BF16),
    }
    ck_bf = cache_k.astype(BF16).reshape(DEC_BATCH, DEPTH, PAST_LEN, DA_W)
    cv_bf = jnp.swapaxes(cache_v.astype(BF16).reshape(DEC_BATCH, DEPTH, PAST_LEN, DA_W), 2, 3)
    zero_state = jnp.zeros((BATCH, 2, HEADS, HEAD_D, HEAD_D), F32)

    yp = x_prompt.reshape(N_CTX_ROWS, D_MODEL)
    ys = x_sample.reshape(N_LAT_ROWS, D_MODEL)
    sorted_buf = jnp.zeros((N_SLOTS, ROW_W), F32)
    ks_out, vs_out, rets_out, hgs_out = [], [], [], []
    for l in range(DEPTH):
        yp, ys, sorted_buf, (k_new, v_new, ret_fin, hg_fin) = _trunk_layer(
            l, yp, ys, sorted_buf, consts, params, state_ret[:, l], state_hgrn[:, l], zero_state, ck_bf, cv_bf)
        ks_out.append(k_new.reshape(BATCH, SEQ, DA_W))
        vs_out.append(v_new.reshape(BATCH, SEQ, DA_W))
        rets_out.append(ret_fin)
        hgs_out.append(hg_fin)
    return (yp.reshape(BATCH, SEQ, D_MODEL), ys.reshape(DEC_BATCH, DEC_SEQ, D_MODEL),
            jnp.stack(ks_out, axis=1).reshape(BATCH, DEPTH, SEQ, HEADS, 2, HEAD_D),
            jnp.stack(vs_out, axis=1).reshape(BATCH, DEPTH, SEQ, HEADS, 2 * HEAD_D),
            jnp.stack(rets_out, axis=1), jnp.stack(hgs_out, axis=1))
```

```python
import functools
import math

import numpy as np
import jax
import jax.numpy as jnp
from jax import lax
from jax.experimental import pallas as pl
from jax.experimental.pallas import tpu as pltpu

F32 = jnp.float32
BF16 = jnp.bfloat16

D_MODEL = 1024
BATCH = 16
SEQ = 256
DEPTH = 4
DEC_BATCH = 4
DEC_SEQ = 4096
PAST_LEN = 256
GRID_W = 64
HEADS = 4
HEAD_D = 64
MIX_G = HEADS * HEAD_D
DA_W = 512
IN_COLS = 3840
ROPE_BASE = 10000.0
ROPE_PAIRS = 16
RMS_EPS = 1e-6
N_GROUPS = 4
EPG = 4
N_EXPERTS = 16
EXPERT_FF = 512

TILE = 256
N_CTX_ROWS = BATCH * SEQ
N_LAT_ROWS = DEC_BATCH * DEC_SEQ
N_ROWS = N_CTX_ROWS + N_LAT_ROWS
CTX_TILES = N_CTX_ROWS // TILE
LAT_TILES_PER_SEQ = DEC_SEQ // TILE
N_TILES = N_ROWS // TILE
PTILE = 512
CTX_PTILES = N_CTX_ROWS // PTILE
LAT_PTILES_PER_SEQ = DEC_SEQ // PTILE
N_COND = 8

N_PAIRS = EPG * (EPG - 1) // 2
N_BUCKETS = N_GROUPS * N_PAIRS
MOE_TILE = 256
MOE_MAX_TILES = N_ROWS // MOE_TILE + N_BUCKETS
DISPATCH_TILES = 4
DMA_GROUP = 8
N_SLOTS = MOE_MAX_TILES * MOE_TILE
ROW_W = D_MODEL + 128

CHUNK = 64
SUB = 16
NSUB = CHUNK // SUB
NCHUNK = TILE // CHUNK

VMEM_LIMIT = 48 * 1024 * 1024
NEG_BIG = -1e30
ONES_ROWS = 16
ATTN_TQ = 512
ATTN_KEY_CHUNK = 512
LOG2_E = 1.4426950408889634

HI = lax.Precision.HIGHEST


def _cparams(n_axes):
    return pltpu.CompilerParams(dimension_semantics=("arbitrary",) * n_axes,
                                vmem_limit_bytes=VMEM_LIMIT)


def _dot(a, b):
    return jnp.dot(a, b, preferred_element_type=F32)


def _dot_nt(a, b):
    return lax.dot_general(a, b, (((1,), (1,)), ((), ())), preferred_element_type=F32)


def _dot_tn(a, b):
    return lax.dot_general(a, b, (((0,), (0,)), ((), ())), preferred_element_type=F32)


def _silu(x):
    return x * (1.0 / (1.0 + jnp.exp(-x)))


def _sigmoid(x):
    return 1.0 / (1.0 + jnp.exp(-x))


def _mod_kernel(cond_ref, w_ref, b_ref, lb_ref, mod_ref, lbo_ref):
    c = cond_ref[...]
    m = jnp.dot(_silu(c), w_ref[...], preferred_element_type=F32, precision=HI)
    mod_ref[...] = m + b_ref[...]
    z = lb_ref[...]
    zmax = jnp.max(z, axis=0, keepdims=True)
    e = jnp.exp(z - zmax)
    p = e / jnp.sum(e, axis=0, keepdims=True)
    rows = [jnp.zeros_like(p[0:1])]
    for l in range(1, DEPTH):
        rows.append(rows[-1] + p[l:l + 1])
    lbo_ref[...] = jnp.concatenate(rows, axis=0)


def _modulation(cond, w_mod, b_mod, hg_lb):
    nblk = 6
    return pl.pallas_call(
        _mod_kernel,
        grid=(DEPTH, nblk),
        in_specs=[
            pl.BlockSpec((N_COND, D_MODEL), lambda l, j: (0, 0)),
            pl.BlockSpec((None, D_MODEL, D_MODEL), lambda l, j: (l, 0, j)),
            pl.BlockSpec((None, 1, D_MODEL), lambda l, j: (l, 0, j)),
            pl.BlockSpec((DEPTH, MIX_G), lambda l, j: (0, 0)),
        ],
        out_specs=[
            pl.BlockSpec((None, N_COND, D_MODEL), lambda l, j: (l, 0, j)),
            pl.BlockSpec((DEPTH, MIX_G), lambda l, j: (0, 0)),
        ],
        out_shape=[jax.ShapeDtypeStruct((DEPTH, N_COND, 6 * D_MODEL), F32),
                   jax.ShapeDtypeStruct((DEPTH, MIX_G), F32)],
        compiler_params=_cparams(2),
        name="modulation",
    )(cond, w_mod, b_mod.reshape(DEPTH, 1, 6 * D_MODEL), hg_lb)


def _group_mean_sq(x, bd_ref, group):
    return _dot((x * x).astype(BF16), bd_ref[...]) * (1.0 / group)


def _swap16(x):
    w = x.shape[-1]
    lane = lax.broadcasted_iota(jnp.int32, x.shape, x.ndim - 1)
    up = pltpu.roll(x, w - 16, x.ndim - 1)
    dn = pltpu.roll(x, 16, x.ndim - 1)
    return jnp.where((lane % 32) < 16, up, dn)


def _inproj_kernel(latent, x_ref, mod_ref, n1_ref, w_ref, wvt_ref, qn_ref, kn_ref, lb_ref, bd_ref, *rest):
    if latent:
        cos_ref, sin_ref = rest[0], rest[1]
        outs = rest[2:]
    else:
        outs = rest
    (rq_ref, rk_ref, rv_ref, rsg_ref, dq_ref, dk_ref, dv_ref,
     hq_ref, hkf_ref, hkb_ref, hv_ref, hgf_ref, hgb_ref, hsg_ref) = outs[:14]

    x = x_ref[...]
    ms = jnp.mean(x * x, axis=-1, keepdims=True)
    shift1 = mod_ref[0:1, :]
    scale1 = mod_ref[1:2, :]
    h = x * lax.rsqrt(ms + RMS_EPS) * n1_ref[...] * (1.0 + scale1) + shift1
    h16 = h.astype(BF16)
    y = _dot(h16, w_ref[...])

    rq_ref[...] = y[:, 0:256].astype(BF16)
    rk_ref[...] = (y[:, 256:512] * (HEAD_D ** -0.5)).astype(BF16)
    rv_ref[...] = y[:, 512:768].astype(BF16)
    rsg_ref[...] = _silu(y[:, 768:1024]).astype(BF16)

    dq = y[:, 1024:1536]
    dk = y[:, 1536:2048]
    qn = dq * lax.rsqrt(_group_mean_sq(dq, bd_ref, HEAD_D) + RMS_EPS) * qn_ref[...]
    kn = dk * lax.rsqrt(_group_mean_sq(dk, bd_ref, HEAD_D) + RMS_EPS) * kn_ref[...]
    if latent:
        cos = jnp.concatenate([cos_ref[...]] * 4, axis=1)
        sin = jnp.concatenate([sin_ref[...]] * 4, axis=1)
        qr = qn * cos + _swap16(qn) * sin
        kr = kn * cos + _swap16(kn) * sin
    else:
        qr, kr = qn, kn
        ck_ref, cv_ref = outs[14], outs[15]
        ck_ref[...] = kn
        cv_ref[...] = y[:, 2048:2560]
    dq_ref[...] = (qr * (HEAD_D ** -0.5 * LOG2_E)).astype(BF16)
    dk_ref[...] = kr.astype(BF16)
    dv_ref[...] = _dot_nt(wvt_ref[...], h16).astype(BF16)

    lb = lb_ref[...]
    f_f = lb + (1.0 - lb) * _sigmoid(y[:, 2816:3072])
    f_b = lb + (1.0 - lb) * _sigmoid(y[:, 3072:3328])
    hq_ref[...] = y[:, 2560:2816].astype(BF16)
    hkf_ref[...] = (1.0 - f_f).astype(BF16)
    hkb_ref[...] = (1.0 - f_b).astype(BF16)
    hgf_ref[...] = jnp.log2(f_f)
    hgb_ref[...] = jnp.log2(f_b)
    hv_ref[...] = y[:, 3328:3584].astype(BF16)
    hsg_ref[...] = _silu(y[:, 3584:3840]).astype(BF16)


def _in_projection(latent, layer, x, mod, norm1, w_in_bf, w_vt_bf, qn_t, kn_t, lb_all, bd512, rope_cos, rope_sin):
    nrows = x.shape[0]
    ntiles = nrows // PTILE
    cond = (lambda t: 1 + t // LAT_PTILES_PER_SEQ) if latent else (lambda t: 0)

    in_specs = [
        pl.BlockSpec((PTILE, D_MODEL), lambda t: (t, 0)),
        pl.BlockSpec((None, None, 6, D_MODEL), lambda t: (layer, cond(t), 0, 0)),
        pl.BlockSpec((None, 1, D_MODEL), lambda t: (layer, 0, 0)),
        pl.BlockSpec((None, D_MODEL, IN_COLS), lambda t: (layer, 0, 0)),
        pl.BlockSpec((None, DA_W, D_MODEL), lambda t: (layer, 0, 0)),
        pl.BlockSpec((None, 1, DA_W), lambda t: (layer, 0, 0)),
        pl.BlockSpec((None, 1, DA_W), lambda t: (layer, 0, 0)),
        pl.BlockSpec((None, 1, MIX_G), lambda t: (layer, 0, 0)),
        pl.BlockSpec((DA_W, DA_W), lambda t: (0, 0)),
    ]
    args = [x, mod, norm1, w_in_bf, w_vt_bf, qn_t, kn_t, lb_all, bd512]
    if latent:
        in_specs += [pl.BlockSpec((PTILE, 128), lambda t: (t % LAT_PTILES_PER_SEQ, 0))] * 2
        args += [rope_cos, rope_sin]

    def o(width, dtype):
        return jax.ShapeDtypeStruct((nrows, width), dtype), pl.BlockSpec((PTILE, width), lambda t: (t, 0))

    dv_t = (jax.ShapeDtypeStruct((DA_W, nrows), BF16), pl.BlockSpec((DA_W, PTILE), lambda t: (0, t)))
    outs = [o(256, BF16), o(256, BF16), o(256, BF16), o(256, BF16),
            o(512, BF16), o(512, BF16), dv_t,
            o(256, BF16), o(256, BF16), o(256, BF16), o(256, BF16), o(256, F32), o(256, F32), o(256, BF16)]
    if not latent:
        outs += [o(512, F32), o(512, F32)]
    return pl.pallas_call(
        functools.partial(_inproj_kernel, latent),
        grid=(ntiles,),
        in_specs=in_specs,
        out_specs=[s for _, s in outs],
        out_shape=[s for s, _ in outs],
        compiler_params=_cparams(1),
        name="in_projection_lat" if latent else "in_projection_ctx",
    )(*args)


def _load_states(s0_ref, st_scr, transposed):
    st_scr[...] = jnp.zeros_like(st_scr)
    for d in range(2):
        for h in range(HEADS):
            blk = s0_ref[d, h]
            st_scr[d, h * HEAD_D:(h + 1) * HEAD_D, h * HEAD_D:(h + 1) * HEAD_D] = blk.T if transposed else blk


def _store_states(st_scr, sfin_ref, transposed):
    for d in range(2):
        for h in range(HEADS):
            blk = st_scr[d, h * HEAD_D:(h + 1) * HEAD_D, h * HEAD_D:(h + 1) * HEAD_D]
            sfin_ref[d, h] = blk.T if transposed else blk


def _head_masks():
    lane = lax.broadcasted_iota(jnp.int32, (1, MIX_G), 1)
    return [(lane // HEAD_D == h).astype(F32) for h in range(HEADS)]


def _scan_chunks(items, st_f, st_b, consts_f, consts_b):
    n = len(items)
    cs = [consts_b if it[4] else consts_f for it in items]
    zero_row = jnp.zeros((1, MIX_G), F32)

    bs = []
    for x in range(n):
        g = items[x][3]
        g_hi = g.astype(BF16)
        g_lo = (g - g_hi.astype(F32)).astype(BF16)
        bs.append(_dot(cs[x][0], g_hi) + _dot(cs[x][0], g_lo))

    prep = []
    for x in range(n):
        q, k, v, _, rev = items[x]
        hmask = cs[x][4]
        b = bs[x]
        r_rows, e_rows = [], []
        for s in range(NSUB):
            lo, hi = s * SUB, s * SUB + SUB - 1
            if not rev:
                r_rows.append(b[lo - 1:lo, :] if s > 0 else zero_row)
                e_rows.append(b[hi:hi + 1, :])
            else:
                r_rows.append(b[hi + 1:hi + 2, :] if s < NSUB - 1 else zero_row)
                e_rows.append(b[lo:lo + 1, :])
        btot = e_rows[NSUB - 1] if not rev else e_rows[0]
        r_full = jnp.concatenate([jnp.broadcast_to(r, (SUB, MIX_G)) for r in r_rows], axis=0)
        e_full = jnp.concatenate([jnp.broadcast_to(e, (SUB, MIX_G)) for e in e_rows], axis=0)
        bl = b - r_full
        qp = q * jnp.exp2(bl)
        kend = k * jnp.exp2(e_full - b)
        pairs = [(i, j) for i in range(NSUB) for j in range(NSUB) if (j < i if not rev else j > i)]
        lh = jnp.concatenate([qp[i * SUB:(i + 1) * SUB, :] * jnp.exp2(r_rows[i] - e_rows[j])
                              for (i, j) in pairs], axis=0)
        prep.append(dict(
            bl=bl, btot=btot, pairs=pairs,
            q_inter=(qp * jnp.exp2(r_full)).astype(BF16),
            k_state=(kend * jnp.exp2(btot - e_full)).astype(BF16),
            kend=kend.astype(BF16),
            lhs4=jnp.concatenate([lh * hmask[h] for h in range(HEADS)], axis=0).astype(BF16)))

    uts = [_dot_tn(items[x][2], prep[x]['k_state']) for x in range(n)]
    scs = [_dot_nt(prep[x]['lhs4'], prep[x]['kend']) for x in range(n)]

    st_in = []
    for x in range(n):
        rev = items[x][4]
        st = st_b if rev else st_f
        st_in.append(st.astype(BF16))
        st = st * jnp.exp2(prep[x]['btot']) + uts[x] * cs[x][1]
        if rev:
            st_b = st
        else:
            st_f = st
    col = lax.broadcasted_iota(jnp.int32, (SUB, CHUNK), 1) // SUB
    jrow = lax.broadcasted_iota(jnp.int32, (SUB, MIX_G), 0)
    pss, pps, vts = [], [], []
    for x in range(n):
        q, k, v, _, rev = items[x]
        pairs, sc, bl = prep[x]['pairs'], scs[x], prep[x]['bl']
        p_rows = []
        for h in range(HEADS):
            for i in range(NSUB):
                acc = None
                for p, (pi, pj) in enumerate(pairs):
                    if pi != i:
                        continue
                    base = (h * len(pairs) + p) * SUB
                    blk = jnp.where(col == pj, sc[base:base + SUB, :], 0.0)
                    acc = blk if acc is None else acc + blk
                p_rows.append(jnp.zeros((SUB, CHUNK), F32) if acc is None else acc)
        pss.append(jnp.concatenate(p_rows, axis=0).astype(BF16))
        pp, vt = [], []
        for s in range(NSUB):
            sl = slice(s * SUB, (s + 1) * SUB)
            bl_s, q_s, k_s = bl[sl, :], q[sl, :].astype(BF16), k[sl, :].astype(BF16)
            for i in range(SUB):
                d = bl_s[i:i + 1, :] - bl_s
                keep = (jrow <= i) if not rev else (jrow >= i)
                e = jnp.exp2(jnp.where(keep, d, NEG_BIG).astype(BF16))
                pp.append(e * q_s[i:i + 1, :] * k_s)
            vt.extend([v[sl, :]] * SUB)
        pps.append(jnp.concatenate(pp, axis=0))
        vts.append(jnp.concatenate(vt, axis=0))

    o_int = [_dot_nt(prep[x]['q_inter'], st_in[x]) for x in range(n)]
    fulls = [_dot(pss[x], items[x][2]) for x in range(n)]
    sbs = [_dot(pps[x], cs[x][2]) for x in range(n)]

    ws = [(sbs[x] * vts[x].astype(F32)).astype(BF16) for x in range(n)]
    o_diag = [_dot(cs[x][3], ws[x]) for x in range(n)]

    outs = []
    for x in range(n):
        hmask = cs[x][4]
        o = o_int[x] + o_diag[x]
        for h in range(HEADS):
            o = o + fulls[x][h * CHUNK:(h + 1) * CHUNK, :] * hmask[h]
        outs.append(o)
    return outs, st_f, st_b


def _scan_consts(rev):
    r = lax.broadcasted_iota(jnp.int32, (CHUNK, CHUNK), 0)
    c = lax.broadcasted_iota(jnp.int32, (CHUNK, CHUNK), 1)
    tri = ((c <= r) if not rev else (c >= r)).astype(BF16)
    rr = lax.broadcasted_iota(jnp.int32, (MIX_G, MIX_G), 0) // HEAD_D
    cc = lax.broadcasted_iota(jnp.int32, (MIX_G, MIX_G), 1) // HEAD_D
    bd_mask = (rr == cc).astype(F32)
    ones_bd = bd_mask.astype(BF16)
    gi = lax.broadcasted_iota(jnp.int32, (CHUNK, CHUNK * SUB), 0)
    gj = lax.broadcasted_iota(jnp.int32, (CHUNK, CHUNK * SUB), 1) // SUB
    gsel = (gi == gj).astype(BF16)
    return tri, bd_mask, ones_bd, gsel, _head_masks()


def _scan_kernel(nt, qf_ref, kf_ref, vf_ref, gf_ref, qb_ref, kb_ref, vb_ref, gb_ref, s0_ref,
                 of_ref, ob_ref, sfin_ref, st_scr):
    t = pl.program_id(1)

    @pl.when(t == 0)
    def _():
        _load_states(s0_ref, st_scr, True)

    cf = _scan_consts(False)
    cb = _scan_consts(True)

    items, rows = [], []
    for c in range(NCHUNK):
        rf = slice(c * CHUNK, (c + 1) * CHUNK)
        rb = slice((NCHUNK - 1 - c) * CHUNK, (NCHUNK - c) * CHUNK)
        items.append((qf_ref[rf, :].astype(F32), kf_ref[rf, :].astype(F32), vf_ref[rf, :], gf_ref[rf, :], False))
        items.append((qb_ref[rb, :].astype(F32), kb_ref[rb, :].astype(F32), vb_ref[rb, :], gb_ref[rb, :], True))
        rows += [(of_ref, rf), (ob_ref, rb)]
    outs, st_f, st_b = _scan_chunks(items, st_scr[0], st_scr[1], cf, cb)
    for (ref, sl), o in zip(rows, outs):
        ref[sl, :] = o
    st_scr[0] = st_f
    st_scr[1] = st_b

    @pl.when(t == nt - 1)
    def _():
        _store_states(st_scr, sfin_ref, True)


def _gated_scan(n_seq, nt, q, k_f, k_b, v, g_f, g_b, s0, name):
    fwd = lambda s, t: (s * nt + t, 0)
    bwd = lambda s, t: (s * nt + (nt - 1 - t), 0)
    blk = lambda im: pl.BlockSpec((TILE, MIX_G), im)
    in_specs = [blk(fwd), blk(fwd), blk(fwd), blk(fwd), blk(bwd), blk(bwd), blk(bwd), blk(bwd),
                pl.BlockSpec((None, 2, HEADS, HEAD_D, HEAD_D), lambda s, t: (s, 0, 0, 0, 0))]
    nrows = n_seq * nt * TILE
    return pl.pallas_call(
        functools.partial(_scan_kernel, nt),
        grid=(n_seq, nt),
        in_specs=in_specs,
        out_specs=[pl.BlockSpec((TILE, MIX_G), lambda s, t: (s * nt + t, 0)),
                   pl.BlockSpec((TILE, MIX_G), lambda s, t: (s * nt + (nt - 1 - t), 0)),
                   pl.BlockSpec((None, 2, HEADS, HEAD_D, HEAD_D), lambda s, t: (s, 0, 0, 0, 0))],
        out_shape=[jax.ShapeDtypeStruct((nrows, MIX_G), F32),
                   jax.ShapeDtypeStruct((nrows, MIX_G), F32),
                   jax.ShapeDtypeStruct((n_seq, 2, HEADS, HEAD_D, HEAD_D), F32)],
        scratch_shapes=[pltpu.VMEM((2, MIX_G, MIX_G), F32)],
        compiler_params=_cparams(2),
        name=name,
    )(q, k_f, v, g_f, q, k_b, v, g_b, s0)


def _retention_tables():
    gam = 1.0 - 2.0 ** (-5.0 - np.arange(HEADS, dtype=np.float64))
    gam_r = gam[::-1]
    i = np.arange(TILE, dtype=np.float64)
    diff = i[:, None] - i[None, :]
    dcomb = np.zeros((HEADS, TILE, TILE))
    for h in range(HEADS):
        lower = np.where(diff > 0, gam[h] ** np.maximum(diff, 0), 0.0)
        upper = np.where(diff < 0, gam_r[h] ** np.maximum(-diff, 0), 0.0)
        dcomb[h] = lower + upper + 2.0 * (diff == 0)
    lanes = lambda per_head: np.repeat(per_head, HEAD_D, axis=-1)
    qd_f = lanes(gam[None, :] ** (i[:, None] + 1.0))
    kd_f = lanes(gam[None, :] ** (TILE - 1.0 - i[:, None]))
    qd_b = lanes(gam_r[None, :] ** (TILE - i[:, None]))
    kd_b = lanes(gam_r[None, :] ** i[:, None])
    blk = (np.arange(MIX_G)[:, None] // HEAD_D) == (np.arange(MIX_G)[None, :] // HEAD_D)
    c_f = np.where(blk, lanes(gam ** TILE)[None, :], 0.0) * np.ones((MIX_G, 1))
    c_b = np.where(blk, lanes(gam_r ** TILE)[None, :], 0.0) * np.ones((MIX_G, 1))
    f = lambda a: jnp.asarray(a, F32)
    return (f(dcomb.reshape(HEADS * TILE, TILE)), f(qd_f), f(kd_f), f(qd_b), f(kd_b), f(c_f), f(c_b))


def _ret_kernel(nt, qf_ref, kf_ref, vf_ref, qb_ref, kb_ref, vb_ref, s0_ref, dcomb_ref, qdf_ref, kdf_ref,
                qdb_ref, kdb_ref, cf_ref, cb_ref, of_ref, ob_ref, sfin_ref, st_scr):
    t = pl.program_id(1)

    @pl.when(t == 0)
    def _():
        _load_states(s0_ref, st_scr, False)

    lane_head = lax.broadcasted_iota(jnp.int32, (TILE, MIX_G), 1) // HEAD_D
    rr = lax.broadcasted_iota(jnp.int32, (MIX_G, MIX_G), 0) // HEAD_D
    cc = lax.broadcasted_iota(jnp.int32, (MIX_G, MIX_G), 1) // HEAD_D
    same_head = rr == cc

    q = qf_ref[...]
    k = kf_ref[...]
    v = vf_ref[...]
    qs = jnp.concatenate([jnp.where(lane_head == h, q, jnp.zeros_like(q)) for h in range(HEADS)], axis=0)
    p = (_dot_nt(qs, k) * dcomb_ref[...]).astype(BF16)
    full = _dot(p, v)
    o = _dot((q.astype(F32) * qdf_ref[...]).astype(BF16), st_scr[0].astype(BF16))
    for h in range(HEADS):
        o = o + jnp.where(lane_head == h, full[h * TILE:(h + 1) * TILE, :], 0.0)
    of_ref[...] = o
    u = _dot_tn((k.astype(F32) * kdf_ref[...]).astype(BF16), v)
    st_scr[0] = st_scr[0] * cf_ref[...] + jnp.where(same_head, u, 0.0)

    qb = qb_ref[...]
    kb = kb_ref[...]
    vb = vb_ref[...]
    ob_ref[...] = _dot((qb.astype(F32) * qdb_ref[...]).astype(BF16), st_scr[1].astype(BF16))
    ub = _dot_tn((kb.astype(F32) * kdb_ref[...]).astype(BF16), vb)
    st_scr[1] = st_scr[1] * cb_ref[...] + jnp.where(same_head, ub, 0.0)

    @pl.when(t == nt - 1)
    def _():
        _store_states(st_scr, sfin_ref, False)


def _retention(n_seq, nt, q, k, v, s0, tables, name):
    fwd = lambda s, t: (s * nt + t, 0)
    bwd = lambda s, t: (s * nt + (nt - 1 - t), 0)
    const = lambda s, t: (0, 0)
    blk = lambda im: pl.BlockSpec((TILE, MIX_G), im)
    sq = lambda: pl.BlockSpec((MIX_G, MIX_G), const)
    in_specs = [blk(fwd), blk(fwd), blk(fwd), blk(bwd), blk(bwd), blk(bwd),
                pl.BlockSpec((None, 2, HEADS, HEAD_D, HEAD_D), lambda s, t: (s, 0, 0, 0, 0)),
                pl.BlockSpec((HEADS * TILE, TILE), const), sq(), sq(), sq(), sq(), sq(), sq()]
    nrows = n_seq * nt * TILE
    return pl.pallas_call(
        functools.partial(_ret_kernel, nt),
        grid=(n_seq, nt),
        in_specs=in_specs,
        out_specs=[blk(fwd), blk(bwd), pl.BlockSpec((None, 2, HEADS, HEAD_D, HEAD_D), lambda s, t: (s, 0, 0, 0, 0))],
        out_shape=[jax.ShapeDtypeStruct((nrows, MIX_G), F32),
                   jax.ShapeDtypeStruct((nrows, MIX_G), F32),
                   jax.ShapeDtypeStruct((n_seq, 2, HEADS, HEAD_D, HEAD_D), F32)],
        scratch_shapes=[pltpu.VMEM((2, MIX_G, MIX_G), F32)],
        compiler_params=_cparams(2),
        name=name,
    )(q, k, v, q, k, v, s0, *tables)


def _attn_step(lam_init, comp, refs, s_write, s_read, part0_ref):
    q_ref, k_ref, ck_ref, v_ref, cv_ref, lam_ref, sub_ref, o_ref = refs
    sw, sw2, mw = s_write
    sr, sr2, mr = s_read

    q = q_ref[...]
    lane = lax.broadcasted_iota(jnp.int32, q.shape, 1)
    qc = jnp.where((lane // HEAD_D) == comp, q, jnp.zeros_like(q))
    n_keys = k_ref.shape[0]
    kc = min(ATTN_KEY_CHUNK, n_keys)
    ones_rows = jnp.ones((ONES_ROWS, kc), BF16)
    m_prev = mr[...]
    oe, m_cur = None, None
    for j in range(n_keys // kc):
        ks = slice(j * kc, (j + 1) * kc)
        d = _dot(jnp.concatenate([v_ref[:, ks], ones_rows], axis=0),
                 jnp.exp2((sr[ks, :] - m_prev).astype(BF16)))
        oe = d if oe is None else oe + d
        s = _dot_nt(k_ref[ks, :], qc)
        sw[ks, :] = s
        mj = jnp.max(s, axis=0, keepdims=True)
        m_cur = mj if m_cur is None else jnp.maximum(m_cur, mj)
    oe = oe + _dot(jnp.concatenate([cv_ref[...], jnp.ones((ONES_ROWS, cv_ref.shape[1]), BF16)], axis=0),
                   jnp.exp2((sr2[...] - m_prev).astype(BF16)))
    s2 = _dot_nt(ck_ref[...], qc)
    sw2[...] = s2
    m_cur = jnp.maximum(m_cur, jnp.max(s2, axis=0, keepdims=True))
    mw[...] = m_cur
    part = oe[:128, :] / oe[128:129, :]
    if comp == 1:
        part0_ref[...] = part
    else:
        lp = lam_ref[...]
        lam = (jnp.exp(jnp.sum(lp[0:1] * lp[1:2], axis=-1, keepdims=True))
               - jnp.exp(jnp.sum(lp[2:3] * lp[3:4], axis=-1, keepdims=True)) + lam_init)
        o = part0_ref[...] - lam * part
        ms = jnp.mean(o * o, axis=0, keepdims=True)
        o = (o * lax.rsqrt(ms + RMS_EPS)).T
        o_ref[...] = (o * sub_ref[...] * (1.0 - lam_init)).astype(BF16)


def _attn_kernel(lam_init, *refs):
    n_in = 8
    io, part0_ref, scr = refs[:n_in], refs[n_in], refs[n_in + 1:]
    half = len(scr) // 2
    buf_a, buf_b = scr[:half], scr[half:]
    u = pl.program_id(0)

    @pl.when(u == 0)
    def _():
        for r in buf_b + (part0_ref,):
            r[...] = jnp.zeros(r.shape, r.dtype)

    @pl.when(u % 2 == 0)
    def _():
        _attn_step(lam_init, 0, io, buf_a, buf_b, part0_ref)

    @pl.when(u % 2 == 1)
    def _():
        _attn_step(lam_init, 1, io, buf_b, buf_a, part0_ref)


def _attn_short_kernel(lam_init, q_ref, k_ref, v_ref, lam_ref, sub_ref, o_ref):
    lp = lam_ref[...]
    lam = (jnp.exp(jnp.sum(lp[0:1] * lp[1:2], axis=-1, keepdims=True))
           - jnp.exp(jnp.sum(lp[2:3] * lp[3:4], axis=-1, keepdims=True)) + lam_init)
    n_keys = k_ref.shape[0]
    ones_rows = jnp.ones((ONES_ROWS, n_keys), BF16)
    lane = lax.broadcasted_iota(jnp.int32, (q_ref.shape[0], 128), 1)
    units = [(h, c) for h in range(HEADS) for c in range(2)]
    scores = []
    for h, c in units:
        q = q_ref[:, h * 128:(h + 1) * 128]
        qc = jnp.where((lane // HEAD_D) == c, q, jnp.zeros_like(q))
        scores.append(_dot_nt(k_ref[:, h * 128:(h + 1) * 128], qc))
    probs = [jnp.exp2((s - jnp.max(s, axis=0, keepdims=True)).astype(BF16)) for s in scores]
    oes = [_dot(jnp.concatenate([v_ref[h * 128:(h + 1) * 128, :], ones_rows], axis=0), p)
           for (h, c), p in zip(units, probs)]
    parts = [oe[:128, :] / oe[128:129, :] for oe in oes]
    outs = []
    for h in range(HEADS):
        o = parts[2 * h] - lam * parts[2 * h + 1]
        ms = jnp.mean(o * o, axis=0, keepdims=True)
        outs.append((o * lax.rsqrt(ms + RMS_EPS)).T * sub_ref[...] * (1.0 - lam_init))
    o_ref[...] = jnp.concatenate(outs, axis=1).astype(BF16)


def _diff_attention_short(layer, n_seq, seq_len, q, k, v_t, da_lambda, subln):
    lam_init = 0.8 - 0.6 * math.exp(-0.3 * layer)
    return pl.pallas_call(
        functools.partial(_attn_short_kernel, lam_init),
        grid=(n_seq,),
        in_specs=[pl.BlockSpec((seq_len, DA_W), lambda b: (b, 0)),
                  pl.BlockSpec((seq_len, DA_W), lambda b: (b, 0)),
                  pl.BlockSpec((DA_W, seq_len), lambda b: (0, b)),
                  pl.BlockSpec((None, 4, HEAD_D), lambda b: (layer, 0, 0)),
                  pl.BlockSpec((None, 1, 128), lambda b: (layer, 0, 0))],
        out_specs=pl.BlockSpec((seq_len, DA_W), lambda b: (b, 0)),
        out_shape=jax.ShapeDtypeStruct((n_seq * seq_len, DA_W), BF16),
        compiler_params=_cparams(1),
        name="diff_attention_ctx",
    )(q, k, v_t, da_lambda, subln)


def _diff_attention(layer, n_seq, seq_len, q, k, v_t, cache_k, cache_v_t, da_lambda, subln):
    tq = min(ATTN_TQ, seq_len)
    nq = seq_len // tq
    n_units = n_seq * HEADS * nq * 2
    lam_init = 0.8 - 0.6 * math.exp(-0.3 * layer)

    cur = lambda u: jnp.minimum(u, n_units - 1) // 2
    prev = lambda u: jnp.maximum(u - 1, 0) // 2
    seq_of = lambda w: w // (HEADS * nq)
    head_of = lambda w: (w // nq) % HEADS
    rows_of = lambda w: seq_of(w) * nq + w % nq

    q_spec = pl.BlockSpec((tq, 128), lambda u: (rows_of(cur(u)), head_of(cur(u))))
    k_spec = pl.BlockSpec((seq_len, 128), lambda u: (seq_of(cur(u)), head_of(cur(u))))
    v_spec = pl.BlockSpec((128, seq_len), lambda u: (head_of(prev(u)), seq_of(prev(u))))
    ck_spec = pl.BlockSpec((None, None, PAST_LEN, 128), lambda u: (seq_of(cur(u)), layer, 0, head_of(cur(u))))
    cv_spec = pl.BlockSpec((None, None, 128, PAST_LEN), lambda u: (seq_of(prev(u)), layer, head_of(prev(u)), 0))
    tail = [pl.BlockSpec((None, 4, HEAD_D), lambda u: (layer, 0, 0)),
            pl.BlockSpec((None, 1, 128), lambda u: (layer, 0, 0))]
    in_specs = [q_spec, k_spec, ck_spec, v_spec, cv_spec] + tail
    args = [q, k, cache_k, v_t, cache_v_t, da_lambda, subln]
    scratch = [pltpu.VMEM((128, tq), F32)] + [pltpu.VMEM((seq_len, tq), F32), pltpu.VMEM((PAST_LEN, tq), F32),
                                              pltpu.VMEM((1, tq), F32)] * 2
    return pl.pallas_call(
        functools.partial(_attn_kernel, lam_init),
        grid=(n_units + 1,),
        in_specs=in_specs,
        out_specs=pl.BlockSpec((tq, 128), lambda u: (rows_of(prev(u)), head_of(prev(u)))),
        out_shape=jax.ShapeDtypeStruct((n_seq * seq_len, DA_W), BF16),
        scratch_shapes=scratch,
        compiler_params=_cparams(1),
        name="diff_attention_lat",
    )(*args)


def _route(logits):
    lane = lax.broadcasted_iota(jnp.int32, logits.shape, 1)
    lane_f = lane.astype(F32)
    big = 1e9
    is_g = jnp.logical_and(lane >= N_EXPERTS, lane < N_EXPERTS + N_GROUPS)
    gl = jnp.where(is_g, logits, NEG_BIG)
    gmax = jnp.max(gl, axis=-1, keepdims=True)
    gsel = jnp.min(jnp.where(gl == gmax, lane_f, big), axis=-1, keepdims=True) - N_EXPERTS
    g_w = 1.0 / jnp.sum(jnp.exp(gl - gmax), axis=-1, keepdims=True)
    in_grp = jnp.logical_and(lane < N_EXPERTS, (lane // EPG).astype(F32) == gsel)
    el = jnp.where(in_grp, logits, NEG_BIG)
    v1 = jnp.max(el, axis=-1, keepdims=True)
    i1 = jnp.min(jnp.where(el == v1, lane_f, big), axis=-1, keepdims=True)
    el2 = jnp.where(lane_f == i1, NEG_BIG, el)
    v2 = jnp.max(el2, axis=-1, keepdims=True)
    i2 = jnp.min(jnp.where(el2 == v2, lane_f, big), axis=-1, keepdims=True)
    t = jnp.exp(v2 - v1)
    w1 = g_w / (1.0 + t)
    w2 = w1 * t
    first_lo = i1 < i2
    a = jnp.minimum(i1, i2) - EPG * gsel
    b = jnp.maximum(i1, i2) - EPG * gsel
    bucket = gsel * N_PAIRS + a * (7.0 - a) * 0.5 + (b - a - 1.0)
    w_lo = jnp.where(first_lo, w1, w2)
    w_hi = jnp.where(first_lo, w2, w1)
    return jnp.where(lane == 0, bucket, jnp.where(lane == 1, w_lo, jnp.where(lane == 2, w_hi, 0.0)))


def _outproj_body(mod_ref, n2_ref, streams, hgn_ref, bd_ref, wout_ref, wrh_ref, wrl_ref, br_ref,
                  x1_ref, hp_ref, route_ref):
    x_ref, rof_ref, rob_ref, rsg_ref, da_ref, hof_ref, hob_ref, hsg_ref = streams
    ro = rof_ref[...] + rob_ref[...]
    ro = ro * lax.rsqrt(_group_mean_sq(ro, bd_ref, HEAD_D) + RMS_EPS) * rsg_ref[...].astype(F32)
    ho = hof_ref[...] + hob_ref[...]
    ho = ho * lax.rsqrt(_group_mean_sq(ho, bd_ref, HEAD_D) + RMS_EPS) * hgn_ref[...] * hsg_ref[...].astype(F32)
    mix = jnp.concatenate([ro.astype(BF16), da_ref[...], ho.astype(BF16)], axis=1)
    mixed = _dot(mix, wout_ref[...])
    gate1 = mod_ref[2:3, :]
    shift2 = mod_ref[3:4, :]
    scale2 = mod_ref[4:5, :]
    x1 = x_ref[...] + gate1 * mixed
    ms = jnp.mean(x1 * x1, axis=-1, keepdims=True)
    h2 = x1 * lax.rsqrt(ms + RMS_EPS) * n2_ref[...] * (1.0 + scale2) + shift2
    h_hi = h2.astype(BF16)
    h_lo = (h2 - h_hi.astype(F32)).astype(BF16)
    logits = _dot(h_hi, wrh_ref[...]) + _dot(h_lo, wrh_ref[...]) + _dot(h_hi, wrl_ref[...]) + br_ref[...]
    route = _route(logits)
    x1_ref[...] = x1
    route_ref[...] = route
    hp_ref[...] = jnp.concatenate([h2, route], axis=1)


def _outproj_kernel(mod_ref, n2_ref, *rest):
    ctx_streams, lat_streams, tail = rest[0:8], rest[8:16], rest[16:]
    t = pl.program_id(0)

    @pl.when(t < CTX_PTILES)
    def _():
        _outproj_body(mod_ref, n2_ref, ctx_streams, *tail)

    @pl.when(t >= CTX_PTILES)
    def _():
        _outproj_body(mod_ref, n2_ref, lat_streams, *tail)


def _cond_of_tile(t):
    return jnp.where(t < CTX_TILES, 0, 1 + (t - CTX_TILES) // LAT_TILES_PER_SEQ)


def _out_projection(layer, mod, norm2, ctx_streams, lat_streams, hgn_t, bd256, w_out_bf, w_route_hi, w_route_lo,
                    b_route):
    row = lambda t: (t, 0)
    ctx_row = lambda t: (jnp.minimum(t, CTX_PTILES - 1), 0)
    lat_row = lambda t: (jnp.maximum(t - CTX_PTILES, 0), 0)
    lay = lambda t: (layer, 0, 0)
    cond = lambda t: jnp.where(t < CTX_PTILES, 0, 1 + (t - CTX_PTILES) // LAT_PTILES_PER_SEQ)
    widths = (D_MODEL, MIX_G, MIX_G, MIX_G, DA_W, MIX_G, MIX_G, MIX_G)
    in_specs = [
        pl.BlockSpec((None, None, 6, D_MODEL), lambda t: (layer, cond(t), 0, 0)),
        pl.BlockSpec((None, 1, D_MODEL), lay),
    ]
    in_specs += [pl.BlockSpec((PTILE, w), ctx_row) for w in widths]
    in_specs += [pl.BlockSpec((PTILE, w), lat_row) for w in widths]
    in_specs += [
        pl.BlockSpec((None, 1, MIX_G), lay),
        pl.BlockSpec((MIX_G, MIX_G), lambda t: (0, 0)),
        pl.BlockSpec((None, D_MODEL, D_MODEL), lay),
        pl.BlockSpec((None, D_MODEL, 128), lay),
        pl.BlockSpec((None, D_MODEL, 128), lay),
        pl.BlockSpec((None, 1, 128), lay),
    ]
    return pl.pallas_call(
        _outproj_kernel,
        grid=(N_ROWS // PTILE,),
        in_specs=in_specs,
        out_specs=[pl.BlockSpec((PTILE, D_MODEL), row), pl.BlockSpec((PTILE, ROW_W), row),
                   pl.BlockSpec((PTILE, 128), row)],
        out_shape=[jax.ShapeDtypeStruct((N_ROWS, D_MODEL), F32), jax.ShapeDtypeStruct((N_ROWS, ROW_W), F32),
                   jax.ShapeDtypeStruct((N_ROWS, 128), F32)],
        compiler_params=_cparams(1),
        name="out_projection",
    )(mod, norm2, *ctx_streams, *lat_streams, hgn_t, bd256, w_out_bf, w_route_hi, w_route_lo, b_route)


def _dispatch_kernel(route_ref, dest_ref, meta_ref, cnt_ref):
    ph = pl.program_id(0)
    t = pl.program_id(1)
    lane = lax.broadcasted_iota(jnp.int32, (TILE, 128), 1).astype(F32)
    onehots = [(lane == route_ref[j * TILE:(j + 1) * TILE, 0:1]).astype(F32) for j in range(DISPATCH_TILES)]
    tile_cnts = [jnp.sum(oh, axis=0, keepdims=True) for oh in onehots]

    @pl.when(jnp.logical_and(ph == 0, t == 0))
    def _():
        cnt_ref[...] = jnp.zeros_like(cnt_ref)

    @pl.when(ph == 0)
    def _():
        cnt_ref[0:1, :] += sum(tile_cnts[1:], tile_cnts[0])

    @pl.when(jnp.logical_and(ph == 1, t == 0))
    def _():
        cnt = cnt_ref[0:1, :]
        padded = jnp.floor((cnt + (MOE_TILE - 1.0)) * (1.0 / MOE_TILE)) * MOE_TILE
        r = lax.broadcasted_iota(jnp.int32, (128, 128), 0)
        c = lax.broadcasted_iota(jnp.int32, (128, 128), 1)
        off = jnp.dot(jnp.broadcast_to(padded, (8, 128)), (r < c).astype(F32),
                      preferred_element_type=F32, precision=HI)[0:1, :]
        cnt_ref[1:2, :] = off
        end = off + padded
        end_col = jnp.sum(jnp.where(r == c, jnp.broadcast_to(end, (128, 128)), 0.0), axis=1, keepdims=True)
        ended = jnp.logical_and(end_col <= c.astype(F32) * MOE_TILE, r < N_BUCKETS)
        tile_bucket = jnp.sum(ended.astype(F32), axis=0, keepdims=True)
        lane1 = lax.broadcasted_iota(jnp.int32, (1, 128), 1)
        tile_idx = lane1.astype(F32)
        n_valid = jnp.max(jnp.where(lane1 < N_BUCKETS, end, 0.0), axis=1, keepdims=True) * (1.0 / MOE_TILE)
        last = jnp.max(jnp.where(tile_idx < n_valid, tile_bucket, 0.0), axis=1, keepdims=True)
        tb = jnp.where(tile_idx < n_valid, tile_bucket, last)
        g = jnp.floor(tb * (1.0 / N_PAIRS))
        pid = tb - g * N_PAIRS
        a = (pid >= 3.0).astype(F32) + (pid >= 5.0).astype(F32)
        b = pid + 1.0 - 2.0 * (a >= 1.0).astype(F32) - (a >= 2.0).astype(F32)
        rows = [EPG * g + a, EPG * g + b, jnp.broadcast_to(n_valid, (1, 128)), jnp.zeros((5, 128), F32)]
        meta_ref[...] = jnp.concatenate(rows, axis=0).astype(jnp.int32)

    @pl.when(ph == 1)
    def _():
        base = cnt_ref[1:2, :]
        rr = lax.broadcasted_iota(jnp.int32, (TILE, TILE), 0)
        cc = lax.broadcasted_iota(jnp.int32, (TILE, TILE), 1)
        for j, onehot in enumerate(onehots):
            oh16 = onehot.astype(BF16)
            same = _dot_nt(oh16, oh16)
            rank = jnp.sum(jnp.where(rr < cc, same, 0.0), axis=0, keepdims=True)
            base_row = lax.dot_general(jnp.broadcast_to(base, (8, 128)), onehot, (((1,), (1,)), ((), ())),
                                       preferred_element_type=F32, precision=HI)[0:1, :]
            dest_ref[j] = (base_row + rank).astype(jnp.int32)
            base = base + tile_cnts[j]
        cnt_ref[1:2, :] = base


def _moe_dispatch(route):
    dest, meta = pl.pallas_call(
        _dispatch_kernel,
        grid=(2, N_TILES // DISPATCH_TILES),
        in_specs=[pl.BlockSpec((DISPATCH_TILES * TILE, 128), lambda ph, t: (t, 0))],
        out_specs=[pl.BlockSpec((DISPATCH_TILES, 1, TILE), lambda ph, t: (t * ph, 0, 0)),
                   pl.BlockSpec((8, 128), lambda ph, t: (0, 0))],
        out_shape=[jax.ShapeDtypeStruct((N_TILES, 1, TILE), jnp.int32),
                   jax.ShapeDtypeStruct((8, 128), jnp.int32)],
        scratch_shapes=[pltpu.VMEM((8, 128), F32)],
        compiler_params=_cparams(2),
        name="moe_dispatch",
    )(route)
    return dest.reshape(N_ROWS), meta[0:3].reshape(3 * 128)


def _dma_params():
    return pltpu.CompilerParams(dimension_semantics=("arbitrary",), vmem_limit_bytes=VMEM_LIMIT,
                                disable_bounds_checks=True)


def _moe_scatter_kernel(dest_ref, h_ref, hs_init_ref, hs_ref, stage, sems):
    del hs_init_ref
    t = pl.program_id(0)
    base = t * TILE
    slot = t % 2
    stage[slot] = h_ref[...]

    def issue(g, carry):
        for j in range(DMA_GROUP):
            i = g * DMA_GROUP + j
            d = dest_ref[base + i]
            pltpu.make_async_copy(stage.at[slot, pl.ds(i, 1)], hs_ref.at[pl.ds(d, 1)],
                                  sems.at[slot]).start(priority=j % 2)
        return carry

    lax.fori_loop(0, TILE // DMA_GROUP, issue, 0)

    def wait_tile(s):
        pltpu.make_async_copy(stage.at[s], hs_ref.at[pl.ds(0, TILE)], sems.at[s]).wait()

    @pl.when(t > 0)
    def _():
        wait_tile(1 - slot)

    @pl.when(t == pl.num_programs(0) - 1)
    def _():
        wait_tile(slot)


def _moe_scatter(dest, hp, hs_init):
    return pl.pallas_call(
        _moe_scatter_kernel,
        grid_spec=pltpu.PrefetchScalarGridSpec(
            num_scalar_prefetch=1,
            grid=(N_TILES,),
            in_specs=[pl.BlockSpec((TILE, ROW_W), lambda t, dest: (t, 0)),
                      pl.BlockSpec(memory_space=pl.ANY)],
            out_specs=pl.BlockSpec(memory_space=pl.ANY),
            scratch_shapes=[pltpu.VMEM((2, TILE, ROW_W), F32), pltpu.SemaphoreType.DMA((2,))],
        ),
        out_shape=jax.ShapeDtypeStruct((N_SLOTS, ROW_W), F32),
        input_output_aliases={2: 0},
        compiler_params=_dma_params(),
        name="moe_scatter",
    )(dest, hp, hs_init)


def _moe_ffn_kernel(meta_ref, hs_ref, wgl_ref, wul_ref, wdl_ref, wgh_ref, wuh_ref, wdh_ref, ys_ref):
    t = pl.program_id(0)
    n_valid = meta_ref[2 * 128]

    @pl.when(t < n_valid)
    def _():
        w = hs_ref[...]
        x = w[:, :D_MODEL].astype(BF16)
        rw = w[:, D_MODEL:]

        def gated(wg_ref, wu_ref, weight):
            return (_silu(_dot(x, wg_ref[...])) * _dot(x, wu_ref[...]) * weight).astype(BF16)

        a_lo = gated(wgl_ref, wul_ref, rw[:, 1:2])
        a_hi = gated(wgh_ref, wuh_ref, rw[:, 2:3])
        ys_ref[...] = _dot(a_lo, wdl_ref[...]) + _dot(a_hi, wdh_ref[...])

    @pl.when(t >= n_valid)
    def _():
        ys_ref[...] = jnp.zeros_like(ys_ref)


def _moe_ffn(layer, meta, hs, w_gate_bf, w_up_bf, w_down_bf):
    lo = lambda t, m: (layer, m[t], 0, 0)
    hi = lambda t, m: (layer, m[128 + t], 0, 0)
    up = lambda im: pl.BlockSpec((None, None, D_MODEL, EXPERT_FF), im)
    down = lambda im: pl.BlockSpec((None, None, EXPERT_FF, D_MODEL), im)
    return pl.pallas_call(
        _moe_ffn_kernel,
        grid_spec=pltpu.PrefetchScalarGridSpec(
            num_scalar_prefetch=1,
            grid=(MOE_MAX_TILES,),
            in_specs=[pl.BlockSpec((MOE_TILE, ROW_W), lambda t, m: (jnp.maximum(jnp.minimum(t, m[2 * 128] - 1), 0), 0)),
                      up(lo), up(lo), down(lo), up(hi), up(hi), down(hi)],
            out_specs=pl.BlockSpec((MOE_TILE, D_MODEL), lambda t, m: (t, 0)),
        ),
        out_shape=jax.ShapeDtypeStruct((N_SLOTS, D_MODEL), F32),
        compiler_params=_cparams(1),
        name="moe_ffn",
    )(meta, hs, w_gate_bf, w_up_bf, w_down_bf, w_gate_bf, w_up_bf, w_down_bf)


def _moe_combine_kernel(dest_ref, x1_ref, mod_ref, ys_ref, out_ctx_ref, out_lat_ref, buf, sems):
    t = pl.program_id(0)
    nt = pl.num_programs(0)

    def issue(tile, slot):
        def body(g, carry):
            for j in range(DMA_GROUP):
                i = g * DMA_GROUP + j
                d = dest_ref[tile * TILE + i]
                pltpu.make_async_copy(ys_ref.at[pl.ds(d, 1)], buf.at[slot, pl.ds(i, 1)],
                                      sems.at[slot]).start(priority=j % 2)
            return carry
        lax.fori_loop(0, TILE // DMA_GROUP, body, 0)

    @pl.when(t == 0)
    def _():
        issue(0, 0)

    @pl.when(t + 1 < nt)
    def _():
        issue(t + 1, (t + 1) % 2)

    slot = t % 2
    pltpu.make_async_copy(ys_ref.at[pl.ds(0, TILE)], buf.at[slot], sems.at[slot]).wait()
    x2 = x1_ref[...] + mod_ref[5:6, :] * buf[slot]

    @pl.when(t < CTX_TILES)
    def _():
        out_ctx_ref[...] = x2

    @pl.when(t >= CTX_TILES)
    def _():
        out_lat_ref[...] = x2


def _moe_combine(layer, dest, x1, mod, ys):
    return pl.pallas_call(
        _moe_combine_kernel,
        grid_spec=pltpu.PrefetchScalarGridSpec(
            num_scalar_prefetch=1,
            grid=(N_TILES,),
            in_specs=[pl.BlockSpec((TILE, D_MODEL), lambda t, dest: (t, 0)),
                      pl.BlockSpec((None, None, 6, D_MODEL), lambda t, dest: (layer, _cond_of_tile(t), 0, 0)),
                      pl.BlockSpec(memory_space=pl.ANY)],
            out_specs=[pl.BlockSpec((TILE, D_MODEL), lambda t, dest: (jnp.minimum(t, CTX_TILES - 1), 0)),
                       pl.BlockSpec((TILE, D_MODEL), lambda t, dest: (jnp.maximum(t - CTX_TILES, 0), 0))],
            scratch_shapes=[pltpu.VMEM((2, TILE, D_MODEL), F32), pltpu.SemaphoreType.DMA((2,))],
        ),
        out_shape=[jax.ShapeDtypeStruct((N_CTX_ROWS, D_MODEL), F32), jax.ShapeDtypeStruct((N_LAT_ROWS, D_MODEL), F32)],
        compiler_params=_dma_params(),
        name="moe_combine",
    )(dest, x1, mod, ys)


def _block_diag_ones(n, group):
    i = np.arange(n) // group
    return jnp.asarray((i[:, None] == i[None, :]).astype(np.float32), dtype=BF16)


def _rope_tables():
    pos = np.arange(DEC_SEQ)
    rows = (pos // GRID_W).astype(np.float32)
    cols = (pos % GRID_W).astype(np.float32)
    inv_freq = (ROPE_BASE ** (-(np.arange(ROPE_PAIRS, dtype=np.float32) / ROPE_PAIRS))).astype(np.float32)
    ang_r = rows[:, None] * inv_freq[None, :]
    ang_c = cols[:, None] * inv_freq[None, :]
    cos64 = np.concatenate([np.cos(ang_r), np.cos(ang_r), np.cos(ang_c), np.cos(ang_c)], axis=1)
    sin64 = np.concatenate([-np.sin(ang_r), np.sin(ang_r), -np.sin(ang_c), np.sin(ang_c)], axis=1)
    return (jnp.asarray(np.tile(cos64, (1, 2)), F32), jnp.asarray(np.tile(sin64, (1, 2)), F32))


def _mixers(latent, layer, x, consts, params, ret_s0, hg_s0, cache_k, cache_v):
    (mod, lb_all, bd512, bd256, rope_cos, rope_sin, ret_tables) = consts
    n_seq = DEC_BATCH if latent else BATCH
    seq_len = DEC_SEQ if latent else SEQ
    nt = seq_len // TILE
    proj = _in_projection(latent, layer, x, mod, params['norm1'], params['w_in'], params['w_vt'], params['da_qn'],
                          params['da_kn'], lb_all, bd512, rope_cos, rope_sin)
    (rq, rk, rv, rsg, dq, dk, dv, hq, hkf, hkb, hv, hgf, hgb, hsg) = proj[:14]
    ret_of, ret_ob, ret_fin = _retention(n_seq, nt, rq, rk, rv, ret_s0, ret_tables,
                                         "retention_lat" if latent else "retention_ctx")
    hg_of, hg_ob, hg_fin = _gated_scan(n_seq, nt, hq, hkf, hkb, hv, hgf, hgb, hg_s0,
                                       "hgrn_scan_lat" if latent else "hgrn_scan_ctx")
    if latent:
        da_o = _diff_attention(layer, n_seq, seq_len, dq, dk, dv, cache_k, cache_v,
                               params['da_lambda'], params['da_subln'])
    else:
        da_o = _diff_attention_short(layer, n_seq, seq_len, dq, dk, dv, params['da_lambda'], params['da_subln'])
    streams = (x, ret_of, ret_ob, rsg, da_o, hg_of, hg_ob, hsg)
    extras = None if latent else (proj[14], proj[15], ret_fin, hg_fin)
    return streams, extras


def _trunk_layer(layer, x_ctx, x_lat, sorted_buf, consts, params, lat_ret_s0, lat_hg_s0, zero_state, cache_k, cache_v):
    mod, bd256 = consts[0], consts[3]
    ctx_streams, extras = _mixers(False, layer, x_ctx, consts, params, zero_state, zero_state, None, None)
    lat_streams, _ = _mixers(True, layer, x_lat, consts, params, lat_ret_s0, lat_hg_s0, cache_k, cache_v)
    x1, hp, route = _out_projection(layer, mod, params['norm2'], ctx_streams, lat_streams, params['hg_norm'],
                                    bd256, params['w_out'], params['w_route_hi'], params['w_route_lo'],
                                    params['b_route'])
    dest, meta = _moe_dispatch(route)
    sorted_buf = _moe_scatter(dest, hp, sorted_buf)
    ys = _moe_ffn(layer, meta, sorted_buf, params['w_gate'], params['w_up'], params['w_down'])
    x_ctx, x_lat = _moe_combine(layer, dest, x1, mod, ys)
    return x_ctx, x_lat, sorted_buf, extras


def kernel(x_prompt, x_sample, cache_k, cache_v, state_ret, state_hgrn, c, c_ctx, norm1, norm2, w_mod, b_mod,
           w_in, w_out, da_qn, da_kn, da_lambda, da_subln, hg_lb, hg_norm, w_group, b_group, w_router,
           b_router, w_gate, w_up, w_down):
    cond = jnp.zeros((N_COND, D_MODEL), F32).at[0].set(c_ctx).at[1:1 + DEC_BATCH].set(c)
    mod, lb_all = _modulation(cond, w_mod, b_mod, hg_lb)
    mod = mod.reshape(DEPTH, N_COND, 6, D_MODEL)
    rope_cos, rope_sin = _rope_tables()
    consts = (mod, lb_all.reshape(DEPTH, 1, MIX_G), _block_diag_ones(DA_W, HEAD_D),
              _block_diag_ones(MIX_G, HEAD_D), rope_cos, rope_sin, _retention_tables())
    pad = jnp.zeros((DEPTH, D_MODEL, 128 - N_EXPERTS - N_GROUPS), F32)
    w_route = jnp.concatenate([w_router, w_group, pad], axis=-1)
    params = {
        'norm1': norm1.reshape(DEPTH, 1, D_MODEL), 'norm2': norm2.reshape(DEPTH, 1, D_MODEL),
        'w_in': w_in.astype(BF16), 'w_out': w_out.astype(BF16),
        'w_vt': jnp.swapaxes(lax.optimization_barrier(w_in[:, :, 2048:2560]), 1, 2).astype(BF16),
        'da_qn': jnp.tile(da_qn, (1, DA_W // HEAD_D)).reshape(DEPTH, 1, DA_W),
        'da_kn': jnp.tile(da_kn, (1, DA_W // HEAD_D)).reshape(DEPTH, 1, DA_W),
        'da_lambda': da_lambda, 'da_subln': da_subln.reshape(DEPTH, 1, 128),
        'hg_norm': jnp.tile(hg_norm, (1, HEADS)).reshape(DEPTH, 1, MIX_G),
        'w_route_hi': w_route.astype(BF16),
        'w_route_lo': (w_route - w_route.astype(BF16).astype(F32)).astype(BF16),
        'b_route': jnp.concatenate([b_router, b_group, pad[:, 0, :]], axis=-1).reshape(DEPTH, 1, 128),
        'w_gate': w_gate.astype(BF16), 'w_up': w_up.astype(BF16), 'w_down': w_down.astype(BF16),
    }
    ck_bf = cache_k.astype(BF16).reshape(DEC_BATCH, DEPTH, PAST_LEN, DA_W)
    cv_bf = jnp.swapaxes(cache_v.astype(BF16).reshape(DEC_BATCH, DEPTH, PAST_LEN, DA_W), 2, 3)
    zero_state = jnp.zeros((BATCH, 2, HEADS, HEAD_D, HEAD_D), F32)

    yp = x_prompt.reshape(N_CTX_ROWS, D_MODEL)
    ys = x_sample.reshape(N_LAT_ROWS, D_MODEL)
    sorted_buf = jnp.zeros((N_SLOTS, ROW_W), F32)
    ks_out, vs_out, rets_out, hgs_out = [], [], [], []
    for l in range(DEPTH):
        yp, ys, sorted_buf, (k_new, v_new, ret_fin, hg_fin) = _trunk_layer(
            l, yp, ys, sorted_buf, consts, params, state_ret[:, l], state_hgrn[:, l], zero_state, ck_bf, cv_bf)
        ks_out.append(k_new.reshape(BATCH, SEQ, DA_W))
        vs_out.append(v_new.reshape(BATCH, SEQ, DA_W))
        rets_out.append(ret_fin)
        hgs_out.append(hg_fin)
    return (yp.reshape(BATCH, SEQ, D_MODEL), ys.reshape(DEC_BATCH, DEC_SEQ, D_MODEL),
            jnp.stack(ks_out, axis=1).reshape(BATCH, DEPTH, SEQ, HEADS, 2, HEAD_D),
            jnp.stack(vs_out, axis=1).reshape(BATCH, DEPTH, SEQ, HEADS, 2 * HEAD_D),
            jnp.stack(rets_out, axis=1), jnp.stack(hgs_out, axis=1))
```

```python
import functools
import math

import numpy as np
import jax
import jax.numpy as jnp
from jax import lax
from jax.experimental import pallas as pl
from jax.experimental.pallas import tpu as pltpu

F32 = jnp.float32
BF16 = jnp.bfloat16

D_MODEL = 1024
BATCH = 16
SEQ = 256
DEPTH = 4
DEC_BATCH = 4
DEC_SEQ = 4096
PAST_LEN = 256
GRID_W = 64
HEADS = 4
HEAD_D = 64
MIX_G = HEADS * HEAD_D
DA_W = 512
IN_COLS = 3840
ROPE_BASE = 10000.0
ROPE_PAIRS = 16
RMS_EPS = 1e-6
N_GROUPS = 4
EPG = 4
N_EXPERTS = 16
EXPERT_FF = 512

TILE = 256
N_CTX_ROWS = BATCH * SEQ
N_LAT_ROWS = DEC_BATCH * DEC_SEQ
N_ROWS = N_CTX_ROWS + N_LAT_ROWS
CTX_TILES = N_CTX_ROWS // TILE
LAT_TILES_PER_SEQ = DEC_SEQ // TILE
N_TILES = N_ROWS // TILE
PTILE = 512
CTX_PTILES = N_CTX_ROWS // PTILE
LAT_PTILES_PER_SEQ = DEC_SEQ // PTILE
N_COND = 8

N_PAIRS = EPG * (EPG - 1) // 2
N_BUCKETS = N_GROUPS * N_PAIRS
MOE_TILE = 256
MOE_MAX_TILES = N_ROWS // MOE_TILE + N_BUCKETS
DISPATCH_TILES = 4
DMA_GROUP = 8
N_SLOTS = MOE_MAX_TILES * MOE_TILE
ROW_W = D_MODEL + 128

CHUNK = 64
SUB = 16
NSUB = CHUNK // SUB
NCHUNK = TILE // CHUNK

VMEM_LIMIT = 48 * 1024 * 1024
NEG_BIG = -1e30
ONES_ROWS = 16
ATTN_TQ = 512
ATTN_KEY_CHUNK = 512
LOG2_E = 1.4426950408889634

HI = lax.Precision.HIGHEST


def _cparams(n_axes):
    return pltpu.CompilerParams(dimension_semantics=("arbitrary",) * n_axes,
                                vmem_limit_bytes=VMEM_LIMIT)


def _dot(a, b):
    return jnp.dot(a, b, preferred_element_type=F32)


def _dot_nt(a, b):
    return lax.dot_general(a, b, (((1,), (1,)), ((), ())), preferred_element_type=F32)


def _dot_tn(a, b):
    return lax.dot_general(a, b, (((0,), (0,)), ((), ())), preferred_element_type=F32)


def _silu(x):
    return x * (1.0 / (1.0 + jnp.exp(-x)))


def _sigmoid(x):
    return 1.0 / (1.0 + jnp.exp(-x))


def _mod_kernel(cond_ref, w_ref, b_ref, lb_ref, mod_ref, lbo_ref):
    c = cond_ref[...]
    m = jnp.dot(_silu(c), w_ref[...], preferred_element_type=F32, precision=HI)
    mod_ref[...] = m + b_ref[...]
    z = lb_ref[...]
    zmax = jnp.max(z, axis=0, keepdims=True)
    e = jnp.exp(z - zmax)
    p = e / jnp.sum(e, axis=0, keepdims=True)
    rows = [jnp.zeros_like(p[0:1])]
    for l in range(1, DEPTH):
        rows.append(rows[-1] + p[l:l + 1])
    lbo_ref[...] = jnp.concatenate(rows, axis=0)


def _modulation(cond, w_mod, b_mod, hg_lb):
    nblk = 6
    return pl.pallas_call(
        _mod_kernel,
        grid=(DEPTH, nblk),
        in_specs=[
            pl.BlockSpec((N_COND, D_MODEL), lambda l, j: (0, 0)),
            pl.BlockSpec((None, D_MODEL, D_MODEL), lambda l, j: (l, 0, j)),
            pl.BlockSpec((None, 1, D_MODEL), lambda l, j: (l, 0, j)),
            pl.BlockSpec((DEPTH, MIX_G), lambda l, j: (0, 0)),
        ],
        out_specs=[
            pl.BlockSpec((None, N_COND, D_MODEL), lambda l, j: (l, 0, j)),
            pl.BlockSpec((DEPTH, MIX_G), lambda l, j: (0, 0)),
        ],
        out_shape=[jax.ShapeDtypeStruct((DEPTH, N_COND, 6 * D_MODEL), F32),
                   jax.ShapeDtypeStruct((DEPTH, MIX_G), F32)],
        compiler_params=_cparams(2),
        name="modulation",
    )(cond, w_mod, b_mod.reshape(DEPTH, 1, 6 * D_MODEL), hg_lb)


def _group_mean_sq(x, bd_ref, group):
    return _dot((x * x).astype(BF16), bd_ref[...]) * (1.0 / group)


def _swap16(x):
    w = x.shape[-1]
    lane = lax.broadcasted_iota(jnp.int32, x.shape, x.ndim - 1)
    up = pltpu.roll(x, w - 16, x.ndim - 1)
    dn = pltpu.roll(x, 16, x.ndim - 1)
    return jnp.where((lane % 32) < 16, up, dn)


def _inproj_kernel(latent, x_ref, mod_ref, n1_ref, w_ref, wvt_ref, qn_ref, kn_ref, lb_ref, bd_ref, *rest):
    if latent:
        cos_ref, sin_ref = rest[0], rest[1]
        outs = rest[2:]
    else:
        outs = rest
    (rq_ref, rk_ref, rv_ref, rsg_ref, dq_ref, dk_ref, dv_ref,
     hq_ref, hkf_ref, hkb_ref, hv_ref, hgf_ref, hgb_ref, hsg_ref) = outs[:14]

    x = x_ref[...]
    ms = jnp.mean(x * x, axis=-1, keepdims=True)
    shift1 = mod_ref[0:1, :]
    scale1 = mod_ref[1:2, :]
    h = x * lax.rsqrt(ms + RMS_EPS) * n1_ref[...] * (1.0 + scale1) + shift1
    h16 = h.astype(BF16)
    y = _dot(h16, w_ref[...])

    rq_ref[...] = y[:, 0:256].astype(BF16)
    rk_ref[...] = (y[:, 256:512] * (HEAD_D ** -0.5)).astype(BF16)
    rv_ref[...] = y[:, 512:768].astype(BF16)
    rsg_ref[...] = _silu(y[:, 768:1024]).astype(BF16)

    dq = y[:, 1024:1536]
    dk = y[:, 1536:2048]
    qn = dq * lax.rsqrt(_group_mean_sq(dq, bd_ref, HEAD_D) + RMS_EPS) * qn_ref[...]
    kn = dk * lax.rsqrt(_group_mean_sq(dk, bd_ref, HEAD_D) + RMS_EPS) * kn_ref[...]
    if latent:
        cos = jnp.concatenate([cos_ref[...]] * 4, axis=1)
        sin = jnp.concatenate([sin_ref[...]] * 4, axis=1)
        qr = qn * cos + _swap16(qn) * sin
        kr = kn * cos + _swap16(kn) * sin
    else:
        qr, kr = qn, kn
        ck_ref, cv_ref = outs[14], outs[15]
        ck_ref[...] = kn
        cv_ref[...] = y[:, 2048:2560]
    dq_ref[...] = (qr * (HEAD_D ** -0.5 * LOG2_E)).astype(BF16)
    dk_ref[...] = kr.astype(BF16)
    dv_ref[...] = _dot_nt(wvt_ref[...], h16).astype(BF16)

    lb = lb_ref[...]
    f_f = lb + (1.0 - lb) * _sigmoid(y[:, 2816:3072])
    f_b = lb + (1.0 - lb) * _sigmoid(y[:, 3072:3328])
    hq_ref[...] = y[:, 2560:2816].astype(BF16)
    hkf_ref[...] = (1.0 - f_f).astype(BF16)
    hkb_ref[...] = (1.0 - f_b).astype(BF16)
    hgf_ref[...] = jnp.log2(f_f)
    hgb_ref[...] = jnp.log2(f_b)
    hv_ref[...] = y[:, 3328:3584].astype(BF16)
    hsg_ref[...] = _silu(y[:, 3584:3840]).astype(BF16)


def _in_projection(latent, layer, x, mod, norm1, w_in_bf, w_vt_bf, qn_t, kn_t, lb_all, bd512, rope_cos, rope_sin):
    nrows = x.shape[0]
    ntiles = nrows // PTILE
    cond = (lambda t: 1 + t // LAT_PTILES_PER_SEQ) if latent else (lambda t: 0)

    in_specs = [
        pl.BlockSpec((PTILE, D_MODEL), lambda t: (t, 0)),
        pl.BlockSpec((None, None, 6, D_MODEL), lambda t: (layer, cond(t), 0, 0)),
        pl.BlockSpec((None, 1, D_MODEL), lambda t: (layer, 0, 0)),
        pl.BlockSpec((None, D_MODEL, IN_COLS), lambda t: (layer, 0, 0)),
        pl.BlockSpec((None, DA_W, D_MODEL), lambda t: (layer, 0, 0)),
        pl.BlockSpec((None, 1, DA_W), lambda t: (layer, 0, 0)),
        pl.BlockSpec((None, 1, DA_W), lambda t: (layer, 0, 0)),
        pl.BlockSpec((None, 1, MIX_G), lambda t: (layer, 0, 0)),
        pl.BlockSpec((DA_W, DA_W), lambda t: (0, 0)),
    ]
    args = [x, mod, norm1, w_in_bf, w_vt_bf, qn_t, kn_t, lb_all, bd512]
    if latent:
        in_specs += [pl.BlockSpec((PTILE, 128), lambda t: (t % LAT_PTILES_PER_SEQ, 0))] * 2
        args += [rope_cos, rope_sin]

    def o(width, dtype):
        return jax.ShapeDtypeStruct((nrows, width), dtype), pl.BlockSpec((PTILE, width), lambda t: (t, 0))

    dv_t = (jax.ShapeDtypeStruct((DA_W, nrows), BF16), pl.BlockSpec((DA_W, PTILE), lambda t: (0, t)))
    outs = [o(256, BF16), o(256, BF16), o(256, BF16), o(256, BF16),
            o(512, BF16), o(512, BF16), dv_t,
            o(256, BF16), o(256, BF16), o(256, BF16), o(256, BF16), o(256, F32), o(256, F32), o(256, BF16)]
    if not latent:
        outs += [o(512, F32), o(512, F32)]
    return pl.pallas_call(
        functools.partial(_inproj_kernel, latent),
        grid=(ntiles,),
        in_specs=in_specs,
        out_specs=[s for _, s in outs],
        out_shape=[s for s, _ in outs],
        compiler_params=_cparams(1),
        name="in_projection_lat" if latent else "in_projection_ctx",
    )(*args)


def _load_states(s0_ref, st_scr, transposed):
    st_scr[...] = jnp.zeros_like(st_scr)
    for d in range(2):
        for h in range(HEADS):
            blk = s0_ref[d, h]
            st_scr[d, h * HEAD_D:(h + 1) * HEAD_D, h * HEAD_D:(h + 1) * HEAD_D] = blk.T if transposed else blk


def _store_states(st_scr, sfin_ref, transposed):
    for d in range(2):
        for h in range(HEADS):
            blk = st_scr[d, h * HEAD_D:(h + 1) * HEAD_D, h * HEAD_D:(h + 1) * HEAD_D]
            sfin_ref[d, h] = blk.T if transposed else blk


def _head_masks():
    lane = lax.broadcasted_iota(jnp.int32, (1, MIX_G), 1)
    return [(lane // HEAD_D == h).astype(F32) for h in range(HEADS)]


def _scan_chunks(items, st_f, st_b, consts_f, consts_b):
    n = len(items)
    cs = [consts_b if it[4] else consts_f for it in items]
    zero_row = jnp.zeros((1, MIX_G), F32)

    bs = []
    for x in range(n):
        g = items[x][3]
        g_hi = g.astype(BF16)
        g_lo = (g - g_hi.astype(F32)).astype(BF16)
        bs.append(_dot(cs[x][0], g_hi) + _dot(cs[x][0], g_lo))

    prep = []
    for x in range(n):
        q, k, v, _, rev = items[x]
        hmask = cs[x][4]
        b = bs[x]
        r_rows, e_rows = [], []
        for s in range(NSUB):
            lo, hi = s * SUB, s * SUB + SUB - 1
            if not rev:
                r_rows.append(b[lo - 1:lo, :] if s > 0 else zero_row)
                e_rows.append(b[hi:hi + 1, :])
            else:
                r_rows.append(b[hi + 1:hi + 2, :] if s < NSUB - 1 else zero_row)
                e_rows.append(b[lo:lo + 1, :])
        btot = e_rows[NSUB - 1] if not rev else e_rows[0]
        r_full = jnp.concatenate([jnp.broadcast_to(r, (SUB, MIX_G)) for r in r_rows], axis=0)
        e_full = jnp.concatenate([jnp.broadcast_to(e, (SUB, MIX_G)) for e in e_rows], axis=0)
        bl = b - r_full
        qp = q * jnp.exp2(bl)
        kend = k * jnp.exp2(e_full - b)
        pairs = [(i, j) for i in range(NSUB) for j in range(NSUB) if (j < i if not rev else j > i)]
        lh = jnp.concatenate([qp[i * SUB:(i + 1) * SUB, :] * jnp.exp2(r_rows[i] - e_rows[j])
                              for (i, j) in pairs], axis=0)
        prep.append(dict(
            bl=bl, btot=btot, pairs=pairs,
            q_inter=(qp * jnp.exp2(r_full)).astype(BF16),
            k_state=(kend * jnp.exp2(btot - e_full)).astype(BF16),
            kend=kend.astype(BF16),
            lhs4=jnp.concatenate([lh * hmask[h] for h in range(HEADS)], axis=0).astype(BF16)))

    uts = [_dot_tn(items[x][2], prep[x]['k_state']) for x in range(n)]
    scs = [_dot_nt(prep[x]['lhs4'], prep[x]['kend']) for x in range(n)]

    st_in = []
    for x in range(n):
        rev = items[x][4]
        st = st_b if rev else st_f
        st_in.append(st.astype(BF16))
        st = st * jnp.exp2(prep[x]['btot']) + uts[x] * cs[x][1]
        if rev:
            st_b = st
        else:
            st_f = st
    col = lax.broadcasted_iota(jnp.int32, (SUB, CHUNK), 1) // SUB
    jrow = lax.broadcasted_iota(jnp.int32, (SUB, MIX_G), 0)
    pss, pps, vts = [], [], []
    for x in range(n):
        q, k, v, _, rev = items[x]
        pairs, sc, bl = prep[x]['pairs'], scs[x], prep[x]['bl']
        p_rows = []
        for h in range(HEADS):
            for i in range(NSUB):
                acc = None
                for p, (pi, pj) in enumerate(pairs):
                    if pi != i:
                        continue
                    base = (h * len(pairs) + p) * SUB
                    blk = jnp.where(col == pj, sc[base:base + SUB, :], 0.0)
                    acc = blk if acc is None else acc + blk
                p_rows.append(jnp.zeros((SUB, CHUNK), F32) if acc is None else acc)
        pss.append(jnp.concatenate(p_rows, axis=0).astype(BF16))
        pp, vt = [], []
        for s in range(NSUB):
            sl = slice(s * SUB, (s + 1) * SUB)
            bl_s, q_s, k_s = bl[sl, :], q[sl, :].astype(BF16), k[sl, :].astype(BF16)
            for i in range(SUB):
                d = bl_s[i:i + 1, :] - bl_s
                keep = (jrow <= i) if not rev else (jrow >= i)
                e = jnp.exp2(jnp.where(keep, d, NEG_BIG).astype(BF16))
                pp.append(e * q_s[i:i + 1, :] * k_s)
            vt.extend([v[sl, :]] * SUB)
        pps.append(jnp.concatenate(pp, axis=0))
        vts.append(jnp.concatenate(vt, axis=0))

    o_int = [_dot_nt(prep[x]['q_inter'], st_in[x]) for x in range(n)]
    fulls = [_dot(pss[x], items[x][2]) for x in range(n)]
    sbs = [_dot(pps[x], cs[x][2]) for x in range(n)]

    ws = [(sbs[x] * vts[x].astype(F32)).astype(BF16) for x in range(n)]
    o_diag = [_dot(cs[x][3], ws[x]) for x in range(n)]

    outs = []
    for x in range(n):
        hmask = cs[x][4]
        o = o_int[x] + o_diag[x]
        for h in range(HEADS):
            o = o + fulls[x][h * CHUNK:(h + 1) * CHUNK, :] * hmask[h]
        outs.append(o)
    return outs, st_f, st_b


def _scan_consts(rev):
    r = lax.broadcasted_iota(jnp.int32, (CHUNK, CHUNK), 0)
    c = lax.broadcasted_iota(jnp.int32, (CHUNK, CHUNK), 1)
    tri = ((c <= r) if not rev else (c >= r)).astype(BF16)
    rr = lax.broadcasted_iota(jnp.int32, (MIX_G, MIX_G), 0) // HEAD_D
    cc = lax.broadcasted_iota(jnp.int32, (MIX_G, MIX_G), 1) // HEAD_D
    bd_mask = (rr == cc).astype(F32)
    ones_bd = bd_mask.astype(BF16)
    gi = lax.broadcasted_iota(jnp.int32, (CHUNK, CHUNK * SUB), 0)
    gj = lax.broadcasted_iota(jnp.int32, (CHUNK, CHUNK * SUB), 1) // SUB
    gsel = (gi == gj).astype(BF16)
    return tri, bd_mask, ones_bd, gsel, _head_masks()


def _scan_kernel(nt, qf_ref, kf_ref, vf_ref, gf_ref, qb_ref, kb_ref, vb_ref, gb_ref, s0_ref,
                 of_ref, ob_ref, sfin_ref, st_scr):
    t = pl.program_id(1)

    @pl.when(t == 0)
    def _():
        _load_states(s0_ref, st_scr, True)

    cf = _scan_consts(False)
    cb = _scan_consts(True)

    items, rows = [], []
    for c in range(NCHUNK):
        rf = slice(c * CHUNK, (c + 1) * CHUNK)
        rb = slice((NCHUNK - 1 - c) * CHUNK, (NCHUNK - c) * CHUNK)
        items.append((qf_ref[rf, :].astype(F32), kf_ref[rf, :].astype(F32), vf_ref[rf, :], gf_ref[rf, :], False))
        items.append((qb_ref[rb, :].astype(F32), kb_ref[rb, :].astype(F32), vb_ref[rb, :], gb_ref[rb, :], True))
        rows += [(of_ref, rf), (ob_ref, rb)]
    outs, st_f, st_b = _scan_chunks(items, st_scr[0], st_scr[1], cf, cb)
    for (ref, sl), o in zip(rows, outs):
        ref[sl, :] = o
    st_scr[0] = st_f
    st_scr[1] = st_b

    @pl.when(t == nt - 1)
    def _():
        _store_states(st_scr, sfin_ref, True)


def _gated_scan(n_seq, nt, q, k_f, k_b, v, g_f, g_b, s0, name):
    fwd = lambda s, t: (s * nt + t, 0)
    bwd = lambda s, t: (s * nt + (nt - 1 - t), 0)
    blk = lambda im: pl.BlockSpec((TILE, MIX_G), im)
    in_specs = [blk(fwd), blk(fwd), blk(fwd), blk(fwd), blk(bwd), blk(bwd), blk(bwd), blk(bwd),
                pl.BlockSpec((None, 2, HEADS, HEAD_D, HEAD_D), lambda s, t: (s, 0, 0, 0, 0))]
    nrows = n_seq * nt * TILE
    return pl.pallas_call(
        functools.partial(_scan_kernel, nt),
        grid=(n_seq, nt),
        in_specs=in_specs,
        out_specs=[pl.BlockSpec((TILE, MIX_G), lambda s, t: (s * nt + t, 0)),
                   pl.BlockSpec((TILE, MIX_G), lambda s, t: (s * nt + (nt - 1 - t), 0)),
                   pl.BlockSpec((None, 2, HEADS, HEAD_D, HEAD_D), lambda s, t: (s, 0, 0, 0, 0))],
        out_shape=[jax.ShapeDtypeStruct((nrows, MIX_G), F32),
                   jax.ShapeDtypeStruct((nrows, MIX_G), F32),
                   jax.ShapeDtypeStruct((n_seq, 2, HEADS, HEAD_D, HEAD_D), F32)],
        scratch_shapes=[pltpu.VMEM((2, MIX_G, MIX_G), F32)],
        compiler_params=_cparams(2),
        name=name,
    )(q, k_f, v, g_f, q, k_b, v, g_b, s0)


def _retention_tables():
    gam = 1.0 - 2.0 ** (-5.0 - np.arange(HEADS, dtype=np.float64))
    gam_r = gam[::-1]
    i = np.arange(TILE, dtype=np.float64)
    diff = i[:, None] - i[None, :]
    dcomb = np.zeros((HEADS, TILE, TILE))
    for h in range(HEADS):
        lower = np.where(diff > 0, gam[h] ** np.maximum(diff, 0), 0.0)
        upper = np.where(diff < 0, gam_r[h] ** np.maximum(-diff, 0), 0.0)
        dcomb[h] = lower + upper + 2.0 * (diff == 0)
    lanes = lambda per_head: np.repeat(per_head, HEAD_D, axis=-1)
    qd_f = lanes(gam[None, :] ** (i[:, None] + 1.0))
    kd_f = lanes(gam[None, :] ** (TILE - 1.0 - i[:, None]))
    qd_b = lanes(gam_r[None, :] ** (TILE - i[:, None]))
    kd_b = lanes(gam_r[None, :] ** i[:, None])
    blk = (np.arange(MIX_G)[:, None] // HEAD_D) == (np.arange(MIX_G)[None, :] // HEAD_D)
    c_f = np.where(blk, lanes(gam ** TILE)[None, :], 0.0) * np.ones((MIX_G, 1))
    c_b = np.where(blk, lanes(gam_r ** TILE)[None, :], 0.0) * np.ones((MIX_G, 1))
    f = lambda a: jnp.asarray(a, F32)
    return (f(dcomb.reshape(HEADS * TILE, TILE)), f(qd_f), f(kd_f), f(qd_b), f(kd_b), f(c_f), f(c_b))


def _ret_kernel(nt, qf_ref, kf_ref, vf_ref, qb_ref, kb_ref, vb_ref, s0_ref, dcomb_ref, qdf_ref, kdf_ref,
                qdb_ref, kdb_ref, cf_ref, cb_ref, of_ref, ob_ref, sfin_ref, st_scr):
    t = pl.program_id(1)

    @pl.when(t == 0)
    def _():
        _load_states(s0_ref, st_scr, False)

    lane_head = lax.broadcasted_iota(jnp.int32, (TILE, MIX_G), 1) // HEAD_D
    rr = lax.broadcasted_iota(jnp.int32, (MIX_G, MIX_G), 0) // HEAD_D
    cc = lax.broadcasted_iota(jnp.int32, (MIX_G, MIX_G), 1) // HEAD_D
    same_head = rr == cc

    q = qf_ref[...]
    k = kf_ref[...]
    v = vf_ref[...]
    qs = jnp.concatenate([jnp.where(lane_head == h, q, jnp.zeros_like(q)) for h in range(HEADS)], axis=0)
    p = (_dot_nt(qs, k) * dcomb_ref[...]).astype(BF16)
    full = _dot(p, v)
    o = _dot((q.astype(F32) * qdf_ref[...]).astype(BF16), st_scr[0].astype(BF16))
    for h in range(HEADS):
        o = o + jnp.where(lane_head == h, full[h * TILE:(h + 1) * TILE, :], 0.0)
    of_ref[...] = o
    u = _dot_tn((k.astype(F32) * kdf_ref[...]).astype(BF16), v)
    st_scr[0] = st_scr[0] * cf_ref[...] + jnp.where(same_head, u, 0.0)

    qb = qb_ref[...]
    kb = kb_ref[...]
    vb = vb_ref[...]
    ob_ref[...] = _dot((qb.astype(F32) * qdb_ref[...]).astype(BF16), st_scr[1].astype(BF16))
    ub = _dot_tn((kb.astype(F32) * kdb_ref[...]).astype(BF16), vb)
    st_scr[1] = st_scr[1] * cb_ref[...] + jnp.where(same_head, ub, 0.0)

    @pl.when(t == nt - 1)
    def _():
        _store_states(st_scr, sfin_ref, False)


def _retention(n_seq, nt, q, k, v, s0, tables, name):
    fwd = lambda s, t: (s * nt + t, 0)
    bwd = lambda s, t: (s * nt + (nt - 1 - t), 0)
    const = lambda s, t: (0, 0)
    blk = lambda im: pl.BlockSpec((TILE, MIX_G), im)
    sq = lambda: pl.BlockSpec((MIX_G, MIX_G), const)
    in_specs = [blk(fwd), blk(fwd), blk(fwd), blk(bwd), blk(bwd), blk(bwd),
                pl.BlockSpec((None, 2, HEADS, HEAD_D, HEAD_D), lambda s, t: (s, 0, 0, 0, 0)),
                pl.BlockSpec((HEADS * TILE, TILE), const), sq(), sq(), sq(), sq(), sq(), sq()]
    nrows = n_seq * nt * TILE
    return pl.pallas_call(
        functools.partial(_ret_kernel, nt),
        grid=(n_seq, nt),
        in_specs=in_specs,
        out_specs=[blk(fwd), blk(bwd), pl.BlockSpec((None, 2, HEADS, HEAD_D, HEAD_D), lambda s, t: (s, 0, 0, 0, 0))],
        out_shape=[jax.ShapeDtypeStruct((nrows, MIX_G), F32),
                   jax.ShapeDtypeStruct((nrows, MIX_G), F32),
                   jax.ShapeDtypeStruct((n_seq, 2, HEADS, HEAD_D, HEAD_D), F32)],
        scratch_shapes=[pltpu.VMEM((2, MIX_G, MIX_G), F32)],
        compiler_params=_cparams(2),
        name=name,
    )(q, k, v, q, k, v, s0, *tables)


def _attn_step(lam_init, comp, refs, s_write, s_read, part0_ref):
    q_ref, k_ref, ck_ref, v_ref, cv_ref, lam_ref, sub_ref, o_ref = refs
    sw, sw2, mw = s_write
    sr, sr2, mr = s_read

    q = q_ref[...]
    lane = lax.broadcasted_iota(jnp.int32, q.shape, 1)
    qc = jnp.where((lane // HEAD_D) == comp, q, jnp.zeros_like(q))
    n_keys = k_ref.shape[0]
    kc = min(ATTN_KEY_CHUNK, n_keys)
    ones_rows = jnp.ones((ONES_ROWS, kc), BF16)
    m_prev = mr[...]
    oe, m_cur = None, None
    for j in range(n_keys // kc):
        ks = slice(j * kc, (j + 1) * kc)
        d = _dot(jnp.concatenate([v_ref[:, ks], ones_rows], axis=0),
                 jnp.exp2((sr[ks, :] - m_prev).astype(BF16)))
        oe = d if oe is None else oe + d
        s = _dot_nt(k_ref[ks, :], qc)
        sw[ks, :] = s
        mj = jnp.max(s, axis=0, keepdims=True)
        m_cur = mj if m_cur is None else jnp.maximum(m_cur, mj)
    oe = oe + _dot(jnp.concatenate([cv_ref[...], jnp.ones((ONES_ROWS, cv_ref.shape[1]), BF16)], axis=0),
                   jnp.exp2((sr2[...] - m_prev).astype(BF16)))
    s2 = _dot_nt(ck_ref[...], qc)
    sw2[...] = s2
    m_cur = jnp.maximum(m_cur, jnp.max(s2, axis=0, keepdims=True))
    mw[...] = m_cur
    part = oe[:128, :] / oe[128:129, :]
    if comp == 1:
        part0_ref[...] = part
    else:
        lp = lam_ref[...]
        lam = (jnp.exp(jnp.sum(lp[0:1] * lp[1:2], axis=-1, keepdims=True))
               - jnp.exp(jnp.sum(lp[2:3] * lp[3:4], axis=-1, keepdims=True)) + lam_init)
        o = part0_ref[...] - lam * part
        ms = jnp.mean(o * o, axis=0, keepdims=True)
        o = (o * lax.rsqrt(ms + RMS_EPS)).T
        o_ref[...] = (o * sub_ref[...] * (1.0 - lam_init)).astype(BF16)


def _attn_kernel(lam_init, *refs):
    n_in = 8
    io, part0_ref, scr = refs[:n_in], refs[n_in], refs[n_in + 1:]
    half = len(scr) // 2
    buf_a, buf_b = scr[:half], scr[half:]
    u = pl.program_id(0)

    @pl.when(u == 0)
    def _():
        for r in buf_b + (part0_ref,):
            r[...] = jnp.zeros(r.shape, r.dtype)

    @pl.when(u % 2 == 0)
    def _():
        _attn_step(lam_init, 0, io, buf_a, buf_b, part0_ref)

    @pl.when(u % 2 == 1)
    def _():
        _attn_step(lam_init, 1, io, buf_b, buf_a, part0_ref)


def _attn_short_kernel(lam_init, q_ref, k_ref, v_ref, lam_ref, sub_ref, o_ref):
    lp = lam_ref[...]
    lam = (jnp.exp(jnp.sum(lp[0:1] * lp[1:2], axis=-1, keepdims=True))
           - jnp.exp(jnp.sum(lp[2:3] * lp[3:4], axis=-1, keepdims=True)) + lam_init)
    n_keys = k_ref.shape[0]
    ones_rows = jnp.ones((ONES_ROWS, n_keys), BF16)
    lane = lax.broadcasted_iota(jnp.int32, (q_ref.shape[0], 128), 1)
    units = [(h, c) for h in range(HEADS) for c in range(2)]
    scores = []
    for h, c in units:
        q = q_ref[:, h * 128:(h + 1) * 128]
        qc = jnp.where((lane // HEAD_D) == c, q, jnp.zeros_like(q))
        scores.append(_dot_nt(k_ref[:, h * 128:(h + 1) * 128], qc))
    probs = [jnp.exp2((s - jnp.max(s, axis=0, keepdims=True)).astype(BF16)) for s in scores]
    oes = [_dot(jnp.concatenate([v_ref[h * 128:(h + 1) * 128, :], ones_rows], axis=0), p)
           for (h, c), p in zip(units, probs)]
    parts = [oe[:128, :] / oe[128:129, :] for oe in oes]
    outs = []
    for h in range(HEADS):
        o = parts[2 * h] - lam * parts[2 * h + 1]
        ms = jnp.mean(o * o, axis=0, keepdims=True)
        outs.append((o * lax.rsqrt(ms + RMS_EPS)).T * sub_ref[...] * (1.0 - lam_init))
    o_ref[...] = jnp.concatenate(outs, axis=1).astype(BF16)


def _diff_attention_short(layer, n_seq, seq_len, q, k, v_t, da_lambda, subln):
    lam_init = 0.8 - 0.6 * math.exp(-0.3 * layer)
    return pl.pallas_call(
        functools.partial(_attn_short_kernel, lam_init),
        grid=(n_seq,),
        in_specs=[pl.BlockSpec((seq_len, DA_W), lambda b: (b, 0)),
                  pl.BlockSpec((seq_len, DA_W), lambda b: (b, 0)),
                  pl.BlockSpec((DA_W, seq_len), lambda b: (0, b)),
                  pl.BlockSpec((None, 4, HEAD_D), lambda b: (layer, 0, 0)),
                  pl.BlockSpec((None, 1, 128), lambda b: (layer, 0, 0))],
        out_specs=pl.BlockSpec((seq_len, DA_W), lambda b: (b, 0)),
        out_shape=jax.ShapeDtypeStruct((n_seq * seq_len, DA_W), BF16),
        compiler_params=_cparams(1),
        name="diff_attention_ctx",
    )(q, k, v_t, da_lambda, subln)


def _diff_attention(layer, n_seq, seq_len, q, k, v_t, cache_k, cache_v_t, da_lambda, subln):
    tq = min(ATTN_TQ, seq_len)
    nq = seq_len // tq
    n_units = n_seq * HEADS * nq * 2
    lam_init = 0.8 - 0.6 * math.exp(-0.3 * layer)

    cur = lambda u: jnp.minimum(u, n_units - 1) // 2
    prev = lambda u: jnp.maximum(u - 1, 0) // 2
    seq_of = lambda w: w // (HEADS * nq)
    head_of = lambda w: (w // nq) % HEADS
    rows_of = lambda w: seq_of(w) * nq + w % nq

    q_spec = pl.BlockSpec((tq, 128), lambda u: (rows_of(cur(u)), head_of(cur(u))))
    k_spec = pl.BlockSpec((seq_len, 128), lambda u: (seq_of(cur(u)), head_of(cur(u))))
    v_spec = pl.BlockSpec((128, seq_len), lambda u: (head_of(prev(u)), seq_of(prev(u))))
    ck_spec = pl.BlockSpec((None, None, PAST_LEN, 128), lambda u: (seq_of(cur(u)), layer, 0, head_of(cur(u))))
    cv_spec = pl.BlockSpec((None, None, 128, PAST_LEN), lambda u: (seq_of(prev(u)), layer, head_of(prev(u)), 0))
    tail = [pl.BlockSpec((None, 4, HEAD_D), lambda u: (layer, 0, 0)),
            pl.BlockSpec((None, 1, 128), lambda u: (layer, 0, 0))]
    in_specs = [q_spec, k_spec, ck_spec, v_spec, cv_spec] + tail
    args = [q, k, cache_k, v_t, cache_v_t, da_lambda, subln]
    scratch = [pltpu.VMEM((128, tq), F32)] + [pltpu.VMEM((seq_len, tq), F32), pltpu.VMEM((PAST_LEN, tq), F32),
                                              pltpu.VMEM((1, tq), F32)] * 2
    return pl.pallas_call(
        functools.partial(_attn_kernel, lam_init),
        grid=(n_units + 1,),
        in_specs=in_specs,
        out_specs=pl.BlockSpec((tq, 128), lambda u: (rows_of(prev(u)), head_of(prev(u)))),
        out_shape=jax.ShapeDtypeStruct((n_seq * seq_len, DA_W), BF16),
        scratch_shapes=scratch,
        compiler_params=_cparams(1),
        name="diff_attention_lat",
    )(*args)


def _route(logits):
    lane = lax.broadcasted_iota(jnp.int32, logits.shape, 1)
    lane_f = lane.astype(F32)
    big = 1e9
    is_g = jnp.logical_and(lane >= N_EXPERTS, lane < N_EXPERTS + N_GROUPS)
    gl = jnp.where(is_g, logits, NEG_BIG)
    gmax = jnp.max(gl, axis=-1, keepdims=True)
    gsel = jnp.min(jnp.where(gl == gmax, lane_f, big), axis=-1, keepdims=True) - N_EXPERTS
    g_w = 1.0 / jnp.sum(jnp.exp(gl - gmax), axis=-1, keepdims=True)
    in_grp = jnp.logical_and(lane < N_EXPERTS, (lane // EPG).astype(F32) == gsel)
    el = jnp.where(in_grp, logits, NEG_BIG)
    v1 = jnp.max(el, axis=-1, keepdims=True)
    i1 = jnp.min(jnp.where(el == v1, lane_f, big), axis=-1, keepdims=True)
    el2 = jnp.where(lane_f == i1, NEG_BIG, el)
    v2 = jnp.max(el2, axis=-1, keepdims=True)
    i2 = jnp.min(jnp.where(el2 == v2, lane_f, big), axis=-1, keepdims=True)
    t = jnp.exp(v2 - v1)
    w1 = g_w / (1.0 + t)
    w2 = w1 * t
    first_lo = i1 < i2
    a = jnp.minimum(i1, i2) - EPG * gsel
    b = jnp.maximum(i1, i2) - EPG * gsel
    bucket = gsel * N_PAIRS + a * (7.0 - a) * 0.5 + (b - a - 1.0)
    w_lo = jnp.where(first_lo, w1, w2)
    w_hi = jnp.where(first_lo, w2, w1)
    return jnp.where(lane == 0, bucket, jnp.where(lane == 1, w_lo, jnp.where(lane == 2, w_hi, 0.0)))


def _outproj_body(mod_ref, n2_ref, streams, hgn_ref, bd_ref, wout_ref, wr_ref, br_ref,
                  x1_ref, hp_ref, route_ref):
    x_ref, rof_ref, rob_ref, rsg_ref, da_ref, hof_ref, hob_ref, hsg_ref = streams
    ro = rof_ref[...] + rob_ref[...]
    ro = ro * lax.rsqrt(_group_mean_sq(ro, bd_ref, HEAD_D) + RMS_EPS) * rsg_ref[...].astype(F32)
    ho = hof_ref[...] + hob_ref[...]
    ho = ho * lax.rsqrt(_group_mean_sq(ho, bd_ref, HEAD_D) + RMS_EPS) * hgn_ref[...] * hsg_ref[...].astype(F32)
    mix = jnp.concatenate([ro.astype(BF16), da_ref[...], ho.astype(BF16)], axis=1)
    mixed = _dot(mix, wout_ref[...])
    gate1 = mod_ref[2:3, :]
    shift2 = mod_ref[3:4, :]
    scale2 = mod_ref[4:5, :]
    x1 = x_ref[...] + gate1 * mixed
    ms = jnp.mean(x1 * x1, axis=-1, keepdims=True)
    h2 = x1 * lax.rsqrt(ms + RMS_EPS) * n2_ref[...] * (1.0 + scale2) + shift2
    h_hi = h2.astype(BF16)
    h_lo = (h2 - h_hi.astype(F32)).astype(BF16)
    both = _dot(h_hi, wr_ref[...])
    logits = both[:, :128] + both[:, 128:] + _dot(h_lo, wr_ref[:, :128]) + br_ref[...]
    route = _route(logits)
    x1_ref[...] = x1
    route_ref[...] = route
    hp_ref[...] = jnp.concatenate([h2, route], axis=1)


def _outproj_kernel(mod_ref, n2_ref, *rest):
    ctx_streams, lat_streams, tail = rest[0:8], rest[8:16], rest[16:]
    t = pl.program_id(0)

    @pl.when(t < CTX_PTILES)
    def _():
        _outproj_body(mod_ref, n2_ref, ctx_streams, *tail)

    @pl.when(t >= CTX_PTILES)
    def _():
        _outproj_body(mod_ref, n2_ref, lat_streams, *tail)


def _cond_of_tile(t):
    return jnp.where(t < CTX_TILES, 0, 1 + (t - CTX_TILES) // LAT_TILES_PER_SEQ)


def _out_projection(layer, mod, norm2, ctx_streams, lat_streams, hgn_t, bd256, w_out_bf, w_route2, b_route):
    row = lambda t: (t, 0)
    ctx_row = lambda t: (jnp.minimum(t, CTX_PTILES - 1), 0)
    lat_row = lambda t: (jnp.maximum(t - CTX_PTILES, 0), 0)
    lay = lambda t: (layer, 0, 0)
    cond = lambda t: jnp.where(t < CTX_PTILES, 0, 1 + (t - CTX_PTILES) // LAT_PTILES_PER_SEQ)
    widths = (D_MODEL, MIX_G, MIX_G, MIX_G, DA_W, MIX_G, MIX_G, MIX_G)
    in_specs = [
        pl.BlockSpec((None, None, 6, D_MODEL), lambda t: (layer, cond(t), 0, 0)),
        pl.BlockSpec((None, 1, D_MODEL), lay),
    ]
    in_specs += [pl.BlockSpec((PTILE, w), ctx_row) for w in widths]
    in_specs += [pl.BlockSpec((PTILE, w), lat_row) for w in widths]
    in_specs += [
        pl.BlockSpec((None, 1, MIX_G), lay),
        pl.BlockSpec((MIX_G, MIX_G), lambda t: (0, 0)),
        pl.BlockSpec((None, D_MODEL, D_MODEL), lay),
        pl.BlockSpec((None, D_MODEL, 256), lay),
        pl.BlockSpec((None, 1, 128), lay),
    ]
    return pl.pallas_call(
        _outproj_kernel,
        grid=(N_ROWS // PTILE,),
        in_specs=in_specs,
        out_specs=[pl.BlockSpec((PTILE, D_MODEL), row), pl.BlockSpec((PTILE, ROW_W), row),
                   pl.BlockSpec((PTILE, 128), row)],
        out_shape=[jax.ShapeDtypeStruct((N_ROWS, D_MODEL), F32), jax.ShapeDtypeStruct((N_ROWS, ROW_W), F32),
                   jax.ShapeDtypeStruct((N_ROWS, 128), F32)],
        compiler_params=_cparams(1),
        name="out_projection",
    )(mod, norm2, *ctx_streams, *lat_streams, hgn_t, bd256, w_out_bf, w_route2, b_route)


def _dispatch_kernel(route_ref, dest_ref, meta_ref, cnt_ref):
    ph = pl.program_id(0)
    t = pl.program_id(1)
    lane = lax.broadcasted_iota(jnp.int32, (TILE, 128), 1).astype(F32)
    onehots = [(lane == route_ref[j * TILE:(j + 1) * TILE, 0:1]).astype(F32) for j in range(DISPATCH_TILES)]
    tile_cnts = [jnp.sum(oh, axis=0, keepdims=True) for oh in onehots]

    @pl.when(jnp.logical_and(ph == 0, t == 0))
    def _():
        cnt_ref[...] = jnp.zeros_like(cnt_ref)

    @pl.when(ph == 0)
    def _():
        cnt_ref[0:1, :] += sum(tile_cnts[1:], tile_cnts[0])

    @pl.when(jnp.logical_and(ph == 1, t == 0))
    def _():
        cnt = cnt_ref[0:1, :]
        padded = jnp.floor((cnt + (MOE_TILE - 1.0)) * (1.0 / MOE_TILE)) * MOE_TILE
        r = lax.broadcasted_iota(jnp.int32, (128, 128), 0)
        c = lax.broadcasted_iota(jnp.int32, (128, 128), 1)
        off = jnp.dot(jnp.broadcast_to(padded, (8, 128)), (r < c).astype(F32),
                      preferred_element_type=F32, precision=HI)[0:1, :]
        cnt_ref[1:2, :] = off
        end = off + padded
        end_col = jnp.sum(jnp.where(r == c, jnp.broadcast_to(end, (128, 128)), 0.0), axis=1, keepdims=True)
        ended = jnp.logical_and(end_col <= c.astype(F32) * MOE_TILE, r < N_BUCKETS)
        tile_bucket = jnp.sum(ended.astype(F32), axis=0, keepdims=True)
        lane1 = lax.broadcasted_iota(jnp.int32, (1, 128), 1)
        tile_idx = lane1.astype(F32)
        n_valid = jnp.max(jnp.where(lane1 < N_BUCKETS, end, 0.0), axis=1, keepdims=True) * (1.0 / MOE_TILE)
        last = jnp.max(jnp.where(tile_idx < n_valid, tile_bucket, 0.0), axis=1, keepdims=True)
        tb = jnp.where(tile_idx < n_valid, tile_bucket, last)
        g = jnp.floor(tb * (1.0 / N_PAIRS))
        pid = tb - g * N_PAIRS
        a = (pid >= 3.0).astype(F32) + (pid >= 5.0).astype(F32)
        b = pid + 1.0 - 2.0 * (a >= 1.0).astype(F32) - (a >= 2.0).astype(F32)
        rows = [EPG * g + a, EPG * g + b, jnp.broadcast_to(n_valid, (1, 128)), jnp.zeros((5, 128), F32)]
        meta_ref[...] = jnp.concatenate(rows, axis=0).astype(jnp.int32)

    @pl.when(ph == 1)
    def _():
        base = cnt_ref[1:2, :]
        rr = lax.broadcasted_iota(jnp.int32, (TILE, TILE), 0)
        cc = lax.broadcasted_iota(jnp.int32, (TILE, TILE), 1)
        for j, onehot in enumerate(onehots):
            oh16 = onehot.astype(BF16)
            same = _dot_nt(oh16, oh16)
            rank = jnp.sum(jnp.where(rr < cc, same, 0.0), axis=0, keepdims=True)
            base_row = lax.dot_general(jnp.broadcast_to(base, (8, 128)), onehot, (((1,), (1,)), ((), ())),
                                       preferred_element_type=F32, precision=HI)[0:1, :]
            dest_ref[j] = (base_row + rank).astype(jnp.int32)
            base = base + tile_cnts[j]
        cnt_ref[1:2, :] = base


def _moe_dispatch(route):
    dest, meta = pl.pallas_call(
        _dispatch_kernel,
        grid=(2, N_TILES // DISPATCH_TILES),
        in_specs=[pl.BlockSpec((DISPATCH_TILES * TILE, 128), lambda ph, t: (t, 0))],
        out_specs=[pl.BlockSpec((DISPATCH_TILES, 1, TILE), lambda ph, t: (t * ph, 0, 0)),
                   pl.BlockSpec((8, 128), lambda ph, t: (0, 0))],
        out_shape=[jax.ShapeDtypeStruct((N_TILES, 1, TILE), jnp.int32),
                   jax.ShapeDtypeStruct((8, 128), jnp.int32)],
        scratch_shapes=[pltpu.VMEM((8, 128), F32)],
        compiler_params=_cparams(2),
        name="moe_dispatch",
    )(route)
    return dest.reshape(N_ROWS), meta[0:3].reshape(3 * 128)


def _dma_params():
    return pltpu.CompilerParams(dimension_semantics=("arbitrary",), vmem_limit_bytes=VMEM_LIMIT,
                                disable_bounds_checks=True)


def _moe_scatter_kernel(dest_ref, h_ref, hs_init_ref, hs_ref, stage, sems):
    del hs_init_ref
    t = pl.program_id(0)
    base = t * TILE
    slot = t % 2
    stage[slot] = h_ref[...]

    def issue(g, carry):
        for j in range(DMA_GROUP):
            i = g * DMA_GROUP + j
            d = dest_ref[base + i]
            pltpu.make_async_copy(stage.at[slot, pl.ds(i, 1)], hs_ref.at[pl.ds(d, 1)],
                                  sems.at[slot]).start(priority=j % 2)
        return carry

    lax.fori_loop(0, TILE // DMA_GROUP, issue, 0)

    def wait_tile(s):
        pltpu.make_async_copy(stage.at[s], hs_ref.at[pl.ds(0, TILE)], sems.at[s]).wait()

    @pl.when(t > 0)
    def _():
        wait_tile(1 - slot)

    @pl.when(t == pl.num_programs(0) - 1)
    def _():
        wait_tile(slot)


def _moe_scatter(dest, hp, hs_init):
    return pl.pallas_call(
        _moe_scatter_kernel,
        grid_spec=pltpu.PrefetchScalarGridSpec(
            num_scalar_prefetch=1,
            grid=(N_TILES,),
            in_specs=[pl.BlockSpec((TILE, ROW_W), lambda t, dest: (t, 0)),
                      pl.BlockSpec(memory_space=pl.ANY)],
            out_specs=pl.BlockSpec(memory_space=pl.ANY),
            scratch_shapes=[pltpu.VMEM((2, TILE, ROW_W), F32), pltpu.SemaphoreType.DMA((2,))],
        ),
        out_shape=jax.ShapeDtypeStruct((N_SLOTS, ROW_W), F32),
        input_output_aliases={2: 0},
        compiler_params=_dma_params(),
        name="moe_scatter",
    )(dest, hp, hs_init)


def _moe_ffn_kernel(meta_ref, hs_ref, wgl_ref, wul_ref, wdl_ref, wgh_ref, wuh_ref, wdh_ref, ys_ref):
    t = pl.program_id(0)
    n_valid = meta_ref[2 * 128]

    @pl.when(t < n_valid)
    def _():
        w = hs_ref[...]
        x = w[:, :D_MODEL].astype(BF16)
        rw = w[:, D_MODEL:]

        def gated(wg_ref, wu_ref, weight):
            return (_silu(_dot(x, wg_ref[...])) * _dot(x, wu_ref[...]) * weight).astype(BF16)

        a_lo = gated(wgl_ref, wul_ref, rw[:, 1:2])
        a_hi = gated(wgh_ref, wuh_ref, rw[:, 2:3])
        ys_ref[...] = _dot(a_lo, wdl_ref[...]) + _dot(a_hi, wdh_ref[...])

    @pl.when(t >= n_valid)
    def _():
        ys_ref[...] = jnp.zeros_like(ys_ref)


def _moe_ffn(layer, meta, hs, w_gate_bf, w_up_bf, w_down_bf):
    lo = lambda t, m: (layer, m[t], 0, 0)
    hi = lambda t, m: (layer, m[128 + t], 0, 0)
    up = lambda im: pl.BlockSpec((None, None, D_MODEL, EXPERT_FF), im)
    down = lambda im: pl.BlockSpec((None, None, EXPERT_FF, D_MODEL), im)
    return pl.pallas_call(
        _moe_ffn_kernel,
        grid_spec=pltpu.PrefetchScalarGridSpec(
            num_scalar_prefetch=1,
            grid=(MOE_MAX_TILES,),
            in_specs=[pl.BlockSpec((MOE_TILE, ROW_W), lambda t, m: (jnp.maximum(jnp.minimum(t, m[2 * 128] - 1), 0), 0)),
                      up(lo), up(lo), down(lo), up(hi), up(hi), down(hi)],
            out_specs=pl.BlockSpec((MOE_TILE, D_MODEL), lambda t, m: (t, 0)),
        ),
        out_shape=jax.ShapeDtypeStruct((N_SLOTS, D_MODEL), F32),
        compiler_params=_cparams(1),
        name="moe_ffn",
    )(meta, hs, w_gate_bf, w_up_bf, w_down_bf, w_gate_bf, w_up_bf, w_down_bf)


def _moe_combine_kernel(dest_ref, x1_ref, mod_ref, ys_ref, out_ctx_ref, out_lat_ref, buf, sems):
    t = pl.program_id(0)
    nt = pl.num_programs(0)

    def issue(tile, slot):
        def body(g, carry):
            for j in range(DMA_GROUP):
                i = g * DMA_GROUP + j
                d = dest_ref[tile * TILE + i]
                pltpu.make_async_copy(ys_ref.at[pl.ds(d, 1)], buf.at[slot, pl.ds(i, 1)],
                                      sems.at[slot]).start(priority=j % 2)
            return carry
        lax.fori_loop(0, TILE // DMA_GROUP, body, 0)

    @pl.when(t == 0)
    def _():
        issue(0, 0)

    @pl.when(t + 1 < nt)
    def _():
        issue(t + 1, (t + 1) % 2)

    slot = t % 2
    pltpu.make_async_copy(ys_ref.at[pl.ds(0, TILE)], buf.at[slot], sems.at[slot]).wait()
    x2 = x1_ref[...] + mod_ref[5:6, :] * buf[slot]

    @pl.when(t < CTX_TILES)
    def _():
        out_ctx_ref[...] = x2

    @pl.when(t >= CTX_TILES)
    def _():
        out_lat_ref[...] = x2


def _moe_combine(layer, dest, x1, mod, ys):
    return pl.pallas_call(
        _moe_combine_kernel,
        grid_spec=pltpu.PrefetchScalarGridSpec(
            num_scalar_prefetch=1,
            grid=(N_TILES,),
            in_specs=[pl.BlockSpec((TILE, D_MODEL), lambda t, dest: (t, 0)),
                      pl.BlockSpec((None, None, 6, D_MODEL), lambda t, dest: (layer, _cond_of_tile(t), 0, 0)),
                      pl.BlockSpec(memory_space=pl.ANY)],
            out_specs=[pl.BlockSpec((TILE, D_MODEL), lambda t, dest: (jnp.minimum(t, CTX_TILES - 1), 0)),
                       pl.BlockSpec((TILE, D_MODEL), lambda t, dest: (jnp.maximum(t - CTX_TILES, 0), 0))],
            scratch_shapes=[pltpu.VMEM((2, TILE, D_MODEL), F32), pltpu.SemaphoreType.DMA((2,))],
        ),
        out_shape=[jax.ShapeDtypeStruct((N_CTX_ROWS, D_MODEL), F32), jax.ShapeDtypeStruct((N_LAT_ROWS, D_MODEL), F32)],
        compiler_params=_dma_params(),
        name="moe_combine",
    )(dest, x1, mod, ys)


def _block_diag_ones(n, group):
    i = np.arange(n) // group
    return jnp.asarray((i[:, None] == i[None, :]).astype(np.float32), dtype=BF16)


def _rope_tables():
    pos = np.arange(DEC_SEQ)
    rows = (pos // GRID_W).astype(np.float32)
    cols = (pos % GRID_W).astype(np.float32)
    inv_freq = (ROPE_BASE ** (-(np.arange(ROPE_PAIRS, dtype=np.float32) / ROPE_PAIRS))).astype(np.float32)
    ang_r = rows[:, None] * inv_freq[None, :]
    ang_c = cols[:, None] * inv_freq[None, :]
    cos64 = np.concatenate([np.cos(ang_r), np.cos(ang_r), np.cos(ang_c), np.cos(ang_c)], axis=1)
    sin64 = np.concatenate([-np.sin(ang_r), np.sin(ang_r), -np.sin(ang_c), np.sin(ang_c)], axis=1)
    return (jnp.asarray(np.tile(cos64, (1, 2)), F32), jnp.asarray(np.tile(sin64, (1, 2)), F32))


def _mixers(latent, layer, x, consts, params, ret_s0, hg_s0, cache_k, cache_v):
    (mod, lb_all, bd512, bd256, rope_cos, rope_sin, ret_tables) = consts
    n_seq = DEC_BATCH if latent else BATCH
    seq_len = DEC_SEQ if latent else SEQ
    nt = seq_len // TILE
    proj = _in_projection(latent, layer, x, mod, params['norm1'], params['w_in'], params['w_vt'], params['da_qn'],
                          params['da_kn'], lb_all, bd512, rope_cos, rope_sin)
    (rq, rk, rv, rsg, dq, dk, dv, hq, hkf, hkb, hv, hgf, hgb, hsg) = proj[:14]
    ret_of, ret_ob, ret_fin = _retention(n_seq, nt, rq, rk, rv, ret_s0, ret_tables,
                                         "retention_lat" if latent else "retention_ctx")
    hg_of, hg_ob, hg_fin = _gated_scan(n_seq, nt, hq, hkf, hkb, hv, hgf, hgb, hg_s0,
                                       "hgrn_scan_lat" if latent else "hgrn_scan_ctx")
    if latent:
        da_o = _diff_attention(layer, n_seq, seq_len, dq, dk, dv, cache_k, cache_v,
                               params['da_lambda'], params['da_subln'])
    else:
        da_o = _diff_attention_short(layer, n_seq, seq_len, dq, dk, dv, params['da_lambda'], params['da_subln'])
    streams = (x, ret_of, ret_ob, rsg, da_o, hg_of, hg_ob, hsg)
    extras = None if latent else (proj[14], proj[15], ret_fin, hg_fin)
    return streams, extras


def _trunk_layer(layer, x_ctx, x_lat, sorted_buf, consts, params, lat_ret_s0, lat_hg_s0, zero_state, cache_k, cache_v):
    mod, bd256 = consts[0], consts[3]
    ctx_streams, extras = _mixers(False, layer, x_ctx, consts, params, zero_state, zero_state, None, None)
    lat_streams, _ = _mixers(True, layer, x_lat, consts, params, lat_ret_s0, lat_hg_s0, cache_k, cache_v)
    x1, hp, route = _out_projection(layer, mod, params['norm2'], ctx_streams, lat_streams, params['hg_norm'],
                                    bd256, params['w_out'], params['w_route2'], params['b_route'])
    dest, meta = _moe_dispatch(route)
    sorted_buf = _moe_scatter(dest, hp, sorted_buf)
    ys = _moe_ffn(layer, meta, sorted_buf, params['w_gate'], params['w_up'], params['w_down'])
    x_ctx, x_lat = _moe_combine(layer, dest, x1, mod, ys)
    return x_ctx, x_lat, sorted_buf, extras


def kernel(x_prompt, x_sample, cache_k, cache_v, state_ret, state_hgrn, c, c_ctx, norm1, norm2, w_mod, b_mod,
           w_in, w_out, da_qn, da_kn, da_lambda, da_subln, hg_lb, hg_norm, w_group, b_group, w_router,
           b_router, w_gate, w_up, w_down):
    cond = jnp.zeros((N_COND, D_MODEL), F32).at[0].set(c_ctx).at[1:1 + DEC_BATCH].set(c)
    mod, lb_all = _modulation(cond, w_mod, b_mod, hg_lb)
    mod = mod.reshape(DEPTH, N_COND, 6, D_MODEL)
    rope_cos, rope_sin = _rope_tables()
    consts = (mod, lb_all.reshape(DEPTH, 1, MIX_G), _block_diag_ones(DA_W, HEAD_D),
              _block_diag_ones(MIX_G, HEAD_D), rope_cos, rope_sin, _retention_tables())
    pad = jnp.zeros((DEPTH, D_MODEL, 128 - N_EXPERTS - N_GROUPS), F32)
    w_route = jnp.concatenate([w_router, w_group, pad], axis=-1)
    params = {
        'norm1': norm1.reshape(DEPTH, 1, D_MODEL), 'norm2': norm2.reshape(DEPTH, 1, D_MODEL),
        'w_in': w_in.astype(BF16), 'w_out': w_out.astype(BF16),
        'w_vt': jnp.swapaxes(lax.optimization_barrier(w_in[:, :, 2048:2560]), 1, 2).astype(BF16),
        'da_qn': jnp.tile(da_qn, (1, DA_W // HEAD_D)).reshape(DEPTH, 1, DA_W),
        'da_kn': jnp.tile(da_kn, (1, DA_W // HEAD_D)).reshape(DEPTH, 1, DA_W),
        'da_lambda': da_lambda, 'da_subln': da_subln.reshape(DEPTH, 1, 128),
        'hg_norm': jnp.tile(hg_norm, (1, HEADS)).reshape(DEPTH, 1, MIX_G),
        'w_route2': jnp.concatenate([w_route.astype(BF16),
                                     (w_route - w_route.astype(BF16).astype(F32)).astype(BF16)], axis=-1),
        'b_route': jnp.concatenate([b_router, b_group, pad[:, 0, :]], axis=-1).reshape(DEPTH, 1, 128),
        'w_gate': w_gate.astype(BF16), 'w_up': w_up.astype(BF16), 'w_down': w_down.astype(BF16),
    }
    ck_bf = cache_k.astype(BF16).reshape(DEC_BATCH, DEPTH, PAST_LEN, DA_W)
    cv_bf = jnp.swapaxes(cache_v.astype(BF16).reshape(DEC_BATCH, DEPTH, PAST_LEN, DA_W), 2, 3)
    zero_state = jnp.zeros((BATCH, 2, HEADS, HEAD_D, HEAD_D), F32)

    yp = x_prompt.reshape(N_CTX_ROWS, D_MODEL)
    ys = x_sample.reshape(N_LAT_ROWS, D_MODEL)
    sorted_buf = jnp.zeros((N_SLOTS, ROW_W), F32)
    ks_out, vs_out, rets_out, hgs_out = [], [], [], []
    for l in range(DEPTH):
        yp, ys, sorted_buf, (k_new, v_new, ret_fin, hg_fin) = _trunk_layer(
            l, yp, ys, sorted_buf, consts, params, state_ret[:, l], state_hgrn[:, l], zero_state, ck_bf, cv_bf)
        ks_out.append(k_new.reshape(BATCH, SEQ, DA_W))
        vs_out.append(v_new.reshape(BATCH, SEQ, DA_W))
        rets_out.append(ret_fin)
        hgs_out.append(hg_fin)
    return (yp.reshape(BATCH, SEQ, D_MODEL), ys.reshape(DEC_BATCH, DEC_SEQ, D_MODEL),
            jnp.stack(ks_out, axis=1).reshape(BATCH, DEPTH, SEQ, HEADS, 2, HEAD_D),
            jnp.stack(vs_out, axis=1).reshape(BATCH, DEPTH, SEQ, HEADS, 2 * HEAD_D),
            jnp.stack(rets_out, axis=1), jnp.stack(hgs_out, axis=1))
```
